```python
import jax, jax.numpy as jnp
from jax import lax
import numpy as np

D_MODEL = 2048
BATCH = 2
SEQ = 4096
DEPTH = 1
DEC_BATCH = 32
DEC_SEQ = 4
PAST_LEN = 8192
PAGE_SIZE = 128

HEAD_DIM = 128
N_HEADS = D_MODEL // HEAD_DIM
NSA_HEADS = N_HEADS // 2
FOX_HEADS = N_HEADS - NSA_HEADS
NSA_KV_HEADS = NSA_HEADS // 4
NSA_GROUP = NSA_HEADS // NSA_KV_HEADS
NSA_WIDTH = NSA_HEADS * HEAD_DIM
FOX_WIDTH = FOX_HEADS * HEAD_DIM
MIX_WIDTH = NSA_WIDTH + FOX_WIDTH
CMP_BLOCK = 32
SEL_BLOCK = 64
SEL_TOPK = 16
WINDOW = 512
Q_BLOCK = 128
FORCED_SCORE = 1e4
RMS_EPS = 1e-6
IN_SPLITS = (NSA_WIDTH, 6 * NSA_KV_HEADS * HEAD_DIM, 3 * NSA_HEADS, NSA_WIDTH,
             FOX_WIDTH, FOX_WIDTH, FOX_WIDTH, FOX_HEADS, FOX_WIDTH)
IN_COLS = sum(IN_SPLITS)

kernel_name = 'nsa_fox_parallel_heads_step'


def rmsnorm(x, g):
    xf = x.astype(jnp.float32)
    y = xf * lax.rsqrt(jnp.mean(xf * xf, axis=-1, keepdims=True) + RMS_EPS)
    return (y * g.astype(jnp.float32)).astype(x.dtype)


def alibi_slopes(n):
    return jnp.exp2(-8.0 * jnp.arange(1, n + 1, dtype=jnp.float32) / n)


def masked_softmax(s, mask):
    s = jnp.where(mask, s.astype(jnp.float32), -jnp.inf)
    m = jnp.max(s, axis=-1, keepdims=True)
    m = jnp.where(jnp.isfinite(m), m, 0.0)
    p = jnp.exp(s - m)
    return p / jnp.maximum(jnp.sum(p, axis=-1, keepdims=True), 1e-30)


def project(h, w_in, b_gate, b_forget):
    B, T, _ = h.shape
    z = jnp.einsum('btd,dc->btc', h, w_in)
    cuts = [int(c) for c in np.cumsum(IN_SPLITS)[:-1]]
    q_n, kv_n, g_n, z_n, q_f, k_f, v_f, f_f, z_f = jnp.split(z, cuts, axis=-1)
    q_n = q_n.reshape(B, T, NSA_HEADS, HEAD_DIM)
    kv_n = kv_n.reshape(B, T, 6, NSA_KV_HEADS, HEAD_DIM)
    gates = jax.nn.sigmoid((g_n + b_gate).astype(jnp.float32)).reshape(B, T, NSA_HEADS, 3)
    q_f = q_f.reshape(B, T, FOX_HEADS, HEAD_DIM)
    k_f = k_f.reshape(B, T, FOX_HEADS, HEAD_DIM)
    v_f = v_f.reshape(B, T, FOX_HEADS, HEAD_DIM)
    logf = jax.nn.log_sigmoid((f_f + b_forget).astype(jnp.float32))
    return q_n, kv_n, gates, z_n, q_f, k_f, v_f, logf, z_f


def compress(rows, pe, w):
    B, L = rows.shape[:2]
    n_cmp = L // CMP_BLOCK
    blk = rows[:, :n_cmp * CMP_BLOCK].reshape(B, n_cmp, CMP_BLOCK, NSA_KV_HEADS, HEAD_DIM)
    blk = blk + pe[None, None, :, None, :]
    out = jnp.einsum('bnchd,cde->bnhe', blk, w)
    end = (jnp.arange(n_cmp) + 1) * CMP_BLOCK - 1
    return out, end


def to_sel_blocks(rows):
    B, L = rows.shape[:2]
    n_sel = -(-L // SEL_BLOCK)
    rows = jnp.pad(rows, ((0, 0), (0, n_sel * SEL_BLOCK - L), (0, 0), (0, 0)))
    return rows.reshape(B, n_sel, SEL_BLOCK, NSA_KV_HEADS, HEAD_DIM).transpose(0, 3, 1, 2, 4)


def nsa_block(q, q_pos, gates, kc, vc, c_end, ks, vs, kw, vw, w_pos, slopes):
    B, Tb = q.shape[:2]
    f32 = jnp.float32
    scale = HEAD_DIM ** -0.5
    qg = q.reshape(B, Tb, NSA_KV_HEADS, NSA_GROUP, HEAD_DIM)
    sl = slopes.reshape(NSA_KV_HEADS, NSA_GROUP)
    dist_c = (q_pos[:, None] - c_end[None, :]).astype(f32)
    s_c = jnp.einsum('bthgd,bnhd->bhgtn', qg, kc).astype(f32) * scale - sl[:, :, None, None] * dist_c
    p_c = masked_softmax(s_c, dist_c >= 0)
    o_c = jnp.einsum('bhgtn,bnhd->bthgd', p_c.astype(vc.dtype), vc)
    n_cmp, n_sel = kc.shape[1], ks.shape[2]
    ratio = SEL_BLOCK // CMP_BLOCK
    p_pad = jnp.pad(p_c, ((0, 0), (0, 0), (0, 0), (0, 0), (0, n_sel * ratio - n_cmp)))
    score = p_pad.reshape(B, NSA_KV_HEADS, NSA_GROUP, Tb, n_sel, ratio).sum(axis=(2, 5))
    blk = jnp.arange(n_sel)[None, :]
    cur = (q_pos // SEL_BLOCK)[:, None]
    forced = (blk == 0) | (blk == cur) | (blk == cur - 1)
    score = jnp.where(blk <= cur, jnp.where(forced, FORCED_SCORE, score), -1.0)
    top_s, idx = lax.top_k(score, min(SEL_TOPK, n_sel))
    bi = jnp.arange(B)[:, None, None, None]
    hi = jnp.arange(NSA_KV_HEADS)[None, :, None, None]
    kg = ks[bi, hi, idx]
    vg = vs[bi, hi, idx]
    n_k = idx.shape[-1] * SEL_BLOCK
    k_pos = idx[..., None] * SEL_BLOCK + jnp.arange(SEL_BLOCK)
    dist_s = (q_pos[None, None, :, None, None] - k_pos).astype(f32)
    mask_s = ((dist_s >= 0) & (top_s >= 0)[..., None]).reshape(B, NSA_KV_HEADS, 1, Tb, n_k)
    s_s = jnp.einsum('bthgd,bhtkcd->bhgtkc', qg, kg).astype(f32) * scale
    s_s = s_s - sl[None, :, :, None, None, None] * dist_s[:, :, None]
    p_s = masked_softmax(s_s.reshape(B, NSA_KV_HEADS, NSA_GROUP, Tb, n_k), mask_s)
    o_s = jnp.einsum('bhgtn,bhtnd->bthgd', p_s.astype(vg.dtype),
                     vg.reshape(B, NSA_KV_HEADS, Tb, n_k, HEAD_DIM))
    dist_w = q_pos[:, None] - w_pos[None, :]
    mask_w = (dist_w >= 0) & (dist_w < WINDOW) & (w_pos[None, :] >= 0)
    s_w = jnp.einsum('bthgd,bshd->bhgts', qg, kw).astype(f32) * scale - sl[:, :, None, None] * dist_w.astype(f32)
    p_w = masked_softmax(s_w, mask_w)
    o_w = jnp.einsum('bhgts,bshd->bthgd', p_w.astype(vw.dtype), vw)
    g = gates.reshape(B, Tb, NSA_KV_HEADS, NSA_GROUP, 3)
    o = g[..., 0:1] * o_c + g[..., 1:2] * o_s + g[..., 2:3] * o_w
    return o.reshape(B, Tb, NSA_HEADS, HEAD_DIM).astype(q.dtype)


def fox_block(q, q_pos, dq, k, v, dk, k_pos):
    s = jnp.einsum('bthd,bshd->bhts', q, k).astype(jnp.float32) * (HEAD_DIM ** -0.5)
    s = s + jnp.swapaxes(dq, 1, 2)[:, :, :, None] - jnp.swapaxes(dk, 1, 2)[:, :, None, :]
    p = masked_softmax(s, k_pos[None, :] <= q_pos[:, None])
    return jnp.einsum('bhts,bshd->bthd', p.astype(v.dtype), v)


def merge(x, o_n, z_n, o_f, z_f, w_out):
    mix = jnp.concatenate([o_n * jax.nn.silu(z_n), o_f * jax.nn.silu(z_f)], axis=-1)
    return x + jnp.einsum('btc,cd->btd', mix, w_out)


def prompt_layer(x, norm_in, w_in, b_gate, b_forget, cmp_pe, cmp_w, w_out):
    B, T, _ = x.shape
    h = rmsnorm(x, norm_in)
    q_n, kv_n, gates, z_n, q_f, k_f, v_f, logf, z_f = project(h, w_in, b_gate, b_forget)
    slopes = alibi_slopes(NSA_HEADS)
    kc, c_end = compress(kv_n[:, :, 0], cmp_pe[0], cmp_w[0])
    vc, _ = compress(kv_n[:, :, 1], cmp_pe[1], cmp_w[1])
    ks = to_sel_blocks(kv_n[:, :, 2])
    vs = to_sel_blocks(kv_n[:, :, 3])
    pad = ((0, 0), (WINDOW, 0), (0, 0), (0, 0))
    kw_pad = jnp.pad(kv_n[:, :, 4], pad)
    vw_pad = jnp.pad(kv_n[:, :, 5], pad)
    d_f = lax.cumsum(logf, axis=1)
    k_pos = jnp.arange(T)

    def sweep(i):
        s0 = i * Q_BLOCK
        q_pos = s0 + jnp.arange(Q_BLOCK)
        blk = lambda a: lax.dynamic_slice_in_dim(a, s0, Q_BLOCK, axis=1)
        band = lambda a: lax.dynamic_slice_in_dim(a, s0, WINDOW + Q_BLOCK, axis=1)
        w_pos = s0 - WINDOW + jnp.arange(WINDOW + Q_BLOCK)
        o_n = nsa_block(blk(q_n), q_pos, blk(gates), kc, vc, c_end, ks, vs,
                        band(kw_pad), band(vw_pad), w_pos, slopes)
        o_f = fox_block(blk(q_f), q_pos, blk(d_f), k_f, v_f, d_f, k_pos)
        return o_n, o_f

    o_n, o_f = lax.map(sweep, jnp.arange(T // Q_BLOCK))
    o_n = jnp.swapaxes(o_n, 0, 1).reshape(B, T, NSA_WIDTH)
    o_f = jnp.swapaxes(o_f, 0, 1).reshape(B, T, FOX_WIDTH)
    y = merge(x, o_n, z_n, o_f, z_f, w_out)
    w_keep = min(WINDOW, T)
    return y, (kv_n[:, :, :4], jnp.stack([k_f, v_f], axis=2), logf, kv_n[:, T - w_keep:, 4:])


def sample_layer(x, cache_nsa_kv, cache_fox_kv, cache_fox_logf, state_nsa_win, page_table,
                 norm_in, w_in, b_gate, b_forget, cmp_pe, cmp_w, w_out):
    B, T, _ = x.shape
    P = page_table.shape[1] * PAGE_SIZE
    h = rmsnorm(x, norm_in)
    q_n, kv_n, gates, z_n, q_f, k_f, v_f, logf, z_f = project(h, w_in, b_gate, b_forget)
    slopes = alibi_slopes(NSA_HEADS)
    q_pos = P + jnp.arange(T)
    past_nsa = cache_nsa_kv[page_table].reshape(B, P, 4, NSA_KV_HEADS, HEAD_DIM)
    nsa_all = jnp.concatenate([past_nsa, kv_n[:, :, :4]], axis=1)
    kc, c_end = compress(nsa_all[:, :, 0], cmp_pe[0], cmp_w[0])
    vc, _ = compress(nsa_all[:, :, 1], cmp_pe[1], cmp_w[1])
    ks = to_sel_blocks(nsa_all[:, :, 2])
    vs = to_sel_blocks(nsa_all[:, :, 3])
    win_all = jnp.concatenate([state_nsa_win, kv_n[:, :, 4:]], axis=1)
    w_buf = state_nsa_win.shape[1]
    w_pos = P - w_buf + jnp.arange(w_buf + T)
    o_n = nsa_block(q_n, q_pos, gates, kc, vc, c_end, ks, vs,
                    win_all[:, :, 0], win_all[:, :, 1], w_pos, slopes)
    past_fox = cache_fox_kv[page_table].reshape(B, P, 2, FOX_HEADS, HEAD_DIM)
    k_all = jnp.concatenate([past_fox[:, :, 0], k_f], axis=1)
    v_all = jnp.concatenate([past_fox[:, :, 1], v_f], axis=1)
    logf_all = jnp.concatenate([cache_fox_logf[page_table].reshape(B, P, FOX_HEADS), logf],
                               axis=1).astype(jnp.float32)
    d_all = lax.cumsum(logf_all, axis=1)
    o_f = fox_block(q_f, q_pos, d_all[:, P:], k_all, v_all, d_all, jnp.arange(P + T))
    y = merge(x, o_n.reshape(B, T, NSA_WIDTH), z_n, o_f.reshape(B, T, FOX_WIDTH), z_f, w_out)
    return y, (kv_n[:, :, :4], jnp.stack([k_f, v_f], axis=2), logf, win_all[:, T:])


def setup_inputs(seed: int = 0) -> dict:
    key = jax.random.key(seed)
    ks = jax.random.split(key, 16)
    f32 = jnp.float32
    n_pages = PAST_LEN // PAGE_SIZE
    n_used = DEC_BATCH * n_pages
    n_pool = (5 * n_used + 3) // 4
    w_buf = min(WINDOW, PAST_LEN)
    nrm = jax.random.normal
    x_prompt = nrm(ks[0], (BATCH, SEQ, D_MODEL), f32)
    x_sample = nrm(ks[1], (DEC_BATCH, DEC_SEQ, D_MODEL), f32)
    cache_nsa_kv = nrm(ks[2], (DEPTH, n_pool, PAGE_SIZE, 4, NSA_KV_HEADS, HEAD_DIM), f32)
    cache_fox_kv = nrm(ks[3], (DEPTH, n_pool, PAGE_SIZE, 2, FOX_HEADS, HEAD_DIM), f32)
    cache_fox_logf = jax.nn.log_sigmoid(2.0 + nrm(ks[4], (DEPTH, n_pool, PAGE_SIZE, FOX_HEADS), f32))
    state_nsa_win = nrm(ks[5], (DEPTH, DEC_BATCH, w_buf, 2, NSA_KV_HEADS, HEAD_DIM), f32)
    page_table = jax.random.permutation(ks[6], n_pool)[:n_used].reshape(DEC_BATCH, n_pages).astype(jnp.int32)
    norm_in = 1.0 + 0.01 * nrm(ks[7], (DEPTH, D_MODEL), f32)
    w_in = nrm(ks[8], (DEPTH, D_MODEL, IN_COLS), f32) * D_MODEL ** -0.5
    b_gate = 0.1 * nrm(ks[9], (DEPTH, 3 * NSA_HEADS), f32)
    b_forget = 2.0 + 0.1 * nrm(ks[10], (DEPTH, FOX_HEADS), f32)
    cmp_pe = 0.1 * nrm(ks[11], (DEPTH, 2, CMP_BLOCK, HEAD_DIM), f32)
    cmp_w = nrm(ks[12], (DEPTH, 2, CMP_BLOCK, HEAD_DIM, HEAD_DIM), f32) * (CMP_BLOCK * HEAD_DIM) ** -0.5
    w_out = nrm(ks[13], (DEPTH, MIX_WIDTH, D_MODEL), f32) * MIX_WIDTH ** -0.5
    norm_final = 1.0 + 0.01 * nrm(ks[14], (D_MODEL,), f32)
    return {'x_prompt': x_prompt, 'x_sample': x_sample, 'cache_nsa_kv': cache_nsa_kv,
            'cache_fox_kv': cache_fox_kv, 'cache_fox_logf': cache_fox_logf,
            'state_nsa_win': state_nsa_win, 'page_table': page_table,
            'norm_in': norm_in, 'w_in': w_in, 'b_gate': b_gate, 'b_forget': b_forget,
            'cmp_pe': cmp_pe, 'cmp_w': cmp_w, 'w_out': w_out, 'norm_final': norm_final}


def reference(x_prompt, x_sample, cache_nsa_kv, cache_fox_kv, cache_fox_logf, state_nsa_win, page_table,
              norm_in, w_in, b_gate, b_forget, cmp_pe, cmp_w, w_out, norm_final):
    hp, hs = x_prompt, x_sample
    st_p, st_s = [], []
    for d in range(DEPTH):
        lw = (norm_in[d], w_in[d], b_gate[d], b_forget[d], cmp_pe[d], cmp_w[d], w_out[d])
        hp, sp = prompt_layer(hp, *lw)
        hs, ss = sample_layer(hs, cache_nsa_kv[d], cache_fox_kv[d], cache_fox_logf[d],
                              state_nsa_win[d], page_table, *lw)
        st_p.append(sp)
        st_s.append(ss)
    y_prompt = rmsnorm(hp, norm_final)
    y_sample = rmsnorm(hs, norm_final)
    nsa_kv_prompt = jnp.stack([s[0] for s in st_p])
    nsa_kv_sample = jnp.stack([s[0] for s in st_s])
    fox_kv_prompt = jnp.stack([s[1] for s in st_p])
    fox_kv_sample = jnp.stack([s[1] for s in st_s])
    fox_logf_prompt = jnp.stack([s[2] for s in st_p])
    fox_logf_sample = jnp.stack([s[2] for s in st_s])
    nsa_win_prompt = jnp.stack([s[3] for s in st_p])
    nsa_win_sample = jnp.stack([s[3] for s in st_s])
    return (y_prompt, y_sample, nsa_kv_prompt, nsa_kv_sample, fox_kv_prompt, fox_kv_sample,
            fox_logf_prompt, fox_logf_sample, nsa_win_prompt, nsa_win_sample)
```

```python
import functools

import jax
import jax.numpy as jnp
from jax import lax
from jax.experimental import pallas as pl
from jax.experimental.pallas import tpu as pltpu

F32 = jnp.float32
BF16 = jnp.bfloat16
NEG_INF = float("-inf")

HEAD_DIM = 128
NSA_HEADS = 8
FOX_HEADS = 8
NSA_KV_HEADS = 2
NSA_GROUP = NSA_HEADS // NSA_KV_HEADS
NSA_WIDTH = NSA_HEADS * HEAD_DIM
FOX_WIDTH = FOX_HEADS * HEAD_DIM
KV_WIDTH = NSA_KV_HEADS * HEAD_DIM
CMP_BLOCK = 32
SEL_BLOCK = 64
SEL_TOPK = 16
WINDOW = 512
PAGE_SIZE = 128
FORCED_SCORE = 1e4
RMS_EPS = 1e-6
SCALE = HEAD_DIM ** -0.5
T_PAD = 8
LANES = 128
VMEM_LIMIT = 56 * 1024 * 1024


def _params(sem):
    return pltpu.CompilerParams(dimension_semantics=sem, vmem_limit_bytes=VMEM_LIMIT)


def _dot(a, b):
    return jnp.dot(a, b, preferred_element_type=F32)


def _dot_nt(a, b):
    return lax.dot_general(a, b, (((1,), (1,)), ((), ())), preferred_element_type=F32)


def _sigmoid(x):
    return 1.0 / (1.0 + jnp.exp(-x))


def _silu(x):
    return x * _sigmoid(x)


def _softmax_init(m_ref, l_ref, acc_ref):
    m_ref[...] = jnp.full(m_ref.shape, NEG_INF, F32)
    l_ref[...] = jnp.zeros(l_ref.shape, F32)
    acc_ref[...] = jnp.zeros(acc_ref.shape, F32)


def _softmax_update(logits, v, m_ref, l_ref, acc_ref):
    m_old = m_ref[...]
    m_new = jnp.maximum(m_old, jnp.max(logits, axis=1, keepdims=True))
    m_safe = jnp.where(m_new == NEG_INF, 0.0, m_new)
    alpha = jnp.exp(m_old - m_safe)
    p = jnp.exp(logits - m_safe)
    l_ref[...] = alpha * l_ref[...] + jnp.sum(p, axis=1, keepdims=True)
    acc_ref[...] = alpha * acc_ref[...] + _dot(p.astype(BF16), v)
    m_ref[...] = m_new


def _softmax_result(l_ref, acc_ref):
    return acc_ref[...] / jnp.maximum(l_ref[...], 1e-30)


def _lane_cumsum(x):
    n = x.shape[-1]
    lane = lax.broadcasted_iota(jnp.int32, x.shape, x.ndim - 1)
    s = 1
    while s < n:
        x = x + jnp.where(lane >= s, pltpu.roll(x, s, axis=x.ndim - 1), 0.0)
        s *= 2
    return x


def _head_slope(head):
    if isinstance(head, int):
        return 2.0 ** -(head + 1)
    return lax.bitcast_convert_type((126 - head) << 23, F32)


def _rms_kernel(x_ref, g_ref, o_ref):
    x = x_ref[...]
    ms = jnp.mean(x * x, axis=-1, keepdims=True)
    o_ref[...] = (x * lax.rsqrt(ms + RMS_EPS) * g_ref[...]).astype(o_ref.dtype)


def _rmsnorm(x2d, g, out_dtype, tm):
    m, d = x2d.shape
    return pl.pallas_call(
        _rms_kernel,
        grid=(m // tm,),
        in_specs=[pl.BlockSpec((tm, d), lambda i: (i, 0)), pl.BlockSpec((1, d), lambda i: (0, 0))],
        out_specs=pl.BlockSpec((tm, d), lambda i: (i, 0)),
        out_shape=jax.ShapeDtypeStruct((m, d), out_dtype),
        compiler_params=_params(("parallel",)),
        name="rmsnorm",
    )(x2d, g.reshape(1, d))


def _proj_kernel(h_ref, w_ref, *o_refs):
    acc = _dot(h_ref[...], w_ref[...])
    for o_ref in o_refs:
        o_ref[...] = acc.astype(o_ref.dtype)


def _project(h, w, out_dtypes, tm, tn, name):
    m, d = h.shape
    n = w.shape[1]
    tn = min(tn, n)
    return pl.pallas_call(
        _proj_kernel,
        grid=(m // tm, n // tn),
        in_specs=[pl.BlockSpec((tm, d), lambda i, j: (i, 0)), pl.BlockSpec((d, tn), lambda i, j: (0, j))],
        out_specs=[pl.BlockSpec((tm, tn), lambda i, j: (i, j)) for _ in out_dtypes],
        out_shape=[jax.ShapeDtypeStruct((m, n), dt) for dt in out_dtypes],
        compiler_params=_params(("parallel", "arbitrary")),
        name=name,
    )(h, w)


def _small_kernel(h_ref, w_ref, b_ref, o_ref, *, n_gate):
    z = _dot(h_ref[...], w_ref[...]) + b_ref[...]
    lane = lax.broadcasted_iota(jnp.int32, z.shape, 1)
    log_sig = jnp.minimum(z, 0.0) - jnp.log1p(jnp.exp(-jnp.abs(z)))
    o_ref[...] = jnp.where(lane < n_gate, _sigmoid(z), log_sig)


def _project_small(h, w, b, tm, n_gate):
    m, d = h.shape
    return pl.pallas_call(
        functools.partial(_small_kernel, n_gate=n_gate),
        grid=(m // tm,),
        in_specs=[pl.BlockSpec((tm, d), lambda i: (i, 0)), pl.BlockSpec((d, LANES), lambda i: (0, 0)),
                  pl.BlockSpec((1, LANES), lambda i: (0, 0))],
        out_specs=pl.BlockSpec((tm, LANES), lambda i: (i, 0)),
        out_shape=jax.ShapeDtypeStruct((m, LANES), F32),
        compiler_params=_params(("parallel",)),
        name="proj_gates_logf",
    )(h, w, b)


def _split_weights(w_in, b_gate, b_forget):
    cuts = [NSA_WIDTH, 6 * KV_WIDTH, 3 * NSA_HEADS, NSA_WIDTH, FOX_WIDTH, FOX_WIDTH, FOX_WIDTH, FOX_HEADS, FOX_WIDTH]
    offs = [0]
    for c in cuts:
        offs.append(offs[-1] + c)
    col = lambda a, b: w_in[:, a:b].astype(BF16)
    n_small = 3 * NSA_HEADS + FOX_HEADS
    w_small = jnp.concatenate([w_in[:, offs[2]:offs[3]], w_in[:, offs[7]:offs[8]]], axis=1)
    w_small = jnp.pad(w_small, ((0, 0), (0, LANES - n_small))).astype(BF16)
    b_small = jnp.pad(jnp.concatenate([b_gate, b_forget]), (0, LANES - n_small)).reshape(1, LANES).astype(F32)
    return {
        "q_n": col(offs[0], offs[1]),
        "kv4": col(offs[1], offs[1] + 4 * KV_WIDTH),
        "kvwin": col(offs[1] + 4 * KV_WIDTH, offs[2]),
        "z_n": col(offs[3], offs[4]),
        "q_f": col(offs[4], offs[5]),
        "kv_f": col(offs[5], offs[7]),
        "z_f": col(offs[8], offs[9]),
        "small": w_small,
        "b_small": b_small,
    }


def _project_all(x2d, norm_in, wts, tm):
    h = _rmsnorm(x2d, norm_in, BF16, min(tm, 512))
    tn = 512
    out = {}
    (out["q_n"],) = _project(h, wts["q_n"], [BF16], tm, tn, "proj_q_n")
    out["kv4"], out["kv4_bf"] = _project(h, wts["kv4"], [F32, BF16], tm, tn, "proj_kv4")
    out["kvwin"], out["kvwin_bf"] = _project(h, wts["kvwin"], [F32, BF16], tm, tn, "proj_kvwin")
    (out["z_n"],) = _project(h, wts["z_n"], [F32], tm, tn, "proj_z_n")
    (out["q_f"],) = _project(h, wts["q_f"], [BF16], tm, tn, "proj_q_f")
    out["kv_f"], out["kv_f_bf"] = _project(h, wts["kv_f"], [F32, BF16], tm, tn, "proj_kv_f")
    (out["z_f"],) = _project(h, wts["z_f"], [F32], tm, tn, "proj_z_f")
    out["small"] = _project_small(h, wts["small"], wts["b_small"], min(tm, 512), 3 * NSA_HEADS)
    return out


def _merge_kernel(x_ref, mn_ref, mf_ref, wn_ref, wf_ref, g_ref, o_ref):
    y = x_ref[...] + _dot(mn_ref[...], wn_ref[...]) + _dot(mf_ref[...], wf_ref[...])
    ms = jnp.mean(y * y, axis=-1, keepdims=True)
    o_ref[...] = y * lax.rsqrt(ms + RMS_EPS) * g_ref[...]


def _merge(x2d, mix_n, mix_f, w_out_bf, norm_final, tm):
    m, d = x2d.shape
    return pl.pallas_call(
        _merge_kernel,
        grid=(m // tm,),
        in_specs=[pl.BlockSpec((tm, d), lambda i: (i, 0)),
                  pl.BlockSpec((tm, NSA_WIDTH), lambda i: (i, 0)),
                  pl.BlockSpec((tm, FOX_WIDTH), lambda i: (i, 0)),
                  pl.BlockSpec((NSA_WIDTH, d), lambda i: (0, 0)),
                  pl.BlockSpec((FOX_WIDTH, d), lambda i: (NSA_WIDTH // FOX_WIDTH, 0)),
                  pl.BlockSpec((1, d), lambda i: (0, 0))],
        out_specs=pl.BlockSpec((tm, d), lambda i: (i, 0)),
        out_shape=jax.ShapeDtypeStruct((m, d), F32),
        compiler_params=_params(("parallel",)),
        name="merge_out_proj",
    )(x2d, mix_n, mix_f, w_out_bf, w_out_bf, norm_final.reshape(1, d))


def _cumsum_kernel(x_ref, o_ref):
    o_ref[...] = _lane_cumsum(x_ref[...])


def _cumsum_rows(x):
    return pl.pallas_call(
        _cumsum_kernel,
        out_shape=jax.ShapeDtypeStruct(x.shape, F32),
        name="forget_cumsum",
    )(x)


def _compress_rows(x_ref, pe_ref, kind, w, xs_ref, nb):
    half = nb // 2
    for c in range(CMP_BLOCK):
        pe_c = pe_ref[kind, c:c + 1, :]
        for par in range(2):
            xc = x_ref[pl.ds(par * CMP_BLOCK + c, half, stride=2 * CMP_BLOCK), :]
            xs_ref[par * half:(par + 1) * half, c * HEAD_DIM:(c + 1) * HEAD_DIM] = (xc + pe_c).astype(BF16)
    return _dot(xs_ref[...], w)


def _compress_kernel(x_ref, pe_ref, w_ref, o_ref, xs_ref, *, nb):
    o_ref[0, 0, 0] = _compress_rows(x_ref, pe_ref, 0, w_ref[0], xs_ref, nb).astype(o_ref.dtype)


def _compress_prompt(kv4, cmp_pe, cmp_w_bf, batch, seq):
    nb = seq // CMP_BLOCK
    return pl.pallas_call(
        functools.partial(_compress_kernel, nb=nb),
        grid=(batch, 2, NSA_KV_HEADS),
        in_specs=[pl.BlockSpec((seq, HEAD_DIM), lambda b, k, h: (b, k * NSA_KV_HEADS + h)),
                  pl.BlockSpec((1, CMP_BLOCK, HEAD_DIM), lambda b, k, h: (k, 0, 0)),
                  pl.BlockSpec((1, CMP_BLOCK * HEAD_DIM, HEAD_DIM), lambda b, k, h: (k, 0, 0))],
        out_specs=pl.BlockSpec((1, 1, 1, nb, HEAD_DIM), lambda b, k, h: (b, k, h, 0, 0)),
        out_shape=jax.ShapeDtypeStruct((batch, 2, NSA_KV_HEADS, nb, HEAD_DIM), BF16),
        scratch_shapes=[pltpu.VMEM((nb, CMP_BLOCK * HEAD_DIM), BF16)],
        compiler_params=_params(("parallel", "parallel", "arbitrary")),
        name="compress_prompt",
    )(kv4, cmp_pe, cmp_w_bf)


def _select_blocks(score_t, cur, n_keep):
    n_blk = score_t.shape[0]
    j = lax.broadcasted_iota(jnp.int32, score_t.shape, 0)
    forced = (j == 0) | (j == cur) | (j == cur - 1)
    s = jnp.where(j <= cur, jnp.where(forced, FORCED_SCORE, score_t), -1.0)
    cnt = jnp.zeros(s.shape, F32)
    for jj in range(n_blk):
        row = s[jj:jj + 1, :]
        cnt = cnt + jnp.where(row > s, 1.0, jnp.where((row == s) & (j > jj), 1.0, 0.0))
    return jnp.where((cnt < n_keep) & (j <= cur), 1.0, 0.0)


def _nsa_prompt_kernel(q_ref, kc_ref, vc_ref, ks_ref, vs_ref, kw_ref, vw_ref, g_ref, z_ref, e_ref, o_ref,
                       m_ref, l_ref, acc_ref, mix_ref, bias_ref, *, tq, tk, tw, ck, seq):
    qi = pl.program_id(1)
    s0 = qi * tq
    n_cmp = seq // CMP_BLOCK
    n_sel = seq // SEL_BLOCK
    row_t = s0 + lax.broadcasted_iota(jnp.int32, (tq, 1), 0)
    for hkv in range(NSA_KV_HEADS):
        heads = [hkv * NSA_GROUP + g for g in range(NSA_GROUP)]
        slopes = [_head_slope(h) for h in heads]
        q = jnp.concatenate([q_ref[:, h * HEAD_DIM:(h + 1) * HEAD_DIM] for h in heads], axis=0)

        def gate_col(br):
            return jnp.concatenate([g_ref[:, 3 * h + br:3 * h + br + 1] for h in heads], axis=0)

        kc = kc_ref[0, 0, hkv]
        vc = vc_ref[0, 0, hkv]
        sc = _dot_nt(q, kc) * SCALE
        lane = lax.broadcasted_iota(jnp.int32, (tq, n_cmp), 1)
        blk = jnp.where(lane < n_cmp // 2, 2 * lane, 2 * lane - (n_cmp - 1))
        dist_c = (row_t - ((blk + 1) * CMP_BLOCK - 1)).astype(F32)
        valid_c = dist_c >= 0
        score = jnp.zeros((tq, n_cmp), F32)
        probs = []
        for g in range(NSA_GROUP):
            lg = jnp.where(valid_c, sc[g * tq:(g + 1) * tq] - slopes[g] * dist_c, NEG_INF)
            mx = jnp.max(lg, axis=1, keepdims=True)
            mx = jnp.where(mx == NEG_INF, 0.0, mx)
            p = jnp.exp(lg - mx)
            p = p / jnp.maximum(jnp.sum(p, axis=1, keepdims=True), 1e-30)
            score = score + p
            probs.append(p)
        o_c = _dot(jnp.concatenate(probs, axis=0).astype(BF16), vc)
        mix_ref[...] = gate_col(0) * o_c

        score_t = score.T
        score_t = score_t[:n_sel] + score_t[n_sel:]
        cur = (s0 + lax.broadcasted_iota(jnp.int32, (n_sel, tq), 1)) // SEL_BLOCK
        sel_t = _select_blocks(score_t, cur, SEL_TOPK)
        sel = jnp.concatenate([sel_t, jnp.zeros((LANES - n_sel, tq), F32)], axis=0).T.astype(BF16)

        def bias_body(c, carry):
            c0 = pl.multiple_of(c * ck, ck)
            expanded = _dot(sel, e_ref[:, pl.ds(c0, ck)])
            key = c0 + lax.broadcasted_iota(jnp.int32, (tq, ck), 1)
            bias_ref[:, pl.ds(c0, ck)] = jnp.where((expanded > 0.5) & (key <= row_t), 0.0, NEG_INF)
            return carry

        lax.fori_loop(0, (s0 + tq + ck - 1) // ck, bias_body, 0)

        _softmax_init(m_ref, l_ref, acc_ref)

        def slc_body(i, carry):
            k0 = pl.multiple_of(i * tk, tk)
            k = ks_ref[pl.ds(k0, tk), hkv * HEAD_DIM:(hkv + 1) * HEAD_DIM]
            v = vs_ref[pl.ds(k0, tk), hkv * HEAD_DIM:(hkv + 1) * HEAD_DIM]
            s = _dot_nt(q, k) * SCALE
            key = k0 + lax.broadcasted_iota(jnp.int32, (tq, tk), 1)
            dist = (row_t - key).astype(F32)
            bias = bias_ref[:, pl.ds(k0, tk)]
            logits = jnp.concatenate(
                [s[g * tq:(g + 1) * tq] + (bias - slopes[g] * dist) for g in range(NSA_GROUP)], axis=0)
            _softmax_update(logits, v, m_ref, l_ref, acc_ref)
            return carry

        lax.fori_loop(0, (s0 + tq + tk - 1) // tk, slc_body, 0)
        mix_ref[...] = mix_ref[...] + gate_col(1) * _softmax_result(l_ref, acc_ref)

        _softmax_init(m_ref, l_ref, acc_ref)

        def win_body(w, carry):
            k0 = pl.multiple_of(s0 - WINDOW + w * tw, tw)
            k = kw_ref[pl.ds(k0, tw), hkv * HEAD_DIM:(hkv + 1) * HEAD_DIM]
            v = vw_ref[pl.ds(k0, tw), hkv * HEAD_DIM:(hkv + 1) * HEAD_DIM]
            s = _dot_nt(q, k) * SCALE
            disti = row_t - (k0 + lax.broadcasted_iota(jnp.int32, (tq, tw), 1))
            valid = (disti >= 0) & (disti < WINDOW)
            dist = disti.astype(F32)
            logits = jnp.concatenate(
                [jnp.where(valid, s[g * tq:(g + 1) * tq] - slopes[g] * dist, NEG_INF) for g in range(NSA_GROUP)],
                axis=0)
            _softmax_update(logits, v, m_ref, l_ref, acc_ref)
            return carry

        n_win_tiles = (WINDOW + tq) // tw
        lax.fori_loop(jnp.maximum(0, (WINDOW - s0) // tw), n_win_tiles, win_body, 0)
        mix = mix_ref[...] + gate_col(2) * _softmax_result(l_ref, acc_ref)

        for g, h in enumerate(heads):
            z = z_ref[:, h * HEAD_DIM:(h + 1) * HEAD_DIM]
            o_ref[:, h * HEAD_DIM:(h + 1) * HEAD_DIM] = (mix[g * tq:(g + 1) * tq] * _silu(z)).astype(o_ref.dtype)


def _nsa_prompt(proj, kcv, expand, batch, seq):
    tq, tk, tw, ck = 128, 256, 128, 512
    assert seq % ck == 0 and WINDOW % tw == 0 and tq % tw == 0 and seq // SEL_BLOCK <= LANES
    nq = seq // tq
    rows = NSA_GROUP * tq
    kernel = functools.partial(_nsa_prompt_kernel, tq=tq, tk=tk, tw=tw, ck=ck, seq=seq)
    cmp_spec = lambda kind: pl.BlockSpec((1, 1, NSA_KV_HEADS, seq // CMP_BLOCK, HEAD_DIM),
                                         lambda b, i: (b, kind, 0, 0, 0))
    return pl.pallas_call(
        kernel,
        grid=(batch, nq),
        in_specs=[pl.BlockSpec((tq, NSA_WIDTH), lambda b, i: (b * nq + i, 0)),
                  cmp_spec(0), cmp_spec(1),
                  pl.BlockSpec((seq, KV_WIDTH), lambda b, i: (b, 2)),
                  pl.BlockSpec((seq, KV_WIDTH), lambda b, i: (b, 3)),
                  pl.BlockSpec((seq, KV_WIDTH), lambda b, i: (b, 0)),
                  pl.BlockSpec((seq, KV_WIDTH), lambda b, i: (b, 1)),
                  pl.BlockSpec((tq, LANES), lambda b, i: (b * nq + i, 0)),
                  pl.BlockSpec((tq, NSA_WIDTH), lambda b, i: (b * nq + i, 0)),
                  pl.BlockSpec((LANES, seq), lambda b, i: (0, 0))],
        out_specs=pl.BlockSpec((tq, NSA_WIDTH), lambda b, i: (b * nq + i, 0)),
        out_shape=jax.ShapeDtypeStruct((batch * seq, NSA_WIDTH), BF16),
        scratch_shapes=[pltpu.VMEM((rows, 1), F32), pltpu.VMEM((rows, 1), F32),
                        pltpu.VMEM((rows, HEAD_DIM), F32), pltpu.VMEM((rows, HEAD_DIM), F32),
                        pltpu.VMEM((tq, seq), F32)],
        compiler_params=_params(("parallel", "arbitrary")),
        name="nsa_prompt",
    )(proj["q_n"], kcv, kcv, proj["kv4_bf"], proj["kv4_bf"], proj["kvwin_bf"], proj["kvwin_bf"],
      proj["small"], proj["z_n"], expand)


def _fox_prompt_kernel(q_ref, k_ref, v_ref, d_ref, z_ref, o_ref, m_ref, l_ref, acc_ref, *, t):
    qi = pl.program_id(2)
    q = q_ref[...]
    _softmax_init(m_ref, l_ref, acc_ref)

    def tile(i, causal):
        k0 = pl.multiple_of(i * t, t)
        s = _dot_nt(q, k_ref[pl.ds(k0, t), :]) * SCALE - d_ref[0, :, pl.ds(k0, t)]
        if causal:
            row = lax.broadcasted_iota(jnp.int32, (t, t), 0)
            col = lax.broadcasted_iota(jnp.int32, (t, t), 1)
            s = jnp.where(col <= row, s, NEG_INF)
        _softmax_update(s, v_ref[pl.ds(k0, t), :], m_ref, l_ref, acc_ref)

    def body(i, carry):
        tile(i, False)
        return carry

    lax.fori_loop(0, qi, body, 0)
    tile(qi, True)
    o_ref[...] = (_softmax_result(l_ref, acc_ref) * _silu(z_ref[...])).astype(o_ref.dtype)


def _fox_prompt(proj, d_rows, batch, seq):
    t = 512
    nq = seq // t
    return pl.pallas_call(
        functools.partial(_fox_prompt_kernel, t=t),
        grid=(batch, FOX_HEADS, nq),
        in_specs=[pl.BlockSpec((t, HEAD_DIM), lambda b, h, i: (b * nq + i, h)),
                  pl.BlockSpec((seq, HEAD_DIM), lambda b, h, i: (b, h)),
                  pl.BlockSpec((seq, HEAD_DIM), lambda b, h, i: (b, FOX_HEADS + h)),
                  pl.BlockSpec((1, 1, seq), lambda b, h, i: (b * FOX_HEADS + h, 0, 0)),
                  pl.BlockSpec((t, HEAD_DIM), lambda b, h, i: (b * nq + i, h))],
        out_specs=pl.BlockSpec((t, HEAD_DIM), lambda b, h, i: (b * nq + i, h)),
        out_shape=jax.ShapeDtypeStruct((batch * seq, FOX_WIDTH), BF16),
        scratch_shapes=[pltpu.VMEM((t, 1), F32), pltpu.VMEM((t, 1), F32), pltpu.VMEM((t, HEAD_DIM), F32)],
        compiler_params=_params(("parallel", "parallel", "arbitrary")),
        name="fox_prompt",
    )(proj["q_f"], proj["kv_f_bf"], proj["kv_f_bf"], d_rows, proj["z_f"])


def _prompt_layer(x, norm_in, wts, cmp_pe, cmp_w_bf, w_out_bf, norm_final):
    batch, seq, d = x.shape
    x2d = x.reshape(batch * seq, d)
    proj = _project_all(x2d, norm_in, wts, 1024)
    n_gate = 3 * NSA_HEADS
    logf = proj["small"][:, n_gate:n_gate + FOX_HEADS].reshape(batch, seq, FOX_HEADS)
    d_rows = _cumsum_rows(jnp.swapaxes(logf, 1, 2).reshape(batch * FOX_HEADS, seq))
    kcv = _compress_prompt(proj["kv4"], cmp_pe, cmp_w_bf, batch, seq)
    blk_of_key = jnp.arange(seq, dtype=jnp.int32) // SEL_BLOCK
    expand = (jnp.arange(LANES, dtype=jnp.int32)[:, None] == blk_of_key[None, :]).astype(BF16)
    mix_n = _nsa_prompt(proj, kcv, expand, batch, seq)
    mix_f = _fox_prompt(proj, d_rows.reshape(batch * FOX_HEADS, 1, seq), batch, seq)
    y = _merge(x2d, mix_n, mix_f, w_out_bf, norm_final, 512).reshape(batch, seq, d)
    w_keep = min(WINDOW, seq)
    state = (proj["kv4"].reshape(batch, seq, 4, NSA_KV_HEADS, HEAD_DIM),
             proj["kv_f"].reshape(batch, seq, 2, FOX_HEADS, HEAD_DIM),
             logf,
             proj["kvwin"].reshape(batch, seq, 2, NSA_KV_HEADS, HEAD_DIM)[:, seq - w_keep:])
    return y, state


def _page_specs(pg, width, col_block):
    def spec(u):
        return pl.BlockSpec((1, PAGE_SIZE, width), lambda b, j, pt: (pt[b, j * pg + u], 0, col_block))
    return [spec(u) for u in range(pg)]


def _row_info(rows, n_cols):
    row = lax.broadcasted_iota(jnp.int32, (rows, n_cols), 0)
    return row % T_PAD, _head_slope(row // T_PAD)


def _nsa_cmp_kernel(pt_ref, *refs, pg, n_pages):
    page_refs = refs[:pg]
    q_ref, pe_ref, w_ref, oc_ref, sel_ref, stage_ref, xs_ref = refs[pg:]
    jg = pl.program_id(1)
    for u in range(pg):
        r0 = pl.multiple_of((jg * pg + u) * PAGE_SIZE, PAGE_SIZE)
        page = page_refs[u][0]
        for ch in range(2 * NSA_KV_HEADS):
            stage_ref[ch, pl.ds(r0, PAGE_SIZE), :] = page[:, ch * HEAD_DIM:(ch + 1) * HEAD_DIM]

    @pl.when(jg == pl.num_programs(1) - 1)
    def _():
        past = n_pages * PAGE_SIZE
        nb = past // CMP_BLOCK
        half = nb // 2
        cmp = [[_compress_rows(stage_ref.at[kind * NSA_KV_HEADS + h], pe_ref, kind, w_ref[kind], xs_ref, nb)
                .astype(BF16) for h in range(NSA_KV_HEADS)] for kind in range(2)]
        rows = NSA_GROUP * T_PAD
        for hkv in range(NSA_KV_HEADS):
            q = q_ref[0, hkv * rows:(hkv + 1) * rows, :].astype(BF16)
            kc, vc = cmp[0][hkv], cmp[1][hkv]
            s = _dot_nt(q, kc) * SCALE
            lane = lax.broadcasted_iota(jnp.int32, (rows, nb), 1)
            blk = jnp.where(lane < half, 2 * lane, 2 * lane - (nb - 1))
            row = lax.broadcasted_iota(jnp.int32, (rows, nb), 0)
            slope = _head_slope(hkv * NSA_GROUP + row // T_PAD)
            dist = (past + row % T_PAD - ((blk + 1) * CMP_BLOCK - 1)).astype(F32)
            lg = jnp.where(dist >= 0, s - slope * dist, NEG_INF)
            mx = jnp.max(lg, axis=1, keepdims=True)
            mx = jnp.where(mx == NEG_INF, 0.0, mx)
            p = jnp.exp(lg - mx)
            p = p / jnp.maximum(jnp.sum(p, axis=1, keepdims=True), 1e-30)
            oc_ref[0, hkv * rows:(hkv + 1) * rows, :] = _dot(p.astype(BF16), vc)
            score = p[0:T_PAD]
            for g in range(1, NSA_GROUP):
                score = score + p[g * T_PAD:(g + 1) * T_PAD]
            score = score[:, :half] + score[:, half:]
            blk_s = lax.broadcasted_iota(jnp.int32, (T_PAD, half), 1).astype(F32)
            forced = (blk_s == 0) | (blk_s == half - 1)
            s_left = jnp.where(forced, NEG_INF, score)
            picked = jnp.zeros((T_PAD, half), F32)
            for _ in range(SEL_TOPK - 3):
                best = jnp.max(s_left, axis=1, keepdims=True)
                first = jnp.min(jnp.where(s_left == best, blk_s, float(half)), axis=1, keepdims=True)
                hit = blk_s == first
                picked = jnp.where(hit, 1.0, picked)
                s_left = jnp.where(hit, NEG_INF, s_left)
            sel_ref[0, hkv * T_PAD:(hkv + 1) * T_PAD, :] = jnp.where(forced, 1.0, picked)


def _nsa_cmp_sample(cache, page_table, q_rows, cmp_pe, cmp_w_bf, pg):
    n_seq, n_pages = page_table.shape
    past = n_pages * PAGE_SIZE
    nb = past // CMP_BLOCK
    assert n_pages % pg == 0 and past // SEL_BLOCK >= SEL_TOPK
    rows = NSA_HEADS * T_PAD
    grid_spec = pltpu.PrefetchScalarGridSpec(
        num_scalar_prefetch=1,
        grid=(n_seq, n_pages // pg),
        in_specs=_page_specs(pg, 2 * KV_WIDTH, 0) + [
            pl.BlockSpec((1, rows, HEAD_DIM), lambda b, j, pt: (b, 0, 0)),
            pl.BlockSpec((2, CMP_BLOCK, HEAD_DIM), lambda b, j, pt: (0, 0, 0)),
            pl.BlockSpec((2, CMP_BLOCK * HEAD_DIM, HEAD_DIM), lambda b, j, pt: (0, 0, 0))],
        out_specs=[pl.BlockSpec((1, rows, HEAD_DIM), lambda b, j, pt: (b, 0, 0)),
                   pl.BlockSpec((1, NSA_KV_HEADS * T_PAD, past // SEL_BLOCK), lambda b, j, pt: (b, 0, 0))],
        scratch_shapes=[pltpu.VMEM((2 * NSA_KV_HEADS, past, HEAD_DIM), F32),
                        pltpu.VMEM((nb, CMP_BLOCK * HEAD_DIM), BF16)],
    )
    return pl.pallas_call(
        functools.partial(_nsa_cmp_kernel, pg=pg, n_pages=n_pages),
        grid_spec=grid_spec,
        out_shape=[jax.ShapeDtypeStruct((n_seq, rows, HEAD_DIM), F32),
                   jax.ShapeDtypeStruct((n_seq, NSA_KV_HEADS * T_PAD, past // SEL_BLOCK), F32)],
        compiler_params=_params(("parallel", "arbitrary")),
        name="nsa_sample_cmp",
    )(page_table, *([cache] * pg), q_rows, cmp_pe, cmp_w_bf)


def _nsa_slc_kernel(pt_ref, *refs, pg, n_pages, w_buf, n_new):
    page_refs = refs[:pg]
    (q_ref, sel_ref, e_ref, new_ref, win_ref, oc_ref, g_ref, z_ref, o_ref,
     qbd_ref, tab_ref, m_ref, l_ref, acc_ref) = refs[pg:]
    jg = pl.program_id(1)
    past = n_pages * PAGE_SIZE
    rows = NSA_HEADS * T_PAD
    half_rows = rows // NSA_KV_HEADS

    def diag(x):
        return jnp.concatenate([x[h * half_rows:(h + 1) * half_rows, h * HEAD_DIM:(h + 1) * HEAD_DIM]
                                for h in range(NSA_KV_HEADS)], axis=0)

    @pl.when(jg == 0)
    def _():
        q = q_ref[0].astype(BF16)
        zero = jnp.zeros((half_rows, HEAD_DIM), BF16)
        for h in range(NSA_KV_HEADS):
            for c in range(NSA_KV_HEADS):
                qbd_ref[h * half_rows:(h + 1) * half_rows, c * HEAD_DIM:(c + 1) * HEAD_DIM] = (
                    q[h * half_rows:(h + 1) * half_rows] if c == h else zero)
        expanded = _dot(sel_ref[0].astype(BF16), e_ref[...])
        key = lax.broadcasted_iota(jnp.int32, (T_PAD, past), 1)
        t_q = lax.broadcasted_iota(jnp.int32, (T_PAD, past), 0)
        dist = (past + t_q - key).astype(F32)
        for head in range(NSA_HEADS):
            hkv = head // NSA_GROUP
            mask = jnp.where(expanded[hkv * T_PAD:(hkv + 1) * T_PAD] > 0.5, 0.0, NEG_INF)
            tab_ref[head * T_PAD:(head + 1) * T_PAD, :] = mask - _head_slope(head) * dist
        _softmax_init(m_ref, l_ref, acc_ref)

    qbd = qbd_ref[...]
    for u in range(pg):
        c0 = pl.multiple_of((jg * pg + u) * PAGE_SIZE, PAGE_SIZE)
        page = page_refs[u][0]
        logits = _dot_nt(qbd, page[:, :KV_WIDTH].astype(BF16)) * SCALE + tab_ref[:, pl.ds(c0, PAGE_SIZE)]
        _softmax_update(logits, page[:, KV_WIDTH:].astype(BF16), m_ref, l_ref, acc_ref)

    @pl.when(jg == pl.num_programs(1) - 1)
    def _():
        new = new_ref[0]
        t_q, slope = _row_info(rows, PAGE_SIZE)
        j_new = lax.broadcasted_iota(jnp.int32, (rows, PAGE_SIZE), 1)
        valid_new = (j_new <= t_q) & (j_new < n_new)
        bias_new = jnp.where(valid_new, -slope * (t_q - j_new).astype(F32), NEG_INF)
        logits = _dot_nt(qbd, new[:, :KV_WIDTH].astype(BF16)) * SCALE + bias_new
        _softmax_update(logits, new[:, KV_WIDTH:2 * KV_WIDTH].astype(BF16), m_ref, l_ref, acc_ref)
        o_s = diag(_softmax_result(l_ref, acc_ref))

        _softmax_init(m_ref, l_ref, acc_ref)
        win = win_ref[0]
        t_w, slope_w = _row_info(rows, w_buf)
        i_w = lax.broadcasted_iota(jnp.int32, (rows, w_buf), 1)
        dist_w = t_w + w_buf - i_w
        valid_w = (dist_w >= 0) & (dist_w < WINDOW) & (past - w_buf + i_w >= 0)
        logits = jnp.where(valid_w, _dot_nt(qbd, win[:, :KV_WIDTH].astype(BF16)) * SCALE
                           - slope_w * dist_w.astype(F32), NEG_INF)
        _softmax_update(logits, win[:, KV_WIDTH:].astype(BF16), m_ref, l_ref, acc_ref)
        logits = _dot_nt(qbd, new[:, 2 * KV_WIDTH:3 * KV_WIDTH].astype(BF16)) * SCALE + bias_new
        _softmax_update(logits, new[:, 3 * KV_WIDTH:].astype(BF16), m_ref, l_ref, acc_ref)
        o_w = diag(_softmax_result(l_ref, acc_ref))

        gates = g_ref[0]
        mix = gates[:, 0:1] * oc_ref[0] + gates[:, 1:2] * o_s + gates[:, 2:3] * o_w
        z = z_ref[0]
        for head in range(NSA_HEADS):
            cols = slice(head * HEAD_DIM, (head + 1) * HEAD_DIM)
            o_ref[0, :, cols] = (mix[head * T_PAD:(head + 1) * T_PAD] * _silu(z[:, cols])).astype(o_ref.dtype)


def _nsa_slc_sample(cache, page_table, q_rows, sel, expand, new_rows, win_state, o_c, gates, z_pad, pg, n_new):
    n_seq, n_pages = page_table.shape
    past = n_pages * PAGE_SIZE
    w_buf = win_state.shape[1]
    rows = NSA_HEADS * T_PAD
    per_seq = lambda shape: pl.BlockSpec((1,) + shape, lambda b, j, pt: (b, 0, 0))
    grid_spec = pltpu.PrefetchScalarGridSpec(
        num_scalar_prefetch=1,
        grid=(n_seq, n_pages // pg),
        in_specs=_page_specs(pg, 2 * KV_WIDTH, 1) + [
            per_seq((rows, HEAD_DIM)),
            per_seq((NSA_KV_HEADS * T_PAD, past // SEL_BLOCK)),
            pl.BlockSpec((past // SEL_BLOCK, past), lambda b, j, pt: (0, 0)),
            per_seq((PAGE_SIZE, 4 * KV_WIDTH)),
            per_seq((w_buf, 2 * KV_WIDTH)),
            per_seq((rows, HEAD_DIM)),
            per_seq((rows, 3)),
            per_seq((T_PAD, NSA_WIDTH))],
        out_specs=per_seq((T_PAD, NSA_WIDTH)),
        scratch_shapes=[pltpu.VMEM((rows, KV_WIDTH), BF16), pltpu.VMEM((rows, past), F32),
                        pltpu.VMEM((rows, 1), F32), pltpu.VMEM((rows, 1), F32),
                        pltpu.VMEM((rows, KV_WIDTH), F32)],
    )
    return pl.pallas_call(
        functools.partial(_nsa_slc_kernel, pg=pg, n_pages=n_pages, w_buf=w_buf, n_new=n_new),
        grid_spec=grid_spec,
        out_shape=jax.ShapeDtypeStruct((n_seq, T_PAD, NSA_WIDTH), F32),
        compiler_params=_params(("parallel", "arbitrary")),
        name="nsa_sample_slc_win",
    )(page_table, *([cache] * pg), q_rows, sel, expand, new_rows, win_state, o_c, gates, z_pad)


def _fox_sample_kernel(pt_ref, *refs, pg, n_new):
    page_refs = refs[:pg]
    (q_ref, lc_ref, new_ref, ln_ref, z_ref, o_ref, qbd_ref, m_ref, l_ref, acc_ref, carry_ref) = refs[pg:]
    b = pl.program_id(0)
    jg = pl.program_id(1)
    rows = FOX_HEADS * T_PAD
    col_head = lax.broadcasted_iota(jnp.int32, (FOX_HEADS, FOX_WIDTH), 1) // HEAD_DIM
    row_head = lax.broadcasted_iota(jnp.int32, (FOX_HEADS, FOX_WIDTH), 0)
    own = col_head == row_head

    @pl.when(jg == 0)
    def _():
        q = q_ref[0]
        blocks = []
        for t in range(T_PAD):
            wide = jnp.concatenate([q[t * FOX_HEADS:(t + 1) * FOX_HEADS]] * FOX_HEADS, axis=1)
            blocks.append(jnp.where(own, wide, 0.0))
        qbd_ref[...] = jnp.concatenate(blocks, axis=0).astype(BF16)
        _softmax_init(m_ref, l_ref, acc_ref)
        carry_ref[...] = jnp.zeros(carry_ref.shape, F32)

    qbd = qbd_ref[...]

    def forget_rows(logf_tile):
        d = _lane_cumsum(logf_tile) + carry_ref[...]
        carry_ref[...] = d[:, PAGE_SIZE - 1:PAGE_SIZE]
        return jnp.concatenate([d] * T_PAD, axis=0)

    for u in range(pg):
        page = page_refs[u][0]
        d_rows = forget_rows(lc_ref[pt_ref[b, jg * pg + u]])
        logits = _dot_nt(qbd, page[:, :FOX_WIDTH].astype(BF16)) * SCALE - d_rows
        _softmax_update(logits, page[:, FOX_WIDTH:].astype(BF16), m_ref, l_ref, acc_ref)

    @pl.when(jg == pl.num_programs(1) - 1)
    def _():
        new = new_ref[0]
        d_rows = forget_rows(ln_ref[0])
        t_q = lax.broadcasted_iota(jnp.int32, (rows, PAGE_SIZE), 0) // FOX_HEADS
        j_new = lax.broadcasted_iota(jnp.int32, (rows, PAGE_SIZE), 1)
        valid = (j_new <= t_q) & (j_new < n_new)
        logits = jnp.where(valid, _dot_nt(qbd, new[:, :FOX_WIDTH].astype(BF16)) * SCALE - d_rows, NEG_INF)
        _softmax_update(logits, new[:, FOX_WIDTH:].astype(BF16), m_ref, l_ref, acc_ref)
        o = _softmax_result(l_ref, acc_ref)
        o = jnp.where(own[None], o.reshape(T_PAD, FOX_HEADS, FOX_WIDTH), 0.0)
        o_ref[0] = (jnp.sum(o, axis=1) * _silu(z_ref[0])).astype(o_ref.dtype)


def _fox_sample(cache, page_table, q_rows, logf_cache_t, new_rows, logf_new_t, z_pad, pg, n_new):
    n_seq, n_pages = page_table.shape
    n_pool = logf_cache_t.shape[0]
    rows = FOX_HEADS * T_PAD
    per_seq = lambda shape: pl.BlockSpec((1,) + shape, lambda b, j, pt: (b, 0, 0))
    grid_spec = pltpu.PrefetchScalarGridSpec(
        num_scalar_prefetch=1,
        grid=(n_seq, n_pages // pg),
        in_specs=_page_specs(pg, 2 * FOX_WIDTH, 0) + [
            per_seq((rows, HEAD_DIM)),
            pl.BlockSpec((n_pool, FOX_HEADS, PAGE_SIZE), lambda b, j, pt: (0, 0, 0)),
            per_seq((PAGE_SIZE, 2 * FOX_WIDTH)),
            per_seq((FOX_HEADS, PAGE_SIZE)),
            per_seq((T_PAD, FOX_WIDTH))],
        out_specs=per_seq((T_PAD, FOX_WIDTH)),
        scratch_shapes=[pltpu.VMEM((rows, FOX_WIDTH), BF16), pltpu.VMEM((rows, 1), F32),
                        pltpu.VMEM((rows, 1), F32), pltpu.VMEM((rows, FOX_WIDTH), F32),
                        pltpu.VMEM((FOX_HEADS, 1), F32)],
    )
    return pl.pallas_call(
        functools.partial(_fox_sample_kernel, pg=pg, n_new=n_new),
        grid_spec=grid_spec,
        out_shape=jax.ShapeDtypeStruct((n_seq, T_PAD, FOX_WIDTH), F32),
        compiler_params=_params(("parallel", "arbitrary")),
        name="fox_sample",
    )(page_table, *([cache] * pg), q_rows, logf_cache_t, new_rows, logf_new_t, z_pad)


def _pad_axis(a, axis, size):
    pad = [(0, 0)] * a.ndim
    pad[axis] = (0, size - a.shape[axis])
    return jnp.pad(a, pad)


def _sample_layer(x, cache_nsa_kv, cache_fox_kv, cache_fox_logf, state_nsa_win, page_table,
                  norm_in, wts, cmp_pe, cmp_w_bf, w_out_bf, norm_final):
    n_seq, t_new, d = x.shape
    n_pool = cache_nsa_kv.shape[0]
    n_pages = page_table.shape[1]
    past = n_pages * PAGE_SIZE
    w_buf = state_nsa_win.shape[1]
    assert t_new <= T_PAD and t_new < SEL_BLOCK and past % SEL_BLOCK == 0
    x2d = x.reshape(n_seq * t_new, d)
    proj = _project_all(x2d, norm_in, wts, n_seq * t_new)
    n_gate = 3 * NSA_HEADS
    small = proj["small"].reshape(n_seq, t_new, LANES)
    logf = small[:, :, n_gate:n_gate + FOX_HEADS]

    def head_rows(a, n_heads, head_major):
        a = _pad_axis(a.reshape(n_seq, t_new, n_heads, -1), 1, T_PAD)
        if head_major:
            a = jnp.swapaxes(a, 1, 2)
        return a.reshape(n_seq, n_heads * T_PAD, a.shape[-1])

    q_n_rows = head_rows(proj["q_n"].astype(F32), NSA_HEADS, True)
    gates_rows = head_rows(small[:, :, :n_gate].reshape(n_seq * t_new, n_gate), NSA_HEADS, True)
    q_f_rows = head_rows(proj["q_f"].astype(F32), FOX_HEADS, False)
    z_n_pad = _pad_axis(proj["z_n"].reshape(n_seq, t_new, NSA_WIDTH), 1, T_PAD)
    z_f_pad = _pad_axis(proj["z_f"].reshape(n_seq, t_new, FOX_WIDTH), 1, T_PAD)
    nsa_new = jnp.concatenate([proj["kv4"].reshape(n_seq, t_new, 4 * KV_WIDTH)[:, :, 2 * KV_WIDTH:],
                               proj["kvwin"].reshape(n_seq, t_new, 2 * KV_WIDTH)], axis=2)
    nsa_new = _pad_axis(nsa_new, 1, PAGE_SIZE)
    fox_new = _pad_axis(proj["kv_f"].reshape(n_seq, t_new, 2 * FOX_WIDTH), 1, PAGE_SIZE)
    logf_new_t = _pad_axis(jnp.swapaxes(logf, 1, 2), 2, PAGE_SIZE)
    logf_cache_t = jnp.swapaxes(cache_fox_logf, 1, 2)
    blk_of_key = jnp.arange(past, dtype=jnp.int32) // SEL_BLOCK
    expand = (jnp.arange(past // SEL_BLOCK, dtype=jnp.int32)[:, None] == blk_of_key[None, :]).astype(BF16)

    nsa_cache = cache_nsa_kv.reshape(n_pool, PAGE_SIZE, 4 * KV_WIDTH)
    fox_cache = cache_fox_kv.reshape(n_pool, PAGE_SIZE, 2 * FOX_WIDTH)
    win_state = state_nsa_win.reshape(n_seq, w_buf, 2 * KV_WIDTH)
    pg = 4
    o_c, sel = _nsa_cmp_sample(nsa_cache, page_table, q_n_rows, cmp_pe, cmp_w_bf, pg)
    mix_n = _nsa_slc_sample(nsa_cache, page_table, q_n_rows, sel, expand, nsa_new, win_state, o_c,
                            gates_rows, z_n_pad, pg, t_new)
    mix_f = _fox_sample(fox_cache, page_table, q_f_rows, logf_cache_t, fox_new, logf_new_t, z_f_pad, pg, t_new)
    mix_n = mix_n[:, :t_new].reshape(n_seq * t_new, NSA_WIDTH).astype(BF16)
    mix_f = mix_f[:, :t_new].reshape(n_seq * t_new, FOX_WIDTH).astype(BF16)
    y = _merge(x2d, mix_n, mix_f, w_out_bf, norm_final, n_seq * t_new).reshape(n_seq, t_new, d)
    kvwin_new = proj["kvwin"].reshape(n_seq, t_new, 2, NSA_KV_HEADS, HEAD_DIM)
    state = (proj["kv4"].reshape(n_seq, t_new, 4, NSA_KV_HEADS, HEAD_DIM),
             proj["kv_f"].reshape(n_seq, t_new, 2, FOX_HEADS, HEAD_DIM),
             logf,
             jnp.concatenate([state_nsa_win, kvwin_new], axis=1)[:, t_new:])
    return y, state


def kernel(x_prompt, x_sample, cache_nsa_kv, cache_fox_kv, cache_fox_logf, state_nsa_win, page_table,
           norm_in, w_in, b_gate, b_forget, cmp_pe, cmp_w, w_out, norm_final):
    assert norm_in.shape[0] == 1, "single-layer trunk"
    wts = _split_weights(w_in[0], b_gate[0], b_forget[0])
    cmp_w_bf = cmp_w[0].reshape(2, CMP_BLOCK * HEAD_DIM, HEAD_DIM).astype(BF16)
    w_out_bf = w_out[0].astype(BF16)
    y_p, st_p = _prompt_layer(x_prompt, norm_in[0], wts, cmp_pe[0], cmp_w_bf, w_out_bf, norm_final)
    y_s, st_s = _sample_layer(x_sample, cache_nsa_kv[0], cache_fox_kv[0], cache_fox_logf[0], state_nsa_win[0],
                              page_table, norm_in[0], wts, cmp_pe[0], cmp_w_bf, w_out_bf, norm_final)
    outs = [y_p, y_s]
    for s_p, s_s in zip(st_p, st_s):
        outs.extend([s_p[None], s_s[None]])
    return tuple(outs)
```

```python
import functools

import jax
import jax.numpy as jnp
from jax import lax
from jax.experimental import pallas as pl
from jax.experimental.pallas import tpu as pltpu

F32 = jnp.float32
BF16 = jnp.bfloat16
NEG_INF = float("-inf")

HEAD_DIM = 128
NSA_HEADS = 8
FOX_HEADS = 8
NSA_KV_HEADS = 2
NSA_GROUP = NSA_HEADS // NSA_KV_HEADS
NSA_WIDTH = NSA_HEADS * HEAD_DIM
FOX_WIDTH = FOX_HEADS * HEAD_DIM
KV_WIDTH = NSA_KV_HEADS * HEAD_DIM
NSA_CHANNELS = 4 * NSA_KV_HEADS
CMP_BLOCK = 32
SEL_BLOCK = 64
SEL_TOPK = 16
WINDOW = 512
PAGE_SIZE = 128
FORCED_SCORE = 1e4
RMS_EPS = 1e-6
SCALE = HEAD_DIM ** -0.5
T_PAD = 8
LANES = 128
SUBLANES = 8
STAGE_PITCH = 40
VMEM_LIMIT = 56 * 1024 * 1024


def _params(sem):
    return pltpu.CompilerParams(dimension_semantics=sem, vmem_limit_bytes=VMEM_LIMIT)


def _dot(a, b):
    return jnp.dot(a, b, preferred_element_type=F32)


def _dot_nt(a, b):
    return lax.dot_general(a, b, (((1,), (1,)), ((), ())), preferred_element_type=F32)


def _sigmoid(x):
    return 1.0 / (1.0 + jnp.exp(-x))


def _silu(x):
    return x * _sigmoid(x)


def _tile_lanes(x, n):
    return x if n == 1 else jnp.concatenate([x] * n, axis=1)


def _with_ones(v):
    return jnp.concatenate([v, jnp.ones(v.shape, v.dtype)], axis=1)


def _state_init(m_ref, acc_ref):
    m_ref[...] = jnp.full(m_ref.shape, NEG_INF, F32)
    acc_ref[...] = jnp.zeros(acc_ref.shape, F32)


def _probs(lg, m_old, may_be_empty):
    m_new = jnp.maximum(m_old, jnp.max(lg, axis=1, keepdims=True))
    m_use = jnp.where(m_new == NEG_INF, 0.0, m_new) if may_be_empty else m_new
    alpha = jnp.exp(m_old - m_use)
    p = jnp.exp(lg - _tile_lanes(m_use, lg.shape[1] // LANES)).astype(BF16)
    return m_new, alpha, p


def _state_update_chunked(logits_of, n_chunks, chunk, v_ext, m_ref, acc_ref, p_ref, alpha_ref, may_be_empty):
    for c in range(n_chunks):
        rows = slice(c * chunk, (c + 1) * chunk)
        m_new, alpha, p = _probs(logits_of(c), m_ref[rows, :], may_be_empty)
        alpha_ref[rows, :] = alpha
        p_ref[rows, :] = p
        m_ref[rows, :] = m_new
    acc_ref[...] = _tile_lanes(alpha_ref[...], 2) * acc_ref[...] + _dot(p_ref[...], v_ext)


def _state_update(lg, pv_of, m_ref, acc_ref, may_be_empty):
    m_new, alpha, p = _probs(lg, m_ref[...], may_be_empty)
    acc_ref[...] = _tile_lanes(alpha, 2) * acc_ref[...] + pv_of(p)
    m_ref[...] = m_new


def _state_result(acc_ref):
    acc = acc_ref[...]
    return acc[:, :HEAD_DIM] / jnp.maximum(acc[:, HEAD_DIM:], 1e-30)


def _lane_cumsum(x):
    n = x.shape[-1]
    lane = lax.broadcasted_iota(jnp.int32, x.shape, x.ndim - 1)
    s = 1
    while s < n:
        x = x + jnp.where(lane >= s, pltpu.roll(x, s, axis=x.ndim - 1), 0.0)
        s *= 2
    return x


def _head_slope(head):
    if isinstance(head, int):
        return 2.0 ** -(head + 1)
    return lax.bitcast_convert_type((126 - head) << 23, F32)


def _rms_kernel(x_ref, g_ref, o_ref):
    x = x_ref[...]
    ms = jnp.mean(x * x, axis=-1, keepdims=True)
    o_ref[...] = (x * lax.rsqrt(ms + RMS_EPS) * g_ref[...]).astype(o_ref.dtype)


def _rmsnorm(x2d, g, out_dtype, tm):
    m, d = x2d.shape
    return pl.pallas_call(
        _rms_kernel,
        grid=(m // tm,),
        in_specs=[pl.BlockSpec((tm, d), lambda i: (i, 0)), pl.BlockSpec((1, d), lambda i: (0, 0))],
        out_specs=pl.BlockSpec((tm, d), lambda i: (i, 0)),
        out_shape=jax.ShapeDtypeStruct((m, d), out_dtype),
        compiler_params=_params(("parallel",)),
        name="rmsnorm",
    )(x2d, g.reshape(1, d))


def _proj_kernel(h_ref, w_ref, *o_refs):
    acc = _dot(h_ref[...], w_ref[...])
    for o_ref in o_refs:
        o_ref[...] = acc.astype(o_ref.dtype)


def _project(h, w, out_dtypes, tm, tn, name):
    m, d = h.shape
    n = w.shape[1]
    tn = min(tn, n)
    return pl.pallas_call(
        _proj_kernel,
        grid=(m // tm, n // tn),
        in_specs=[pl.BlockSpec((tm, d), lambda i, j: (i, 0)), pl.BlockSpec((d, tn), lambda i, j: (0, j))],
        out_specs=[pl.BlockSpec((tm, tn), lambda i, j: (i, j)) for _ in out_dtypes],
        out_shape=[jax.ShapeDtypeStruct((m, n), dt) for dt in out_dtypes],
        compiler_params=_params(("parallel", "arbitrary")),
        name=name,
    )(h, w)


def _small_kernel(h_ref, w_ref, b_ref, o_ref, *, n_gate):
    z = _dot(h_ref[...], w_ref[...]) + b_ref[...]
    lane = lax.broadcasted_iota(jnp.int32, z.shape, 1)
    log_sig = jnp.minimum(z, 0.0) - jnp.log1p(jnp.exp(-jnp.abs(z)))
    o_ref[...] = jnp.where(lane < n_gate, _sigmoid(z), log_sig)


def _project_small(h, w, b, tm, n_gate):
    m, d = h.shape
    return pl.pallas_call(
        functools.partial(_small_kernel, n_gate=n_gate),
        grid=(m // tm,),
        in_specs=[pl.BlockSpec((tm, d), lambda i: (i, 0)), pl.BlockSpec((d, LANES), lambda i: (0, 0)),
                  pl.BlockSpec((1, LANES), lambda i: (0, 0))],
        out_specs=pl.BlockSpec((tm, LANES), lambda i: (i, 0)),
        out_shape=jax.ShapeDtypeStruct((m, LANES), F32),
        compiler_params=_params(("parallel",)),
        name="proj_gates_logf",
    )(h, w, b)


def _split_weights(w_in, b_gate, b_forget):
    cuts = [NSA_WIDTH, 6 * KV_WIDTH, 3 * NSA_HEADS, NSA_WIDTH, FOX_WIDTH, FOX_WIDTH, FOX_WIDTH, FOX_HEADS, FOX_WIDTH]
    offs = [0]
    for c in cuts:
        offs.append(offs[-1] + c)
    col = lambda a, b: w_in[:, a:b].astype(BF16)
    n_small = 3 * NSA_HEADS + FOX_HEADS
    w_small = jnp.concatenate([w_in[:, offs[2]:offs[3]], w_in[:, offs[7]:offs[8]]], axis=1)
    w_small = jnp.pad(w_small, ((0, 0), (0, LANES - n_small))).astype(BF16)
    b_small = jnp.pad(jnp.concatenate([b_gate, b_forget]), (0, LANES - n_small)).reshape(1, LANES).astype(F32)
    return {
        "q_n": col(offs[0], offs[1]),
        "kv4": col(offs[1], offs[1] + 4 * KV_WIDTH),
        "kvwin": col(offs[1] + 4 * KV_WIDTH, offs[2]),
        "z_n": col(offs[3], offs[4]),
        "q_f": col(offs[4], offs[5]),
        "kv_f": col(offs[5], offs[7]),
        "z_f": col(offs[8], offs[9]),
        "small": w_small,
        "b_small": b_small,
    }


def _project_all(x2d, norm_in, wts, tm):
    h = _rmsnorm(x2d, norm_in, BF16, min(tm, 512))
    tn = 512
    out = {}
    (out["q_n"],) = _project(h, wts["q_n"], [BF16], tm, tn, "proj_q_n")
    out["kv4"], out["kv4_bf"] = _project(h, wts["kv4"], [F32, BF16], tm, tn, "proj_kv4")
    out["kvwin"], out["kvwin_bf"] = _project(h, wts["kvwin"], [F32, BF16], tm, tn, "proj_kvwin")
    (out["z_n"],) = _project(h, wts["z_n"], [F32], tm, tn, "proj_z_n")
    (out["q_f"],) = _project(h, wts["q_f"], [BF16], tm, tn, "proj_q_f")
    out["kv_f"], out["kv_f_bf"] = _project(h, wts["kv_f"], [F32, BF16], tm, tn, "proj_kv_f")
    (out["z_f"],) = _project(h, wts["z_f"], [F32], tm, tn, "proj_z_f")
    out["small"] = _project_small(h, wts["small"], wts["b_small"], min(tm, 512), 3 * NSA_HEADS)
    return out


def _merge_kernel(x_ref, mn_ref, mf_ref, wn_ref, wf_ref, g_ref, o_ref):
    y = x_ref[...] + _dot(mn_ref[...], wn_ref[...]) + _dot(mf_ref[...], wf_ref[...])
    ms = jnp.mean(y * y, axis=-1, keepdims=True)
    o_ref[...] = y * lax.rsqrt(ms + RMS_EPS) * g_ref[...]


def _merge(x2d, mix_n, mix_f, w_out_bf, norm_final, tm):
    m, d = x2d.shape
    return pl.pallas_call(
        _merge_kernel,
        grid=(m // tm,),
        in_specs=[pl.BlockSpec((tm, d), lambda i: (i, 0)),
                  pl.BlockSpec((tm, NSA_WIDTH), lambda i: (i, 0)),
                  pl.BlockSpec((tm, FOX_WIDTH), lambda i: (i, 0)),
                  pl.BlockSpec((NSA_WIDTH, d), lambda i: (0, 0)),
                  pl.BlockSpec((FOX_WIDTH, d), lambda i: (NSA_WIDTH // FOX_WIDTH, 0)),
                  pl.BlockSpec((1, d), lambda i: (0, 0))],
        out_specs=pl.BlockSpec((tm, d), lambda i: (i, 0)),
        out_shape=jax.ShapeDtypeStruct((m, d), F32),
        compiler_params=_params(("parallel",)),
        name="merge_out_proj",
    )(x2d, mix_n, mix_f, w_out_bf, w_out_bf, norm_final.reshape(1, d))


def _cumsum_kernel(x_ref, o_ref):
    o_ref[...] = _lane_cumsum(x_ref[...])


def _cumsum_rows(x):
    return pl.pallas_call(
        _cumsum_kernel,
        out_shape=jax.ShapeDtypeStruct(x.shape, F32),
        name="forget_cumsum",
    )(x)


def _compress_rows(x_ref, pe_ref, kind, w, xs_ref, nb, starts, stride):
    half = nb // 2
    for c in range(CMP_BLOCK):
        pe_c = pe_ref[kind, c:c + 1, :]
        for par in range(2):
            xc = x_ref[pl.ds(starts[par] + c, half, stride=stride), :]
            xs_ref[par * half:(par + 1) * half, c * HEAD_DIM:(c + 1) * HEAD_DIM] = (xc + pe_c).astype(BF16)
    return _dot(xs_ref[...], w)


def _compress_kernel(x_ref, pe_ref, w_ref, o_ref, xs_ref, *, nb):
    out = _compress_rows(x_ref, pe_ref, 0, w_ref[0], xs_ref, nb, (0, CMP_BLOCK), 2 * CMP_BLOCK)
    o_ref[0, 0, 0] = out.astype(o_ref.dtype)


def _compress_prompt(kv4, cmp_pe, cmp_w_bf, batch, seq):
    nb = seq // CMP_BLOCK
    return pl.pallas_call(
        functools.partial(_compress_kernel, nb=nb),
        grid=(batch, 2, NSA_KV_HEADS),
        in_specs=[pl.BlockSpec((seq, HEAD_DIM), lambda b, k, h: (b, k * NSA_KV_HEADS + h)),
                  pl.BlockSpec((1, CMP_BLOCK, HEAD_DIM), lambda b, k, h: (k, 0, 0)),
                  pl.BlockSpec((1, CMP_BLOCK * HEAD_DIM, HEAD_DIM), lambda b, k, h: (k, 0, 0))],
        out_specs=pl.BlockSpec((1, 1, 1, nb, HEAD_DIM), lambda b, k, h: (b, k, h, 0, 0)),
        out_shape=jax.ShapeDtypeStruct((batch, 2, NSA_KV_HEADS, nb, HEAD_DIM), BF16),
        scratch_shapes=[pltpu.VMEM((nb, CMP_BLOCK * HEAD_DIM), BF16)],
        compiler_params=_params(("parallel", "parallel", "arbitrary")),
        name="compress_prompt",
    )(kv4, cmp_pe, cmp_w_bf)


def _select_blocks(score_t, cur, n_keep):
    n_blk = score_t.shape[0]
    j = lax.broadcasted_iota(jnp.int32, score_t.shape, 0)
    forced = (j == 0) | (j == cur) | (j == cur - 1)
    s = jnp.where(j <= cur, jnp.where(forced, FORCED_SCORE, score_t), -1.0)
    cnt = jnp.zeros(s.shape, F32)
    for jj in range(n_blk):
        row = s[jj:jj + 1, :]
        cnt = cnt + jnp.where(row > s, 1.0, jnp.where((row == s) & (j > jj), 1.0, 0.0))
    return jnp.where((cnt < n_keep) & (j <= cur), 1.0, 0.0)


def _nsa_prompt_kernel(q_ref, kc_ref, vc_ref, ks_ref, vs_ref, kw_ref, vw_ref, g_ref, z_ref, e_ref, o_ref,
                       m_ref, acc_ref, alpha_ref, ps_ref, pw_ref, mix_ref, bias_ref, *, tq, tk, tw, ck, seq):
    qi = pl.program_id(1)
    s0 = qi * tq
    n_cmp = seq // CMP_BLOCK
    n_sel = seq // SEL_BLOCK
    row_t = s0 + lax.broadcasted_iota(jnp.int32, (tq, 1), 0)
    kv_heads = range(NSA_KV_HEADS)
    heads = lambda hkv: [hkv * NSA_GROUP + g for g in range(NSA_GROUP)]
    slopes = lambda hkv: [_head_slope(h) for h in heads(hkv)]
    cols = lambda hkv: slice(hkv * HEAD_DIM, (hkv + 1) * HEAD_DIM)

    def q_rows(hkv):
        return jnp.concatenate([q_ref[:, h * HEAD_DIM:(h + 1) * HEAD_DIM] for h in heads(hkv)], axis=0)

    def gate_col(hkv, br):
        return jnp.concatenate([g_ref[:, 3 * h + br:3 * h + br + 1] for h in heads(hkv)], axis=0)

    for hkv in kv_heads:
        sc = _dot_nt(q_rows(hkv), kc_ref[0, 0, hkv]) * SCALE
        lane = lax.broadcasted_iota(jnp.int32, (tq, n_cmp), 1)
        blk = jnp.where(lane < n_cmp // 2, 2 * lane, 2 * lane - (n_cmp - 1))
        dist_c = (row_t - ((blk + 1) * CMP_BLOCK - 1)).astype(F32)
        valid_c = dist_c >= 0
        score = jnp.zeros((tq, n_cmp), F32)
        probs = []
        for g, slope in enumerate(slopes(hkv)):
            lg = jnp.where(valid_c, sc[g * tq:(g + 1) * tq] - slope * dist_c, NEG_INF)
            mx = jnp.max(lg, axis=1, keepdims=True)
            mx = jnp.where(mx == NEG_INF, 0.0, mx)
            p = jnp.exp(lg - mx)
            p = p / jnp.maximum(jnp.sum(p, axis=1, keepdims=True), 1e-30)
            score = score + p
            probs.append(p)
        o_c = _dot(jnp.concatenate(probs, axis=0).astype(BF16), vc_ref[0, 0, hkv])
        mix_ref[hkv] = gate_col(hkv, 0) * o_c

        score_t = score.T
        score_t = score_t[:n_sel] + score_t[n_sel:]
        cur = (s0 + lax.broadcasted_iota(jnp.int32, (n_sel, tq), 1)) // SEL_BLOCK
        sel_t = _select_blocks(score_t, cur, SEL_TOPK)
        sel = jnp.concatenate([sel_t, jnp.zeros((LANES - n_sel, tq), F32)], axis=0).T.astype(BF16)

        def bias_body(c, carry, hkv=hkv, sel=sel):
            c0 = pl.multiple_of(c * ck, ck)
            expanded = _dot(sel, e_ref[:, pl.ds(c0, ck)])
            key = c0 + lax.broadcasted_iota(jnp.int32, (tq, ck), 1)
            bias_ref[hkv, :, pl.ds(c0, ck)] = jnp.where((expanded > 0.5) & (key <= row_t), 0.0, NEG_INF)
            return carry

        lax.fori_loop(0, (s0 + tq + ck - 1) // ck, bias_body, 0)

    def stream(hkv, s, bias, key_rel, v, p_ref, may_be_empty):
        sl = slopes(hkv)
        logits_of = lambda g: s[g * tq:(g + 1) * tq] * SCALE + (bias + sl[g] * key_rel)
        _state_update_chunked(logits_of, NSA_GROUP, tq, _with_ones(v), m_ref.at[hkv], acc_ref.at[hkv],
                              p_ref.at[hkv], alpha_ref.at[hkv], may_be_empty)

    _state_init(m_ref, acc_ref)

    def slc_body(i, carry):
        k0 = pl.multiple_of(i * tk, tk)
        key_rel = (k0 - s0 + lax.broadcasted_iota(jnp.int32, (1, tk), 1)).astype(F32)
        for hkv in kv_heads:
            s = _dot_nt(q_rows(hkv), ks_ref[pl.ds(k0, tk), cols(hkv)])
            stream(hkv, s, bias_ref[hkv, :, pl.ds(k0, tk)], key_rel, vs_ref[pl.ds(k0, tk), cols(hkv)], ps_ref, False)
        return carry

    lax.fori_loop(0, (s0 + tq + tk - 1) // tk, slc_body, 0)
    for hkv in kv_heads:
        mix_ref[hkv] = mix_ref[hkv] + gate_col(hkv, 1) * _state_result(acc_ref.at[hkv])

    _state_init(m_ref, acc_ref)

    def win_body(w, carry):
        k0 = pl.multiple_of(s0 - WINDOW + w * tw, tw)
        disti = row_t - (k0 + lax.broadcasted_iota(jnp.int32, (tq, tw), 1))
        bias = jnp.where((disti >= 0) & (disti < WINDOW), 0.0, NEG_INF)
        key_rel = (k0 - s0 + lax.broadcasted_iota(jnp.int32, (1, tw), 1)).astype(F32)
        for hkv in kv_heads:
            s = _dot_nt(q_rows(hkv), kw_ref[pl.ds(k0, tw), cols(hkv)])
            stream(hkv, s, bias, key_rel, vw_ref[pl.ds(k0, tw), cols(hkv)], pw_ref, True)
        return carry

    n_win_tiles = (WINDOW + tq) // tw
    lax.fori_loop(jnp.maximum(0, (WINDOW - s0) // tw), n_win_tiles, win_body, 0)

    for hkv in kv_heads:
        mix = mix_ref[hkv] + gate_col(hkv, 2) * _state_result(acc_ref.at[hkv])
        for g, h in enumerate(heads(hkv)):
            z = z_ref[:, h * HEAD_DIM:(h + 1) * HEAD_DIM]
            o_ref[:, h * HEAD_DIM:(h + 1) * HEAD_DIM] = (mix[g * tq:(g + 1) * tq] * _silu(z)).astype(o_ref.dtype)


def _nsa_prompt(proj, kcv, expand, batch, seq):
    tq, tk, tw, ck = 128, 256, 128, 512
    assert seq % ck == 0 and WINDOW % tw == 0 and tq % tw == 0 and seq // SEL_BLOCK <= LANES
    nq = seq // tq
    rows = NSA_GROUP * tq
    kernel = functools.partial(_nsa_prompt_kernel, tq=tq, tk=tk, tw=tw, ck=ck, seq=seq)
    cmp_spec = lambda kind: pl.BlockSpec((1, 1, NSA_KV_HEADS, seq // CMP_BLOCK, HEAD_DIM),
                                         lambda b, i: (b, kind, 0, 0, 0))
    return pl.pallas_call(
        kernel,
        grid=(batch, nq),
        in_specs=[pl.BlockSpec((tq, NSA_WIDTH), lambda b, i: (b * nq + i, 0)),
                  cmp_spec(0), cmp_spec(1),
                  pl.BlockSpec((seq, KV_WIDTH), lambda b, i: (b, 2)),
                  pl.BlockSpec((seq, KV_WIDTH), lambda b, i: (b, 3)),
                  pl.BlockSpec((seq, KV_WIDTH), lambda b, i: (b, 0)),
                  pl.BlockSpec((seq, KV_WIDTH), lambda b, i: (b, 1)),
                  pl.BlockSpec((tq, LANES), lambda b, i: (b * nq + i, 0)),
                  pl.BlockSpec((tq, NSA_WIDTH), lambda b, i: (b * nq + i, 0)),
                  pl.BlockSpec((LANES, seq), lambda b, i: (0, 0))],
        out_specs=pl.BlockSpec((tq, NSA_WIDTH), lambda b, i: (b * nq + i, 0)),
        out_shape=jax.ShapeDtypeStruct((batch * seq, NSA_WIDTH), BF16),
        scratch_shapes=[pltpu.VMEM((NSA_KV_HEADS, rows, LANES), F32),
                        pltpu.VMEM((NSA_KV_HEADS, rows, 2 * HEAD_DIM), F32),
                        pltpu.VMEM((NSA_KV_HEADS, rows, LANES), F32),
                        pltpu.VMEM((NSA_KV_HEADS, rows, tk), BF16),
                        pltpu.VMEM((NSA_KV_HEADS, rows, tw), BF16),
                        pltpu.VMEM((NSA_KV_HEADS, rows, HEAD_DIM), F32),
                        pltpu.VMEM((NSA_KV_HEADS, tq, seq), F32)],
        compiler_params=_params(("parallel", "arbitrary")),
        name="nsa_prompt",
    )(proj["q_n"], kcv, kcv, proj["kv4_bf"], proj["kv4_bf"], proj["kvwin_bf"], proj["kvwin_bf"],
      proj["small"], proj["z_n"], expand)


def _fox_prompt_kernel(q_ref, k_ref, v_ref, d_ref, z_ref, o_ref, m_ref, acc_ref, alpha_ref, p_ref, *,
                       t, chunk, n_heads):
    qi = pl.program_id(2)
    _state_init(m_ref, acc_ref)

    def tile(i, causal):
        k0 = pl.multiple_of(i * t, t)
        for hh in range(n_heads):
            cols = slice(hh * HEAD_DIM, (hh + 1) * HEAD_DIM)
            s = _dot_nt(q_ref[:, cols], k_ref[pl.ds(k0, t), cols])
            d_row = d_ref[hh, :, pl.ds(k0, t)]

            def logits_of(c, s=s, d_row=d_row):
                lg = s[c * chunk:(c + 1) * chunk] * SCALE - d_row
                if causal:
                    row = c * chunk + lax.broadcasted_iota(jnp.int32, (chunk, t), 0)
                    col = lax.broadcasted_iota(jnp.int32, (chunk, t), 1)
                    lg = jnp.where(col <= row, lg, NEG_INF)
                return lg

            _state_update_chunked(logits_of, t // chunk, chunk, _with_ones(v_ref[pl.ds(k0, t), cols]),
                                  m_ref.at[hh], acc_ref.at[hh], p_ref.at[hh], alpha_ref.at[hh], False)

    def body(i, carry):
        tile(i, False)
        return carry

    lax.fori_loop(0, qi, body, 0)
    tile(qi, True)
    for hh in range(n_heads):
        cols = slice(hh * HEAD_DIM, (hh + 1) * HEAD_DIM)
        o_ref[:, cols] = (_state_result(acc_ref.at[hh]) * _silu(z_ref[:, cols])).astype(o_ref.dtype)


def _fox_prompt(proj, d_rows, batch, seq):
    t, chunk, n_heads = 512, 128, 2
    nq = seq // t
    width = n_heads * HEAD_DIM
    n_hg = FOX_HEADS // n_heads
    return pl.pallas_call(
        functools.partial(_fox_prompt_kernel, t=t, chunk=chunk, n_heads=n_heads),
        grid=(batch, n_hg, nq),
        in_specs=[pl.BlockSpec((t, width), lambda b, h, i: (b * nq + i, h)),
                  pl.BlockSpec((seq, width), lambda b, h, i: (b, h)),
                  pl.BlockSpec((seq, width), lambda b, h, i: (b, n_hg + h)),
                  pl.BlockSpec((n_heads, 1, seq), lambda b, h, i: (b * n_hg + h, 0, 0)),
                  pl.BlockSpec((t, width), lambda b, h, i: (b * nq + i, h))],
        out_specs=pl.BlockSpec((t, width), lambda b, h, i: (b * nq + i, h)),
        out_shape=jax.ShapeDtypeStruct((batch * seq, FOX_WIDTH), BF16),
        scratch_shapes=[pltpu.VMEM((n_heads, t, LANES), F32), pltpu.VMEM((n_heads, t, 2 * HEAD_DIM), F32),
                        pltpu.VMEM((n_heads, t, LANES), F32), pltpu.VMEM((n_heads, t, t), BF16)],
        compiler_params=_params(("parallel", "parallel", "arbitrary")),
        name="fox_prompt",
    )(proj["q_f"], proj["kv_f_bf"], proj["kv_f_bf"], d_rows, proj["z_f"])


def _prompt_layer(x, norm_in, wts, cmp_pe, cmp_w_bf, w_out_bf, norm_final):
    batch, seq, d = x.shape
    x2d = x.reshape(batch * seq, d)
    proj = _project_all(x2d, norm_in, wts, 1024)
    n_gate = 3 * NSA_HEADS
    logf = proj["small"][:, n_gate:n_gate + FOX_HEADS].reshape(batch, seq, FOX_HEADS)
    d_rows = _cumsum_rows(jnp.swapaxes(logf, 1, 2).reshape(batch * FOX_HEADS, seq))
    kcv = _compress_prompt(proj["kv4"], cmp_pe, cmp_w_bf, batch, seq)
    blk_of_key = jnp.arange(seq, dtype=jnp.int32) // SEL_BLOCK
    expand = (jnp.arange(LANES, dtype=jnp.int32)[:, None] == blk_of_key[None, :]).astype(BF16)
    mix_n = _nsa_prompt(proj, kcv, expand, batch, seq)
    mix_f = _fox_prompt(proj, d_rows.reshape(batch * FOX_HEADS, 1, seq), batch, seq)
    y = _merge(x2d, mix_n, mix_f, w_out_bf, norm_final, 512).reshape(batch, seq, d)
    w_keep = min(WINDOW, seq)
    state = (proj["kv4"].reshape(batch, seq, 4, NSA_KV_HEADS, HEAD_DIM),
             proj["kv_f"].reshape(batch, seq, 2, FOX_HEADS, HEAD_DIM),
             logf,
             proj["kvwin"].reshape(batch, seq, 2, NSA_KV_HEADS, HEAD_DIM)[:, seq - w_keep:])
    return y, state


def _nsa_page_specs(pg):
    def spec(u):
        return pl.BlockSpec((1, 1, PAGE_SIZE * NSA_CHANNELS, HEAD_DIM),
                            lambda b, j, pt: (0, pt[b, j * pg + u], 0, 0))
    return [spec(u) for u in range(pg)]


def _channel(page_ref, ch, start=0, n=PAGE_SIZE):
    return page_ref[0, 0, pl.ds(start * NSA_CHANNELS + ch, n, stride=NSA_CHANNELS), :]


def _nsa_cmp_kernel(pt_ref, *refs, pg, n_pages):
    page_refs = refs[:pg]
    q_ref, pe_ref, w_ref, oc_ref, sel_ref, stage_ref, xs_ref = refs[pg:]
    jg = pl.program_id(1)
    past = n_pages * PAGE_SIZE
    nb = past // CMP_BLOCK
    half = nb // 2
    blocks_per_page = PAGE_SIZE // CMP_BLOCK
    for u in range(pg):
        for bl in range(blocks_per_page):
            m = (jg * pg + u) * (blocks_per_page // 2) + bl // 2
            dst = pl.multiple_of(((bl % 2) * half + m) * STAGE_PITCH, SUBLANES)
            for ch in range(2 * NSA_KV_HEADS):
                stage_ref[ch, pl.ds(dst, CMP_BLOCK), :] = _channel(page_refs[u], ch, bl * CMP_BLOCK, CMP_BLOCK)

    @pl.when(jg == pl.num_programs(1) - 1)
    def _():
        cmp = [[_compress_rows(stage_ref.at[kind * NSA_KV_HEADS + h], pe_ref, kind, w_ref[kind], xs_ref, nb,
                               (0, half * STAGE_PITCH), STAGE_PITCH).astype(BF16)
                for h in range(NSA_KV_HEADS)] for kind in range(2)]
        rows = NSA_GROUP * T_PAD
        for hkv in range(NSA_KV_HEADS):
            q = q_ref[0, hkv * rows:(hkv + 1) * rows, :].astype(BF16)
            kc, vc = cmp[0][hkv], cmp[1][hkv]
            s = _dot_nt(q, kc) * SCALE
            lane = lax.broadcasted_iota(jnp.int32, (rows, nb), 1)
            blk = jnp.where(lane < half, 2 * lane, 2 * lane - (nb - 1))
            row = lax.broadcasted_iota(jnp.int32, (rows, nb), 0)
            slope = _head_slope(hkv * NSA_GROUP + row // T_PAD)
            dist = (past + row % T_PAD - ((blk + 1) * CMP_BLOCK - 1)).astype(F32)
            lg = jnp.where(dist >= 0, s - slope * dist, NEG_INF)
            mx = jnp.max(lg, axis=1, keepdims=True)
            mx = jnp.where(mx == NEG_INF, 0.0, mx)
            p = jnp.exp(lg - mx)
            p = p / jnp.maximum(jnp.sum(p, axis=1, keepdims=True), 1e-30)
            oc_ref[0, hkv * rows:(hkv + 1) * rows, :] = _dot(p.astype(BF16), vc)
            score = p[0:T_PAD]
            for g in range(1, NSA_GROUP):
                score = score + p[g * T_PAD:(g + 1) * T_PAD]
            score = score[:, :half] + score[:, half:]
            blk_s = lax.broadcasted_iota(jnp.int32, (T_PAD, half), 1).astype(F32)
            forced = (blk_s == 0) | (blk_s == half - 1)
            s_left = jnp.where(forced, NEG_INF, score)
            picked = jnp.zeros((T_PAD, half), F32)
            for _ in range(SEL_TOPK - 3):
                best = jnp.max(s_left, axis=1, keepdims=True)
                first = jnp.min(jnp.where(s_left == best, blk_s, float(half)), axis=1, keepdims=True)
                hit = blk_s == first
                picked = jnp.where(hit, 1.0, picked)
                s_left = jnp.where(hit, NEG_INF, s_left)
            sel_ref[0, hkv * T_PAD:(hkv + 1) * T_PAD, :] = jnp.where(forced, 1.0, picked)


def _nsa_cmp_sample(cache, page_table, q_rows, cmp_pe, cmp_w_bf, pg):
    n_seq, n_pages = page_table.shape
    past = n_pages * PAGE_SIZE
    nb = past // CMP_BLOCK
    assert n_pages % pg == 0 and past // SEL_BLOCK >= SEL_TOPK
    rows = NSA_HEADS * T_PAD
    grid_spec = pltpu.PrefetchScalarGridSpec(
        num_scalar_prefetch=1,
        grid=(n_seq, n_pages // pg),
        in_specs=_nsa_page_specs(pg) + [
            pl.BlockSpec((1, rows, HEAD_DIM), lambda b, j, pt: (b, 0, 0)),
            pl.BlockSpec((2, CMP_BLOCK, HEAD_DIM), lambda b, j, pt: (0, 0, 0)),
            pl.BlockSpec((2, CMP_BLOCK * HEAD_DIM, HEAD_DIM), lambda b, j, pt: (0, 0, 0))],
        out_specs=[pl.BlockSpec((1, rows, HEAD_DIM), lambda b, j, pt: (b, 0, 0)),
                   pl.BlockSpec((1, NSA_KV_HEADS * T_PAD, past // SEL_BLOCK), lambda b, j, pt: (b, 0, 0))],
        scratch_shapes=[pltpu.VMEM((2 * NSA_KV_HEADS, nb * STAGE_PITCH, HEAD_DIM), F32),
                        pltpu.VMEM((nb, CMP_BLOCK * HEAD_DIM), BF16)],
    )
    return pl.pallas_call(
        functools.partial(_nsa_cmp_kernel, pg=pg, n_pages=n_pages),
        grid_spec=grid_spec,
        out_shape=[jax.ShapeDtypeStruct((n_seq, rows, HEAD_DIM), F32),
                   jax.ShapeDtypeStruct((n_seq, NSA_KV_HEADS * T_PAD, past // SEL_BLOCK), F32)],
        compiler_params=_params(("parallel", "arbitrary")),
        name="nsa_sample_cmp",
    )(page_table, *([cache] * pg), q_rows, cmp_pe, cmp_w_bf)


def _nsa_slc_kernel(pt_ref, *refs, pg, n_pages, w_buf, n_new):
    page_refs = refs[:pg]
    (q_ref, sel_ref, e_ref, new_ref, win_ref, oc_ref, g_ref, z_ref, o_ref, tab_ref, m_ref, acc_ref) = refs[pg:]
    jg = pl.program_id(1)
    past = n_pages * PAGE_SIZE
    hr = n_new * NSA_GROUP
    rows = NSA_KV_HEADS * hr
    row = lax.broadcasted_iota(jnp.int32, (rows, 1), 0)
    t_q = (row % hr) // NSA_GROUP
    slope = _head_slope((row // hr) * NSA_GROUP + row % NSA_GROUP)
    q = [q_ref[0, h * hr:(h + 1) * hr, :].astype(BF16) for h in range(NSA_KV_HEADS)]

    def scores(k_of):
        return jnp.concatenate([_dot_nt(q[h], k_of(h)) for h in range(NSA_KV_HEADS)], axis=0) * SCALE

    def pv(v_of):
        return lambda p: jnp.concatenate(
            [_dot(p[h * hr:(h + 1) * hr], _with_ones(v_of(h))) for h in range(NSA_KV_HEADS)], axis=0)


    @pl.when(jg == 0)
    def _():
        expanded = _dot(sel_ref[0].astype(BF16), e_ref[...])
        key = lax.broadcasted_iota(jnp.int32, (rows, past), 1)
        tab_ref[...] = jnp.where(expanded > 0.5, 0.0, NEG_INF) + slope * (key - past).astype(F32)
        _state_init(m_ref, acc_ref)

    def page_stream(ch):
        return jnp.concatenate([_channel(page_refs[u], ch).astype(BF16) for u in range(pg)], axis=0)

    c0 = pl.multiple_of(jg * (pg * PAGE_SIZE), pg * PAGE_SIZE)
    lg = scores(lambda h: page_stream(2 * NSA_KV_HEADS + h)) + tab_ref[:, pl.ds(c0, pg * PAGE_SIZE)]
    _state_update(lg, pv(lambda h: page_stream(3 * NSA_KV_HEADS + h)), m_ref, acc_ref, False)

    @pl.when(jg == pl.num_programs(1) - 1)
    def _():
        new = lambda ch: new_ref[0, ch].astype(BF16)
        j_new = lax.broadcasted_iota(jnp.int32, (rows, PAGE_SIZE), 1)
        bias_new = jnp.where((j_new <= t_q) & (j_new < n_new), slope * j_new.astype(F32), NEG_INF)
        _state_update(scores(lambda h: new(h)) + bias_new, pv(lambda h: new(NSA_KV_HEADS + h)),
                      m_ref, acc_ref, False)
        o_s = _state_result(acc_ref)

        _state_init(m_ref, acc_ref)
        i_w = lax.broadcasted_iota(jnp.int32, (rows, w_buf), 1)
        dist_w = t_q + w_buf - i_w
        valid_w = (dist_w >= 0) & (dist_w < WINDOW) & (past - w_buf + i_w >= 0)
        bias_w = jnp.where(valid_w, slope * (i_w - w_buf).astype(F32), NEG_INF)
        win = lambda ch: win_ref[0, ch].astype(BF16)
        _state_update(scores(lambda h: win(h)) + bias_w, pv(lambda h: win(NSA_KV_HEADS + h)),
                      m_ref, acc_ref, True)
        _state_update(scores(lambda h: new(2 * NSA_KV_HEADS + h)) + bias_new,
                      pv(lambda h: new(3 * NSA_KV_HEADS + h)), m_ref, acc_ref, True)
        o_w = _state_result(acc_ref)

        gates = g_ref[0]
        mix = gates[:, 0:1] * oc_ref[0] + gates[:, 1:2] * o_s + gates[:, 2:3] * o_w
        o_ref[0] = mix * _silu(z_ref[0])


def _nsa_slc_sample(cache, page_table, q_rows, sel_rows, expand, new_rows, win_state, o_c_rows, gates_rows, z_rows,
                    pg, n_new):
    n_seq, n_pages = page_table.shape
    past = n_pages * PAGE_SIZE
    w_buf = win_state.shape[2]
    rows = n_new * NSA_HEADS
    per_seq = lambda *shape: pl.BlockSpec((1,) + shape, lambda b, j, pt: (b,) + (0,) * len(shape))
    grid_spec = pltpu.PrefetchScalarGridSpec(
        num_scalar_prefetch=1,
        grid=(n_seq, n_pages // pg),
        in_specs=_nsa_page_specs(pg) + [
            per_seq(rows, HEAD_DIM),
            per_seq(rows, past // SEL_BLOCK),
            pl.BlockSpec((past // SEL_BLOCK, past), lambda b, j, pt: (0, 0)),
            per_seq(4 * NSA_KV_HEADS, PAGE_SIZE, HEAD_DIM),
            per_seq(2 * NSA_KV_HEADS, w_buf, HEAD_DIM),
            per_seq(rows, HEAD_DIM),
            per_seq(rows, 3),
            per_seq(rows, HEAD_DIM)],
        out_specs=per_seq(rows, HEAD_DIM),
        scratch_shapes=[pltpu.VMEM((rows, past), F32), pltpu.VMEM((rows, LANES), F32),
                        pltpu.VMEM((rows, 2 * HEAD_DIM), F32)],
    )
    return pl.pallas_call(
        functools.partial(_nsa_slc_kernel, pg=pg, n_pages=n_pages, w_buf=w_buf, n_new=n_new),
        grid_spec=grid_spec,
        out_shape=jax.ShapeDtypeStruct((n_seq, rows, HEAD_DIM), F32),
        compiler_params=_params(("parallel", "arbitrary")),
        name="nsa_sample_slc_win",
    )(page_table, *([cache] * pg), q_rows, sel_rows, expand, new_rows, win_state, o_c_rows, gates_rows, z_rows)


def _page_forget_scan(x):
    n = x.shape[1]
    lane = lax.broadcasted_iota(jnp.int32, x.shape, 1)
    s = FOX_HEADS
    while s < n:
        x = x + jnp.where(lane >= s, pltpu.roll(x, s, axis=1), 0.0)
        s *= 2
    total = jnp.where(lane >= n - FOX_HEADS, x, 0.0)
    s = FOX_HEADS
    while s < n:
        total = total + pltpu.roll(total, n - s, axis=1)
        s *= 2
    return x, total


def _fox_sample_kernel(pt_ref, *refs, pg, n_new):
    page_refs = refs[:pg]
    (q_ref, lc_ref, kn_ref, vn_ref, ln_ref, z_ref, o_ref, m_ref, acc_ref, carry_ref, lg_ref) = refs[pg:]
    b = pl.program_id(0)
    jg = pl.program_id(1)
    rows = n_new * FOX_HEADS
    keys = PAGE_SIZE * FOX_HEADS

    @pl.when(jg == 0)
    def _():
        _state_init(m_ref, acc_ref)
        carry_ref[...] = jnp.zeros(carry_ref.shape, F32)

    q = q_ref[0].astype(BF16)
    own = (lax.broadcasted_iota(jnp.int32, (rows, keys), 1) % FOX_HEADS
           == lax.broadcasted_iota(jnp.int32, (rows, keys), 0) % FOX_HEADS)

    for u in range(pg):
        lg_ref[u:u + 1, :] = lc_ref[pl.ds(pt_ref[b, jg * pg + u], 1), :]
    within, total = _page_forget_scan(lg_ref[...])
    carry = carry_ref[...]
    logits = []
    for u in range(pg):
        d_u = within[u:u + 1] + carry
        carry = carry + total[u:u + 1]
        k = page_refs[u][0, 0, :, 0].reshape(keys, HEAD_DIM).astype(BF16)
        logits.append(jnp.where(own, _dot_nt(q, k) * SCALE - d_u, NEG_INF))
    carry_ref[...] = carry

    def pv(p):
        out = None
        for u in range(pg):
            v = page_refs[u][0, 0, :, 1].reshape(keys, HEAD_DIM).astype(BF16)
            part = _dot(p[:, u * keys:(u + 1) * keys], _with_ones(v))
            out = part if out is None else out + part
        return out

    _state_update(jnp.concatenate(logits, axis=1), pv, m_ref, acc_ref, False)

    @pl.when(jg == pl.num_programs(1) - 1)
    def _():
        within_new, _ = _page_forget_scan(ln_ref[0])
        d_new = (within_new + carry_ref[...])[:, :PAGE_SIZE]
        c = lax.broadcasted_iota(jnp.int32, (rows, PAGE_SIZE), 1)
        r = lax.broadcasted_iota(jnp.int32, (rows, PAGE_SIZE), 0)
        valid = (c < rows) & (c % FOX_HEADS == r % FOX_HEADS) & (c // FOX_HEADS <= r // FOX_HEADS)
        lg = jnp.where(valid, _dot_nt(q, kn_ref[0].astype(BF16)) * SCALE - d_new, NEG_INF)
        _state_update(lg, lambda p: _dot(p, _with_ones(vn_ref[0].astype(BF16))), m_ref, acc_ref, False)
        o_ref[0] = _state_result(acc_ref) * _silu(z_ref[0])


def _fox_sample(cache, page_table, q_rows, logf_cache, k_new, v_new, logf_new, z_rows, pg, n_new):
    n_seq, n_pages = page_table.shape
    n_pool = logf_cache.shape[0]
    rows = n_new * FOX_HEADS
    keys = PAGE_SIZE * FOX_HEADS
    per_seq = lambda *shape: pl.BlockSpec((1,) + shape, lambda b, j, pt: (b,) + (0,) * len(shape))

    def page_spec(u):
        return pl.BlockSpec((1, 1, PAGE_SIZE, 2, FOX_HEADS, HEAD_DIM),
                            lambda b, j, pt: (0, pt[b, j * pg + u], 0, 0, 0, 0))

    grid_spec = pltpu.PrefetchScalarGridSpec(
        num_scalar_prefetch=1,
        grid=(n_seq, n_pages // pg),
        in_specs=[page_spec(u) for u in range(pg)] + [
            per_seq(rows, HEAD_DIM),
            pl.BlockSpec((n_pool, keys), lambda b, j, pt: (0, 0)),
            per_seq(PAGE_SIZE, HEAD_DIM),
            per_seq(PAGE_SIZE, HEAD_DIM),
            per_seq(1, keys),
            per_seq(rows, HEAD_DIM)],
        out_specs=per_seq(rows, HEAD_DIM),
        scratch_shapes=[pltpu.VMEM((rows, LANES), F32), pltpu.VMEM((rows, 2 * HEAD_DIM), F32),
                        pltpu.VMEM((1, keys), F32), pltpu.VMEM((pg, keys), F32)],
    )
    return pl.pallas_call(
        functools.partial(_fox_sample_kernel, pg=pg, n_new=n_new),
        grid_spec=grid_spec,
        out_shape=jax.ShapeDtypeStruct((n_seq, rows, HEAD_DIM), F32),
        compiler_params=_params(("parallel", "arbitrary")),
        name="fox_sample",
    )(page_table, *([cache] * pg), q_rows, logf_cache, k_new, v_new, logf_new, z_rows)


def _pad_axis(a, axis, size):
    pad = [(0, 0)] * a.ndim
    pad[axis] = (0, size - a.shape[axis])
    return jnp.pad(a, pad)


def _sample_layer(x, cache_nsa_kv, cache_fox_kv, cache_fox_logf, state_nsa_win, page_table,
                  norm_in, wts, cmp_pe, cmp_w_bf, w_out_bf, norm_final):
    n_seq, t_new, d = x.shape
    n_pool = cache_nsa_kv.shape[1]
    n_pages = page_table.shape[1]
    past = n_pages * PAGE_SIZE
    w_buf = state_nsa_win.shape[2]
    assert t_new <= T_PAD and t_new < SEL_BLOCK and past % SEL_BLOCK == 0
    x2d = x.reshape(n_seq * t_new, d)
    proj = _project_all(x2d, norm_in, wts, n_seq * t_new)
    n_gate = 3 * NSA_HEADS
    small = proj["small"].reshape(n_seq, t_new, LANES)
    logf = small[:, :, n_gate:n_gate + FOX_HEADS]

    def head_rows(a):
        a = _pad_axis(a.reshape(n_seq, t_new, NSA_HEADS, -1), 1, T_PAD)
        return jnp.swapaxes(a, 1, 2).reshape(n_seq, NSA_HEADS * T_PAD, a.shape[-1])

    def group_rows(a):
        a = a.reshape(n_seq, t_new, NSA_KV_HEADS, NSA_GROUP, a.shape[-1])
        return jnp.swapaxes(a, 1, 2).reshape(n_seq, NSA_HEADS * t_new, a.shape[-1])

    q_n = proj["q_n"].astype(F32)
    gates = small[:, :, :n_gate].reshape(n_seq, t_new, NSA_HEADS, 3)
    z_n = proj["z_n"].reshape(n_seq, t_new, NSA_HEADS, HEAD_DIM)
    nsa_new = jnp.concatenate([proj["kv4"].reshape(n_seq, t_new, 4 * NSA_KV_HEADS, HEAD_DIM)[:, :, 2 * NSA_KV_HEADS:],
                               proj["kvwin"].reshape(n_seq, t_new, 2 * NSA_KV_HEADS, HEAD_DIM)], axis=2)
    nsa_new = _pad_axis(jnp.swapaxes(nsa_new, 1, 2), 2, PAGE_SIZE)
    win_state = jnp.swapaxes(state_nsa_win[0].reshape(n_seq, w_buf, 2 * NSA_KV_HEADS, HEAD_DIM), 1, 2)
    blk_of_key = jnp.arange(past, dtype=jnp.int32) // SEL_BLOCK
    expand = (jnp.arange(past // SEL_BLOCK, dtype=jnp.int32)[:, None] == blk_of_key[None, :]).astype(BF16)
    nsa_cache = cache_nsa_kv.reshape(cache_nsa_kv.shape[0], n_pool, PAGE_SIZE * NSA_CHANNELS, HEAD_DIM)

    pg = 8
    o_c, sel = _nsa_cmp_sample(nsa_cache, page_table, head_rows(q_n), cmp_pe, cmp_w_bf, pg)
    o_c = jnp.swapaxes(o_c.reshape(n_seq, NSA_HEADS, T_PAD, HEAD_DIM)[:, :, :t_new], 1, 2)
    sel = sel.reshape(n_seq, NSA_KV_HEADS, T_PAD, past // SEL_BLOCK)[:, :, :t_new]
    sel = jnp.broadcast_to(sel[:, :, :, None, :], (n_seq, NSA_KV_HEADS, t_new, NSA_GROUP, past // SEL_BLOCK))
    mix_n = _nsa_slc_sample(nsa_cache, page_table, group_rows(q_n.reshape(n_seq, t_new, NSA_HEADS, HEAD_DIM)),
                            sel.reshape(n_seq, NSA_HEADS * t_new, past // SEL_BLOCK), expand, nsa_new, win_state,
                            group_rows(o_c), group_rows(gates), group_rows(z_n), pg, t_new)
    mix_n = jnp.swapaxes(mix_n.reshape(n_seq, NSA_KV_HEADS, t_new, NSA_GROUP * HEAD_DIM), 1, 2)
    mix_n = mix_n.reshape(n_seq * t_new, NSA_WIDTH).astype(BF16)

    fox_rows = lambda a: a.reshape(n_seq, t_new * FOX_HEADS, HEAD_DIM)
    kv_f = proj["kv_f"].reshape(n_seq, t_new, 2, FOX_HEADS * HEAD_DIM)
    k_new = _pad_axis(fox_rows(kv_f[:, :, 0]), 1, PAGE_SIZE)
    v_new = _pad_axis(fox_rows(kv_f[:, :, 1]), 1, PAGE_SIZE)
    logf_new = _pad_axis(logf.reshape(n_seq, 1, t_new * FOX_HEADS), 2, PAGE_SIZE * FOX_HEADS)
    logf_cache = cache_fox_logf[0].reshape(n_pool, PAGE_SIZE * FOX_HEADS)
    mix_f = _fox_sample(cache_fox_kv, page_table, fox_rows(proj["q_f"].astype(F32)), logf_cache, k_new, v_new,
                        logf_new, fox_rows(proj["z_f"]), pg, t_new)
    mix_f = mix_f.reshape(n_seq * t_new, FOX_WIDTH).astype(BF16)

    y = _merge(x2d, mix_n, mix_f, w_out_bf, norm_final, n_seq * t_new).reshape(n_seq, t_new, d)
    kvwin_new = proj["kvwin"].reshape(n_seq, t_new, 2, NSA_KV_HEADS, HEAD_DIM)
    state = (proj["kv4"].reshape(n_seq, t_new, 4, NSA_KV_HEADS, HEAD_DIM),
             proj["kv_f"].reshape(n_seq, t_new, 2, FOX_HEADS, HEAD_DIM),
             logf,
             jnp.concatenate([state_nsa_win[0], kvwin_new], axis=1)[:, t_new:])
    return y, state


def kernel(x_prompt, x_sample, cache_nsa_kv, cache_fox_kv, cache_fox_logf, state_nsa_win, page_table,
           norm_in, w_in, b_gate, b_forget, cmp_pe, cmp_w, w_out, norm_final):
    assert norm_in.shape[0] == 1, "single-layer trunk"
    wts = _split_weights(w_in[0], b_gate[0], b_forget[0])
    cmp_w_bf = cmp_w[0].reshape(2, CMP_BLOCK * HEAD_DIM, HEAD_DIM).astype(BF16)
    w_out_bf = w_out[0].astype(BF16)
    y_p, st_p = _prompt_layer(x_prompt, norm_in[0], wts, cmp_pe[0], cmp_w_bf, w_out_bf, norm_final)
    y_s, st_s = _sample_layer(x_sample, cache_nsa_kv, cache_fox_kv, cache_fox_logf, state_nsa_win,
                              page_table, norm_in[0], wts, cmp_pe[0], cmp_w_bf, w_out_bf, norm_final)
    outs = [y_p, y_s]
    for s_p, s_s in zip(st_p, st_s):
        outs.extend([s_p[None], s_s[None]])
    return tuple(outs)
```

```python
import functools

import jax
import jax.numpy as jnp
from jax import lax
from jax.experimental import pallas as pl
from jax.experimental.pallas import tpu as pltpu

F32 = jnp.float32
BF16 = jnp.bfloat16
NEG_INF = float("-inf")

HEAD_DIM = 128
NSA_HEADS = 8
FOX_HEADS = 8
NSA_KV_HEADS = 2
NSA_GROUP = NSA_HEADS // NSA_KV_HEADS
NSA_WIDTH = NSA_HEADS * HEAD_DIM
FOX_WIDTH = FOX_HEADS * HEAD_DIM
KV_WIDTH = NSA_KV_HEADS * HEAD_DIM
NSA_CHANNELS = 4 * NSA_KV_HEADS
CMP_BLOCK = 32
SEL_BLOCK = 64
SEL_TOPK = 16
WINDOW = 512
PAGE_SIZE = 128
FORCED_SCORE = 1e4
RMS_EPS = 1e-6
SCALE = HEAD_DIM ** -0.5
LOG2E = 1.4426950408889634
T_PAD = 8
LANES = 128
SUBLANES = 8
PROJ_TN = 512
STAGE_PITCH = 40
VMEM_LIMIT = 56 * 1024 * 1024


def _params(sem):
    return pltpu.CompilerParams(dimension_semantics=sem, vmem_limit_bytes=VMEM_LIMIT)


def _dot(a, b):
    return jnp.dot(a, b, preferred_element_type=F32)


def _dot_nt(a, b):
    return lax.dot_general(a, b, (((1,), (1,)), ((), ())), preferred_element_type=F32)


def _sigmoid(x):
    return 1.0 / (1.0 + jnp.exp(-x))


def _silu(x):
    return x * _sigmoid(x)


def _tile_lanes(x, n):
    return x if n == 1 else jnp.concatenate([x] * n, axis=1)


def _with_ones(v):
    return jnp.concatenate([v, jnp.ones(v.shape, v.dtype)], axis=1)


def _state_init(m_ref, acc_ref):
    m_ref[...] = jnp.full(m_ref.shape, NEG_INF, F32)
    acc_ref[...] = jnp.zeros(acc_ref.shape, F32)


def _probs(lg, m_old, may_be_empty, exp=jnp.exp):
    m_new = jnp.maximum(m_old, jnp.max(lg, axis=1, keepdims=True))
    m_use = jnp.where(m_new == NEG_INF, 0.0, m_new) if may_be_empty else m_new
    alpha = exp(m_old - m_use)
    p = exp(lg - _tile_lanes(m_use, lg.shape[1] // LANES)).astype(BF16)
    return m_new, alpha, p


def _state_update_chunked(logits_of, n_chunks, chunk, v_ext, m_ref, acc_ref, p_ref, alpha_ref, may_be_empty):
    for c in range(n_chunks):
        rows = slice(c * chunk, (c + 1) * chunk)
        m_new, alpha, p = _probs(logits_of(c), m_ref[rows, :], may_be_empty, jnp.exp2)
        alpha_ref[rows, :] = alpha
        p_ref[rows, :] = p
        m_ref[rows, :] = m_new
    acc_ref[...] = _tile_lanes(alpha_ref[...], 2) * acc_ref[...] + _dot(p_ref[...], v_ext)


def _state_update(lg, pv_of, m_ref, acc_ref, may_be_empty):
    m_new, alpha, p = _probs(lg, m_ref[...], may_be_empty)
    acc_ref[...] = _tile_lanes(alpha, 2) * acc_ref[...] + pv_of(p)
    m_ref[...] = m_new


def _state_update_parts(parts, m_ref, acc_ref, may_be_empty):
    local = []
    for lg, pv_of in parts:
        m_c = jnp.broadcast_to(jnp.max(lg, axis=1, keepdims=True), m_ref.shape)
        m_use = jnp.where(m_c == NEG_INF, 0.0, m_c) if may_be_empty else m_c
        p = jnp.exp(lg - _tile_lanes(m_use, lg.shape[1] // LANES)).astype(BF16)
        local.append((m_c, pv_of(p)))
    m_old = m_ref[...]
    m_new = m_old
    for m_c, _ in local:
        m_new = jnp.maximum(m_new, m_c)
    acc = _tile_lanes(jnp.exp(m_old - m_new), 2) * acc_ref[...]
    for m_c, pv in local:
        acc = acc + _tile_lanes(jnp.exp(m_c - m_new), 2) * pv
    acc_ref[...] = acc
    m_ref[...] = m_new


def _state_result(acc_ref):
    acc = acc_ref[...]
    return acc[:, :HEAD_DIM] / jnp.maximum(acc[:, HEAD_DIM:], 1e-30)


def _lane_cumsum(x):
    n = x.shape[-1]
    lane = lax.broadcasted_iota(jnp.int32, x.shape, x.ndim - 1)
    s = 1
    while s < n:
        x = x + jnp.where(lane >= s, pltpu.roll(x, s, axis=x.ndim - 1), 0.0)
        s *= 2
    return x


def _head_slope(head):
    if isinstance(head, int):
        return 2.0 ** -(head + 1)
    return lax.bitcast_convert_type((126 - head) << 23, F32)


def _rms_kernel(x_ref, g_ref, o_ref):
    x = x_ref[...]
    ms = jnp.mean(x * x, axis=-1, keepdims=True)
    o_ref[...] = (x * lax.rsqrt(ms + RMS_EPS) * g_ref[...]).astype(o_ref.dtype)


def _rmsnorm(x2d, g, out_dtype, tm):
    m, d = x2d.shape
    return pl.pallas_call(
        _rms_kernel,
        grid=(m // tm,),
        in_specs=[pl.BlockSpec((tm, d), lambda i: (i, 0)), pl.BlockSpec((1, d), lambda i: (0, 0))],
        out_specs=pl.BlockSpec((tm, d), lambda i: (i, 0)),
        out_shape=jax.ShapeDtypeStruct((m, d), out_dtype),
        compiler_params=_params(("parallel",)),
        name="rmsnorm",
    )(x2d, g.reshape(1, d))


def _proj_kernel(h_ref, w_ref, *o_refs):
    acc = _dot(h_ref[...], w_ref[...])
    for o_ref in o_refs:
        o_ref[...] = acc.astype(o_ref.dtype)


def _project(h, w, col0, n, out_dtypes, tm, name):
    m, d = h.shape
    j0 = col0 // PROJ_TN
    return pl.pallas_call(
        _proj_kernel,
        grid=(m // tm, n // PROJ_TN),
        in_specs=[pl.BlockSpec((tm, d), lambda i, j: (i, 0)), pl.BlockSpec((d, PROJ_TN), lambda i, j: (0, j0 + j))],
        out_specs=[pl.BlockSpec((tm, PROJ_TN), lambda i, j: (i, j)) for _ in out_dtypes],
        out_shape=[jax.ShapeDtypeStruct((m, n), dt) for dt in out_dtypes],
        compiler_params=_params(("parallel", "arbitrary")),
        name=name,
    )(h, w)


def _small_kernel(h_ref, w_ref, b_ref, o_ref, *, n_gate):
    z = _dot(h_ref[...], w_ref[...]) + b_ref[...]
    lane = lax.broadcasted_iota(jnp.int32, z.shape, 1)
    log_sig = jnp.minimum(z, 0.0) - jnp.log1p(jnp.exp(-jnp.abs(z)))
    o_ref[...] = jnp.where(lane < n_gate, _sigmoid(z), log_sig)


def _project_small(h, w, b, tm, n_gate):
    m, d = h.shape
    return pl.pallas_call(
        functools.partial(_small_kernel, n_gate=n_gate),
        grid=(m // tm,),
        in_specs=[pl.BlockSpec((tm, d), lambda i: (i, 0)), pl.BlockSpec((d, LANES), lambda i: (0, 0)),
                  pl.BlockSpec((1, LANES), lambda i: (0, 0))],
        out_specs=pl.BlockSpec((tm, LANES), lambda i: (i, 0)),
        out_shape=jax.ShapeDtypeStruct((m, LANES), F32),
        compiler_params=_params(("parallel",)),
        name="proj_gates_logf",
    )(h, w, b)


PROJ_GROUPS = (("q_n", NSA_WIDTH), ("kv4", 4 * KV_WIDTH), ("kvwin", 2 * KV_WIDTH), ("z_n", NSA_WIDTH),
               ("q_f", FOX_WIDTH), ("kv_f", 2 * FOX_WIDTH), ("z_f", FOX_WIDTH))
PROJ_OUT_DTYPES = {"q_n": (BF16,), "kv4": (F32, BF16), "kvwin": (F32, BF16), "z_n": (F32,),
                   "q_f": (BF16,), "kv_f": (F32, BF16), "z_f": (F32,)}


def _split_weights(w_in, b_gate, b_forget):
    cuts = [NSA_WIDTH, 6 * KV_WIDTH, 3 * NSA_HEADS, NSA_WIDTH, FOX_WIDTH, FOX_WIDTH, FOX_WIDTH, FOX_HEADS, FOX_WIDTH]
    offs = [0]
    for c in cuts:
        offs.append(offs[-1] + c)
    main = jnp.concatenate([w_in[:, offs[0]:offs[2]], w_in[:, offs[3]:offs[7]], w_in[:, offs[8]:offs[9]]],
                           axis=1).astype(BF16)
    n_small = 3 * NSA_HEADS + FOX_HEADS
    w_small = jnp.concatenate([w_in[:, offs[2]:offs[3]], w_in[:, offs[7]:offs[8]]], axis=1)
    w_small = jnp.pad(w_small, ((0, 0), (0, LANES - n_small))).astype(BF16)
    b_small = jnp.pad(jnp.concatenate([b_gate, b_forget]), (0, LANES - n_small)).reshape(1, LANES).astype(F32)
    return {"main": main, "small": w_small, "b_small": b_small}


def _project_all(x2d, norm_in, wts, tm):
    h = _rmsnorm(x2d, norm_in, BF16, min(tm, 512))
    out = {}
    col0 = 0
    for name, width in PROJ_GROUPS:
        res = _project(h, wts["main"], col0, width, PROJ_OUT_DTYPES[name], tm, "proj_" + name)
        out[name] = res[0]
        if len(res) > 1:
            out[name + "_bf"] = res[1]
        col0 += width
    out["small"] = _project_small(h, wts["small"], wts["b_small"], min(tm, 512), 3 * NSA_HEADS)
    return out


def _merge_kernel(x_ref, mn_ref, mf_ref, wn_ref, wf_ref, g_ref, o_ref):
    y = x_ref[...] + _dot(mn_ref[...], wn_ref[...]) + _dot(mf_ref[...], wf_ref[...])
    ms = jnp.mean(y * y, axis=-1, keepdims=True)
    o_ref[...] = y * lax.rsqrt(ms + RMS_EPS) * g_ref[...]


def _merge(x2d, mix_n, mix_f, w_out_bf, norm_final, tm):
    m, d = x2d.shape
    return pl.pallas_call(
        _merge_kernel,
        grid=(m // tm,),
        in_specs=[pl.BlockSpec((tm, d), lambda i: (i, 0)),
                  pl.BlockSpec((tm, NSA_WIDTH), lambda i: (i, 0)),
                  pl.BlockSpec((tm, FOX_WIDTH), lambda i: (i, 0)),
                  pl.BlockSpec((NSA_WIDTH, d), lambda i: (0, 0)),
                  pl.BlockSpec((FOX_WIDTH, d), lambda i: (NSA_WIDTH // FOX_WIDTH, 0)),
                  pl.BlockSpec((1, d), lambda i: (0, 0))],
        out_specs=pl.BlockSpec((tm, d), lambda i: (i, 0)),
        out_shape=jax.ShapeDtypeStruct((m, d), F32),
        compiler_params=_params(("parallel",)),
        name="merge_out_proj",
    )(x2d, mix_n, mix_f, w_out_bf, w_out_bf, norm_final.reshape(1, d))


def _cumsum_kernel(x_ref, o_ref):
    o_ref[...] = _lane_cumsum(x_ref[...])


def _cumsum_rows(x):
    return pl.pallas_call(
        _cumsum_kernel,
        out_shape=jax.ShapeDtypeStruct(x.shape, F32),
        name="forget_cumsum",
    )(x)


def _compress_rows(x_ref, pe_ref, kind, w, xs_ref, nb, starts, stride):
    half = nb // 2
    for c in range(CMP_BLOCK):
        pe_c = pe_ref[kind, c:c + 1, :]
        for par in range(2):
            xc = x_ref[pl.ds(starts[par] + c, half, stride=stride), :]
            xs_ref[par * half:(par + 1) * half, c * HEAD_DIM:(c + 1) * HEAD_DIM] = (xc + pe_c).astype(BF16)
    return _dot(xs_ref[...], w)


def _compress_kernel(x_ref, pe_ref, w_ref, o_ref, xs_ref, *, nb):
    out = _compress_rows(x_ref, pe_ref, 0, w_ref[0], xs_ref, nb, (0, CMP_BLOCK), 2 * CMP_BLOCK)
    o_ref[0, 0, 0] = out.astype(o_ref.dtype)


def _compress_prompt(kv4, cmp_pe, cmp_w_bf, batch, seq):
    nb = seq // CMP_BLOCK
    return pl.pallas_call(
        functools.partial(_compress_kernel, nb=nb),
        grid=(batch, 2, NSA_KV_HEADS),
        in_specs=[pl.BlockSpec((seq, HEAD_DIM), lambda b, k, h: (b, k * NSA_KV_HEADS + h)),
                  pl.BlockSpec((1, CMP_BLOCK, HEAD_DIM), lambda b, k, h: (k, 0, 0)),
                  pl.BlockSpec((1, CMP_BLOCK * HEAD_DIM, HEAD_DIM), lambda b, k, h: (k, 0, 0))],
        out_specs=pl.BlockSpec((1, 1, 1, nb, HEAD_DIM), lambda b, k, h: (b, k, h, 0, 0)),
        out_shape=jax.ShapeDtypeStruct((batch, 2, NSA_KV_HEADS, nb, HEAD_DIM), BF16),
        scratch_shapes=[pltpu.VMEM((nb, CMP_BLOCK * HEAD_DIM), BF16)],
        compiler_params=_params(("parallel", "parallel", "arbitrary")),
        name="compress_prompt",
    )(kv4, cmp_pe, cmp_w_bf)


def _select_blocks(score_t, cur, n_keep):
    n_blk = score_t.shape[0]
    j = lax.broadcasted_iota(jnp.int32, score_t.shape, 0)
    forced = (j == 0) | (j == cur) | (j == cur - 1)
    s = jnp.where(j <= cur, jnp.where(forced, FORCED_SCORE, score_t), -1.0)
    n_grp = n_blk // SUBLANES
    grp = [s[r * SUBLANES:(r + 1) * SUBLANES] for r in range(n_grp)]
    cnt = [jnp.zeros(grp[0].shape, F32) for _ in range(n_grp)]
    sub = lax.broadcasted_iota(jnp.int32, grp[0].shape, 0)
    for jj in range(n_blk):
        row = s[jj:jj + 1, :]
        r_j, off = divmod(jj, SUBLANES)
        for r in range(n_grp):
            if r < r_j:
                hit = row > grp[r]
            elif r > r_j:
                hit = row >= grp[r]
            else:
                hit = (row > grp[r]) | ((row == grp[r]) & (sub > off))
            cnt[r] = cnt[r] + jnp.where(hit, 1.0, 0.0)
    return jnp.where((jnp.concatenate(cnt, axis=0) < n_keep) & (j <= cur), 1.0, 0.0)


def _nsa_prompt_kernel(q_ref, kc_ref, vc_ref, ks_ref, vs_ref, kw_ref, vw_ref, g_ref, z_ref, e_ref, o_ref,
                       m_ref, acc_ref, alpha_ref, ps_ref, pw_ref, mix_ref, bias_ref, *, tq, tk, tw, ck, seq):
    qi = pl.program_id(1)
    s0 = qi * tq
    n_cmp = seq // CMP_BLOCK
    n_sel = seq // SEL_BLOCK
    row_t = s0 + lax.broadcasted_iota(jnp.int32, (tq, 1), 0)
    kv_heads = range(NSA_KV_HEADS)
    heads = lambda hkv: [hkv * NSA_GROUP + g for g in range(NSA_GROUP)]
    slopes = lambda hkv: [_head_slope(h) for h in heads(hkv)]
    cols = lambda hkv: slice(hkv * HEAD_DIM, (hkv + 1) * HEAD_DIM)

    def q_rows(hkv):
        return jnp.concatenate([q_ref[:, h * HEAD_DIM:(h + 1) * HEAD_DIM] for h in heads(hkv)], axis=0)

    def gate_col(hkv, br):
        return jnp.concatenate([g_ref[:, 3 * h + br:3 * h + br + 1] for h in heads(hkv)], axis=0)

    for hkv in kv_heads:
        sc = _dot_nt(q_rows(hkv), kc_ref[0, 0, hkv]) * SCALE
        lane = lax.broadcasted_iota(jnp.int32, (tq, n_cmp), 1)
        blk = jnp.where(lane < n_cmp // 2, 2 * lane, 2 * lane - (n_cmp - 1))
        dist_c = (row_t - ((blk + 1) * CMP_BLOCK - 1)).astype(F32)
        valid_c = dist_c >= 0
        score = jnp.zeros((tq, n_cmp), F32)
        probs = []
        for g, slope in enumerate(slopes(hkv)):
            lg = jnp.where(valid_c, sc[g * tq:(g + 1) * tq] - slope * dist_c, NEG_INF)
            mx = jnp.max(lg, axis=1, keepdims=True)
            mx = jnp.where(mx == NEG_INF, 0.0, mx)
            p = jnp.exp(lg - mx)
            p = p / jnp.maximum(jnp.sum(p, axis=1, keepdims=True), 1e-30)
            score = score + p
            probs.append(p)
        o_c = _dot(jnp.concatenate(probs, axis=0).astype(BF16), vc_ref[0, 0, hkv])
        mix_ref[hkv] = gate_col(hkv, 0) * o_c

        score_t = score.T
        score_t = score_t[:n_sel] + score_t[n_sel:]
        cur = (s0 + lax.broadcasted_iota(jnp.int32, (n_sel, tq), 1)) // SEL_BLOCK
        sel_t = _select_blocks(score_t, cur, SEL_TOPK)
        sel = jnp.concatenate([sel_t, jnp.zeros((LANES - n_sel, tq), F32)], axis=0).T.astype(BF16)

        def bias_body(c, carry, hkv=hkv, sel=sel):
            c0 = pl.multiple_of(c * ck, ck)
            expanded = _dot(sel, e_ref[:, pl.ds(c0, ck)])
            key = c0 + lax.broadcasted_iota(jnp.int32, (tq, ck), 1)
            bias_ref[hkv, :, pl.ds(c0, ck)] = jnp.where((expanded > 0.5) & (key <= row_t), 0.0, NEG_INF)
            return carry

        lax.fori_loop(0, (s0 + tq + ck - 1) // ck, bias_body, 0)

    def stream(hkv, s, bias, key_rel, v, p_ref, may_be_empty):
        sl = slopes(hkv)
        logits_of = lambda g: s[g * tq:(g + 1) * tq] * (SCALE * LOG2E) + (bias + (sl[g] * LOG2E) * key_rel)
        _state_update_chunked(logits_of, NSA_GROUP, tq, _with_ones(v), m_ref.at[hkv], acc_ref.at[hkv],
                              p_ref.at[hkv], alpha_ref.at[hkv], may_be_empty)

    _state_init(m_ref, acc_ref)

    def slc_body(i, carry):
        k0 = pl.multiple_of(i * tk, tk)
        key_rel = (k0 - s0 + lax.broadcasted_iota(jnp.int32, (1, tk), 1)).astype(F32)
        for hkv in kv_heads:
            s = _dot_nt(q_rows(hkv), ks_ref[pl.ds(k0, tk), cols(hkv)])
            stream(hkv, s, bias_ref[hkv, :, pl.ds(k0, tk)], key_rel, vs_ref[pl.ds(k0, tk), cols(hkv)], ps_ref, False)
        return carry

    lax.fori_loop(0, (s0 + tq + tk - 1) // tk, slc_body, 0)
    for hkv in kv_heads:
        mix_ref[hkv] = mix_ref[hkv] + gate_col(hkv, 1) * _state_result(acc_ref.at[hkv])

    _state_init(m_ref, acc_ref)

    def win_body(w, carry):
        k0 = pl.multiple_of(s0 - WINDOW + w * tw, tw)
        disti = row_t - (k0 + lax.broadcasted_iota(jnp.int32, (tq, tw), 1))
        bias = jnp.where((disti >= 0) & (disti < WINDOW), 0.0, NEG_INF)
        key_rel = (k0 - s0 + lax.broadcasted_iota(jnp.int32, (1, tw), 1)).astype(F32)
        for hkv in kv_heads:
            s = _dot_nt(q_rows(hkv), kw_ref[pl.ds(k0, tw), cols(hkv)])
            stream(hkv, s, bias, key_rel, vw_ref[pl.ds(k0, tw), cols(hkv)], pw_ref, True)
        return carry

    n_win_tiles = (WINDOW + tq) // tw
    lax.fori_loop(jnp.maximum(0, (WINDOW - s0) // tw), n_win_tiles, win_body, 0)

    for hkv in kv_heads:
        mix = mix_ref[hkv] + gate_col(hkv, 2) * _state_result(acc_ref.at[hkv])
        for g, h in enumerate(heads(hkv)):
            z = z_ref[:, h * HEAD_DIM:(h + 1) * HEAD_DIM]
            o_ref[:, h * HEAD_DIM:(h + 1) * HEAD_DIM] = (mix[g * tq:(g + 1) * tq] * _silu(z)).astype(o_ref.dtype)


def _nsa_prompt(proj, kcv, expand, batch, seq):
    tq, tk, tw, ck = 256, 256, 256, 512
    assert seq % ck == 0 and WINDOW % tw == 0 and tq % tw == 0 and seq // SEL_BLOCK <= LANES
    nq = seq // tq
    rows = NSA_GROUP * tq
    kernel = functools.partial(_nsa_prompt_kernel, tq=tq, tk=tk, tw=tw, ck=ck, seq=seq)
    cmp_spec = lambda kind: pl.BlockSpec((1, 1, NSA_KV_HEADS, seq // CMP_BLOCK, HEAD_DIM),
                                         lambda b, i: (b, kind, 0, 0, 0))
    return pl.pallas_call(
        kernel,
        grid=(batch, nq),
        in_specs=[pl.BlockSpec((tq, NSA_WIDTH), lambda b, i: (b * nq + i, 0)),
                  cmp_spec(0), cmp_spec(1),
                  pl.BlockSpec((seq, KV_WIDTH), lambda b, i: (b, 2)),
                  pl.BlockSpec((seq, KV_WIDTH), lambda b, i: (b, 3)),
                  pl.BlockSpec((seq, KV_WIDTH), lambda b, i: (b, 0)),
                  pl.BlockSpec((seq, KV_WIDTH), lambda b, i: (b, 1)),
                  pl.BlockSpec((tq, LANES), lambda b, i: (b * nq + i, 0)),
                  pl.BlockSpec((tq, NSA_WIDTH), lambda b, i: (b * nq + i, 0)),
                  pl.BlockSpec((LANES, seq), lambda b, i: (0, 0))],
        out_specs=pl.BlockSpec((tq, NSA_WIDTH), lambda b, i: (b * nq + i, 0)),
        out_shape=jax.ShapeDtypeStruct((batch * seq, NSA_WIDTH), BF16),
        scratch_shapes=[pltpu.VMEM((NSA_KV_HEADS, rows, LANES), F32),
                        pltpu.VMEM((NSA_KV_HEADS, rows, 2 * HEAD_DIM), F32),
                        pltpu.VMEM((NSA_KV_HEADS, rows, LANES), F32),
                        pltpu.VMEM((NSA_KV_HEADS, rows, tk), BF16),
                        pltpu.VMEM((NSA_KV_HEADS, rows, tw), BF16),
                        pltpu.VMEM((NSA_KV_HEADS, rows, HEAD_DIM), F32),
                        pltpu.VMEM((NSA_KV_HEADS, tq, seq), F32)],
        compiler_params=_params(("parallel", "arbitrary")),
        name="nsa_prompt",
    )(proj["q_n"], kcv, kcv, proj["kv4_bf"], proj["kv4_bf"], proj["kvwin_bf"], proj["kvwin_bf"],
      proj["small"], proj["z_n"], expand)


def _fox_prompt_kernel(q_ref, k_ref, v_ref, d_ref, z_ref, o_ref, m_ref, acc_ref, alpha_ref, p_ref, *,
                       t, chunk, n_heads):
    qi = pl.program_id(2)
    _state_init(m_ref, acc_ref)

    def tile(i, causal):
        k0 = pl.multiple_of(i * t, t)
        for hh in range(n_heads):
            cols = slice(hh * HEAD_DIM, (hh + 1) * HEAD_DIM)
            s = _dot_nt(q_ref[:, cols], k_ref[pl.ds(k0, t), cols])
            d_row = d_ref[hh, :, pl.ds(k0, t)] * LOG2E

            def logits_of(c, s=s, d_row=d_row):
                lg = s[c * chunk:(c + 1) * chunk] * (SCALE * LOG2E) - d_row
                if causal:
                    row = c * chunk + lax.broadcasted_iota(jnp.int32, (chunk, t), 0)
                    col = lax.broadcasted_iota(jnp.int32, (chunk, t), 1)
                    lg = jnp.where(col <= row, lg, NEG_INF)
                return lg

            _state_update_chunked(logits_of, t // chunk, chunk, _with_ones(v_ref[pl.ds(k0, t), cols]),
                                  m_ref.at[hh], acc_ref.at[hh], p_ref.at[hh], alpha_ref.at[hh], False)

    def body(i, carry):
        tile(i, False)
        return carry

    lax.fori_loop(0, qi, body, 0)
    tile(qi, True)
    for hh in range(n_heads):
        cols = slice(hh * HEAD_DIM, (hh + 1) * HEAD_DIM)
        o_ref[:, cols] = (_state_result(acc_ref.at[hh]) * _silu(z_ref[:, cols])).astype(o_ref.dtype)


def _fox_prompt(proj, d_rows, batch, seq):
    t, chunk, n_heads = 512, 128, 4
    nq = seq // t
    width = n_heads * HEAD_DIM
    n_hg = FOX_HEADS // n_heads
    return pl.pallas_call(
        functools.partial(_fox_prompt_kernel, t=t, chunk=chunk, n_heads=n_heads),
        grid=(batch, n_hg, nq),
        in_specs=[pl.BlockSpec((t, width), lambda b, h, i: (b * nq + i, h)),
                  pl.BlockSpec((seq, width), lambda b, h, i: (b, h)),
                  pl.BlockSpec((seq, width), lambda b, h, i: (b, n_hg + h)),
                  pl.BlockSpec((n_heads, 1, seq), lambda b, h, i: (b * n_hg + h, 0, 0)),
                  pl.BlockSpec((t, width), lambda b, h, i: (b * nq + i, h))],
        out_specs=pl.BlockSpec((t, width), lambda b, h, i: (b * nq + i, h)),
        out_shape=jax.ShapeDtypeStruct((batch * seq, FOX_WIDTH), BF16),
        scratch_shapes=[pltpu.VMEM((n_heads, t, LANES), F32), pltpu.VMEM((n_heads, t, 2 * HEAD_DIM), F32),
                        pltpu.VMEM((n_heads, t, LANES), F32), pltpu.VMEM((n_heads, t, t), BF16)],
        compiler_params=_params(("parallel", "parallel", "arbitrary")),
        name="fox_prompt",
    )(proj["q_f"], proj["kv_f_bf"], proj["kv_f_bf"], d_rows, proj["z_f"])


def _prompt_layer(x, norm_in, wts, cmp_pe, cmp_w_bf, w_out_bf, norm_final):
    batch, seq, d = x.shape
    x2d = x.reshape(batch * seq, d)
    proj = _project_all(x2d, norm_in, wts, 1024)
    n_gate = 3 * NSA_HEADS
    logf = proj["small"][:, n_gate:n_gate + FOX_HEADS].reshape(batch, seq, FOX_HEADS)
    d_rows = _cumsum_rows(jnp.swapaxes(logf, 1, 2).reshape(batch * FOX_HEADS, seq))
    kcv = _compress_prompt(proj["kv4"], cmp_pe, cmp_w_bf, batch, seq)
    blk_of_key = jnp.arange(seq, dtype=jnp.int32) // SEL_BLOCK
    expand = (jnp.arange(LANES, dtype=jnp.int32)[:, None] == blk_of_key[None, :]).astype(BF16)
    mix_n = _nsa_prompt(proj, kcv, expand, batch, seq)
    mix_f = _fox_prompt(proj, d_rows.reshape(batch * FOX_HEADS, 1, seq), batch, seq)
    y = _merge(x2d, mix_n, mix_f, w_out_bf, norm_final, 512).reshape(batch, seq, d)
    w_keep = min(WINDOW, seq)
    state = (proj["kv4"].reshape(batch, seq, 4, NSA_KV_HEADS, HEAD_DIM),
             proj["kv_f"].reshape(batch, seq, 2, FOX_HEADS, HEAD_DIM),
             logf,
             proj["kvwin"].reshape(batch, seq, 2, NSA_KV_HEADS, HEAD_DIM)[:, seq - w_keep:])
    return y, state


def _nsa_page_specs(pg):
    def spec(u):
        return pl.BlockSpec((1, 1, PAGE_SIZE * NSA_CHANNELS, HEAD_DIM),
                            lambda b, j, pt: (0, pt[b, j * pg + u], 0, 0))
    return [spec(u) for u in range(pg)]


def _channel(page_ref, ch, start=0, n=PAGE_SIZE):
    return page_ref[0, 0, pl.ds(start * NSA_CHANNELS + ch, n, stride=NSA_CHANNELS), :]


def _nsa_cmp_kernel(pt_ref, *refs, pg, n_pages):
    page_refs = refs[:pg]
    q_ref, pe_ref, w_ref, oc_ref, sel_ref, stage_ref, xs_ref = refs[pg:]
    jg = pl.program_id(1)
    past = n_pages * PAGE_SIZE
    nb = past // CMP_BLOCK
    half = nb // 2
    blocks_per_page = PAGE_SIZE // CMP_BLOCK
    for u in range(pg):
        for bl in range(blocks_per_page):
            m = (jg * pg + u) * (blocks_per_page // 2) + bl // 2
            dst = pl.multiple_of(((bl % 2) * half + m) * STAGE_PITCH, SUBLANES)
            for ch in range(2 * NSA_KV_HEADS):
                stage_ref[ch, pl.ds(dst, CMP_BLOCK), :] = _channel(page_refs[u], ch, bl * CMP_BLOCK, CMP_BLOCK)

    @pl.when(jg == pl.num_programs(1) - 1)
    def _():
        cmp = [[_compress_rows(stage_ref.at[kind * NSA_KV_HEADS + h], pe_ref, kind, w_ref[kind], xs_ref, nb,
                               (0, half * STAGE_PITCH), STAGE_PITCH).astype(BF16)
                for h in range(NSA_KV_HEADS)] for kind in range(2)]
        rows = NSA_GROUP * T_PAD
        for hkv in range(NSA_KV_HEADS):
            q = q_ref[0, hkv * rows:(hkv + 1) * rows, :].astype(BF16)
            kc, vc = cmp[0][hkv], cmp[1][hkv]
            s = _dot_nt(q, kc) * SCALE
            lane = lax.broadcasted_iota(jnp.int32, (rows, nb), 1)
            blk = jnp.where(lane < half, 2 * lane, 2 * lane - (nb - 1))
            row = lax.broadcasted_iota(jnp.int32, (rows, nb), 0)
            slope = _head_slope(hkv * NSA_GROUP + row // T_PAD)
            dist = (past + row % T_PAD - ((blk + 1) * CMP_BLOCK - 1)).astype(F32)
            lg = jnp.where(dist >= 0, s - slope * dist, NEG_INF)
            mx = jnp.max(lg, axis=1, keepdims=True)
            mx = jnp.where(mx == NEG_INF, 0.0, mx)
            p = jnp.exp(lg - mx)
            p = p / jnp.maximum(jnp.sum(p, axis=1, keepdims=True), 1e-30)
            oc_ref[0, hkv * rows:(hkv + 1) * rows, :] = _dot(p.astype(BF16), vc)
            score = p[0:T_PAD]
            for g in range(1, NSA_GROUP):
                score = score + p[g * T_PAD:(g + 1) * T_PAD]
            score = score[:, :half] + score[:, half:]
            blk_s = lax.broadcasted_iota(jnp.int32, (T_PAD, half), 1).astype(F32)
            forced = (blk_s == 0) | (blk_s == half - 1)
            s_left = jnp.where(forced, NEG_INF, score)
            picked = jnp.zeros((T_PAD, half), F32)
            for _ in range(SEL_TOPK - 3):
                best = jnp.max(s_left, axis=1, keepdims=True)
                first = jnp.min(jnp.where(s_left == best, blk_s, float(half)), axis=1, keepdims=True)
                hit = blk_s == first
                picked = jnp.where(hit, 1.0, picked)
                s_left = jnp.where(hit, NEG_INF, s_left)
            sel_ref[0, hkv * T_PAD:(hkv + 1) * T_PAD, :] = jnp.where(forced, 1.0, picked)


def _nsa_cmp_sample(cache, page_table, q_rows, cmp_pe, cmp_w_bf, pg):
    n_seq, n_pages = page_table.shape
    past = n_pages * PAGE_SIZE
    nb = past // CMP_BLOCK
    assert n_pages % pg == 0 and past // SEL_BLOCK >= SEL_TOPK
    rows = NSA_HEADS * T_PAD
    grid_spec = pltpu.PrefetchScalarGridSpec(
        num_scalar_prefetch=1,
        grid=(n_seq, n_pages // pg),
        in_specs=_nsa_page_specs(pg) + [
            pl.BlockSpec((1, rows, HEAD_DIM), lambda b, j, pt: (b, 0, 0)),
            pl.BlockSpec((2, CMP_BLOCK, HEAD_DIM), lambda b, j, pt: (0, 0, 0)),
            pl.BlockSpec((2, CMP_BLOCK * HEAD_DIM, HEAD_DIM), lambda b, j, pt: (0, 0, 0))],
        out_specs=[pl.BlockSpec((1, rows, HEAD_DIM), lambda b, j, pt: (b, 0, 0)),
                   pl.BlockSpec((1, NSA_KV_HEADS * T_PAD, past // SEL_BLOCK), lambda b, j, pt: (b, 0, 0))],
        scratch_shapes=[pltpu.VMEM((2 * NSA_KV_HEADS, nb * STAGE_PITCH, HEAD_DIM), F32),
                        pltpu.VMEM((nb, CMP_BLOCK * HEAD_DIM), BF16)],
    )
    return pl.pallas_call(
        functools.partial(_nsa_cmp_kernel, pg=pg, n_pages=n_pages),
        grid_spec=grid_spec,
        out_shape=[jax.ShapeDtypeStruct((n_seq, rows, HEAD_DIM), F32),
                   jax.ShapeDtypeStruct((n_seq, NSA_KV_HEADS * T_PAD, past // SEL_BLOCK), F32)],
        compiler_params=_params(("parallel", "arbitrary")),
        name="nsa_sample_cmp",
    )(page_table, *([cache] * pg), q_rows, cmp_pe, cmp_w_bf)


def _nsa_slc_kernel(pt_ref, *refs, pg, part_pages, n_pages, w_buf, n_new):
    page_refs = refs[:pg]
    (q_ref, sel_ref, e_ref, new_ref, win_ref, oc_ref, g_ref, z_ref, o_ref, tab_ref, m_ref, acc_ref) = refs[pg:]
    jg = pl.program_id(1)
    past = n_pages * PAGE_SIZE
    hr = n_new * NSA_GROUP
    rows = NSA_KV_HEADS * hr
    row = lax.broadcasted_iota(jnp.int32, (rows, 1), 0)
    t_q = (row % hr) // NSA_GROUP
    slope = _head_slope((row // hr) * NSA_GROUP + row % NSA_GROUP)
    q = [q_ref[0, h * hr:(h + 1) * hr, :].astype(BF16) for h in range(NSA_KV_HEADS)]

    def scores(k_of):
        return jnp.concatenate([_dot_nt(q[h], k_of(h)) for h in range(NSA_KV_HEADS)], axis=0) * SCALE

    def pv(v_of):
        return lambda p: jnp.concatenate(
            [_dot(p[h * hr:(h + 1) * hr], _with_ones(v_of(h))) for h in range(NSA_KV_HEADS)], axis=0)


    @pl.when(jg == 0)
    def _():
        expanded = _dot(sel_ref[0].astype(BF16), e_ref[...])
        key = lax.broadcasted_iota(jnp.int32, (rows, past), 1)
        tab_ref[...] = jnp.where(expanded > 0.5, 0.0, NEG_INF) + slope * (key - past).astype(F32)
        _state_init(m_ref, acc_ref)

    def page_stream(ch, pages):
        return jnp.concatenate([_channel(page_refs[u], ch).astype(BF16) for u in pages], axis=0)

    parts = []
    for first in range(0, pg, part_pages):
        pages = range(first, first + part_pages)
        c0 = pl.multiple_of((jg * pg + first) * PAGE_SIZE, part_pages * PAGE_SIZE)
        lg = (scores(lambda h: page_stream(2 * NSA_KV_HEADS + h, pages))
              + tab_ref[:, pl.ds(c0, part_pages * PAGE_SIZE)])
        parts.append((lg, pv(lambda h, pages=pages: page_stream(3 * NSA_KV_HEADS + h, pages))))
    _state_update_parts(parts, m_ref, acc_ref, True)

    @pl.when(jg == pl.num_programs(1) - 1)
    def _():
        new = lambda ch: new_ref[0, ch].astype(BF16)
        j_new = lax.broadcasted_iota(jnp.int32, (rows, PAGE_SIZE), 1)
        bias_new = jnp.where((j_new <= t_q) & (j_new < n_new), slope * j_new.astype(F32), NEG_INF)
        _state_update(scores(lambda h: new(h)) + bias_new, pv(lambda h: new(NSA_KV_HEADS + h)),
                      m_ref, acc_ref, False)
        o_s = _state_result(acc_ref)

        _state_init(m_ref, acc_ref)
        i_w = lax.broadcasted_iota(jnp.int32, (rows, w_buf), 1)
        dist_w = t_q + w_buf - i_w
        valid_w = (dist_w >= 0) & (dist_w < WINDOW) & (past - w_buf + i_w >= 0)
        bias_w = jnp.where(valid_w, slope * (i_w - w_buf).astype(F32), NEG_INF)
        win = lambda ch: win_ref[0, ch].astype(BF16)
        _state_update(scores(lambda h: win(h)) + bias_w, pv(lambda h: win(NSA_KV_HEADS + h)),
                      m_ref, acc_ref, True)
        _state_update(scores(lambda h: new(2 * NSA_KV_HEADS + h)) + bias_new,
                      pv(lambda h: new(3 * NSA_KV_HEADS + h)), m_ref, acc_ref, True)
        o_w = _state_result(acc_ref)

        gates = g_ref[0]
        mix = gates[:, 0:1] * oc_ref[0] + gates[:, 1:2] * o_s + gates[:, 2:3] * o_w
        o_ref[0] = mix * _silu(z_ref[0])


def _nsa_slc_sample(cache, page_table, q_rows, sel_rows, expand, new_rows, win_state, o_c_rows, gates_rows, z_rows,
                    pg, n_new):
    n_seq, n_pages = page_table.shape
    past = n_pages * PAGE_SIZE
    w_buf = win_state.shape[2]
    rows = n_new * NSA_HEADS
    per_seq = lambda *shape: pl.BlockSpec((1,) + shape, lambda b, j, pt: (b,) + (0,) * len(shape))
    grid_spec = pltpu.PrefetchScalarGridSpec(
        num_scalar_prefetch=1,
        grid=(n_seq, n_pages // pg),
        in_specs=_nsa_page_specs(pg) + [
            per_seq(rows, HEAD_DIM),
            per_seq(rows, past // SEL_BLOCK),
            pl.BlockSpec((past // SEL_BLOCK, past), lambda b, j, pt: (0, 0)),
            per_seq(4 * NSA_KV_HEADS, PAGE_SIZE, HEAD_DIM),
            per_seq(2 * NSA_KV_HEADS, w_buf, HEAD_DIM),
            per_seq(rows, HEAD_DIM),
            per_seq(rows, 3),
            per_seq(rows, HEAD_DIM)],
        out_specs=per_seq(rows, HEAD_DIM),
        scratch_shapes=[pltpu.VMEM((rows, past), F32), pltpu.VMEM((rows, LANES), F32),
                        pltpu.VMEM((rows, 2 * HEAD_DIM), F32)],
    )
    return pl.pallas_call(
        functools.partial(_nsa_slc_kernel, pg=pg, part_pages=2, n_pages=n_pages, w_buf=w_buf, n_new=n_new),
        grid_spec=grid_spec,
        out_shape=jax.ShapeDtypeStruct((n_seq, rows, HEAD_DIM), F32),
        compiler_params=_params(("parallel", "arbitrary")),
        name="nsa_sample_slc_win",
    )(page_table, *([cache] * pg), q_rows, sel_rows, expand, new_rows, win_state, o_c_rows, gates_rows, z_rows)


def _page_forget_scan(x):
    n = x.shape[1]
    lane = lax.broadcasted_iota(jnp.int32, x.shape, 1)
    s = FOX_HEADS
    while s < n:
        x = x + jnp.where(lane >= s, pltpu.roll(x, s, axis=1), 0.0)
        s *= 2
    total = jnp.where(lane >= n - FOX_HEADS, x, 0.0)
    s = FOX_HEADS
    while s < n:
        total = total + pltpu.roll(total, n - s, axis=1)
        s *= 2
    return x, total


def _fox_sample_kernel(pt_ref, *refs, pg, part_pages, n_new):
    page_refs = refs[:pg]
    (q_ref, lc_ref, kn_ref, vn_ref, ln_ref, z_ref, o_ref, m_ref, acc_ref, carry_ref, lg_ref) = refs[pg:]
    b = pl.program_id(0)
    jg = pl.program_id(1)
    rows = n_new * FOX_HEADS
    keys = PAGE_SIZE * FOX_HEADS

    @pl.when(jg == 0)
    def _():
        _state_init(m_ref, acc_ref)
        carry_ref[...] = jnp.zeros(carry_ref.shape, F32)

    q = q_ref[0].astype(BF16)
    own = (lax.broadcasted_iota(jnp.int32, (rows, keys), 1) % FOX_HEADS
           == lax.broadcasted_iota(jnp.int32, (rows, keys), 0) % FOX_HEADS)

    for u in range(pg):
        lg_ref[u:u + 1, :] = lc_ref[pl.ds(pt_ref[b, jg * pg + u], 1), :]
    within, total = _page_forget_scan(lg_ref[...])
    carry = carry_ref[...]
    logits = []
    for u in range(pg):
        d_u = within[u:u + 1] + carry
        carry = carry + total[u:u + 1]
        k = page_refs[u][0, 0, :, 0].reshape(keys, HEAD_DIM).astype(BF16)
        logits.append(jnp.where(own, _dot_nt(q, k) * SCALE - d_u, NEG_INF))
    carry_ref[...] = carry

    def pv(pages):
        def pv_of(p):
            out = None
            for i, u in enumerate(pages):
                v = page_refs[u][0, 0, :, 1].reshape(keys, HEAD_DIM).astype(BF16)
                part = _dot(p[:, i * keys:(i + 1) * keys], _with_ones(v))
                out = part if out is None else out + part
            return out
        return pv_of

    parts = []
    for first in range(0, pg, part_pages):
        pages = range(first, first + part_pages)
        parts.append((jnp.concatenate([logits[u] for u in pages], axis=1), pv(pages)))
    _state_update_parts(parts, m_ref, acc_ref, False)

    @pl.when(jg == pl.num_programs(1) - 1)
    def _():
        within_new, _ = _page_forget_scan(ln_ref[0])
        d_new = (within_new + carry_ref[...])[:, :PAGE_SIZE]
        c = lax.broadcasted_iota(jnp.int32, (rows, PAGE_SIZE), 1)
        r = lax.broadcasted_iota(jnp.int32, (rows, PAGE_SIZE), 0)
        valid = (c < rows) & (c % FOX_HEADS == r % FOX_HEADS) & (c // FOX_HEADS <= r // FOX_HEADS)
        lg = jnp.where(valid, _dot_nt(q, kn_ref[0].astype(BF16)) * SCALE - d_new, NEG_INF)
        _state_update(lg, lambda p: _dot(p, _with_ones(vn_ref[0].astype(BF16))), m_ref, acc_ref, False)
        o_ref[0] = _state_result(acc_ref) * _silu(z_ref[0])


def _fox_sample(cache, page_table, q_rows, logf_cache, k_new, v_new, logf_new, z_rows, pg, n_new):
    n_seq, n_pages = page_table.shape
    n_pool = logf_cache.shape[0]
    rows = n_new * FOX_HEADS
    keys = PAGE_SIZE * FOX_HEADS
    per_seq = lambda *shape: pl.BlockSpec((1,) + shape, lambda b, j, pt: (b,) + (0,) * len(shape))

    def page_spec(u):
        return pl.BlockSpec((1, 1, PAGE_SIZE, 2, FOX_HEADS, HEAD_DIM),
                            lambda b, j, pt: (0, pt[b, j * pg + u], 0, 0, 0, 0))

    grid_spec = pltpu.PrefetchScalarGridSpec(
        num_scalar_prefetch=1,
        grid=(n_seq, n_pages // pg),
        in_specs=[page_spec(u) for u in range(pg)] + [
            per_seq(rows, HEAD_DIM),
            pl.BlockSpec((n_pool, keys), lambda b, j, pt: (0, 0)),
            per_seq(PAGE_SIZE, HEAD_DIM),
            per_seq(PAGE_SIZE, HEAD_DIM),
            per_seq(1, keys),
            per_seq(rows, HEAD_DIM)],
        out_specs=per_seq(rows, HEAD_DIM),
        scratch_shapes=[pltpu.VMEM((rows, LANES), F32), pltpu.VMEM((rows, 2 * HEAD_DIM), F32),
                        pltpu.VMEM((1, keys), F32), pltpu.VMEM((pg, keys), F32)],
    )
    return pl.pallas_call(
        functools.partial(_fox_sample_kernel, pg=pg, part_pages=2, n_new=n_new),
        grid_spec=grid_spec,
        out_shape=jax.ShapeDtypeStruct((n_seq, rows, HEAD_DIM), F32),
        compiler_params=_params(("parallel", "arbitrary")),
        name="fox_sample",
    )(page_table, *([cache] * pg), q_rows, logf_cache, k_new, v_new, logf_new, z_rows)


def _pad_axis(a, axis, size):
    pad = [(0, 0)] * a.ndim
    pad[axis] = (0, size - a.shape[axis])
    return jnp.pad(a, pad)


def _sample_layer(x, cache_nsa_kv, cache_fox_kv, cache_fox_logf, state_nsa_win, page_table,
                  norm_in, wts, cmp_pe, cmp_w_bf, w_out_bf, norm_final):
    n_seq, t_new, d = x.shape
    n_pool = cache_nsa_kv.shape[1]
    n_pages = page_table.shape[1]
    past = n_pages * PAGE_SIZE
    w_buf = state_nsa_win.shape[2]
    assert t_new <= T_PAD and t_new < SEL_BLOCK and past % SEL_BLOCK == 0
    x2d = x.reshape(n_seq * t_new, d)
    proj = _project_all(x2d, norm_in, wts, n_seq * t_new)
    n_gate = 3 * NSA_HEADS
    small = proj["small"].reshape(n_seq, t_new, LANES)
    logf = small[:, :, n_gate:n_gate + FOX_HEADS]

    def head_rows(a):
        a = _pad_axis(a.reshape(n_seq, t_new, NSA_HEADS, -1), 1, T_PAD)
        return jnp.swapaxes(a, 1, 2).reshape(n_seq, NSA_HEADS * T_PAD, a.shape[-1])

    def group_rows(a):
        a = a.reshape(n_seq, t_new, NSA_KV_HEADS, NSA_GROUP, a.shape[-1])
        return jnp.swapaxes(a, 1, 2).reshape(n_seq, NSA_HEADS * t_new, a.shape[-1])

    q_n = proj["q_n"].astype(F32)
    gates = small[:, :, :n_gate].reshape(n_seq, t_new, NSA_HEADS, 3)
    z_n = proj["z_n"].reshape(n_seq, t_new, NSA_HEADS, HEAD_DIM)
    nsa_new = jnp.concatenate([proj["kv4"].reshape(n_seq, t_new, 4 * NSA_KV_HEADS, HEAD_DIM)[:, :, 2 * NSA_KV_HEADS:],
                               proj["kvwin"].reshape(n_seq, t_new, 2 * NSA_KV_HEADS, HEAD_DIM)], axis=2)
    nsa_new = _pad_axis(jnp.swapaxes(nsa_new, 1, 2), 2, PAGE_SIZE)
    win_state = jnp.swapaxes(state_nsa_win[0].reshape(n_seq, w_buf, 2 * NSA_KV_HEADS, HEAD_DIM), 1, 2)
    blk_of_key = jnp.arange(past, dtype=jnp.int32) // SEL_BLOCK
    expand = (jnp.arange(past // SEL_BLOCK, dtype=jnp.int32)[:, None] == blk_of_key[None, :]).astype(BF16)
    nsa_cache = cache_nsa_kv.reshape(cache_nsa_kv.shape[0], n_pool, PAGE_SIZE * NSA_CHANNELS, HEAD_DIM)

    pg = 8
    o_c, sel = _nsa_cmp_sample(nsa_cache, page_table, head_rows(q_n), cmp_pe, cmp_w_bf, pg)
    o_c = jnp.swapaxes(o_c.reshape(n_seq, NSA_HEADS, T_PAD, HEAD_DIM)[:, :, :t_new], 1, 2)
    sel = sel.reshape(n_seq, NSA_KV_HEADS, T_PAD, past // SEL_BLOCK)[:, :, :t_new]
    sel = jnp.broadcast_to(sel[:, :, :, None, :], (n_seq, NSA_KV_HEADS, t_new, NSA_GROUP, past // SEL_BLOCK))
    mix_n = _nsa_slc_sample(nsa_cache, page_table, group_rows(q_n.reshape(n_seq, t_new, NSA_HEADS, HEAD_DIM)),
                            sel.reshape(n_seq, NSA_HEADS * t_new, past // SEL_BLOCK), expand, nsa_new, win_state,
                            group_rows(o_c), group_rows(gates), group_rows(z_n), pg, t_new)
    mix_n = jnp.swapaxes(mix_n.reshape(n_seq, NSA_KV_HEADS, t_new, NSA_GROUP * HEAD_DIM), 1, 2)
    mix_n = mix_n.reshape(n_seq * t_new, NSA_WIDTH).astype(BF16)

    fox_rows = lambda a: a.reshape(n_seq, t_new * FOX_HEADS, HEAD_DIM)
    kv_f = proj["kv_f"].reshape(n_seq, t_new, 2, FOX_HEADS * HEAD_DIM)
    k_new = _pad_axis(fox_rows(kv_f[:, :, 0]), 1, PAGE_SIZE)
    v_new = _pad_axis(fox_rows(kv_f[:, :, 1]), 1, PAGE_SIZE)
    logf_new = _pad_axis(logf.reshape(n_seq, 1, t_new * FOX_HEADS), 2, PAGE_SIZE * FOX_HEADS)
    logf_cache = cache_fox_logf[0].reshape(n_pool, PAGE_SIZE * FOX_HEADS)
    mix_f = _fox_sample(cache_fox_kv, page_table, fox_rows(proj["q_f"].astype(F32)), logf_cache, k_new, v_new,
                        logf_new, fox_rows(proj["z_f"]), pg, t_new)
    mix_f = mix_f.reshape(n_seq * t_new, FOX_WIDTH).astype(BF16)

    y = _merge(x2d, mix_n, mix_f, w_out_bf, norm_final, n_seq * t_new).reshape(n_seq, t_new, d)
    kvwin_new = proj["kvwin"].reshape(n_seq, t_new, 2, NSA_KV_HEADS, HEAD_DIM)
    state = (proj["kv4"].reshape(n_seq, t_new, 4, NSA_KV_HEADS, HEAD_DIM),
             proj["kv_f"].reshape(n_seq, t_new, 2, FOX_HEADS, HEAD_DIM),
             logf,
             jnp.concatenate([state_nsa_win[0], kvwin_new], axis=1)[:, t_new:])
    return y, state


def kernel(x_prompt, x_sample, cache_nsa_kv, cache_fox_kv, cache_fox_logf, state_nsa_win, page_table,
           norm_in, w_in, b_gate, b_forget, cmp_pe, cmp_w, w_out, norm_final):
    assert norm_in.shape[0] == 1, "single-layer trunk"
    wts = _split_weights(w_in[0], b_gate[0], b_forget[0])
    cmp_w_bf = cmp_w[0].reshape(2, CMP_BLOCK * HEAD_DIM, HEAD_DIM).astype(BF16)
    w_out_bf = w_out[0].astype(BF16)
    y_p, st_p = _prompt_layer(x_prompt, norm_in[0], wts, cmp_pe[0], cmp_w_bf, w_out_bf, norm_final)
    y_s, st_s = _sample_layer(x_sample, cache_nsa_kv, cache_fox_kv, cache_fox_logf, state_nsa_win,
                              page_table, norm_in[0], wts, cmp_pe[0], cmp_w_bf, w_out_bf, norm_final)
    outs = [y_p, y_s]
    for s_p, s_s in zip(st_p, st_s):
        outs.extend([s_p[None], s_s[None]])
    return tuple(outs)
```

```python
import functools

import jax
import jax.numpy as jnp
from jax import lax
from jax.experimental import pallas as pl
from jax.experimental.pallas import tpu as pltpu

F32 = jnp.float32
BF16 = jnp.bfloat16
NEG_INF = float("-inf")

HEAD_DIM = 128
NSA_HEADS = 8
FOX_HEADS = 8
NSA_KV_HEADS = 2
NSA_GROUP = NSA_HEADS // NSA_KV_HEADS
NSA_WIDTH = NSA_HEADS * HEAD_DIM
FOX_WIDTH = FOX_HEADS * HEAD_DIM
KV_WIDTH = NSA_KV_HEADS * HEAD_DIM
NSA_CHANNELS = 4 * NSA_KV_HEADS
CMP_BLOCK = 32
SEL_BLOCK = 64
SEL_TOPK = 16
WINDOW = 512
PAGE_SIZE = 128
FORCED_SCORE = 1e4
RMS_EPS = 1e-6
SCALE = HEAD_DIM ** -0.5
LOG2E = 1.4426950408889634
T_PAD = 8
LANES = 128
SUBLANES = 8
PROJ_TN = 512
STAGE_PITCH = 40
VMEM_LIMIT = 56 * 1024 * 1024


def _params(sem):
    return pltpu.CompilerParams(dimension_semantics=sem, vmem_limit_bytes=VMEM_LIMIT)


def _dot(a, b):
    return jnp.dot(a, b, preferred_element_type=F32)


def _dot_nt(a, b):
    return lax.dot_general(a, b, (((1,), (1,)), ((), ())), preferred_element_type=F32)


def _sigmoid(x):
    return 1.0 / (1.0 + jnp.exp(-x))


def _silu(x):
    return x * _sigmoid(x)


def _tile_lanes(x, n):
    return x if n == 1 else jnp.concatenate([x] * n, axis=1)


def _with_ones(v):
    return jnp.concatenate([v, jnp.ones(v.shape, v.dtype)], axis=1)


def _state_init(m_ref, acc_ref):
    m_ref[...] = jnp.full(m_ref.shape, NEG_INF, F32)
    acc_ref[...] = jnp.zeros(acc_ref.shape, F32)


def _probs(lg, m_old, may_be_empty, exp=jnp.exp):
    m_new = jnp.maximum(m_old, jnp.max(lg, axis=1, keepdims=True))
    m_use = jnp.where(m_new == NEG_INF, 0.0, m_new) if may_be_empty else m_new
    alpha = exp(m_old - m_use)
    p = exp(lg - _tile_lanes(m_use, lg.shape[1] // LANES)).astype(BF16)
    return m_new, alpha, p


def _state_update_chunked(logits_of, n_chunks, chunk, v_ext, m_ref, acc_ref, p_ref, alpha_ref, may_be_empty):
    for c in range(n_chunks):
        rows = slice(c * chunk, (c + 1) * chunk)
        m_new, alpha, p = _probs(logits_of(c), m_ref[rows, :], may_be_empty, jnp.exp2)
        alpha_ref[rows, :] = alpha
        p_ref[rows, :] = p
        m_ref[rows, :] = m_new
    acc_ref[...] = _tile_lanes(alpha_ref[...], 2) * acc_ref[...] + _dot(p_ref[...], v_ext)


def _state_update(lg, pv_of, m_ref, acc_ref, may_be_empty):
    m_new, alpha, p = _probs(lg, m_ref[...], may_be_empty)
    acc_ref[...] = _tile_lanes(alpha, 2) * acc_ref[...] + pv_of(p)
    m_ref[...] = m_new


def _state_update_parts(parts, m_ref, acc_ref, may_be_empty):
    local = []
    for lg, pv_of in parts:
        m_c = jnp.broadcast_to(jnp.max(lg, axis=1, keepdims=True), m_ref.shape)
        m_use = jnp.where(m_c == NEG_INF, 0.0, m_c) if may_be_empty else m_c
        p = jnp.exp(lg - _tile_lanes(m_use, lg.shape[1] // LANES)).astype(BF16)
        local.append((m_c, pv_of(p)))
    m_old = m_ref[...]
    m_new = m_old
    for m_c, _ in local:
        m_new = jnp.maximum(m_new, m_c)
    acc = _tile_lanes(jnp.exp(m_old - m_new), 2) * acc_ref[...]
    for m_c, pv in local:
        acc = acc + _tile_lanes(jnp.exp(m_c - m_new), 2) * pv
    acc_ref[...] = acc
    m_ref[...] = m_new


def _state_result(acc_ref):
    acc = acc_ref[...]
    return acc[:, :HEAD_DIM] / jnp.maximum(acc[:, HEAD_DIM:], 1e-30)


def _lane_cumsum(x):
    n = x.shape[-1]
    lane = lax.broadcasted_iota(jnp.int32, x.shape, x.ndim - 1)
    s = 1
    while s < n:
        x = x + jnp.where(lane >= s, pltpu.roll(x, s, axis=x.ndim - 1), 0.0)
        s *= 2
    return x


def _head_slope(head):
    if isinstance(head, int):
        return 2.0 ** -(head + 1)
    return lax.bitcast_convert_type((126 - head) << 23, F32)


def _rms_kernel(x_ref, g_ref, o_ref):
    x = x_ref[...]
    ms = jnp.mean(x * x, axis=-1, keepdims=True)
    o_ref[...] = (x * lax.rsqrt(ms + RMS_EPS) * g_ref[...]).astype(o_ref.dtype)


def _rmsnorm(x2d, g, out_dtype, tm):
    m, d = x2d.shape
    return pl.pallas_call(
        _rms_kernel,
        grid=(m // tm,),
        in_specs=[pl.BlockSpec((tm, d), lambda i: (i, 0)), pl.BlockSpec((1, d), lambda i: (0, 0))],
        out_specs=pl.BlockSpec((tm, d), lambda i: (i, 0)),
        out_shape=jax.ShapeDtypeStruct((m, d), out_dtype),
        compiler_params=_params(("parallel",)),
        name="rmsnorm",
    )(x2d, g.reshape(1, d))


def _proj_kernel(h_ref, w_ref, *o_refs):
    acc = _dot(h_ref[...], w_ref[...])
    for o_ref in o_refs:
        o_ref[...] = acc.astype(o_ref.dtype)


def _project(h, w, col0, n, out_dtypes, tm, name):
    m, d = h.shape
    j0 = col0 // PROJ_TN
    return pl.pallas_call(
        _proj_kernel,
        grid=(m // tm, n // PROJ_TN),
        in_specs=[pl.BlockSpec((tm, d), lambda i, j: (i, 0)), pl.BlockSpec((d, PROJ_TN), lambda i, j: (0, j0 + j))],
        out_specs=[pl.BlockSpec((tm, PROJ_TN), lambda i, j: (i, j)) for _ in out_dtypes],
        out_shape=[jax.ShapeDtypeStruct((m, n), dt) for dt in out_dtypes],
        compiler_params=_params(("parallel", "arbitrary")),
        name=name,
    )(h, w)


def _small_kernel(h_ref, w_ref, b_ref, o_ref, *, n_gate):
    z = _dot(h_ref[...], w_ref[...]) + b_ref[...]
    lane = lax.broadcasted_iota(jnp.int32, z.shape, 1)
    log_sig = jnp.minimum(z, 0.0) - jnp.log1p(jnp.exp(-jnp.abs(z)))
    o_ref[...] = jnp.where(lane < n_gate, _sigmoid(z), log_sig)


def _project_small(h, w, b, tm, n_gate):
    m, d = h.shape
    return pl.pallas_call(
        functools.partial(_small_kernel, n_gate=n_gate),
        grid=(m // tm,),
        in_specs=[pl.BlockSpec((tm, d), lambda i: (i, 0)), pl.BlockSpec((d, LANES), lambda i: (0, 0)),
                  pl.BlockSpec((1, LANES), lambda i: (0, 0))],
        out_specs=pl.BlockSpec((tm, LANES), lambda i: (i, 0)),
        out_shape=jax.ShapeDtypeStruct((m, LANES), F32),
        compiler_params=_params(("parallel",)),
        name="proj_gates_logf",
    )(h, w, b)


PROJ_GROUPS = (("q_n", NSA_WIDTH), ("kv4", 4 * KV_WIDTH), ("kvwin", 2 * KV_WIDTH), ("z_n", NSA_WIDTH),
               ("q_f", FOX_WIDTH), ("kv_f", 2 * FOX_WIDTH), ("z_f", FOX_WIDTH))
PROJ_OUT_DTYPES = {"q_n": (BF16,), "kv4": (F32, BF16), "kvwin": (F32, BF16), "z_n": (F32,),
                   "q_f": (BF16,), "kv_f": (F32, BF16), "z_f": (F32,)}


def _split_weights(w_in, b_gate, b_forget):
    cuts = [NSA_WIDTH, 6 * KV_WIDTH, 3 * NSA_HEADS, NSA_WIDTH, FOX_WIDTH, FOX_WIDTH, FOX_WIDTH, FOX_HEADS, FOX_WIDTH]
    offs = [0]
    for c in cuts:
        offs.append(offs[-1] + c)
    main = jnp.concatenate([w_in[:, offs[0]:offs[2]], w_in[:, offs[3]:offs[7]], w_in[:, offs[8]:offs[9]]],
                           axis=1).astype(BF16)
    n_small = 3 * NSA_HEADS + FOX_HEADS
    w_small = jnp.concatenate([w_in[:, offs[2]:offs[3]], w_in[:, offs[7]:offs[8]]], axis=1)
    w_small = jnp.pad(w_small, ((0, 0), (0, LANES - n_small))).astype(BF16)
    b_small = jnp.pad(jnp.concatenate([b_gate, b_forget]), (0, LANES - n_small)).reshape(1, LANES).astype(F32)
    return {"main": main, "small": w_small, "b_small": b_small}


def _project_all(x2d, norm_in, wts, tm):
    h = _rmsnorm(x2d, norm_in, BF16, min(tm, 512))
    out = {}
    col0 = 0
    for name, width in PROJ_GROUPS:
        res = _project(h, wts["main"], col0, width, PROJ_OUT_DTYPES[name], tm, "proj_" + name)
        out[name] = res[0]
        if len(res) > 1:
            out[name + "_bf"] = res[1]
        col0 += width
    out["small"] = _project_small(h, wts["small"], wts["b_small"], min(tm, 512), 3 * NSA_HEADS)
    return out


def _merge_kernel(x_ref, mn_ref, mf_ref, wn_ref, wf_ref, g_ref, o_ref):
    y = x_ref[...] + _dot(mn_ref[...], wn_ref[...]) + _dot(mf_ref[...], wf_ref[...])
    ms = jnp.mean(y * y, axis=-1, keepdims=True)
    o_ref[...] = y * lax.rsqrt(ms + RMS_EPS) * g_ref[...]


def _merge(x2d, mix_n, mix_f, w_out_bf, norm_final, tm):
    m, d = x2d.shape
    return pl.pallas_call(
        _merge_kernel,
        grid=(m // tm,),
        in_specs=[pl.BlockSpec((tm, d), lambda i: (i, 0)),
                  pl.BlockSpec((tm, NSA_WIDTH), lambda i: (i, 0)),
                  pl.BlockSpec((tm, FOX_WIDTH), lambda i: (i, 0)),
                  pl.BlockSpec((NSA_WIDTH, d), lambda i: (0, 0)),
                  pl.BlockSpec((FOX_WIDTH, d), lambda i: (NSA_WIDTH // FOX_WIDTH, 0)),
                  pl.BlockSpec((1, d), lambda i: (0, 0))],
        out_specs=pl.BlockSpec((tm, d), lambda i: (i, 0)),
        out_shape=jax.ShapeDtypeStruct((m, d), F32),
        compiler_params=_params(("parallel",)),
        name="merge_out_proj",
    )(x2d, mix_n, mix_f, w_out_bf, w_out_bf, norm_final.reshape(1, d))


def _cumsum_kernel(x_ref, o_ref):
    o_ref[...] = _lane_cumsum(x_ref[...])


def _cumsum_rows(x):
    return pl.pallas_call(
        _cumsum_kernel,
        out_shape=jax.ShapeDtypeStruct(x.shape, F32),
        name="forget_cumsum",
    )(x)


def _compress_rows(x_ref, pe_ref, kind, w, xs_ref, nb, starts, stride):
    half = nb // 2
    for c in range(CMP_BLOCK):
        pe_c = pe_ref[kind, c:c + 1, :]
        for par in range(2):
            xc = x_ref[pl.ds(starts[par] + c, half, stride=stride), :]
            xs_ref[par * half:(par + 1) * half, c * HEAD_DIM:(c + 1) * HEAD_DIM] = (xc + pe_c).astype(BF16)
    return _dot(xs_ref[...], w)


def _compress_kernel(x_ref, pe_ref, w_ref, o_ref, xs_ref, *, nb):
    out = _compress_rows(x_ref, pe_ref, 0, w_ref[0], xs_ref, nb, (0, CMP_BLOCK), 2 * CMP_BLOCK)
    o_ref[0, 0, 0] = out.astype(o_ref.dtype)


def _compress_prompt(kv4, cmp_pe, cmp_w_bf, batch, seq):
    nb = seq // CMP_BLOCK
    return pl.pallas_call(
        functools.partial(_compress_kernel, nb=nb),
        grid=(batch, 2, NSA_KV_HEADS),
        in_specs=[pl.BlockSpec((seq, HEAD_DIM), lambda b, k, h: (b, k * NSA_KV_HEADS + h)),
                  pl.BlockSpec((1, CMP_BLOCK, HEAD_DIM), lambda b, k, h: (k, 0, 0)),
                  pl.BlockSpec((1, CMP_BLOCK * HEAD_DIM, HEAD_DIM), lambda b, k, h: (k, 0, 0))],
        out_specs=pl.BlockSpec((1, 1, 1, nb, HEAD_DIM), lambda b, k, h: (b, k, h, 0, 0)),
        out_shape=jax.ShapeDtypeStruct((batch, 2, NSA_KV_HEADS, nb, HEAD_DIM), BF16),
        scratch_shapes=[pltpu.VMEM((nb, CMP_BLOCK * HEAD_DIM), BF16)],
        compiler_params=_params(("parallel", "parallel", "arbitrary")),
        name="compress_prompt",
    )(kv4, cmp_pe, cmp_w_bf)


def _select_blocks(score_t, cur, n_keep):
    n_blk = score_t.shape[0]
    j = lax.broadcasted_iota(jnp.int32, score_t.shape, 0)
    forced = (j == 0) | (j == cur) | (j == cur - 1)
    s = jnp.where(j <= cur, jnp.where(forced, FORCED_SCORE, score_t), -1.0)
    n_grp = n_blk // SUBLANES
    grp = [s[r * SUBLANES:(r + 1) * SUBLANES] for r in range(n_grp)]
    cnt = [jnp.zeros(grp[0].shape, F32) for _ in range(n_grp)]
    sub = lax.broadcasted_iota(jnp.int32, grp[0].shape, 0)
    for jj in range(n_blk):
        row = s[jj:jj + 1, :]
        r_j, off = divmod(jj, SUBLANES)
        for r in range(n_grp):
            if r < r_j:
                hit = row > grp[r]
            elif r > r_j:
                hit = row >= grp[r]
            else:
                hit = (row > grp[r]) | ((row == grp[r]) & (sub > off))
            cnt[r] = cnt[r] + jnp.where(hit, 1.0, 0.0)
    return jnp.where((jnp.concatenate(cnt, axis=0) < n_keep) & (j <= cur), 1.0, 0.0)


def _nsa_prompt_kernel(q_ref, kc_ref, vc_ref, ks_ref, vs_ref, kw_ref, vw_ref, g_ref, z_ref, e_ref, o_ref,
                       m_ref, acc_ref, alpha_ref, ps_ref, pw_ref, mix_ref, bias_ref, *, tq, tk, tw, ck, seq):
    qi = pl.program_id(1)
    s0 = qi * tq
    n_cmp = seq // CMP_BLOCK
    n_sel = seq // SEL_BLOCK
    row_t = s0 + lax.broadcasted_iota(jnp.int32, (tq, 1), 0)
    kv_heads = range(NSA_KV_HEADS)
    heads = lambda hkv: [hkv * NSA_GROUP + g for g in range(NSA_GROUP)]
    slopes = lambda hkv: [_head_slope(h) for h in heads(hkv)]
    cols = lambda hkv: slice(hkv * HEAD_DIM, (hkv + 1) * HEAD_DIM)

    def q_rows(hkv):
        return jnp.concatenate([q_ref[:, h * HEAD_DIM:(h + 1) * HEAD_DIM] for h in heads(hkv)], axis=0)

    def gate_col(hkv, br):
        return jnp.concatenate([g_ref[:, 3 * h + br:3 * h + br + 1] for h in heads(hkv)], axis=0)

    for hkv in kv_heads:
        sc = _dot_nt(q_rows(hkv), kc_ref[0, 0, hkv]) * SCALE
        lane = lax.broadcasted_iota(jnp.int32, (tq, n_cmp), 1)
        blk = jnp.where(lane < n_cmp // 2, 2 * lane, 2 * lane - (n_cmp - 1))
        dist_c = (row_t - ((blk + 1) * CMP_BLOCK - 1)).astype(F32)
        valid_c = dist_c >= 0
        score = jnp.zeros((tq, n_cmp), F32)
        probs = []
        for g, slope in enumerate(slopes(hkv)):
            lg = jnp.where(valid_c, sc[g * tq:(g + 1) * tq] - slope * dist_c, NEG_INF)
            mx = jnp.max(lg, axis=1, keepdims=True)
            mx = jnp.where(mx == NEG_INF, 0.0, mx)
            p = jnp.exp(lg - mx)
            p = p / jnp.maximum(jnp.sum(p, axis=1, keepdims=True), 1e-30)
            score = score + p
            probs.append(p)
        o_c = _dot(jnp.concatenate(probs, axis=0).astype(BF16), vc_ref[0, 0, hkv])
        mix_ref[hkv] = gate_col(hkv, 0) * o_c

        score_t = score.T
        score_t = score_t[:n_sel] + score_t[n_sel:]
        cur = (s0 + lax.broadcasted_iota(jnp.int32, (n_sel, tq), 1)) // SEL_BLOCK
        sel_t = _select_blocks(score_t, cur, SEL_TOPK)
        sel = jnp.concatenate([sel_t, jnp.zeros((LANES - n_sel, tq), F32)], axis=0).T.astype(BF16)

        def bias_body(c, carry, hkv=hkv, sel=sel):
            c0 = pl.multiple_of(c * ck, ck)
            expanded = _dot(sel, e_ref[:, pl.ds(c0, ck)])
            key = c0 + lax.broadcasted_iota(jnp.int32, (tq, ck), 1)
            bias_ref[hkv, :, pl.ds(c0, ck)] = jnp.where((expanded > 0.5) & (key <= row_t), 0.0, NEG_INF)
            return carry

        lax.fori_loop(0, (s0 + tq + ck - 1) // ck, bias_body, 0)

    def stream(hkv, s, bias, key_rel, v, p_ref, may_be_empty):
        sl = slopes(hkv)
        logits_of = lambda g: s[g * tq:(g + 1) * tq] * (SCALE * LOG2E) + (bias + (sl[g] * LOG2E) * key_rel)
        _state_update_chunked(logits_of, NSA_GROUP, tq, _with_ones(v), m_ref.at[hkv], acc_ref.at[hkv],
                              p_ref.at[hkv], alpha_ref.at[hkv], may_be_empty)

    _state_init(m_ref, acc_ref)

    def slc_body(i, carry):
        k0 = pl.multiple_of(i * tk, tk)
        key_rel = (k0 - s0 + lax.broadcasted_iota(jnp.int32, (1, tk), 1)).astype(F32)
        for hkv in kv_heads:
            s = _dot_nt(q_rows(hkv), ks_ref[pl.ds(k0, tk), cols(hkv)])
            stream(hkv, s, bias_ref[hkv, :, pl.ds(k0, tk)], key_rel, vs_ref[pl.ds(k0, tk), cols(hkv)], ps_ref, False)
        return carry

    lax.fori_loop(0, (s0 + tq + tk - 1) // tk, slc_body, 0)
    for hkv in kv_heads:
        mix_ref[hkv] = mix_ref[hkv] + gate_col(hkv, 1) * _state_result(acc_ref.at[hkv])

    _state_init(m_ref, acc_ref)

    def win_body(w, carry):
        k0 = pl.multiple_of(s0 - WINDOW + w * tw, tw)
        disti = row_t - (k0 + lax.broadcasted_iota(jnp.int32, (tq, tw), 1))
        bias = jnp.where((disti >= 0) & (disti < WINDOW), 0.0, NEG_INF)
        key_rel = (k0 - s0 + lax.broadcasted_iota(jnp.int32, (1, tw), 1)).astype(F32)
        for hkv in kv_heads:
            s = _dot_nt(q_rows(hkv), kw_ref[pl.ds(k0, tw), cols(hkv)])
            stream(hkv, s, bias, key_rel, vw_ref[pl.ds(k0, tw), cols(hkv)], pw_ref, True)
        return carry

    n_win_tiles = (WINDOW + tq) // tw
    lax.fori_loop(jnp.maximum(0, (WINDOW - s0) // tw), n_win_tiles, win_body, 0)

    for hkv in kv_heads:
        mix = mix_ref[hkv] + gate_col(hkv, 2) * _state_result(acc_ref.at[hkv])
        for g, h in enumerate(heads(hkv)):
            z = z_ref[:, h * HEAD_DIM:(h + 1) * HEAD_DIM]
            o_ref[:, h * HEAD_DIM:(h + 1) * HEAD_DIM] = (mix[g * tq:(g + 1) * tq] * _silu(z)).astype(o_ref.dtype)


def _nsa_prompt(proj, kcv, expand, batch, seq):
    tq, tk, tw, ck = 256, 256, 256, 512
    assert seq % ck == 0 and WINDOW % tw == 0 and tq % tw == 0 and seq // SEL_BLOCK <= LANES
    nq = seq // tq
    rows = NSA_GROUP * tq
    kernel = functools.partial(_nsa_prompt_kernel, tq=tq, tk=tk, tw=tw, ck=ck, seq=seq)
    cmp_spec = lambda kind: pl.BlockSpec((1, 1, NSA_KV_HEADS, seq // CMP_BLOCK, HEAD_DIM),
                                         lambda b, i: (b, kind, 0, 0, 0))
    return pl.pallas_call(
        kernel,
        grid=(batch, nq),
        in_specs=[pl.BlockSpec((tq, NSA_WIDTH), lambda b, i: (b * nq + i, 0)),
                  cmp_spec(0), cmp_spec(1),
                  pl.BlockSpec((seq, KV_WIDTH), lambda b, i: (b, 2)),
                  pl.BlockSpec((seq, KV_WIDTH), lambda b, i: (b, 3)),
                  pl.BlockSpec((seq, KV_WIDTH), lambda b, i: (b, 0)),
                  pl.BlockSpec((seq, KV_WIDTH), lambda b, i: (b, 1)),
                  pl.BlockSpec((tq, LANES), lambda b, i: (b * nq + i, 0)),
                  pl.BlockSpec((tq, NSA_WIDTH), lambda b, i: (b * nq + i, 0)),
                  pl.BlockSpec((LANES, seq), lambda b, i: (0, 0))],
        out_specs=pl.BlockSpec((tq, NSA_WIDTH), lambda b, i: (b * nq + i, 0)),
        out_shape=jax.ShapeDtypeStruct((batch * seq, NSA_WIDTH), BF16),
        scratch_shapes=[pltpu.VMEM((NSA_KV_HEADS, rows, LANES), F32),
                        pltpu.VMEM((NSA_KV_HEADS, rows, 2 * HEAD_DIM), F32),
                        pltpu.VMEM((NSA_KV_HEADS, rows, LANES), F32),
                        pltpu.VMEM((NSA_KV_HEADS, rows, tk), BF16),
                        pltpu.VMEM((NSA_KV_HEADS, rows, tw), BF16),
                        pltpu.VMEM((NSA_KV_HEADS, rows, HEAD_DIM), F32),
                        pltpu.VMEM((NSA_KV_HEADS, tq, seq), F32)],
        compiler_params=_params(("parallel", "arbitrary")),
        name="nsa_prompt",
    )(proj["q_n"], kcv, kcv, proj["kv4_bf"], proj["kv4_bf"], proj["kvwin_bf"], proj["kvwin_bf"],
      proj["small"], proj["z_n"], expand)


def _fox_prompt_kernel(q_ref, k_ref, v_ref, d_ref, z_ref, o_ref, m_ref, acc_ref, alpha_ref, p_ref, *,
                       t, chunk, n_heads):
    qi = pl.program_id(2)
    _state_init(m_ref, acc_ref)

    def tile(i, causal):
        k0 = pl.multiple_of(i * t, t)
        for hh in range(n_heads):
            cols = slice(hh * HEAD_DIM, (hh + 1) * HEAD_DIM)
            s = _dot_nt(q_ref[:, cols], k_ref[pl.ds(k0, t), cols])
            d_row = d_ref[hh, :, pl.ds(k0, t)] * LOG2E

            def logits_of(c, s=s, d_row=d_row):
                lg = s[c * chunk:(c + 1) * chunk] * (SCALE * LOG2E) - d_row
                if causal:
                    row = c * chunk + lax.broadcasted_iota(jnp.int32, (chunk, t), 0)
                    col = lax.broadcasted_iota(jnp.int32, (chunk, t), 1)
                    lg = jnp.where(col <= row, lg, NEG_INF)
                return lg

            _state_update_chunked(logits_of, t // chunk, chunk, _with_ones(v_ref[pl.ds(k0, t), cols]),
                                  m_ref.at[hh], acc_ref.at[hh], p_ref.at[hh], alpha_ref.at[hh], False)

    def body(i, carry):
        tile(i, False)
        return carry

    lax.fori_loop(0, qi, body, 0)
    tile(qi, True)
    for hh in range(n_heads):
        cols = slice(hh * HEAD_DIM, (hh + 1) * HEAD_DIM)
        o_ref[:, cols] = (_state_result(acc_ref.at[hh]) * _silu(z_ref[:, cols])).astype(o_ref.dtype)


def _fox_prompt(proj, d_rows, batch, seq):
    t, chunk, n_heads = 512, 128, 4
    nq = seq // t
    width = n_heads * HEAD_DIM
    n_hg = FOX_HEADS // n_heads
    return pl.pallas_call(
        functools.partial(_fox_prompt_kernel, t=t, chunk=chunk, n_heads=n_heads),
        grid=(batch, n_hg, nq),
        in_specs=[pl.BlockSpec((t, width), lambda b, h, i: (b * nq + i, h)),
                  pl.BlockSpec((seq, width), lambda b, h, i: (b, h)),
                  pl.BlockSpec((seq, width), lambda b, h, i: (b, n_hg + h)),
                  pl.BlockSpec((n_heads, 1, seq), lambda b, h, i: (b * n_hg + h, 0, 0)),
                  pl.BlockSpec((t, width), lambda b, h, i: (b * nq + i, h))],
        out_specs=pl.BlockSpec((t, width), lambda b, h, i: (b * nq + i, h)),
        out_shape=jax.ShapeDtypeStruct((batch * seq, FOX_WIDTH), BF16),
        scratch_shapes=[pltpu.VMEM((n_heads, t, LANES), F32), pltpu.VMEM((n_heads, t, 2 * HEAD_DIM), F32),
                        pltpu.VMEM((n_heads, t, LANES), F32), pltpu.VMEM((n_heads, t, t), BF16)],
        compiler_params=_params(("parallel", "parallel", "arbitrary")),
        name="fox_prompt",
    )(proj["q_f"], proj["kv_f_bf"], proj["kv_f_bf"], d_rows, proj["z_f"])


def _prompt_layer(x, norm_in, wts, cmp_pe, cmp_w_bf, w_out_bf, norm_final):
    batch, seq, d = x.shape
    x2d = x.reshape(batch * seq, d)
    proj = _project_all(x2d, norm_in, wts, 1024)
    n_gate = 3 * NSA_HEADS
    logf = proj["small"][:, n_gate:n_gate + FOX_HEADS].reshape(batch, seq, FOX_HEADS)
    d_rows = _cumsum_rows(jnp.swapaxes(logf, 1, 2).reshape(batch * FOX_HEADS, seq))
    kcv = _compress_prompt(proj["kv4"], cmp_pe, cmp_w_bf, batch, seq)
    blk_of_key = jnp.arange(seq, dtype=jnp.int32) // SEL_BLOCK
    expand = (jnp.arange(LANES, dtype=jnp.int32)[:, None] == blk_of_key[None, :]).astype(BF16)
    mix_n = _nsa_prompt(proj, kcv, expand, batch, seq)
    mix_f = _fox_prompt(proj, d_rows.reshape(batch * FOX_HEADS, 1, seq), batch, seq)
    y = _merge(x2d, mix_n, mix_f, w_out_bf, norm_final, 512).reshape(batch, seq, d)
    w_keep = min(WINDOW, seq)
    state = (proj["kv4"].reshape(batch, seq, 4, NSA_KV_HEADS, HEAD_DIM),
             proj["kv_f"].reshape(batch, seq, 2, FOX_HEADS, HEAD_DIM),
             logf,
             proj["kvwin"].reshape(batch, seq, 2, NSA_KV_HEADS, HEAD_DIM)[:, seq - w_keep:])
    return y, state


HALF_CHANNELS = NSA_CHANNELS // 2


def _nsa_page_specs(pg, half):
    def spec(u):
        return pl.BlockSpec((1, 1, PAGE_SIZE, 1, HALF_CHANNELS, HEAD_DIM),
                            lambda b, j, pt: (0, pt[b, j * pg + u], 0, half, 0, 0))
    return [spec(u) for u in range(pg)]


def _channel(page_ref, ch, start=0, n=PAGE_SIZE):
    rows = page_ref.reshape(PAGE_SIZE * HALF_CHANNELS, HEAD_DIM)
    return rows[pl.ds(start * HALF_CHANNELS + ch, n, stride=HALF_CHANNELS), :]


def _nsa_cmp_kernel(pt_ref, *refs, pg, n_pages):
    page_refs = refs[:pg]
    q_ref, pe_ref, w_ref, oc_ref, sel_ref, stage_ref, xs_ref = refs[pg:]
    jg = pl.program_id(1)
    past = n_pages * PAGE_SIZE
    nb = past // CMP_BLOCK
    half = nb // 2
    blocks_per_page = PAGE_SIZE // CMP_BLOCK
    for u in range(pg):
        for bl in range(blocks_per_page):
            m = (jg * pg + u) * (blocks_per_page // 2) + bl // 2
            dst = pl.multiple_of(((bl % 2) * half + m) * STAGE_PITCH, SUBLANES)
            for ch in range(2 * NSA_KV_HEADS):
                stage_ref[ch, pl.ds(dst, CMP_BLOCK), :] = _channel(page_refs[u], ch, bl * CMP_BLOCK, CMP_BLOCK)

    @pl.when(jg == pl.num_programs(1) - 1)
    def _():
        cmp = [[_compress_rows(stage_ref.at[kind * NSA_KV_HEADS + h], pe_ref, kind, w_ref[kind], xs_ref, nb,
                               (0, half * STAGE_PITCH), STAGE_PITCH).astype(BF16)
                for h in range(NSA_KV_HEADS)] for kind in range(2)]
        rows = NSA_GROUP * T_PAD
        for hkv in range(NSA_KV_HEADS):
            q = q_ref[0, hkv * rows:(hkv + 1) * rows, :].astype(BF16)
            kc, vc = cmp[0][hkv], cmp[1][hkv]
            s = _dot_nt(q, kc) * SCALE
            lane = lax.broadcasted_iota(jnp.int32, (rows, nb), 1)
            blk = jnp.where(lane < half, 2 * lane, 2 * lane - (nb - 1))
            row = lax.broadcasted_iota(jnp.int32, (rows, nb), 0)
            slope = _head_slope(hkv * NSA_GROUP + row // T_PAD)
            dist = (past + row % T_PAD - ((blk + 1) * CMP_BLOCK - 1)).astype(F32)
            lg = jnp.where(dist >= 0, s - slope * dist, NEG_INF)
            mx = jnp.max(lg, axis=1, keepdims=True)
            mx = jnp.where(mx == NEG_INF, 0.0, mx)
            p = jnp.exp(lg - mx)
            p = p / jnp.maximum(jnp.sum(p, axis=1, keepdims=True), 1e-30)
            oc_ref[0, hkv * rows:(hkv + 1) * rows, :] = _dot(p.astype(BF16), vc)
            score = p[0:T_PAD]
            for g in range(1, NSA_GROUP):
                score = score + p[g * T_PAD:(g + 1) * T_PAD]
            score = score[:, :half] + score[:, half:]
            blk_s = lax.broadcasted_iota(jnp.int32, (T_PAD, half), 1).astype(F32)
            forced = (blk_s == 0) | (blk_s == half - 1)
            s_left = jnp.where(forced, NEG_INF, score)
            picked = jnp.zeros((T_PAD, half), F32)
            for _ in range(SEL_TOPK - 3):
                best = jnp.max(s_left, axis=1, keepdims=True)
                first = jnp.min(jnp.where(s_left == best, blk_s, float(half)), axis=1, keepdims=True)
                hit = blk_s == first
                picked = jnp.where(hit, 1.0, picked)
                s_left = jnp.where(hit, NEG_INF, s_left)
            sel_ref[0, hkv * T_PAD:(hkv + 1) * T_PAD, :] = jnp.where(forced, 1.0, picked)


def _nsa_cmp_sample(cache, page_table, q_rows, cmp_pe, cmp_w_bf, pg):
    n_seq, n_pages = page_table.shape
    past = n_pages * PAGE_SIZE
    nb = past // CMP_BLOCK
    assert n_pages % pg == 0 and past // SEL_BLOCK >= SEL_TOPK
    rows = NSA_HEADS * T_PAD
    grid_spec = pltpu.PrefetchScalarGridSpec(
        num_scalar_prefetch=1,
        grid=(n_seq, n_pages // pg),
        in_specs=_nsa_page_specs(pg, 0) + [
            pl.BlockSpec((1, rows, HEAD_DIM), lambda b, j, pt: (b, 0, 0)),
            pl.BlockSpec((2, CMP_BLOCK, HEAD_DIM), lambda b, j, pt: (0, 0, 0)),
            pl.BlockSpec((2, CMP_BLOCK * HEAD_DIM, HEAD_DIM), lambda b, j, pt: (0, 0, 0))],
        out_specs=[pl.BlockSpec((1, rows, HEAD_DIM), lambda b, j, pt: (b, 0, 0)),
                   pl.BlockSpec((1, NSA_KV_HEADS * T_PAD, past // SEL_BLOCK), lambda b, j, pt: (b, 0, 0))],
        scratch_shapes=[pltpu.VMEM((2 * NSA_KV_HEADS, nb * STAGE_PITCH, HEAD_DIM), F32),
                        pltpu.VMEM((nb, CMP_BLOCK * HEAD_DIM), BF16)],
    )
    return pl.pallas_call(
        functools.partial(_nsa_cmp_kernel, pg=pg, n_pages=n_pages),
        grid_spec=grid_spec,
        out_shape=[jax.ShapeDtypeStruct((n_seq, rows, HEAD_DIM), F32),
                   jax.ShapeDtypeStruct((n_seq, NSA_KV_HEADS * T_PAD, past // SEL_BLOCK), F32)],
        compiler_params=_params(("parallel", "arbitrary")),
        name="nsa_sample_cmp",
    )(page_table, *([cache] * pg), q_rows, cmp_pe, cmp_w_bf)


def _nsa_slc_kernel(pt_ref, *refs, pg, part_pages, n_pages, w_buf, n_new):
    page_refs = refs[:pg]
    (q_ref, sel_ref, e_ref, new_ref, win_ref, oc_ref, g_ref, z_ref, o_ref, tab_ref, m_ref, acc_ref) = refs[pg:]
    jg = pl.program_id(1)
    past = n_pages * PAGE_SIZE
    hr = n_new * NSA_GROUP
    rows = NSA_KV_HEADS * hr
    row = lax.broadcasted_iota(jnp.int32, (rows, 1), 0)
    t_q = (row % hr) // NSA_GROUP
    slope = _head_slope((row // hr) * NSA_GROUP + row % NSA_GROUP)
    q = [q_ref[0, h * hr:(h + 1) * hr, :].astype(BF16) for h in range(NSA_KV_HEADS)]

    def scores(k_of):
        return jnp.concatenate([_dot_nt(q[h], k_of(h)) for h in range(NSA_KV_HEADS)], axis=0) * SCALE

    def pv(v_of):
        return lambda p: jnp.concatenate(
            [_dot(p[h * hr:(h + 1) * hr], _with_ones(v_of(h))) for h in range(NSA_KV_HEADS)], axis=0)


    @pl.when(jg == 0)
    def _():
        expanded = _dot(sel_ref[0].astype(BF16), e_ref[...])
        key = lax.broadcasted_iota(jnp.int32, (rows, past), 1)
        tab_ref[...] = jnp.where(expanded > 0.5, 0.0, NEG_INF) + slope * (key - past).astype(F32)
        _state_init(m_ref, acc_ref)

    def page_stream(ch, pages):
        return jnp.concatenate([_channel(page_refs[u], ch).astype(BF16) for u in pages], axis=0)

    parts = []
    for first in range(0, pg, part_pages):
        pages = range(first, first + part_pages)
        c0 = pl.multiple_of((jg * pg + first) * PAGE_SIZE, part_pages * PAGE_SIZE)
        lg = (scores(lambda h: page_stream(h, pages))
              + tab_ref[:, pl.ds(c0, part_pages * PAGE_SIZE)])
        parts.append((lg, pv(lambda h, pages=pages: page_stream(NSA_KV_HEADS + h, pages))))
    _state_update_parts(parts, m_ref, acc_ref, True)

    @pl.when(jg == pl.num_programs(1) - 1)
    def _():
        new = lambda ch: new_ref[0, ch].astype(BF16)
        j_new = lax.broadcasted_iota(jnp.int32, (rows, PAGE_SIZE), 1)
        bias_new = jnp.where((j_new <= t_q) & (j_new < n_new), slope * j_new.astype(F32), NEG_INF)
        _state_update(scores(lambda h: new(h)) + bias_new, pv(lambda h: new(NSA_KV_HEADS + h)),
                      m_ref, acc_ref, False)
        o_s = _state_result(acc_ref)

        _state_init(m_ref, acc_ref)
        i_w = lax.broadcasted_iota(jnp.int32, (rows, w_buf), 1)
        dist_w = t_q + w_buf - i_w
        valid_w = (dist_w >= 0) & (dist_w < WINDOW) & (past - w_buf + i_w >= 0)
        bias_w = jnp.where(valid_w, slope * (i_w - w_buf).astype(F32), NEG_INF)
        win = lambda ch: win_ref[0, ch].astype(BF16)
        _state_update(scores(lambda h: win(h)) + bias_w, pv(lambda h: win(NSA_KV_HEADS + h)),
                      m_ref, acc_ref, True)
        _state_update(scores(lambda h: new(2 * NSA_KV_HEADS + h)) + bias_new,
                      pv(lambda h: new(3 * NSA_KV_HEADS + h)), m_ref, acc_ref, True)
        o_w = _state_result(acc_ref)

        gates = g_ref[0]
        mix = gates[:, 0:1] * oc_ref[0] + gates[:, 1:2] * o_s + gates[:, 2:3] * o_w
        o_ref[0] = mix * _silu(z_ref[0])


def _nsa_slc_sample(cache, page_table, q_rows, sel_rows, expand, new_rows, win_state, o_c_rows, gates_rows, z_rows,
                    pg, n_new):
    n_seq, n_pages = page_table.shape
    past = n_pages * PAGE_SIZE
    w_buf = win_state.shape[2]
    rows = n_new * NSA_HEADS
    per_seq = lambda *shape: pl.BlockSpec((1,) + shape, lambda b, j, pt: (b,) + (0,) * len(shape))
    grid_spec = pltpu.PrefetchScalarGridSpec(
        num_scalar_prefetch=1,
        grid=(n_seq, n_pages // pg),
        in_specs=_nsa_page_specs(pg, 1) + [
            per_seq(rows, HEAD_DIM),
            per_seq(rows, past // SEL_BLOCK),
            pl.BlockSpec((past // SEL_BLOCK, past), lambda b, j, pt: (0, 0)),
            per_seq(4 * NSA_KV_HEADS, PAGE_SIZE, HEAD_DIM),
            per_seq(2 * NSA_KV_HEADS, w_buf, HEAD_DIM),
            per_seq(rows, HEAD_DIM),
            per_seq(rows, 3),
            per_seq(rows, HEAD_DIM)],
        out_specs=per_seq(rows, HEAD_DIM),
        scratch_shapes=[pltpu.VMEM((rows, past), F32), pltpu.VMEM((rows, LANES), F32),
                        pltpu.VMEM((rows, 2 * HEAD_DIM), F32)],
    )
    return pl.pallas_call(
        functools.partial(_nsa_slc_kernel, pg=pg, part_pages=2, n_pages=n_pages, w_buf=w_buf, n_new=n_new),
        grid_spec=grid_spec,
        out_shape=jax.ShapeDtypeStruct((n_seq, rows, HEAD_DIM), F32),
        compiler_params=_params(("parallel", "arbitrary")),
        name="nsa_sample_slc_win",
    )(page_table, *([cache] * pg), q_rows, sel_rows, expand, new_rows, win_state, o_c_rows, gates_rows, z_rows)


def _page_forget_scan(x):
    n = x.shape[1]
    lane = lax.broadcasted_iota(jnp.int32, x.shape, 1)
    s = FOX_HEADS
    while s < n:
        x = x + jnp.where(lane >= s, pltpu.roll(x, s, axis=1), 0.0)
        s *= 2
    total = jnp.where(lane >= n - FOX_HEADS, x, 0.0)
    s = FOX_HEADS
    while s < n:
        total = total + pltpu.roll(total, n - s, axis=1)
        s *= 2
    return x, total


def _fox_sample_kernel(pt_ref, *refs, pg, part_pages, n_new):
    page_refs = refs[:pg]
    (q_ref, lc_ref, kn_ref, vn_ref, ln_ref, z_ref, o_ref, m_ref, acc_ref, carry_ref, lg_ref) = refs[pg:]
    b = pl.program_id(0)
    jg = pl.program_id(1)
    rows = n_new * FOX_HEADS
    keys = PAGE_SIZE * FOX_HEADS

    @pl.when(jg == 0)
    def _():
        _state_init(m_ref, acc_ref)
        carry_ref[...] = jnp.zeros(carry_ref.shape, F32)

    q = q_ref[0].astype(BF16)
    own = (lax.broadcasted_iota(jnp.int32, (rows, keys), 1) % FOX_HEADS
           == lax.broadcasted_iota(jnp.int32, (rows, keys), 0) % FOX_HEADS)

    for u in range(pg):
        lg_ref[u:u + 1, :] = lc_ref[pl.ds(pt_ref[b, jg * pg + u], 1), :]
    within, total = _page_forget_scan(lg_ref[...])
    carry = carry_ref[...]
    logits = []
    for u in range(pg):
        d_u = within[u:u + 1] + carry
        carry = carry + total[u:u + 1]
        k = page_refs[u][0, 0, :, 0].reshape(keys, HEAD_DIM).astype(BF16)
        logits.append(jnp.where(own, _dot_nt(q, k) * SCALE - d_u, NEG_INF))
    carry_ref[...] = carry

    def pv(pages):
        def pv_of(p):
            out = None
            for i, u in enumerate(pages):
                v = page_refs[u][0, 0, :, 1].reshape(keys, HEAD_DIM).astype(BF16)
                part = _dot(p[:, i * keys:(i + 1) * keys], _with_ones(v))
                out = part if out is None else out + part
            return out
        return pv_of

    parts = []
    for first in range(0, pg, part_pages):
        pages = range(first, first + part_pages)
        parts.append((jnp.concatenate([logits[u] for u in pages], axis=1), pv(pages)))
    _state_update_parts(parts, m_ref, acc_ref, False)

    @pl.when(jg == pl.num_programs(1) - 1)
    def _():
        within_new, _ = _page_forget_scan(ln_ref[0])
        d_new = (within_new + carry_ref[...])[:, :PAGE_SIZE]
        c = lax.broadcasted_iota(jnp.int32, (rows, PAGE_SIZE), 1)
        r = lax.broadcasted_iota(jnp.int32, (rows, PAGE_SIZE), 0)
        valid = (c < rows) & (c % FOX_HEADS == r % FOX_HEADS) & (c // FOX_HEADS <= r // FOX_HEADS)
        lg = jnp.where(valid, _dot_nt(q, kn_ref[0].astype(BF16)) * SCALE - d_new, NEG_INF)
        _state_update(lg, lambda p: _dot(p, _with_ones(vn_ref[0].astype(BF16))), m_ref, acc_ref, False)
        o_ref[0] = _state_result(acc_ref) * _silu(z_ref[0])


def _fox_sample(cache, page_table, q_rows, logf_cache, k_new, v_new, logf_new, z_rows, pg, n_new):
    n_seq, n_pages = page_table.shape
    n_pool = logf_cache.shape[0]
    rows = n_new * FOX_HEADS
    keys = PAGE_SIZE * FOX_HEADS
    per_seq = lambda *shape: pl.BlockSpec((1,) + shape, lambda b, j, pt: (b,) + (0,) * len(shape))

    def page_spec(u):
        return pl.BlockSpec((1, 1, PAGE_SIZE, 2, FOX_HEADS, HEAD_DIM),
                            lambda b, j, pt: (0, pt[b, j * pg + u], 0, 0, 0, 0))

    grid_spec = pltpu.PrefetchScalarGridSpec(
        num_scalar_prefetch=1,
        grid=(n_seq, n_pages // pg),
        in_specs=[page_spec(u) for u in range(pg)] + [
            per_seq(rows, HEAD_DIM),
            pl.BlockSpec((n_pool, keys), lambda b, j, pt: (0, 0)),
            per_seq(PAGE_SIZE, HEAD_DIM),
            per_seq(PAGE_SIZE, HEAD_DIM),
            per_seq(1, keys),
            per_seq(rows, HEAD_DIM)],
        out_specs=per_seq(rows, HEAD_DIM),
        scratch_shapes=[pltpu.VMEM((rows, LANES), F32), pltpu.VMEM((rows, 2 * HEAD_DIM), F32),
                        pltpu.VMEM((1, keys), F32), pltpu.VMEM((pg, keys), F32)],
    )
    return pl.pallas_call(
        functools.partial(_fox_sample_kernel, pg=pg, part_pages=2, n_new=n_new),
        grid_spec=grid_spec,
        out_shape=jax.ShapeDtypeStruct((n_seq, rows, HEAD_DIM), F32),
        compiler_params=_params(("parallel", "arbitrary")),
        name="fox_sample",
    )(page_table, *([cache] * pg), q_rows, logf_cache, k_new, v_new, logf_new, z_rows)


def _pad_axis(a, axis, size):
    pad = [(0, 0)] * a.ndim
    pad[axis] = (0, size - a.shape[axis])
    return jnp.pad(a, pad)


def _sample_layer(x, cache_nsa_kv, cache_fox_kv, cache_fox_logf, state_nsa_win, page_table,
                  norm_in, wts, cmp_pe, cmp_w_bf, w_out_bf, norm_final):
    n_seq, t_new, d = x.shape
    n_pool = cache_nsa_kv.shape[1]
    n_pages = page_table.shape[1]
    past = n_pages * PAGE_SIZE
    w_buf = state_nsa_win.shape[2]
    assert t_new <= T_PAD and t_new < SEL_BLOCK and past % SEL_BLOCK == 0
    x2d = x.reshape(n_seq * t_new, d)
    proj = _project_all(x2d, norm_in, wts, n_seq * t_new)
    n_gate = 3 * NSA_HEADS
    small = proj["small"].reshape(n_seq, t_new, LANES)
    logf = small[:, :, n_gate:n_gate + FOX_HEADS]

    def head_rows(a):
        a = _pad_axis(a.reshape(n_seq, t_new, NSA_HEADS, -1), 1, T_PAD)
        return jnp.swapaxes(a, 1, 2).reshape(n_seq, NSA_HEADS * T_PAD, a.shape[-1])

    def group_rows(a):
        a = a.reshape(n_seq, t_new, NSA_KV_HEADS, NSA_GROUP, a.shape[-1])
        return jnp.swapaxes(a, 1, 2).reshape(n_seq, NSA_HEADS * t_new, a.shape[-1])

    q_n = proj["q_n"].astype(F32)
    gates = small[:, :, :n_gate].reshape(n_seq, t_new, NSA_HEADS, 3)
    z_n = proj["z_n"].reshape(n_seq, t_new, NSA_HEADS, HEAD_DIM)
    nsa_new = jnp.concatenate([proj["kv4"].reshape(n_seq, t_new, 4 * NSA_KV_HEADS, HEAD_DIM)[:, :, 2 * NSA_KV_HEADS:],
                               proj["kvwin"].reshape(n_seq, t_new, 2 * NSA_KV_HEADS, HEAD_DIM)], axis=2)
    nsa_new = _pad_axis(jnp.swapaxes(nsa_new, 1, 2), 2, PAGE_SIZE)
    win_state = jnp.swapaxes(state_nsa_win[0].reshape(n_seq, w_buf, 2 * NSA_KV_HEADS, HEAD_DIM), 1, 2)
    blk_of_key = jnp.arange(past, dtype=jnp.int32) // SEL_BLOCK
    expand = (jnp.arange(past // SEL_BLOCK, dtype=jnp.int32)[:, None] == blk_of_key[None, :]).astype(BF16)
    nsa_cache = cache_nsa_kv.reshape(cache_nsa_kv.shape[0], n_pool, PAGE_SIZE, 2, HALF_CHANNELS, HEAD_DIM)

    pg = 8
    o_c, sel = _nsa_cmp_sample(nsa_cache, page_table, head_rows(q_n), cmp_pe, cmp_w_bf, pg)
    o_c = jnp.swapaxes(o_c.reshape(n_seq, NSA_HEADS, T_PAD, HEAD_DIM)[:, :, :t_new], 1, 2)
    sel = sel.reshape(n_seq, NSA_KV_HEADS, T_PAD, past // SEL_BLOCK)[:, :, :t_new]
    sel = jnp.broadcast_to(sel[:, :, :, None, :], (n_seq, NSA_KV_HEADS, t_new, NSA_GROUP, past // SEL_BLOCK))
    mix_n = _nsa_slc_sample(nsa_cache, page_table, group_rows(q_n.reshape(n_seq, t_new, NSA_HEADS, HEAD_DIM)),
                            sel.reshape(n_seq, NSA_HEADS * t_new, past // SEL_BLOCK), expand, nsa_new, win_state,
                            group_rows(o_c), group_rows(gates), group_rows(z_n), pg, t_new)
    mix_n = jnp.swapaxes(mix_n.reshape(n_seq, NSA_KV_HEADS, t_new, NSA_GROUP * HEAD_DIM), 1, 2)
    mix_n = mix_n.reshape(n_seq * t_new, NSA_WIDTH).astype(BF16)

    fox_rows = lambda a: a.reshape(n_seq, t_new * FOX_HEADS, HEAD_DIM)
    kv_f = proj["kv_f"].reshape(n_seq, t_new, 2, FOX_HEADS * HEAD_DIM)
    k_new = _pad_axis(fox_rows(kv_f[:, :, 0]), 1, PAGE_SIZE)
    v_new = _pad_axis(fox_rows(kv_f[:, :, 1]), 1, PAGE_SIZE)
    logf_new = _pad_axis(logf.reshape(n_seq, 1, t_new * FOX_HEADS), 2, PAGE_SIZE * FOX_HEADS)
    logf_cache = cache_fox_logf[0].reshape(n_pool, PAGE_SIZE * FOX_HEADS)
    mix_f = _fox_sample(cache_fox_kv, page_table, fox_rows(proj["q_f"].astype(F32)), logf_cache, k_new, v_new,
                        logf_new, fox_rows(proj["z_f"]), pg, t_new)
    mix_f = mix_f.reshape(n_seq * t_new, FOX_WIDTH).astype(BF16)

    y = _merge(x2d, mix_n, mix_f, w_out_bf, norm_final, n_seq * t_new).reshape(n_seq, t_new, d)
    kvwin_new = proj["kvwin"].reshape(n_seq, t_new, 2, NSA_KV_HEADS, HEAD_DIM)
    state = (proj["kv4"].reshape(n_seq, t_new, 4, NSA_KV_HEADS, HEAD_DIM),
             proj["kv_f"].reshape(n_seq, t_new, 2, FOX_HEADS, HEAD_DIM),
             logf,
             jnp.concatenate([state_nsa_win[0], kvwin_new], axis=1)[:, t_new:])
    return y, state


def kernel(x_prompt, x_sample, cache_nsa_kv, cache_fox_kv, cache_fox_logf, state_nsa_win, page_table,
           norm_in, w_in, b_gate, b_forget, cmp_pe, cmp_w, w_out, norm_final):
    assert norm_in.shape[0] == 1, "single-layer trunk"
    wts = _split_weights(w_in[0], b_gate[0], b_forget[0])
    cmp_w_bf = cmp_w[0].reshape(2, CMP_BLOCK * HEAD_DIM, HEAD_DIM).astype(BF16)
    w_out_bf = w_out[0].astype(BF16)
    y_p, st_p = _prompt_layer(x_prompt, norm_in[0], wts, cmp_pe[0], cmp_w_bf, w_out_bf, norm_final)
    y_s, st_s = _sample_layer(x_sample, cache_nsa_kv, cache_fox_kv, cache_fox_logf, state_nsa_win,
                              page_table, norm_in[0], wts, cmp_pe[0], cmp_w_bf, w_out_bf, norm_final)
    outs = [y_p, y_s]
    for s_p, s_s in zip(st_p, st_s):
        outs.extend([s_p[None], s_s[None]])
    return tuple(outs)
```

```python
import functools

import jax
import jax.numpy as jnp
from jax import lax
from jax.experimental import pallas as pl
from jax.experimental.pallas import tpu as pltpu

F32 = jnp.float32
BF16 = jnp.bfloat16
NEG_INF = float("-inf")

HEAD_DIM = 128
NSA_HEADS = 8
FOX_HEADS = 8
NSA_KV_HEADS = 2
NSA_GROUP = NSA_HEADS // NSA_KV_HEADS
NSA_WIDTH = NSA_HEADS * HEAD_DIM
FOX_WIDTH = FOX_HEADS * HEAD_DIM
KV_WIDTH = NSA_KV_HEADS * HEAD_DIM
NSA_CHANNELS = 4 * NSA_KV_HEADS
CMP_BLOCK = 32
SEL_BLOCK = 64
SEL_TOPK = 16
WINDOW = 512
PAGE_SIZE = 128
FORCED_SCORE = 1e4
RMS_EPS = 1e-6
SCALE = HEAD_DIM ** -0.5
LOG2E = 1.4426950408889634
T_PAD = 8
LANES = 128
SUBLANES = 8
PROJ_TN = 512
STAGE_PITCH = 40
VMEM_LIMIT = 56 * 1024 * 1024


def _params(sem):
    return pltpu.CompilerParams(dimension_semantics=sem, vmem_limit_bytes=VMEM_LIMIT)


def _dot(a, b):
    return jnp.dot(a, b, preferred_element_type=F32)


def _dot_nt(a, b):
    return lax.dot_general(a, b, (((1,), (1,)), ((), ())), preferred_element_type=F32)


def _sigmoid(x):
    return 1.0 / (1.0 + jnp.exp(-x))


def _silu(x):
    return x * _sigmoid(x)


def _tile_lanes(x, n):
    return x if n == 1 else jnp.concatenate([x] * n, axis=1)


def _with_ones(v):
    return jnp.concatenate([v, jnp.ones(v.shape, v.dtype)], axis=1)


def _state_init(m_ref, acc_ref):
    m_ref[...] = jnp.full(m_ref.shape, NEG_INF, F32)
    acc_ref[...] = jnp.zeros(acc_ref.shape, F32)


def _probs(lg, m_old, may_be_empty, exp=jnp.exp):
    m_new = jnp.maximum(m_old, jnp.max(lg, axis=1, keepdims=True))
    m_use = jnp.where(m_new == NEG_INF, 0.0, m_new) if may_be_empty else m_new
    alpha = exp(m_old - m_use)
    p = exp(lg - _tile_lanes(m_use, lg.shape[1] // LANES)).astype(BF16)
    return m_new, alpha, p


def _state_update_chunked(logits_of, n_chunks, chunk, v_ext, m_ref, acc_ref, p_ref, alpha_ref, may_be_empty):
    for c in range(n_chunks):
        rows = slice(c * chunk, (c + 1) * chunk)
        m_new, alpha, p = _probs(logits_of(c), m_ref[rows, :], may_be_empty, jnp.exp2)
        alpha_ref[rows, :] = alpha
        p_ref[rows, :] = p
        m_ref[rows, :] = m_new
    acc_ref[...] = _tile_lanes(alpha_ref[...], 2) * acc_ref[...] + _dot(p_ref[...], v_ext)


def _state_update(lg, pv_of, m_ref, acc_ref, may_be_empty):
    m_new, alpha, p = _probs(lg, m_ref[...], may_be_empty)
    acc_ref[...] = _tile_lanes(alpha, 2) * acc_ref[...] + pv_of(p)
    m_ref[...] = m_new


def _state_update_parts(parts, m_ref, acc_ref, may_be_empty):
    local = []
    for lg, pv_of in parts:
        m_c = jnp.broadcast_to(jnp.max(lg, axis=1, keepdims=True), m_ref.shape)
        m_use = jnp.where(m_c == NEG_INF, 0.0, m_c) if may_be_empty else m_c
        p = jnp.exp(lg - _tile_lanes(m_use, lg.shape[1] // LANES)).astype(BF16)
        local.append((m_c, pv_of(p)))
    m_old = m_ref[...]
    m_new = m_old
    for m_c, _ in local:
        m_new = jnp.maximum(m_new, m_c)
    acc = _tile_lanes(jnp.exp(m_old - m_new), 2) * acc_ref[...]
    for m_c, pv in local:
        acc = acc + _tile_lanes(jnp.exp(m_c - m_new), 2) * pv
    acc_ref[...] = acc
    m_ref[...] = m_new


def _state_result(acc_ref):
    acc = acc_ref[...]
    return acc[:, :HEAD_DIM] / jnp.maximum(acc[:, HEAD_DIM:], 1e-30)


def _lane_cumsum(x):
    n = x.shape[-1]
    lane = lax.broadcasted_iota(jnp.int32, x.shape, x.ndim - 1)
    s = 1
    while s < n:
        x = x + jnp.where(lane >= s, pltpu.roll(x, s, axis=x.ndim - 1), 0.0)
        s *= 2
    return x


def _head_slope(head):
    if isinstance(head, int):
        return 2.0 ** -(head + 1)
    return lax.bitcast_convert_type((126 - head) << 23, F32)


def _rms_kernel(x_ref, g_ref, o_ref):
    x = x_ref[...]
    ms = jnp.mean(x * x, axis=-1, keepdims=True)
    o_ref[...] = (x * lax.rsqrt(ms + RMS_EPS) * g_ref[...]).astype(o_ref.dtype)


def _rmsnorm(x2d, g, out_dtype, tm):
    m, d = x2d.shape
    return pl.pallas_call(
        _rms_kernel,
        grid=(m // tm,),
        in_specs=[pl.BlockSpec((tm, d), lambda i: (i, 0)), pl.BlockSpec((1, d), lambda i: (0, 0))],
        out_specs=pl.BlockSpec((tm, d), lambda i: (i, 0)),
        out_shape=jax.ShapeDtypeStruct((m, d), out_dtype),
        compiler_params=_params(("parallel",)),
        name="rmsnorm",
    )(x2d, g.reshape(1, d))


def _proj_kernel(h_ref, w_ref, *o_refs):
    acc = _dot(h_ref[...], w_ref[...])
    for o_ref in o_refs:
        o_ref[...] = acc.astype(o_ref.dtype)


def _project(h, w, col0, n, out_dtypes, tm, name):
    m, d = h.shape
    j0 = col0 // PROJ_TN
    return pl.pallas_call(
        _proj_kernel,
        grid=(m // tm, n // PROJ_TN),
        in_specs=[pl.BlockSpec((tm, d), lambda i, j: (i, 0)), pl.BlockSpec((d, PROJ_TN), lambda i, j: (0, j0 + j))],
        out_specs=[pl.BlockSpec((tm, PROJ_TN), lambda i, j: (i, j)) for _ in out_dtypes],
        out_shape=[jax.ShapeDtypeStruct((m, n), dt) for dt in out_dtypes],
        compiler_params=_params(("parallel", "arbitrary")),
        name=name,
    )(h, w)


def _small_kernel(h_ref, w_ref, b_ref, o_ref, *, n_gate):
    z = _dot(h_ref[...], w_ref[...]) + b_ref[...]
    lane = lax.broadcasted_iota(jnp.int32, z.shape, 1)
    log_sig = jnp.minimum(z, 0.0) - jnp.log1p(jnp.exp(-jnp.abs(z)))
    o_ref[...] = jnp.where(lane < n_gate, _sigmoid(z), log_sig)


def _project_small(h, w, b, tm, n_gate):
    m, d = h.shape
    return pl.pallas_call(
        functools.partial(_small_kernel, n_gate=n_gate),
        grid=(m // tm,),
        in_specs=[pl.BlockSpec((tm, d), lambda i: (i, 0)), pl.BlockSpec((d, LANES), lambda i: (0, 0)),
                  pl.BlockSpec((1, LANES), lambda i: (0, 0))],
        out_specs=pl.BlockSpec((tm, LANES), lambda i: (i, 0)),
        out_shape=jax.ShapeDtypeStruct((m, LANES), F32),
        compiler_params=_params(("parallel",)),
        name="proj_gates_logf",
    )(h, w, b)


PROJ_GROUPS = (("q_n", NSA_WIDTH), ("kv4", 4 * KV_WIDTH), ("kvwin", 2 * KV_WIDTH), ("z_n", NSA_WIDTH),
               ("q_f", FOX_WIDTH), ("kv_f", 2 * FOX_WIDTH), ("z_f", FOX_WIDTH))
PROJ_OUT_DTYPES = {"q_n": (BF16,), "kv4": (F32, BF16), "kvwin": (F32, BF16), "z_n": (F32,),
                   "q_f": (BF16,), "kv_f": (F32, BF16), "z_f": (F32,)}


def _split_weights(w_in, b_gate, b_forget):
    cuts = [NSA_WIDTH, 6 * KV_WIDTH, 3 * NSA_HEADS, NSA_WIDTH, FOX_WIDTH, FOX_WIDTH, FOX_WIDTH, FOX_HEADS, FOX_WIDTH]
    offs = [0]
    for c in cuts:
        offs.append(offs[-1] + c)
    main = jnp.concatenate([w_in[:, offs[0]:offs[2]], w_in[:, offs[3]:offs[7]], w_in[:, offs[8]:offs[9]]],
                           axis=1).astype(BF16)
    n_small = 3 * NSA_HEADS + FOX_HEADS
    w_small = jnp.concatenate([w_in[:, offs[2]:offs[3]], w_in[:, offs[7]:offs[8]]], axis=1)
    w_small = jnp.pad(w_small, ((0, 0), (0, LANES - n_small))).astype(BF16)
    b_small = jnp.pad(jnp.concatenate([b_gate, b_forget]), (0, LANES - n_small)).reshape(1, LANES).astype(F32)
    return {"main": main, "small": w_small, "b_small": b_small}


def _project_all(x2d, norm_in, wts, tm):
    h = _rmsnorm(x2d, norm_in, BF16, min(tm, 512))
    out = {}
    col0 = 0
    for name, width in PROJ_GROUPS:
        res = _project(h, wts["main"], col0, width, PROJ_OUT_DTYPES[name], tm, "proj_" + name)
        out[name] = res[0]
        if len(res) > 1:
            out[name + "_bf"] = res[1]
        col0 += width
    out["small"] = _project_small(h, wts["small"], wts["b_small"], min(tm, 512), 3 * NSA_HEADS)
    return out


def _merge_kernel(x_ref, mn_ref, mf_ref, wn_ref, wf_ref, g_ref, o_ref):
    y = x_ref[...] + _dot(mn_ref[...], wn_ref[...]) + _dot(mf_ref[...], wf_ref[...])
    ms = jnp.mean(y * y, axis=-1, keepdims=True)
    o_ref[...] = y * lax.rsqrt(ms + RMS_EPS) * g_ref[...]


def _merge(x2d, mix_n, mix_f, w_out_bf, norm_final, tm):
    m, d = x2d.shape
    return pl.pallas_call(
        _merge_kernel,
        grid=(m // tm,),
        in_specs=[pl.BlockSpec((tm, d), lambda i: (i, 0)),
                  pl.BlockSpec((tm, NSA_WIDTH), lambda i: (i, 0)),
                  pl.BlockSpec((tm, FOX_WIDTH), lambda i: (i, 0)),
                  pl.BlockSpec((NSA_WIDTH, d), lambda i: (0, 0)),
                  pl.BlockSpec((FOX_WIDTH, d), lambda i: (NSA_WIDTH // FOX_WIDTH, 0)),
                  pl.BlockSpec((1, d), lambda i: (0, 0))],
        out_specs=pl.BlockSpec((tm, d), lambda i: (i, 0)),
        out_shape=jax.ShapeDtypeStruct((m, d), F32),
        compiler_params=_params(("parallel",)),
        name="merge_out_proj",
    )(x2d, mix_n, mix_f, w_out_bf, w_out_bf, norm_final.reshape(1, d))


def _cumsum_kernel(x_ref, o_ref):
    o_ref[...] = _lane_cumsum(x_ref[...])


def _cumsum_rows(x):
    return pl.pallas_call(
        _cumsum_kernel,
        out_shape=jax.ShapeDtypeStruct(x.shape, F32),
        name="forget_cumsum",
    )(x)


def _compress_rows(x_ref, pe_ref, kind, w, xs_ref, nb, starts, stride):
    half = nb // 2
    for c in range(CMP_BLOCK):
        pe_c = pe_ref[kind, c:c + 1, :]
        for par in range(2):
            xc = x_ref[pl.ds(starts[par] + c, half, stride=stride), :]
            xs_ref[par * half:(par + 1) * half, c * HEAD_DIM:(c + 1) * HEAD_DIM] = (xc + pe_c).astype(BF16)
    return _dot(xs_ref[...], w)


def _compress_kernel(x_ref, pe_ref, w_ref, o_ref, xs_ref, *, nb):
    out = _compress_rows(x_ref, pe_ref, 0, w_ref[0], xs_ref, nb, (0, CMP_BLOCK), 2 * CMP_BLOCK)
    o_ref[0, 0, 0] = out.astype(o_ref.dtype)


def _compress_prompt(kv4, cmp_pe, cmp_w_bf, batch, seq):
    nb = seq // CMP_BLOCK
    return pl.pallas_call(
        functools.partial(_compress_kernel, nb=nb),
        grid=(batch, 2, NSA_KV_HEADS),
        in_specs=[pl.BlockSpec((seq, HEAD_DIM), lambda b, k, h: (b, k * NSA_KV_HEADS + h)),
                  pl.BlockSpec((1, CMP_BLOCK, HEAD_DIM), lambda b, k, h: (k, 0, 0)),
                  pl.BlockSpec((1, CMP_BLOCK * HEAD_DIM, HEAD_DIM), lambda b, k, h: (k, 0, 0))],
        out_specs=pl.BlockSpec((1, 1, 1, nb, HEAD_DIM), lambda b, k, h: (b, k, h, 0, 0)),
        out_shape=jax.ShapeDtypeStruct((batch, 2, NSA_KV_HEADS, nb, HEAD_DIM), BF16),
        scratch_shapes=[pltpu.VMEM((nb, CMP_BLOCK * HEAD_DIM), BF16)],
        compiler_params=_params(("parallel", "parallel", "arbitrary")),
        name="compress_prompt",
    )(kv4, cmp_pe, cmp_w_bf)


def _select_blocks(score_t, cur, n_keep):
    n_blk = score_t.shape[0]
    j = lax.broadcasted_iota(jnp.int32, score_t.shape, 0)
    forced = (j == 0) | (j == cur) | (j == cur - 1)
    s = jnp.where(j <= cur, jnp.where(forced, FORCED_SCORE, score_t), -1.0)
    n_grp = n_blk // SUBLANES
    grp = [s[r * SUBLANES:(r + 1) * SUBLANES] for r in range(n_grp)]
    cnt = [jnp.zeros(grp[0].shape, F32) for _ in range(n_grp)]
    sub = lax.broadcasted_iota(jnp.int32, grp[0].shape, 0)
    for jj in range(n_blk):
        row = s[jj:jj + 1, :]
        r_j, off = divmod(jj, SUBLANES)
        for r in range(n_grp):
            if r < r_j:
                hit = row > grp[r]
            elif r > r_j:
                hit = row >= grp[r]
            else:
                hit = (row > grp[r]) | ((row == grp[r]) & (sub > off))
            cnt[r] = cnt[r] + jnp.where(hit, 1.0, 0.0)
    return jnp.where((jnp.concatenate(cnt, axis=0) < n_keep) & (j <= cur), 1.0, 0.0)


def _nsa_prompt_kernel(q_ref, kc_ref, vc_ref, ks_ref, vs_ref, kw_ref, vw_ref, g_ref, z_ref, e_ref, o_ref,
                       m_ref, acc_ref, alpha_ref, ps_ref, pw_ref, mix_ref, bias_ref, *, tq, tk, tw, ck, seq):
    qi = pl.program_id(1)
    s0 = qi * tq
    n_cmp = seq // CMP_BLOCK
    n_sel = seq // SEL_BLOCK
    row_t = s0 + lax.broadcasted_iota(jnp.int32, (tq, 1), 0)
    kv_heads = range(NSA_KV_HEADS)
    heads = lambda hkv: [hkv * NSA_GROUP + g for g in range(NSA_GROUP)]
    slopes = lambda hkv: [_head_slope(h) for h in heads(hkv)]
    cols = lambda hkv: slice(hkv * HEAD_DIM, (hkv + 1) * HEAD_DIM)

    def q_rows(hkv):
        return jnp.concatenate([q_ref[:, h * HEAD_DIM:(h + 1) * HEAD_DIM] for h in heads(hkv)], axis=0)

    def gate_col(hkv, br):
        return jnp.concatenate([g_ref[:, 3 * h + br:3 * h + br + 1] for h in heads(hkv)], axis=0)

    for hkv in kv_heads:
        sc = _dot_nt(q_rows(hkv), kc_ref[0, 0, hkv]) * SCALE
        lane = lax.broadcasted_iota(jnp.int32, (tq, n_cmp), 1)
        blk = jnp.where(lane < n_cmp // 2, 2 * lane, 2 * lane - (n_cmp - 1))
        dist_c = (row_t - ((blk + 1) * CMP_BLOCK - 1)).astype(F32)
        valid_c = dist_c >= 0
        score = jnp.zeros((tq, n_cmp), F32)
        probs = []
        for g, slope in enumerate(slopes(hkv)):
            lg = jnp.where(valid_c, sc[g * tq:(g + 1) * tq] - slope * dist_c, NEG_INF)
            mx = jnp.max(lg, axis=1, keepdims=True)
            mx = jnp.where(mx == NEG_INF, 0.0, mx)
            p = jnp.exp(lg - mx)
            p = p / jnp.maximum(jnp.sum(p, axis=1, keepdims=True), 1e-30)
            score = score + p
            probs.append(p)
        o_c = _dot(jnp.concatenate(probs, axis=0).astype(BF16), vc_ref[0, 0, hkv])
        mix_ref[hkv] = gate_col(hkv, 0) * o_c

        score_t = score.T
        score_t = score_t[:n_sel] + score_t[n_sel:]
        cur = (s0 + lax.broadcasted_iota(jnp.int32, (n_sel, tq), 1)) // SEL_BLOCK
        sel_t = _select_blocks(score_t, cur, SEL_TOPK)
        sel = jnp.concatenate([sel_t, jnp.zeros((LANES - n_sel, tq), F32)], axis=0).T.astype(BF16)

        def bias_body(c, carry, hkv=hkv, sel=sel):
            c0 = pl.multiple_of(c * ck, ck)
            expanded = _dot(sel, e_ref[:, pl.ds(c0, ck)])
            key = c0 + lax.broadcasted_iota(jnp.int32, (tq, ck), 1)
            bias_ref[hkv, :, pl.ds(c0, ck)] = jnp.where((expanded > 0.5) & (key <= row_t), 0.0, NEG_INF)
            return carry

        lax.fori_loop(0, (s0 + tq + ck - 1) // ck, bias_body, 0)

    def stream(hkv, s, bias, key_rel, v, p_ref, may_be_empty):
        sl = slopes(hkv)
        logits_of = lambda g: s[g * tq:(g + 1) * tq] * (SCALE * LOG2E) + (bias + (sl[g] * LOG2E) * key_rel)
        _state_update_chunked(logits_of, NSA_GROUP, tq, _with_ones(v), m_ref.at[hkv], acc_ref.at[hkv],
                              p_ref.at[hkv], alpha_ref.at[hkv], may_be_empty)

    _state_init(m_ref, acc_ref)

    def slc_body(i, carry):
        k0 = pl.multiple_of(i * tk, tk)
        key_rel = (k0 - s0 + lax.broadcasted_iota(jnp.int32, (1, tk), 1)).astype(F32)
        for hkv in kv_heads:
            s = _dot_nt(q_rows(hkv), ks_ref[pl.ds(k0, tk), cols(hkv)])
            stream(hkv, s, bias_ref[hkv, :, pl.ds(k0, tk)], key_rel, vs_ref[pl.ds(k0, tk), cols(hkv)], ps_ref, False)
        return carry

    lax.fori_loop(0, (s0 + tq + tk - 1) // tk, slc_body, 0)
    for hkv in kv_heads:
        mix_ref[hkv] = mix_ref[hkv] + gate_col(hkv, 1) * _state_result(acc_ref.at[hkv])

    _state_init(m_ref, acc_ref)

    def win_body(w, carry):
        k0 = pl.multiple_of(s0 - WINDOW + w * tw, tw)
        disti = row_t - (k0 + lax.broadcasted_iota(jnp.int32, (tq, tw), 1))
        bias = jnp.where((disti >= 0) & (disti < WINDOW), 0.0, NEG_INF)
        key_rel = (k0 - s0 + lax.broadcasted_iota(jnp.int32, (1, tw), 1)).astype(F32)
        for hkv in kv_heads:
            s = _dot_nt(q_rows(hkv), kw_ref[pl.ds(k0, tw), cols(hkv)])
            stream(hkv, s, bias, key_rel, vw_ref[pl.ds(k0, tw), cols(hkv)], pw_ref, True)
        return carry

    n_win_tiles = (WINDOW + tq) // tw
    lax.fori_loop(jnp.maximum(0, (WINDOW - s0) // tw), n_win_tiles, win_body, 0)

    for hkv in kv_heads:
        mix = mix_ref[hkv] + gate_col(hkv, 2) * _state_result(acc_ref.at[hkv])
        for g, h in enumerate(heads(hkv)):
            z = z_ref[:, h * HEAD_DIM:(h + 1) * HEAD_DIM]
            o_ref[:, h * HEAD_DIM:(h + 1) * HEAD_DIM] = (mix[g * tq:(g + 1) * tq] * _silu(z)).astype(o_ref.dtype)


def _nsa_prompt(proj, kcv, expand, batch, seq):
    tq, tk, tw, ck = 256, 256, 256, 512
    assert seq % ck == 0 and WINDOW % tw == 0 and tq % tw == 0 and seq // SEL_BLOCK <= LANES
    nq = seq // tq
    rows = NSA_GROUP * tq
    kernel = functools.partial(_nsa_prompt_kernel, tq=tq, tk=tk, tw=tw, ck=ck, seq=seq)
    cmp_spec = lambda kind: pl.BlockSpec((1, 1, NSA_KV_HEADS, seq // CMP_BLOCK, HEAD_DIM),
                                         lambda b, i: (b, kind, 0, 0, 0))
    return pl.pallas_call(
        kernel,
        grid=(batch, nq),
        in_specs=[pl.BlockSpec((tq, NSA_WIDTH), lambda b, i: (b * nq + i, 0)),
                  cmp_spec(0), cmp_spec(1),
                  pl.BlockSpec((seq, KV_WIDTH), lambda b, i: (b, 2)),
                  pl.BlockSpec((seq, KV_WIDTH), lambda b, i: (b, 3)),
                  pl.BlockSpec((seq, KV_WIDTH), lambda b, i: (b, 0)),
                  pl.BlockSpec((seq, KV_WIDTH), lambda b, i: (b, 1)),
                  pl.BlockSpec((tq, LANES), lambda b, i: (b * nq + i, 0)),
                  pl.BlockSpec((tq, NSA_WIDTH), lambda b, i: (b * nq + i, 0)),
                  pl.BlockSpec((LANES, seq), lambda b, i: (0, 0))],
        out_specs=pl.BlockSpec((tq, NSA_WIDTH), lambda b, i: (b * nq + i, 0)),
        out_shape=jax.ShapeDtypeStruct((batch * seq, NSA_WIDTH), BF16),
        scratch_shapes=[pltpu.VMEM((NSA_KV_HEADS, rows, LANES), F32),
                        pltpu.VMEM((NSA_KV_HEADS, rows, 2 * HEAD_DIM), F32),
                        pltpu.VMEM((NSA_KV_HEADS, rows, LANES), F32),
                        pltpu.VMEM((NSA_KV_HEADS, rows, tk), BF16),
                        pltpu.VMEM((NSA_KV_HEADS, rows, tw), BF16),
                        pltpu.VMEM((NSA_KV_HEADS, rows, HEAD_DIM), F32),
                        pltpu.VMEM((NSA_KV_HEADS, tq, seq), F32)],
        compiler_params=_params(("parallel", "arbitrary")),
        name="nsa_prompt",
    )(proj["q_n"], kcv, kcv, proj["kv4_bf"], proj["kv4_bf"], proj["kvwin_bf"], proj["kvwin_bf"],
      proj["small"], proj["z_n"], expand)


def _fox_prompt_kernel(q_ref, k_ref, v_ref, d_ref, z_ref, o_ref, m_ref, acc_ref, alpha_ref, p_ref, *,
                       t, chunk, n_heads):
    qi = pl.program_id(2)
    _state_init(m_ref, acc_ref)

    def tile(i, causal):
        k0 = pl.multiple_of(i * t, t)
        for hh in range(n_heads):
            cols = slice(hh * HEAD_DIM, (hh + 1) * HEAD_DIM)
            s = _dot_nt(q_ref[:, cols], k_ref[pl.ds(k0, t), cols])
            d_row = d_ref[hh, :, pl.ds(k0, t)] * LOG2E

            def logits_of(c, s=s, d_row=d_row):
                lg = s[c * chunk:(c + 1) * chunk] * (SCALE * LOG2E) - d_row
                if causal:
                    row = c * chunk + lax.broadcasted_iota(jnp.int32, (chunk, t), 0)
                    col = lax.broadcasted_iota(jnp.int32, (chunk, t), 1)
                    lg = jnp.where(col <= row, lg, NEG_INF)
                return lg

            _state_update_chunked(logits_of, t // chunk, chunk, _with_ones(v_ref[pl.ds(k0, t), cols]),
                                  m_ref.at[hh], acc_ref.at[hh], p_ref.at[hh], alpha_ref.at[hh], False)

    def body(i, carry):
        tile(i, False)
        return carry

    lax.fori_loop(0, qi, body, 0)
    tile(qi, True)
    for hh in range(n_heads):
        cols = slice(hh * HEAD_DIM, (hh + 1) * HEAD_DIM)
        o_ref[:, cols] = (_state_result(acc_ref.at[hh]) * _silu(z_ref[:, cols])).astype(o_ref.dtype)


def _fox_prompt(proj, d_rows, batch, seq):
    t, chunk, n_heads = 512, 128, 4
    nq = seq // t
    width = n_heads * HEAD_DIM
    n_hg = FOX_HEADS // n_heads
    return pl.pallas_call(
        functools.partial(_fox_prompt_kernel, t=t, chunk=chunk, n_heads=n_heads),
        grid=(batch, n_hg, nq),
        in_specs=[pl.BlockSpec((t, width), lambda b, h, i: (b * nq + i, h)),
                  pl.BlockSpec((seq, width), lambda b, h, i: (b, h)),
                  pl.BlockSpec((seq, width), lambda b, h, i: (b, n_hg + h)),
                  pl.BlockSpec((n_heads, 1, seq), lambda b, h, i: (b * n_hg + h, 0, 0)),
                  pl.BlockSpec((t, width), lambda b, h, i: (b * nq + i, h))],
        out_specs=pl.BlockSpec((t, width), lambda b, h, i: (b * nq + i, h)),
        out_shape=jax.ShapeDtypeStruct((batch * seq, FOX_WIDTH), BF16),
        scratch_shapes=[pltpu.VMEM((n_heads, t, LANES), F32), pltpu.VMEM((n_heads, t, 2 * HEAD_DIM), F32),
                        pltpu.VMEM((n_heads, t, LANES), F32), pltpu.VMEM((n_heads, t, t), BF16)],
        compiler_params=_params(("parallel", "parallel", "arbitrary")),
        name="fox_prompt",
    )(proj["q_f"], proj["kv_f_bf"], proj["kv_f_bf"], d_rows, proj["z_f"])


def _prompt_layer(x, norm_in, wts, cmp_pe, cmp_w_bf, w_out_bf, norm_final):
    batch, seq, d = x.shape
    x2d = x.reshape(batch * seq, d)
    proj = _project_all(x2d, norm_in, wts, 1024)
    n_gate = 3 * NSA_HEADS
    logf = proj["small"][:, n_gate:n_gate + FOX_HEADS].reshape(batch, seq, FOX_HEADS)
    d_rows = _cumsum_rows(jnp.swapaxes(logf, 1, 2).reshape(batch * FOX_HEADS, seq))
    kcv = _compress_prompt(proj["kv4"], cmp_pe, cmp_w_bf, batch, seq)
    blk_of_key = jnp.arange(seq, dtype=jnp.int32) // SEL_BLOCK
    expand = (jnp.arange(LANES, dtype=jnp.int32)[:, None] == blk_of_key[None, :]).astype(BF16)
    mix_n = _nsa_prompt(proj, kcv, expand, batch, seq)
    mix_f = _fox_prompt(proj, d_rows.reshape(batch * FOX_HEADS, 1, seq), batch, seq)
    y = _merge(x2d, mix_n, mix_f, w_out_bf, norm_final, 512).reshape(batch, seq, d)
    w_keep = min(WINDOW, seq)
    state = (proj["kv4"].reshape(batch, seq, 4, NSA_KV_HEADS, HEAD_DIM),
             proj["kv_f"].reshape(batch, seq, 2, FOX_HEADS, HEAD_DIM),
             logf,
             proj["kvwin"].reshape(batch, seq, 2, NSA_KV_HEADS, HEAD_DIM)[:, seq - w_keep:])
    return y, state


HALF_CHANNELS = NSA_CHANNELS // 2


def _nsa_page_specs(pg, half):
    def spec(u):
        return pl.BlockSpec((1, 1, PAGE_SIZE, 1, HALF_CHANNELS, HEAD_DIM),
                            lambda b, j, pt: (0, pt[b, j * pg + u], 0, half, 0, 0))
    return [spec(u) for u in range(pg)]


def _channel(page_ref, ch, start=0, n=PAGE_SIZE):
    rows = page_ref.reshape(PAGE_SIZE * HALF_CHANNELS, HEAD_DIM)
    return rows[pl.ds(start * HALF_CHANNELS + ch, n, stride=HALF_CHANNELS), :]


def _nsa_cmp_kernel(pt_ref, *refs, pg, n_pages):
    page_refs = refs[:pg]
    q_ref, pe_ref, w_ref, oc_ref, sel_ref, stage_ref, xs_ref = refs[pg:]
    jg = pl.program_id(1)
    past = n_pages * PAGE_SIZE
    nb = past // CMP_BLOCK
    half = nb // 2
    blocks_per_page = PAGE_SIZE // CMP_BLOCK
    for u in range(pg):
        for bl in range(blocks_per_page):
            m = (jg * pg + u) * (blocks_per_page // 2) + bl // 2
            dst = pl.multiple_of(((bl % 2) * half + m) * STAGE_PITCH, SUBLANES)
            for ch in range(2 * NSA_KV_HEADS):
                stage_ref[ch, pl.ds(dst, CMP_BLOCK), :] = _channel(page_refs[u], ch, bl * CMP_BLOCK, CMP_BLOCK)

    @pl.when(jg == pl.num_programs(1) - 1)
    def _():
        cmp = [[_compress_rows(stage_ref.at[kind * NSA_KV_HEADS + h], pe_ref, kind, w_ref[kind], xs_ref, nb,
                               (0, half * STAGE_PITCH), STAGE_PITCH).astype(BF16)
                for h in range(NSA_KV_HEADS)] for kind in range(2)]
        rows = NSA_GROUP * T_PAD
        for hkv in range(NSA_KV_HEADS):
            q = q_ref[0, hkv * rows:(hkv + 1) * rows, :].astype(BF16)
            kc, vc = cmp[0][hkv], cmp[1][hkv]
            s = _dot_nt(q, kc) * SCALE
            lane = lax.broadcasted_iota(jnp.int32, (rows, nb), 1)
            blk = jnp.where(lane < half, 2 * lane, 2 * lane - (nb - 1))
            row = lax.broadcasted_iota(jnp.int32, (rows, nb), 0)
            slope = _head_slope(hkv * NSA_GROUP + row // T_PAD)
            dist = (past + row % T_PAD - ((blk + 1) * CMP_BLOCK - 1)).astype(F32)
            lg = jnp.where(dist >= 0, s - slope * dist, NEG_INF)
            mx = jnp.max(lg, axis=1, keepdims=True)
            mx = jnp.where(mx == NEG_INF, 0.0, mx)
            p = jnp.exp(lg - mx)
            p = p / jnp.maximum(jnp.sum(p, axis=1, keepdims=True), 1e-30)
            oc_ref[0, hkv * rows:(hkv + 1) * rows, :] = _dot(p.astype(BF16), vc)
            score = p[0:T_PAD]
            for g in range(1, NSA_GROUP):
                score = score + p[g * T_PAD:(g + 1) * T_PAD]
            score = score[:, :half] + score[:, half:]
            blk_s = lax.broadcasted_iota(jnp.int32, (T_PAD, half), 1).astype(F32)
            forced = (blk_s == 0) | (blk_s == half - 1)
            s_left = jnp.where(forced, NEG_INF, score)
            picked = jnp.zeros((T_PAD, half), F32)
            for _ in range(SEL_TOPK - 3):
                best = jnp.max(s_left, axis=1, keepdims=True)
                first = jnp.min(jnp.where(s_left == best, blk_s, float(half)), axis=1, keepdims=True)
                hit = blk_s == first
                picked = jnp.where(hit, 1.0, picked)
                s_left = jnp.where(hit, NEG_INF, s_left)
            sel_ref[0, hkv * T_PAD:(hkv + 1) * T_PAD, :] = jnp.where(forced, 1.0, picked)


def _nsa_cmp_sample(cache, page_table, q_rows, cmp_pe, cmp_w_bf, pg):
    n_seq, n_pages = page_table.shape
    past = n_pages * PAGE_SIZE
    nb = past // CMP_BLOCK
    assert n_pages % pg == 0 and past // SEL_BLOCK >= SEL_TOPK
    rows = NSA_HEADS * T_PAD
    grid_spec = pltpu.PrefetchScalarGridSpec(
        num_scalar_prefetch=1,
        grid=(n_seq, n_pages // pg),
        in_specs=_nsa_page_specs(pg, 0) + [
            pl.BlockSpec((1, rows, HEAD_DIM), lambda b, j, pt: (b, 0, 0)),
            pl.BlockSpec((2, CMP_BLOCK, HEAD_DIM), lambda b, j, pt: (0, 0, 0)),
            pl.BlockSpec((2, CMP_BLOCK * HEAD_DIM, HEAD_DIM), lambda b, j, pt: (0, 0, 0))],
        out_specs=[pl.BlockSpec((1, rows, HEAD_DIM), lambda b, j, pt: (b, 0, 0)),
                   pl.BlockSpec((1, NSA_KV_HEADS * T_PAD, past // SEL_BLOCK), lambda b, j, pt: (b, 0, 0))],
        scratch_shapes=[pltpu.VMEM((2 * NSA_KV_HEADS, nb * STAGE_PITCH, HEAD_DIM), F32),
                        pltpu.VMEM((nb, CMP_BLOCK * HEAD_DIM), BF16)],
    )
    return pl.pallas_call(
        functools.partial(_nsa_cmp_kernel, pg=pg, n_pages=n_pages),
        grid_spec=grid_spec,
        out_shape=[jax.ShapeDtypeStruct((n_seq, rows, HEAD_DIM), F32),
                   jax.ShapeDtypeStruct((n_seq, NSA_KV_HEADS * T_PAD, past // SEL_BLOCK), F32)],
        compiler_params=_params(("parallel", "arbitrary")),
        name="nsa_sample_cmp",
    )(page_table, *([cache] * pg), q_rows, cmp_pe, cmp_w_bf)


def _nsa_slc_kernel(pt_ref, *refs, pg, part_pages, n_pages, w_buf, n_new):
    page_refs = refs[:pg]
    (q_ref, sel_ref, e_ref, new_ref, win_ref, oc_ref, g_ref, z_ref, o_ref, tab_ref, m_ref, acc_ref) = refs[pg:]
    jg = pl.program_id(1)
    past = n_pages * PAGE_SIZE
    hr = n_new * NSA_GROUP
    rows = NSA_KV_HEADS * hr
    row = lax.broadcasted_iota(jnp.int32, (rows, 1), 0)
    t_q = (row % hr) // NSA_GROUP
    slope = _head_slope((row // hr) * NSA_GROUP + row % NSA_GROUP)
    q = [q_ref[0, h * hr:(h + 1) * hr, :].astype(BF16) for h in range(NSA_KV_HEADS)]

    def scores(k_of):
        return jnp.concatenate([_dot_nt(q[h], k_of(h)) for h in range(NSA_KV_HEADS)], axis=0) * SCALE

    def pv(v_of):
        return lambda p: jnp.concatenate(
            [_dot(p[h * hr:(h + 1) * hr], _with_ones(v_of(h))) for h in range(NSA_KV_HEADS)], axis=0)


    @pl.when(jg == 0)
    def _():
        expanded = _dot(sel_ref[0].astype(BF16), e_ref[...])
        key = lax.broadcasted_iota(jnp.int32, (rows, past), 1)
        tab_ref[...] = jnp.where(expanded > 0.5, 0.0, NEG_INF) + slope * (key - past).astype(F32)
        _state_init(m_ref, acc_ref)

    def page_stream(ch, pages):
        return jnp.concatenate([_channel(page_refs[u], ch).astype(BF16) for u in pages], axis=0)

    parts = []
    for first in range(0, pg, part_pages):
        pages = range(first, first + part_pages)
        c0 = pl.multiple_of((jg * pg + first) * PAGE_SIZE, part_pages * PAGE_SIZE)
        lg = (scores(lambda h: page_stream(h, pages))
              + tab_ref[:, pl.ds(c0, part_pages * PAGE_SIZE)])
        parts.append((lg, pv(lambda h, pages=pages: page_stream(NSA_KV_HEADS + h, pages))))
    _state_update_parts(parts, m_ref, acc_ref, True)

    @pl.when(jg == pl.num_programs(1) - 1)
    def _():
        new = lambda ch: new_ref[0, ch].astype(BF16)
        j_new = lax.broadcasted_iota(jnp.int32, (rows, PAGE_SIZE), 1)
        bias_new = jnp.where((j_new <= t_q) & (j_new < n_new), slope * j_new.astype(F32), NEG_INF)
        _state_update(scores(lambda h: new(h)) + bias_new, pv(lambda h: new(NSA_KV_HEADS + h)),
                      m_ref, acc_ref, False)
        o_s = _state_result(acc_ref)

        _state_init(m_ref, acc_ref)
        i_w = lax.broadcasted_iota(jnp.int32, (rows, w_buf), 1)
        dist_w = t_q + w_buf - i_w
        valid_w = (dist_w >= 0) & (dist_w < WINDOW) & (past - w_buf + i_w >= 0)
        bias_w = jnp.where(valid_w, slope * (i_w - w_buf).astype(F32), NEG_INF)
        win = lambda ch: win_ref[0, ch].astype(BF16)
        _state_update(scores(lambda h: win(h)) + bias_w, pv(lambda h: win(NSA_KV_HEADS + h)),
                      m_ref, acc_ref, True)
        _state_update(scores(lambda h: new(2 * NSA_KV_HEADS + h)) + bias_new,
                      pv(lambda h: new(3 * NSA_KV_HEADS + h)), m_ref, acc_ref, True)
        o_w = _state_result(acc_ref)

        gates = g_ref[0]
        mix = gates[:, 0:1] * oc_ref[0] + gates[:, 1:2] * o_s + gates[:, 2:3] * o_w
        o_ref[0] = mix * _silu(z_ref[0])


def _nsa_slc_sample(cache, page_table, q_rows, sel_rows, expand, new_rows, win_state, o_c_rows, gates_rows, z_rows,
                    pg, n_new):
    n_seq, n_pages = page_table.shape
    past = n_pages * PAGE_SIZE
    w_buf = win_state.shape[2]
    rows = n_new * NSA_HEADS
    per_seq = lambda *shape: pl.BlockSpec((1,) + shape, lambda b, j, pt: (b,) + (0,) * len(shape))
    grid_spec = pltpu.PrefetchScalarGridSpec(
        num_scalar_prefetch=1,
        grid=(n_seq, n_pages // pg),
        in_specs=_nsa_page_specs(pg, 1) + [
            per_seq(rows, HEAD_DIM),
            per_seq(rows, past // SEL_BLOCK),
            pl.BlockSpec((past // SEL_BLOCK, past), lambda b, j, pt: (0, 0)),
            per_seq(4 * NSA_KV_HEADS, PAGE_SIZE, HEAD_DIM),
            per_seq(2 * NSA_KV_HEADS, w_buf, HEAD_DIM),
            per_seq(rows, HEAD_DIM),
            per_seq(rows, 3),
            per_seq(rows, HEAD_DIM)],
        out_specs=per_seq(rows, HEAD_DIM),
        scratch_shapes=[pltpu.VMEM((rows, past), F32), pltpu.VMEM((rows, LANES), F32),
                        pltpu.VMEM((rows, 2 * HEAD_DIM), F32)],
    )
    return pl.pallas_call(
        functools.partial(_nsa_slc_kernel, pg=pg, part_pages=2, n_pages=n_pages, w_buf=w_buf, n_new=n_new),
        grid_spec=grid_spec,
        out_shape=jax.ShapeDtypeStruct((n_seq, rows, HEAD_DIM), F32),
        compiler_params=_params(("parallel", "arbitrary")),
        name="nsa_sample_slc_win",
    )(page_table, *([cache] * pg), q_rows, sel_rows, expand, new_rows, win_state, o_c_rows, gates_rows, z_rows)


def _page_forget_scan(x):
    n = x.shape[1]
    lane = lax.broadcasted_iota(jnp.int32, x.shape, 1)
    s = FOX_HEADS
    while s < n:
        x = x + jnp.where(lane >= s, pltpu.roll(x, s, axis=1), 0.0)
        s *= 2
    total = jnp.where(lane >= n - FOX_HEADS, x, 0.0)
    s = FOX_HEADS
    while s < n:
        total = total + pltpu.roll(total, n - s, axis=1)
        s *= 2
    return x, total


def _fox_sample_kernel(pt_ref, *refs, pg, part_pages, n_new):
    page_refs = refs[:pg]
    (q_ref, lc_ref, kn_ref, vn_ref, ln_ref, z_ref, o_ref, m_ref, acc_ref, carry_ref, lg_ref) = refs[pg:]
    b = pl.program_id(0)
    jg = pl.program_id(1)
    rows = n_new * FOX_HEADS
    keys = PAGE_SIZE * FOX_HEADS

    @pl.when(jg == 0)
    def _():
        _state_init(m_ref, acc_ref)
        carry_ref[...] = jnp.zeros(carry_ref.shape, F32)

    q = q_ref[0].astype(BF16)
    own = (lax.broadcasted_iota(jnp.int32, (rows, keys), 1) % FOX_HEADS
           == lax.broadcasted_iota(jnp.int32, (rows, keys), 0) % FOX_HEADS)

    for u in range(pg):
        lg_ref[u:u + 1, :] = lc_ref[pl.ds(pt_ref[b, jg * pg + u], 1), :]
    within, total = _page_forget_scan(lg_ref[...])
    carry = carry_ref[...]
    logits = []
    for u in range(pg):
        d_u = within[u:u + 1] + carry
        carry = carry + total[u:u + 1]
        k = page_refs[u][0, 0, :, 0].reshape(keys, HEAD_DIM).astype(BF16)
        logits.append(jnp.where(own, _dot_nt(q, k) * SCALE - d_u, NEG_INF))
    carry_ref[...] = carry

    def pv(pages):
        def pv_of(p):
            out = None
            for i, u in enumerate(pages):
                v = page_refs[u][0, 0, :, 1].reshape(keys, HEAD_DIM).astype(BF16)
                part = _dot(p[:, i * keys:(i + 1) * keys], _with_ones(v))
                out = part if out is None else out + part
            return out
        return pv_of

    parts = []
    for first in range(0, pg, part_pages):
        pages = range(first, first + part_pages)
        parts.append((jnp.concatenate([logits[u] for u in pages], axis=1), pv(pages)))
    _state_update_parts(parts, m_ref, acc_ref, False)

    @pl.when(jg == pl.num_programs(1) - 1)
    def _():
        within_new, _ = _page_forget_scan(ln_ref[0])
        d_new = (within_new + carry_ref[...])[:, :PAGE_SIZE]
        c = lax.broadcasted_iota(jnp.int32, (rows, PAGE_SIZE), 1)
        r = lax.broadcasted_iota(jnp.int32, (rows, PAGE_SIZE), 0)
        valid = (c < rows) & (c % FOX_HEADS == r % FOX_HEADS) & (c // FOX_HEADS <= r // FOX_HEADS)
        lg = jnp.where(valid, _dot_nt(q, kn_ref[0].astype(BF16)) * SCALE - d_new, NEG_INF)
        _state_update(lg, lambda p: _dot(p, _with_ones(vn_ref[0].astype(BF16))), m_ref, acc_ref, False)
        o_ref[0] = _state_result(acc_ref) * _silu(z_ref[0])


def _fox_sample(cache, page_table, q_rows, logf_cache, k_new, v_new, logf_new, z_rows, pg, n_new):
    n_seq, n_pages = page_table.shape
    n_pool = logf_cache.shape[0]
    rows = n_new * FOX_HEADS
    keys = PAGE_SIZE * FOX_HEADS
    per_seq = lambda *shape: pl.BlockSpec((1,) + shape, lambda b, j, pt: (b,) + (0,) * len(shape))

    def page_spec(u):
        return pl.BlockSpec((1, 1, PAGE_SIZE, 2, FOX_HEADS, HEAD_DIM),
                            lambda b, j, pt: (0, pt[b, j * pg + u], 0, 0, 0, 0))

    grid_spec = pltpu.PrefetchScalarGridSpec(
        num_scalar_prefetch=1,
        grid=(n_seq, n_pages // pg),
        in_specs=[page_spec(u) for u in range(pg)] + [
            per_seq(rows, HEAD_DIM),
            pl.BlockSpec((n_pool, keys), lambda b, j, pt: (0, 0)),
            per_seq(PAGE_SIZE, HEAD_DIM),
            per_seq(PAGE_SIZE, HEAD_DIM),
            per_seq(1, keys),
            per_seq(rows, HEAD_DIM)],
        out_specs=per_seq(rows, HEAD_DIM),
        scratch_shapes=[pltpu.VMEM((rows, LANES), F32), pltpu.VMEM((rows, 2 * HEAD_DIM), F32),
                        pltpu.VMEM((1, keys), F32), pltpu.VMEM((pg, keys), F32)],
    )
    return pl.pallas_call(
        functools.partial(_fox_sample_kernel, pg=pg, part_pages=2, n_new=n_new),
        grid_spec=grid_spec,
        out_shape=jax.ShapeDtypeStruct((n_seq, rows, HEAD_DIM), F32),
        compiler_params=_params(("parallel", "arbitrary")),
        name="fox_sample",
    )(page_table, *([cache] * pg), q_rows, logf_cache, k_new, v_new, logf_new, z_rows)


def _held_pages(page_table, needed, pg):
    n_seq, n_pages = page_table.shape
    slots = page_table.reshape(n_seq * n_pages // pg, pg)
    want = needed.reshape(slots.shape).at[0].set(True)
    step = jnp.arange(slots.shape[0], dtype=jnp.int32)[:, None]
    filled_at = lax.cummax(jnp.where(want, step, 0), axis=0)
    return jnp.take_along_axis(slots, filled_at, axis=0).reshape(n_seq, n_pages)


def _pad_axis(a, axis, size):
    pad = [(0, 0)] * a.ndim
    pad[axis] = (0, size - a.shape[axis])
    return jnp.pad(a, pad)


def _sample_layer(x, cache_nsa_kv, cache_fox_kv, cache_fox_logf, state_nsa_win, page_table,
                  norm_in, wts, cmp_pe, cmp_w_bf, w_out_bf, norm_final):
    n_seq, t_new, d = x.shape
    n_pool = cache_nsa_kv.shape[1]
    n_pages = page_table.shape[1]
    past = n_pages * PAGE_SIZE
    w_buf = state_nsa_win.shape[2]
    assert t_new <= T_PAD and t_new < SEL_BLOCK and past % SEL_BLOCK == 0
    x2d = x.reshape(n_seq * t_new, d)
    proj = _project_all(x2d, norm_in, wts, n_seq * t_new)
    n_gate = 3 * NSA_HEADS
    small = proj["small"].reshape(n_seq, t_new, LANES)
    logf = small[:, :, n_gate:n_gate + FOX_HEADS]

    def head_rows(a):
        a = _pad_axis(a.reshape(n_seq, t_new, NSA_HEADS, -1), 1, T_PAD)
        return jnp.swapaxes(a, 1, 2).reshape(n_seq, NSA_HEADS * T_PAD, a.shape[-1])

    def group_rows(a):
        a = a.reshape(n_seq, t_new, NSA_KV_HEADS, NSA_GROUP, a.shape[-1])
        return jnp.swapaxes(a, 1, 2).reshape(n_seq, NSA_HEADS * t_new, a.shape[-1])

    q_n = proj["q_n"].astype(F32)
    gates = small[:, :, :n_gate].reshape(n_seq, t_new, NSA_HEADS, 3)
    z_n = proj["z_n"].reshape(n_seq, t_new, NSA_HEADS, HEAD_DIM)
    nsa_new = jnp.concatenate([proj["kv4"].reshape(n_seq, t_new, 4 * NSA_KV_HEADS, HEAD_DIM)[:, :, 2 * NSA_KV_HEADS:],
                               proj["kvwin"].reshape(n_seq, t_new, 2 * NSA_KV_HEADS, HEAD_DIM)], axis=2)
    nsa_new = _pad_axis(jnp.swapaxes(nsa_new, 1, 2), 2, PAGE_SIZE)
    win_state = jnp.swapaxes(state_nsa_win[0].reshape(n_seq, w_buf, 2 * NSA_KV_HEADS, HEAD_DIM), 1, 2)
    blk_of_key = jnp.arange(past, dtype=jnp.int32) // SEL_BLOCK
    expand = (jnp.arange(past // SEL_BLOCK, dtype=jnp.int32)[:, None] == blk_of_key[None, :]).astype(BF16)
    nsa_cache = cache_nsa_kv.reshape(cache_nsa_kv.shape[0], n_pool, PAGE_SIZE, 2, HALF_CHANNELS, HEAD_DIM)

    pg = 16
    o_c, sel = _nsa_cmp_sample(nsa_cache, page_table, head_rows(q_n), cmp_pe, cmp_w_bf, pg)
    o_c = jnp.swapaxes(o_c.reshape(n_seq, NSA_HEADS, T_PAD, HEAD_DIM)[:, :, :t_new], 1, 2)
    sel = sel.reshape(n_seq, NSA_KV_HEADS, T_PAD, past // SEL_BLOCK)[:, :, :t_new]
    page_needed = jnp.any(sel.reshape(n_seq, -1, n_pages, PAGE_SIZE // SEL_BLOCK) > 0.5, axis=(1, 3))
    slc_pages = _held_pages(page_table, page_needed, pg)
    sel = jnp.broadcast_to(sel[:, :, :, None, :], (n_seq, NSA_KV_HEADS, t_new, NSA_GROUP, past // SEL_BLOCK))
    mix_n = _nsa_slc_sample(nsa_cache, slc_pages, group_rows(q_n.reshape(n_seq, t_new, NSA_HEADS, HEAD_DIM)),
                            sel.reshape(n_seq, NSA_HEADS * t_new, past // SEL_BLOCK), expand, nsa_new, win_state,
                            group_rows(o_c), group_rows(gates), group_rows(z_n), pg, t_new)
    mix_n = jnp.swapaxes(mix_n.reshape(n_seq, NSA_KV_HEADS, t_new, NSA_GROUP * HEAD_DIM), 1, 2)
    mix_n = mix_n.reshape(n_seq * t_new, NSA_WIDTH).astype(BF16)

    fox_rows = lambda a: a.reshape(n_seq, t_new * FOX_HEADS, HEAD_DIM)
    kv_f = proj["kv_f"].reshape(n_seq, t_new, 2, FOX_HEADS * HEAD_DIM)
    k_new = _pad_axis(fox_rows(kv_f[:, :, 0]), 1, PAGE_SIZE)
    v_new = _pad_axis(fox_rows(kv_f[:, :, 1]), 1, PAGE_SIZE)
    logf_new = _pad_axis(logf.reshape(n_seq, 1, t_new * FOX_HEADS), 2, PAGE_SIZE * FOX_HEADS)
    logf_cache = cache_fox_logf[0].reshape(n_pool, PAGE_SIZE * FOX_HEADS)
    mix_f = _fox_sample(cache_fox_kv, page_table, fox_rows(proj["q_f"].astype(F32)), logf_cache, k_new, v_new,
                        logf_new, fox_rows(proj["z_f"]), 8, t_new)
    mix_f = mix_f.reshape(n_seq * t_new, FOX_WIDTH).astype(BF16)

    y = _merge(x2d, mix_n, mix_f, w_out_bf, norm_final, n_seq * t_new).reshape(n_seq, t_new, d)
    kvwin_new = proj["kvwin"].reshape(n_seq, t_new, 2, NSA_KV_HEADS, HEAD_DIM)
    state = (proj["kv4"].reshape(n_seq, t_new, 4, NSA_KV_HEADS, HEAD_DIM),
             proj["kv_f"].reshape(n_seq, t_new, 2, FOX_HEADS, HEAD_DIM),
             logf,
             jnp.concatenate([state_nsa_win[0], kvwin_new], axis=1)[:, t_new:])
    return y, state


def kernel(x_prompt, x_sample, cache_nsa_kv, cache_fox_kv, cache_fox_logf, state_nsa_win, page_table,
           norm_in, w_in, b_gate, b_forget, cmp_pe, cmp_w, w_out, norm_final):
    assert norm_in.shape[0] == 1, "single-layer trunk"
    wts = _split_weights(w_in[0], b_gate[0], b_forget[0])
    cmp_w_bf = cmp_w[0].reshape(2, CMP_BLOCK * HEAD_DIM, HEAD_DIM).astype(BF16)
    w_out_bf = w_out[0].astype(BF16)
    y_p, st_p = _prompt_layer(x_prompt, norm_in[0], wts, cmp_pe[0], cmp_w_bf, w_out_bf, norm_final)
    y_s, st_s = _sample_layer(x_sample, cache_nsa_kv, cache_fox_kv, cache_fox_logf, state_nsa_win,
                              page_table, norm_in[0], wts, cmp_pe[0], cmp_w_bf, w_out_bf, norm_final)
    outs = [y_p, y_s]
    for s_p, s_s in zip(st_p, st_s):
        outs.extend([s_p[None], s_s[None]])
    return tuple(outs)
```

```python
import functools

import jax
import jax.numpy as jnp
from jax import lax
from jax.experimental import pallas as pl
from jax.experimental.pallas import tpu as pltpu

F32 = jnp.float32
BF16 = jnp.bfloat16
NEG_INF = float("-inf")

HEAD_DIM = 128
NSA_HEADS = 8
FOX_HEADS = 8
NSA_KV_HEADS = 2
NSA_GROUP = NSA_HEADS // NSA_KV_HEADS
NSA_WIDTH = NSA_HEADS * HEAD_DIM
FOX_WIDTH = FOX_HEADS * HEAD_DIM
KV_WIDTH = NSA_KV_HEADS * HEAD_DIM
NSA_CHANNELS = 4 * NSA_KV_HEADS
CMP_BLOCK = 32
SEL_BLOCK = 64
SEL_TOPK = 16
WINDOW = 512
PAGE_SIZE = 128
FORCED_SCORE = 1e4
RMS_EPS = 1e-6
SCALE = HEAD_DIM ** -0.5
LOG2E = 1.4426950408889634
T_PAD = 8
LANES = 128
SUBLANES = 8
PROJ_TN = 512
STAGE_PITCH = 40
VMEM_LIMIT = 56 * 1024 * 1024


def _params(sem):
    return pltpu.CompilerParams(dimension_semantics=sem, vmem_limit_bytes=VMEM_LIMIT)


def _dot(a, b):
    return jnp.dot(a, b, preferred_element_type=F32)


def _dot_nt(a, b):
    return lax.dot_general(a, b, (((1,), (1,)), ((), ())), preferred_element_type=F32)


def _sigmoid(x):
    return 1.0 / (1.0 + jnp.exp(-x))


def _silu(x):
    return x * _sigmoid(x)


def _tile_lanes(x, n):
    return x if n == 1 else jnp.concatenate([x] * n, axis=1)


def _with_ones(v):
    return jnp.concatenate([v, jnp.ones(v.shape, v.dtype)], axis=1)


def _state_init(m_ref, acc_ref):
    m_ref[...] = jnp.full(m_ref.shape, NEG_INF, F32)
    acc_ref[...] = jnp.zeros(acc_ref.shape, F32)


def _probs(lg, m_old, may_be_empty, exp=jnp.exp):
    m_new = jnp.maximum(m_old, jnp.max(lg, axis=1, keepdims=True))
    m_use = jnp.where(m_new == NEG_INF, 0.0, m_new) if may_be_empty else m_new
    alpha = exp(m_old - m_use)
    p = exp(lg - _tile_lanes(m_use, lg.shape[1] // LANES)).astype(BF16)
    return m_new, alpha, p


def _state_update_chunked(logits_of, n_chunks, chunk, v_ext, m_ref, acc_ref, p_ref, alpha_ref, may_be_empty):
    for c in range(n_chunks):
        rows = slice(c * chunk, (c + 1) * chunk)
        m_new, alpha, p = _probs(logits_of(c), m_ref[rows, :], may_be_empty, jnp.exp2)
        alpha_ref[rows, :] = alpha
        p_ref[rows, :] = p
        m_ref[rows, :] = m_new
    acc_ref[...] = _tile_lanes(alpha_ref[...], 2) * acc_ref[...] + _dot(p_ref[...], v_ext)


def _state_update(lg, pv_of, m_ref, acc_ref, may_be_empty):
    m_new, alpha, p = _probs(lg, m_ref[...], may_be_empty)
    acc_ref[...] = _tile_lanes(alpha, 2) * acc_ref[...] + pv_of(p)
    m_ref[...] = m_new


def _state_update_parts(parts, m_ref, acc_ref, may_be_empty):
    local = []
    for lg, pv_of in parts:
        m_c = jnp.broadcast_to(jnp.max(lg, axis=1, keepdims=True), m_ref.shape)
        m_use = jnp.where(m_c == NEG_INF, 0.0, m_c) if may_be_empty else m_c
        p = jnp.exp(lg - _tile_lanes(m_use, lg.shape[1] // LANES)).astype(BF16)
        local.append((m_c, pv_of(p)))
    m_old = m_ref[...]
    m_new = m_old
    for m_c, _ in local:
        m_new = jnp.maximum(m_new, m_c)
    acc = _tile_lanes(jnp.exp(m_old - m_new), 2) * acc_ref[...]
    for m_c, pv in local:
        acc = acc + _tile_lanes(jnp.exp(m_c - m_new), 2) * pv
    acc_ref[...] = acc
    m_ref[...] = m_new


def _state_result(acc_ref):
    acc = acc_ref[...]
    return acc[:, :HEAD_DIM] / jnp.maximum(acc[:, HEAD_DIM:], 1e-30)


def _lane_cumsum(x):
    n = x.shape[-1]
    lane = lax.broadcasted_iota(jnp.int32, x.shape, x.ndim - 1)
    s = 1
    while s < n:
        x = x + jnp.where(lane >= s, pltpu.roll(x, s, axis=x.ndim - 1), 0.0)
        s *= 2
    return x


def _head_slope(head):
    if isinstance(head, int):
        return 2.0 ** -(head + 1)
    return lax.bitcast_convert_type((126 - head) << 23, F32)


def _rms_kernel(x_ref, g_ref, o_ref):
    x = x_ref[...]
    ms = jnp.mean(x * x, axis=-1, keepdims=True)
    o_ref[...] = (x * lax.rsqrt(ms + RMS_EPS) * g_ref[...]).astype(o_ref.dtype)


def _rmsnorm(x2d, g, out_dtype, tm):
    m, d = x2d.shape
    return pl.pallas_call(
        _rms_kernel,
        grid=(m // tm,),
        in_specs=[pl.BlockSpec((tm, d), lambda i: (i, 0)), pl.BlockSpec((1, d), lambda i: (0, 0))],
        out_specs=pl.BlockSpec((tm, d), lambda i: (i, 0)),
        out_shape=jax.ShapeDtypeStruct((m, d), out_dtype),
        compiler_params=_params(("parallel",)),
        name="rmsnorm",
    )(x2d, g.reshape(1, d))


def _proj_kernel(h_ref, w_ref, *o_refs):
    acc = _dot(h_ref[...], w_ref[...])
    for o_ref in o_refs:
        o_ref[...] = acc.astype(o_ref.dtype)


def _project(h, w, col0, n, out_dtypes, tm, name):
    m, d = h.shape
    j0 = col0 // PROJ_TN
    return pl.pallas_call(
        _proj_kernel,
        grid=(m // tm, n // PROJ_TN),
        in_specs=[pl.BlockSpec((tm, d), lambda i, j: (i, 0)), pl.BlockSpec((d, PROJ_TN), lambda i, j: (0, j0 + j))],
        out_specs=[pl.BlockSpec((tm, PROJ_TN), lambda i, j: (i, j)) for _ in out_dtypes],
        out_shape=[jax.ShapeDtypeStruct((m, n), dt) for dt in out_dtypes],
        compiler_params=_params(("parallel", "arbitrary")),
        name=name,
    )(h, w)


def _small_kernel(h_ref, w_ref, b_ref, o_ref, *, n_gate):
    z = _dot(h_ref[...], w_ref[...]) + b_ref[...]
    lane = lax.broadcasted_iota(jnp.int32, z.shape, 1)
    log_sig = jnp.minimum(z, 0.0) - jnp.log1p(jnp.exp(-jnp.abs(z)))
    o_ref[...] = jnp.where(lane < n_gate, _sigmoid(z), log_sig)


def _project_small(h, w, b, tm, n_gate):
    m, d = h.shape
    return pl.pallas_call(
        functools.partial(_small_kernel, n_gate=n_gate),
        grid=(m // tm,),
        in_specs=[pl.BlockSpec((tm, d), lambda i: (i, 0)), pl.BlockSpec((d, LANES), lambda i: (0, 0)),
                  pl.BlockSpec((1, LANES), lambda i: (0, 0))],
        out_specs=pl.BlockSpec((tm, LANES), lambda i: (i, 0)),
        out_shape=jax.ShapeDtypeStruct((m, LANES), F32),
        compiler_params=_params(("parallel",)),
        name="proj_gates_logf",
    )(h, w, b)


PROJ_GROUPS = (("q_n", NSA_WIDTH), ("kv4", 4 * KV_WIDTH), ("kvwin", 2 * KV_WIDTH), ("z_n", NSA_WIDTH),
               ("q_f", FOX_WIDTH), ("kv_f", 2 * FOX_WIDTH), ("z_f", FOX_WIDTH))
PROJ_OUT_DTYPES = {"q_n": (BF16,), "kv4": (F32, BF16), "kvwin": (F32, BF16), "z_n": (F32,),
                   "q_f": (BF16,), "kv_f": (F32, BF16), "z_f": (F32,)}


def _regroup_kernel(w_ref, o_ref, *, spans):
    col = 0
    for a, b in spans:
        o_ref[:, col:col + (b - a)] = w_ref[:, a:b].astype(o_ref.dtype)
        col += b - a


def _regroup_columns(w, spans):
    rows, cols = w.shape
    tr = 256
    width = sum(b - a for a, b in spans)
    return pl.pallas_call(
        functools.partial(_regroup_kernel, spans=spans),
        grid=(rows // tr,),
        in_specs=[pl.BlockSpec((tr, cols), lambda i: (i, 0))],
        out_specs=pl.BlockSpec((tr, width), lambda i: (i, 0)),
        out_shape=jax.ShapeDtypeStruct((rows, width), BF16),
        compiler_params=_params(("parallel",)),
        name="regroup_weights",
    )(w)


def _split_weights(w_in, b_gate, b_forget):
    cuts = [NSA_WIDTH, 6 * KV_WIDTH, 3 * NSA_HEADS, NSA_WIDTH, FOX_WIDTH, FOX_WIDTH, FOX_WIDTH, FOX_HEADS, FOX_WIDTH]
    offs = [0]
    for c in cuts:
        offs.append(offs[-1] + c)
    main = _regroup_columns(w_in, ((offs[0], offs[2]), (offs[3], offs[7]), (offs[8], offs[9])))
    n_small = 3 * NSA_HEADS + FOX_HEADS
    w_small = jnp.concatenate([w_in[:, offs[2]:offs[3]], w_in[:, offs[7]:offs[8]]], axis=1)
    w_small = jnp.pad(w_small, ((0, 0), (0, LANES - n_small))).astype(BF16)
    b_small = jnp.pad(jnp.concatenate([b_gate, b_forget]), (0, LANES - n_small)).reshape(1, LANES).astype(F32)
    return {"main": main, "small": w_small, "b_small": b_small}


def _project_all(x2d, norm_in, wts, tm):
    h = _rmsnorm(x2d, norm_in, BF16, min(tm, 512))
    out = {}
    col0 = 0
    for name, width in PROJ_GROUPS:
        res = _project(h, wts["main"], col0, width, PROJ_OUT_DTYPES[name], tm, "proj_" + name)
        out[name] = res[0]
        if len(res) > 1:
            out[name + "_bf"] = res[1]
        col0 += width
    out["small"] = _project_small(h, wts["small"], wts["b_small"], min(tm, 512), 3 * NSA_HEADS)
    return out


def _merge_kernel(x_ref, mn_ref, mf_ref, wn_ref, wf_ref, g_ref, o_ref):
    y = x_ref[...] + _dot(mn_ref[...], wn_ref[...]) + _dot(mf_ref[...], wf_ref[...])
    ms = jnp.mean(y * y, axis=-1, keepdims=True)
    o_ref[...] = y * lax.rsqrt(ms + RMS_EPS) * g_ref[...]


def _merge(x2d, mix_n, mix_f, w_out_bf, norm_final, tm):
    m, d = x2d.shape
    return pl.pallas_call(
        _merge_kernel,
        grid=(m // tm,),
        in_specs=[pl.BlockSpec((tm, d), lambda i: (i, 0)),
                  pl.BlockSpec((tm, NSA_WIDTH), lambda i: (i, 0)),
                  pl.BlockSpec((tm, FOX_WIDTH), lambda i: (i, 0)),
                  pl.BlockSpec((NSA_WIDTH, d), lambda i: (0, 0)),
                  pl.BlockSpec((FOX_WIDTH, d), lambda i: (NSA_WIDTH // FOX_WIDTH, 0)),
                  pl.BlockSpec((1, d), lambda i: (0, 0))],
        out_specs=pl.BlockSpec((tm, d), lambda i: (i, 0)),
        out_shape=jax.ShapeDtypeStruct((m, d), F32),
        compiler_params=_params(("parallel",)),
        name="merge_out_proj",
    )(x2d, mix_n, mix_f, w_out_bf, w_out_bf, norm_final.reshape(1, d))


def _cumsum_kernel(x_ref, o_ref):
    o_ref[...] = _lane_cumsum(x_ref[...])


def _cumsum_rows(x):
    return pl.pallas_call(
        _cumsum_kernel,
        out_shape=jax.ShapeDtypeStruct(x.shape, F32),
        name="forget_cumsum",
    )(x)


def _compress_rows(x_ref, pe_ref, kind, w, xs_ref, nb, starts, stride):
    half = nb // 2
    for c in range(CMP_BLOCK):
        pe_c = pe_ref[kind, c:c + 1, :]
        for par in range(2):
            xc = x_ref[pl.ds(starts[par] + c, half, stride=stride), :]
            xs_ref[par * half:(par + 1) * half, c * HEAD_DIM:(c + 1) * HEAD_DIM] = (xc + pe_c).astype(BF16)
    return _dot(xs_ref[...], w)


def _compress_kernel(x_ref, pe_ref, w_ref, o_ref, xs_ref, *, nb):
    out = _compress_rows(x_ref, pe_ref, 0, w_ref[0], xs_ref, nb, (0, CMP_BLOCK), 2 * CMP_BLOCK)
    o_ref[0, 0, 0] = out.astype(o_ref.dtype)


def _compress_prompt(kv4, cmp_pe, cmp_w_bf, batch, seq):
    nb = seq // CMP_BLOCK
    return pl.pallas_call(
        functools.partial(_compress_kernel, nb=nb),
        grid=(batch, 2, NSA_KV_HEADS),
        in_specs=[pl.BlockSpec((seq, HEAD_DIM), lambda b, k, h: (b, k * NSA_KV_HEADS + h)),
                  pl.BlockSpec((1, CMP_BLOCK, HEAD_DIM), lambda b, k, h: (k, 0, 0)),
                  pl.BlockSpec((1, CMP_BLOCK * HEAD_DIM, HEAD_DIM), lambda b, k, h: (k, 0, 0))],
        out_specs=pl.BlockSpec((1, 1, 1, nb, HEAD_DIM), lambda b, k, h: (b, k, h, 0, 0)),
        out_shape=jax.ShapeDtypeStruct((batch, 2, NSA_KV_HEADS, nb, HEAD_DIM), BF16),
        scratch_shapes=[pltpu.VMEM((nb, CMP_BLOCK * HEAD_DIM), BF16)],
        compiler_params=_params(("parallel", "parallel", "arbitrary")),
        name="compress_prompt",
    )(kv4, cmp_pe, cmp_w_bf)


def _select_blocks(score_t, cur, n_keep):
    n_blk = score_t.shape[0]
    j = lax.broadcasted_iota(jnp.int32, score_t.shape, 0)
    forced = (j == 0) | (j == cur) | (j == cur - 1)
    s = jnp.where(j <= cur, jnp.where(forced, FORCED_SCORE, score_t), -1.0)
    n_grp = n_blk // SUBLANES
    grp = [s[r * SUBLANES:(r + 1) * SUBLANES] for r in range(n_grp)]
    cnt = [jnp.zeros(grp[0].shape, F32) for _ in range(n_grp)]
    sub = lax.broadcasted_iota(jnp.int32, grp[0].shape, 0)
    for jj in range(n_blk):
        row = s[jj:jj + 1, :]
        r_j, off = divmod(jj, SUBLANES)
        for r in range(n_grp):
            if r < r_j:
                hit = row > grp[r]
            elif r > r_j:
                hit = row >= grp[r]
            else:
                hit = (row > grp[r]) | ((row == grp[r]) & (sub > off))
            cnt[r] = cnt[r] + jnp.where(hit, 1.0, 0.0)
    return jnp.where((jnp.concatenate(cnt, axis=0) < n_keep) & (j <= cur), 1.0, 0.0)


def _nsa_prompt_kernel(q_ref, kc_ref, vc_ref, ks_ref, vs_ref, kw_ref, vw_ref, g_ref, z_ref, e_ref, o_ref,
                       m_ref, acc_ref, alpha_ref, ps_ref, pw_ref, mix_ref, bias_ref, *, tq, tk, tw, ck, seq):
    qi = pl.program_id(1)
    s0 = qi * tq
    n_cmp = seq // CMP_BLOCK
    n_sel = seq // SEL_BLOCK
    row_t = s0 + lax.broadcasted_iota(jnp.int32, (tq, 1), 0)
    kv_heads = range(NSA_KV_HEADS)
    heads = lambda hkv: [hkv * NSA_GROUP + g for g in range(NSA_GROUP)]
    slopes = lambda hkv: [_head_slope(h) for h in heads(hkv)]
    cols = lambda hkv: slice(hkv * HEAD_DIM, (hkv + 1) * HEAD_DIM)

    def q_rows(hkv):
        return jnp.concatenate([q_ref[:, h * HEAD_DIM:(h + 1) * HEAD_DIM] for h in heads(hkv)], axis=0)

    def gate_col(hkv, br):
        return jnp.concatenate([g_ref[:, 3 * h + br:3 * h + br + 1] for h in heads(hkv)], axis=0)

    sels = []
    for hkv in kv_heads:
        sc = _dot_nt(q_rows(hkv), kc_ref[0, 0, hkv]) * SCALE
        lane = lax.broadcasted_iota(jnp.int32, (tq, n_cmp), 1)
        blk = jnp.where(lane < n_cmp // 2, 2 * lane, 2 * lane - (n_cmp - 1))
        dist_c = (row_t - ((blk + 1) * CMP_BLOCK - 1)).astype(F32)
        valid_c = dist_c >= 0
        score = jnp.zeros((tq, n_cmp), F32)
        probs = []
        for g, slope in enumerate(slopes(hkv)):
            lg = jnp.where(valid_c, sc[g * tq:(g + 1) * tq] - slope * dist_c, NEG_INF)
            mx = jnp.max(lg, axis=1, keepdims=True)
            mx = jnp.where(mx == NEG_INF, 0.0, mx)
            p = jnp.exp(lg - mx)
            p = p / jnp.maximum(jnp.sum(p, axis=1, keepdims=True), 1e-30)
            score = score + p
            probs.append(p)
        o_c = _dot(jnp.concatenate(probs, axis=0).astype(BF16), vc_ref[0, 0, hkv])
        mix_ref[hkv] = gate_col(hkv, 0) * o_c

        score_t = score.T
        score_t = score_t[:n_sel] + score_t[n_sel:]
        cur = (s0 + lax.broadcasted_iota(jnp.int32, (n_sel, tq), 1)) // SEL_BLOCK
        sel_t = _select_blocks(score_t, cur, SEL_TOPK)
        sels.append(jnp.concatenate([sel_t, jnp.zeros((LANES - n_sel, tq), F32)], axis=0).T.astype(BF16))

    def bias_body(c, carry):
        c0 = pl.multiple_of(c * ck, ck)
        causal = c0 + lax.broadcasted_iota(jnp.int32, (tq, ck), 1) <= row_t
        for hkv in kv_heads:
            expanded = _dot(sels[hkv], e_ref[:, pl.ds(c0, ck)])
            bias_ref[hkv, :, pl.ds(c0, ck)] = jnp.where((expanded > 0.5) & causal, 0.0, NEG_INF)
        return carry

    lax.fori_loop(0, (s0 + tq + ck - 1) // ck, bias_body, 0)

    def stream(hkv, k, bias, key_rel, v, p_ref, may_be_empty):
        s = _dot_nt(q_rows(hkv), k)
        sl = slopes(hkv)
        logits_of = lambda g: s[g * tq:(g + 1) * tq] * (SCALE * LOG2E) + (bias + (sl[g] * LOG2E) * key_rel)
        _state_update_chunked(logits_of, NSA_GROUP, tq, _with_ones(v), m_ref.at[hkv], acc_ref.at[hkv],
                              p_ref.at[hkv], alpha_ref.at[hkv], may_be_empty)

    _state_init(m_ref, acc_ref)

    def slc_body(i, carry):
        k0 = pl.multiple_of(i * tk, tk)
        key_rel = (k0 - s0 + lax.broadcasted_iota(jnp.int32, (1, tk), 1)).astype(F32)
        for hkv in kv_heads:
            stream(hkv, ks_ref[pl.ds(k0, tk), cols(hkv)], bias_ref[hkv, :, pl.ds(k0, tk)], key_rel,
                   vs_ref[pl.ds(k0, tk), cols(hkv)], ps_ref, False)
        return carry

    lax.fori_loop(0, (s0 + tq + tk - 1) // tk, slc_body, 0)
    for hkv in kv_heads:
        mix_ref[hkv] = mix_ref[hkv] + gate_col(hkv, 1) * _state_result(acc_ref.at[hkv])

    _state_init(m_ref, acc_ref)

    def win_body(w, carry):
        k0 = pl.multiple_of(s0 - WINDOW + w * tw, tw)
        disti = row_t - (k0 + lax.broadcasted_iota(jnp.int32, (tq, tw), 1))
        bias = jnp.where((disti >= 0) & (disti < WINDOW), 0.0, NEG_INF)
        key_rel = (k0 - s0 + lax.broadcasted_iota(jnp.int32, (1, tw), 1)).astype(F32)
        for hkv in kv_heads:
            stream(hkv, kw_ref[pl.ds(k0, tw), cols(hkv)], bias, key_rel,
                   vw_ref[pl.ds(k0, tw), cols(hkv)], pw_ref, True)
        return carry

    n_win_tiles = (WINDOW + tq) // tw
    lax.fori_loop(jnp.maximum(0, (WINDOW - s0) // tw), n_win_tiles, win_body, 0)

    for hkv in kv_heads:
        mix = mix_ref[hkv] + gate_col(hkv, 2) * _state_result(acc_ref.at[hkv])
        for g, h in enumerate(heads(hkv)):
            z = z_ref[:, h * HEAD_DIM:(h + 1) * HEAD_DIM]
            o_ref[:, h * HEAD_DIM:(h + 1) * HEAD_DIM] = (mix[g * tq:(g + 1) * tq] * _silu(z)).astype(o_ref.dtype)


def _nsa_prompt(proj, kcv, expand, batch, seq):
    tq, tk, tw, ck = 256, 256, 256, 512
    assert seq % ck == 0 and WINDOW % tw == 0 and tq % tw == 0 and seq // SEL_BLOCK <= LANES
    nq = seq // tq
    rows = NSA_GROUP * tq
    kernel = functools.partial(_nsa_prompt_kernel, tq=tq, tk=tk, tw=tw, ck=ck, seq=seq)
    cmp_spec = lambda kind: pl.BlockSpec((1, 1, NSA_KV_HEADS, seq // CMP_BLOCK, HEAD_DIM),
                                         lambda b, i: (b, kind, 0, 0, 0))
    return pl.pallas_call(
        kernel,
        grid=(batch, nq),
        in_specs=[pl.BlockSpec((tq, NSA_WIDTH), lambda b, i: (b * nq + i, 0)),
                  cmp_spec(0), cmp_spec(1),
                  pl.BlockSpec((seq, KV_WIDTH), lambda b, i: (b, 2)),
                  pl.BlockSpec((seq, KV_WIDTH), lambda b, i: (b, 3)),
                  pl.BlockSpec((seq, KV_WIDTH), lambda b, i: (b, 0)),
                  pl.BlockSpec((seq, KV_WIDTH), lambda b, i: (b, 1)),
                  pl.BlockSpec((tq, LANES), lambda b, i: (b * nq + i, 0)),
                  pl.BlockSpec((tq, NSA_WIDTH), lambda b, i: (b * nq + i, 0)),
                  pl.BlockSpec((LANES, seq), lambda b, i: (0, 0))],
        out_specs=pl.BlockSpec((tq, NSA_WIDTH), lambda b, i: (b * nq + i, 0)),
        out_shape=jax.ShapeDtypeStruct((batch * seq, NSA_WIDTH), BF16),
        scratch_shapes=[pltpu.VMEM((NSA_KV_HEADS, rows, LANES), F32),
                        pltpu.VMEM((NSA_KV_HEADS, rows, 2 * HEAD_DIM), F32),
                        pltpu.VMEM((NSA_KV_HEADS, rows, LANES), F32),
                        pltpu.VMEM((NSA_KV_HEADS, rows, tk), BF16),
                        pltpu.VMEM((NSA_KV_HEADS, rows, tw), BF16),
                        pltpu.VMEM((NSA_KV_HEADS, rows, HEAD_DIM), F32),
                        pltpu.VMEM((NSA_KV_HEADS, tq, seq), F32)],
        compiler_params=_params(("parallel", "arbitrary")),
        name="nsa_prompt",
    )(proj["q_n"], kcv, kcv, proj["kv4_bf"], proj["kv4_bf"], proj["kvwin_bf"], proj["kvwin_bf"],
      proj["small"], proj["z_n"], expand)


def _fox_prompt_kernel(q_ref, k_ref, v_ref, d_ref, z_ref, o_ref, m_ref, acc_ref, alpha_ref, p_ref, *,
                       t, chunk, n_heads):
    qi = pl.program_id(2)
    _state_init(m_ref, acc_ref)

    def tile(i, causal):
        k0 = pl.multiple_of(i * t, t)
        for hh in range(n_heads):
            cols = slice(hh * HEAD_DIM, (hh + 1) * HEAD_DIM)
            s = _dot_nt(q_ref[:, cols], k_ref[pl.ds(k0, t), cols])
            d_row = d_ref[hh, :, pl.ds(k0, t)] * LOG2E

            def logits_of(c, s=s, d_row=d_row):
                lg = s[c * chunk:(c + 1) * chunk] * (SCALE * LOG2E) - d_row
                if causal:
                    row = c * chunk + lax.broadcasted_iota(jnp.int32, (chunk, t), 0)
                    col = lax.broadcasted_iota(jnp.int32, (chunk, t), 1)
                    lg = jnp.where(col <= row, lg, NEG_INF)
                return lg

            _state_update_chunked(logits_of, t // chunk, chunk, _with_ones(v_ref[pl.ds(k0, t), cols]),
                                  m_ref.at[hh], acc_ref.at[hh], p_ref.at[hh], alpha_ref.at[hh], False)

    def body(i, carry):
        tile(i, False)
        return carry

    lax.fori_loop(0, qi, body, 0)
    tile(qi, True)
    for hh in range(n_heads):
        cols = slice(hh * HEAD_DIM, (hh + 1) * HEAD_DIM)
        o_ref[:, cols] = (_state_result(acc_ref.at[hh]) * _silu(z_ref[:, cols])).astype(o_ref.dtype)


def _fox_prompt(proj, d_rows, batch, seq):
    t, chunk, n_heads = 512, 128, 4
    nq = seq // t
    width = n_heads * HEAD_DIM
    n_hg = FOX_HEADS // n_heads
    return pl.pallas_call(
        functools.partial(_fox_prompt_kernel, t=t, chunk=chunk, n_heads=n_heads),
        grid=(batch, n_hg, nq),
        in_specs=[pl.BlockSpec((t, width), lambda b, h, i: (b * nq + i, h)),
                  pl.BlockSpec((seq, width), lambda b, h, i: (b, h)),
                  pl.BlockSpec((seq, width), lambda b, h, i: (b, n_hg + h)),
                  pl.BlockSpec((n_heads, 1, seq), lambda b, h, i: (b * n_hg + h, 0, 0)),
                  pl.BlockSpec((t, width), lambda b, h, i: (b * nq + i, h))],
        out_specs=pl.BlockSpec((t, width), lambda b, h, i: (b * nq + i, h)),
        out_shape=jax.ShapeDtypeStruct((batch * seq, FOX_WIDTH), BF16),
        scratch_shapes=[pltpu.VMEM((n_heads, t, LANES), F32), pltpu.VMEM((n_heads, t, 2 * HEAD_DIM), F32),
                        pltpu.VMEM((n_heads, t, LANES), F32), pltpu.VMEM((n_heads, t, t), BF16)],
        compiler_params=_params(("parallel", "parallel", "arbitrary")),
        name="fox_prompt",
    )(proj["q_f"], proj["kv_f_bf"], proj["kv_f_bf"], d_rows, proj["z_f"])


def _prompt_layer(x, norm_in, wts, cmp_pe, cmp_w_bf, w_out_bf, norm_final):
    batch, seq, d = x.shape
    x2d = x.reshape(batch * seq, d)
    proj = _project_all(x2d, norm_in, wts, 1024)
    n_gate = 3 * NSA_HEADS
    logf = proj["small"][:, n_gate:n_gate + FOX_HEADS].reshape(batch, seq, FOX_HEADS)
    d_rows = _cumsum_rows(jnp.swapaxes(logf, 1, 2).reshape(batch * FOX_HEADS, seq))
    kcv = _compress_prompt(proj["kv4"], cmp_pe, cmp_w_bf, batch, seq)
    blk_of_key = jnp.arange(seq, dtype=jnp.int32) // SEL_BLOCK
    expand = (jnp.arange(LANES, dtype=jnp.int32)[:, None] == blk_of_key[None, :]).astype(BF16)
    mix_n = _nsa_prompt(proj, kcv, expand, batch, seq)
    mix_f = _fox_prompt(proj, d_rows.reshape(batch * FOX_HEADS, 1, seq), batch, seq)
    y = _merge(x2d, mix_n, mix_f, w_out_bf, norm_final, 512).reshape(batch, seq, d)
    w_keep = min(WINDOW, seq)
    state = (proj["kv4"].reshape(batch, seq, 4, NSA_KV_HEADS, HEAD_DIM),
             proj["kv_f"].reshape(batch, seq, 2, FOX_HEADS, HEAD_DIM),
             logf,
             proj["kvwin"].reshape(batch, seq, 2 * KV_WIDTH)[:, seq - w_keep:].reshape(
                 batch, w_keep, 2, NSA_KV_HEADS, HEAD_DIM))
    return y, state


HALF_CHANNELS = NSA_CHANNELS // 2


def _nsa_page_specs(pg, half):
    def spec(u):
        return pl.BlockSpec((1, 1, PAGE_SIZE, 1, HALF_CHANNELS, HEAD_DIM),
                            lambda b, j, pt: (0, pt[b, j * pg + u], 0, half, 0, 0))
    return [spec(u) for u in range(pg)]


def _channel(page_ref, ch, start=0, n=PAGE_SIZE):
    rows = page_ref.reshape(PAGE_SIZE * HALF_CHANNELS, HEAD_DIM)
    return rows[pl.ds(start * HALF_CHANNELS + ch, n, stride=HALF_CHANNELS), :]


def _nsa_cmp_kernel(pt_ref, *refs, pg, n_pages):
    page_refs = refs[:pg]
    q_ref, pe_ref, w_ref, oc_ref, sel_ref, stage_ref, xs_ref = refs[pg:]
    jg = pl.program_id(1)
    past = n_pages * PAGE_SIZE
    nb = past // CMP_BLOCK
    half = nb // 2
    blocks_per_page = PAGE_SIZE // CMP_BLOCK
    for u in range(pg):
        for bl in range(blocks_per_page):
            m = (jg * pg + u) * (blocks_per_page // 2) + bl // 2
            dst = pl.multiple_of(((bl % 2) * half + m) * STAGE_PITCH, SUBLANES)
            for ch in range(2 * NSA_KV_HEADS):
                stage_ref[ch, pl.ds(dst, CMP_BLOCK), :] = _channel(page_refs[u], ch, bl * CMP_BLOCK, CMP_BLOCK)

    @pl.when(jg == pl.num_programs(1) - 1)
    def _():
        cmp = [[_compress_rows(stage_ref.at[kind * NSA_KV_HEADS + h], pe_ref, kind, w_ref[kind], xs_ref, nb,
                               (0, half * STAGE_PITCH), STAGE_PITCH).astype(BF16)
                for h in range(NSA_KV_HEADS)] for kind in range(2)]
        rows = NSA_GROUP * T_PAD
        for hkv in range(NSA_KV_HEADS):
            q = q_ref[0, hkv * rows:(hkv + 1) * rows, :].astype(BF16)
            kc, vc = cmp[0][hkv], cmp[1][hkv]
            s = _dot_nt(q, kc) * SCALE
            lane = lax.broadcasted_iota(jnp.int32, (rows, nb), 1)
            blk = jnp.where(lane < half, 2 * lane, 2 * lane - (nb - 1))
            row = lax.broadcasted_iota(jnp.int32, (rows, nb), 0)
            slope = _head_slope(hkv * NSA_GROUP + row // T_PAD)
            dist = (past + row % T_PAD - ((blk + 1) * CMP_BLOCK - 1)).astype(F32)
            lg = jnp.where(dist >= 0, s - slope * dist, NEG_INF)
            mx = jnp.max(lg, axis=1, keepdims=True)
            mx = jnp.where(mx == NEG_INF, 0.0, mx)
            p = jnp.exp(lg - mx)
            p = p / jnp.maximum(jnp.sum(p, axis=1, keepdims=True), 1e-30)
            oc_ref[0, hkv * rows:(hkv + 1) * rows, :] = _dot(p.astype(BF16), vc)
            score = p[0:T_PAD]
            for g in range(1, NSA_GROUP):
                score = score + p[g * T_PAD:(g + 1) * T_PAD]
            score = score[:, :half] + score[:, half:]
            blk_s = lax.broadcasted_iota(jnp.int32, (T_PAD, half), 1).astype(F32)
            forced = (blk_s == 0) | (blk_s == half - 1)
            s_left = jnp.where(forced, NEG_INF, score)
            picked = jnp.zeros((T_PAD, half), F32)
            for _ in range(SEL_TOPK - 3):
                best = jnp.max(s_left, axis=1, keepdims=True)
                first = jnp.min(jnp.where(s_left == best, blk_s, float(half)), axis=1, keepdims=True)
                hit = blk_s == first
                picked = jnp.where(hit, 1.0, picked)
                s_left = jnp.where(hit, NEG_INF, s_left)
            sel_ref[0, hkv * T_PAD:(hkv + 1) * T_PAD, :] = jnp.where(forced, 1.0, picked)


def _nsa_cmp_sample(cache, page_table, q_rows, cmp_pe, cmp_w_bf, pg):
    n_seq, n_pages = page_table.shape
    past = n_pages * PAGE_SIZE
    nb = past // CMP_BLOCK
    assert n_pages % pg == 0 and past // SEL_BLOCK >= SEL_TOPK
    rows = NSA_HEADS * T_PAD
    grid_spec = pltpu.PrefetchScalarGridSpec(
        num_scalar_prefetch=1,
        grid=(n_seq, n_pages // pg),
        in_specs=_nsa_page_specs(pg, 0) + [
            pl.BlockSpec((1, rows, HEAD_DIM), lambda b, j, pt: (b, 0, 0)),
            pl.BlockSpec((2, CMP_BLOCK, HEAD_DIM), lambda b, j, pt: (0, 0, 0)),
            pl.BlockSpec((2, CMP_BLOCK * HEAD_DIM, HEAD_DIM), lambda b, j, pt: (0, 0, 0))],
        out_specs=[pl.BlockSpec((1, rows, HEAD_DIM), lambda b, j, pt: (b, 0, 0)),
                   pl.BlockSpec((1, NSA_KV_HEADS * T_PAD, past // SEL_BLOCK), lambda b, j, pt: (b, 0, 0))],
        scratch_shapes=[pltpu.VMEM((2 * NSA_KV_HEADS, nb * STAGE_PITCH, HEAD_DIM), F32),
                        pltpu.VMEM((nb, CMP_BLOCK * HEAD_DIM), BF16)],
    )
    return pl.pallas_call(
        functools.partial(_nsa_cmp_kernel, pg=pg, n_pages=n_pages),
        grid_spec=grid_spec,
        out_shape=[jax.ShapeDtypeStruct((n_seq, rows, HEAD_DIM), F32),
                   jax.ShapeDtypeStruct((n_seq, NSA_KV_HEADS * T_PAD, past // SEL_BLOCK), F32)],
        compiler_params=_params(("parallel", "arbitrary")),
        name="nsa_sample_cmp",
    )(page_table, *([cache] * pg), q_rows, cmp_pe, cmp_w_bf)


def _nsa_slc_kernel(pt_ref, *refs, pg, part_pages, n_pages, w_buf, n_new):
    page_refs = refs[:pg]
    (q_ref, sel_ref, e_ref, new_ref, win_ref, oc_ref, g_ref, z_ref, o_ref, tab_ref, m_ref, acc_ref) = refs[pg:]
    jg = pl.program_id(1)
    past = n_pages * PAGE_SIZE
    hr = n_new * NSA_GROUP
    rows = NSA_KV_HEADS * hr
    row = lax.broadcasted_iota(jnp.int32, (rows, 1), 0)
    t_q = (row % hr) // NSA_GROUP
    slope = _head_slope((row // hr) * NSA_GROUP + row % NSA_GROUP)
    q = [q_ref[0, h * hr:(h + 1) * hr, :].astype(BF16) for h in range(NSA_KV_HEADS)]

    def scores(k_of):
        return jnp.concatenate([_dot_nt(q[h], k_of(h)) for h in range(NSA_KV_HEADS)], axis=0) * SCALE

    def pv(v_of):
        return lambda p: jnp.concatenate(
            [_dot(p[h * hr:(h + 1) * hr], _with_ones(v_of(h))) for h in range(NSA_KV_HEADS)], axis=0)


    @pl.when(jg == 0)
    def _():
        expanded = _dot(sel_ref[0].astype(BF16), e_ref[...])
        key = lax.broadcasted_iota(jnp.int32, (rows, past), 1)
        tab_ref[...] = jnp.where(expanded > 0.5, 0.0, NEG_INF) + slope * (key - past).astype(F32)
        _state_init(m_ref, acc_ref)

    def page_stream(ch, pages):
        return jnp.concatenate([_channel(page_refs[u], ch).astype(BF16) for u in pages], axis=0)

    parts = []
    for first in range(0, pg, part_pages):
        pages = range(first, first + part_pages)
        c0 = pl.multiple_of((jg * pg + first) * PAGE_SIZE, part_pages * PAGE_SIZE)
        lg = (scores(lambda h: page_stream(h, pages))
              + tab_ref[:, pl.ds(c0, part_pages * PAGE_SIZE)])
        parts.append((lg, pv(lambda h, pages=pages: page_stream(NSA_KV_HEADS + h, pages))))
    _state_update_parts(parts, m_ref, acc_ref, True)

    @pl.when(jg == pl.num_programs(1) - 1)
    def _():
        new = lambda ch: new_ref[0, ch].astype(BF16)
        j_new = lax.broadcasted_iota(jnp.int32, (rows, PAGE_SIZE), 1)
        bias_new = jnp.where((j_new <= t_q) & (j_new < n_new), slope * j_new.astype(F32), NEG_INF)
        _state_update(scores(lambda h: new(h)) + bias_new, pv(lambda h: new(NSA_KV_HEADS + h)),
                      m_ref, acc_ref, False)
        o_s = _state_result(acc_ref)

        _state_init(m_ref, acc_ref)
        i_w = lax.broadcasted_iota(jnp.int32, (rows, w_buf), 1)
        dist_w = t_q + w_buf - i_w
        valid_w = (dist_w >= 0) & (dist_w < WINDOW) & (past - w_buf + i_w >= 0)
        bias_w = jnp.where(valid_w, slope * (i_w - w_buf).astype(F32), NEG_INF)
        win = lambda ch: win_ref[0, ch].astype(BF16)
        _state_update(scores(lambda h: win(h)) + bias_w, pv(lambda h: win(NSA_KV_HEADS + h)),
                      m_ref, acc_ref, True)
        _state_update(scores(lambda h: new(2 * NSA_KV_HEADS + h)) + bias_new,
                      pv(lambda h: new(3 * NSA_KV_HEADS + h)), m_ref, acc_ref, True)
        o_w = _state_result(acc_ref)

        gates = g_ref[0]
        mix = gates[:, 0:1] * oc_ref[0] + gates[:, 1:2] * o_s + gates[:, 2:3] * o_w
        o_ref[0] = mix * _silu(z_ref[0])


def _nsa_slc_sample(cache, page_table, q_rows, sel_rows, expand, new_rows, win_state, o_c_rows, gates_rows, z_rows,
                    pg, n_new):
    n_seq, n_pages = page_table.shape
    past = n_pages * PAGE_SIZE
    w_buf = win_state.shape[2]
    rows = n_new * NSA_HEADS
    per_seq = lambda *shape: pl.BlockSpec((1,) + shape, lambda b, j, pt: (b,) + (0,) * len(shape))
    grid_spec = pltpu.PrefetchScalarGridSpec(
        num_scalar_prefetch=1,
        grid=(n_seq, n_pages // pg),
        in_specs=_nsa_page_specs(pg, 1) + [
            per_seq(rows, HEAD_DIM),
            per_seq(rows, past // SEL_BLOCK),
            pl.BlockSpec((past // SEL_BLOCK, past), lambda b, j, pt: (0, 0)),
            per_seq(4 * NSA_KV_HEADS, PAGE_SIZE, HEAD_DIM),
            per_seq(2 * NSA_KV_HEADS, w_buf, HEAD_DIM),
            per_seq(rows, HEAD_DIM),
            per_seq(rows, 3),
            per_seq(rows, HEAD_DIM)],
        out_specs=per_seq(rows, HEAD_DIM),
        scratch_shapes=[pltpu.VMEM((rows, past), F32), pltpu.VMEM((rows, LANES), F32),
                        pltpu.VMEM((rows, 2 * HEAD_DIM), F32)],
    )
    return pl.pallas_call(
        functools.partial(_nsa_slc_kernel, pg=pg, part_pages=2, n_pages=n_pages, w_buf=w_buf, n_new=n_new),
        grid_spec=grid_spec,
        out_shape=jax.ShapeDtypeStruct((n_seq, rows, HEAD_DIM), F32),
        compiler_params=_params(("parallel", "arbitrary")),
        name="nsa_sample_slc_win",
    )(page_table, *([cache] * pg), q_rows, sel_rows, expand, new_rows, win_state, o_c_rows, gates_rows, z_rows)


def _page_forget_scan(x):
    n = x.shape[1]
    lane = lax.broadcasted_iota(jnp.int32, x.shape, 1)
    s = FOX_HEADS
    while s < n:
        x = x + jnp.where(lane >= s, pltpu.roll(x, s, axis=1), 0.0)
        s *= 2
    total = jnp.where(lane >= n - FOX_HEADS, x, 0.0)
    s = FOX_HEADS
    while s < n:
        total = total + pltpu.roll(total, n - s, axis=1)
        s *= 2
    return x, total


def _fox_sample_kernel(pt_ref, *refs, pg, part_pages, n_new):
    page_refs = refs[:pg]
    (q_ref, lc_ref, kn_ref, vn_ref, ln_ref, z_ref, o_ref, m_ref, acc_ref, carry_ref, lg_ref) = refs[pg:]
    b = pl.program_id(0)
    jg = pl.program_id(1)
    rows = n_new * FOX_HEADS
    keys = PAGE_SIZE * FOX_HEADS

    @pl.when(jg == 0)
    def _():
        _state_init(m_ref, acc_ref)
        carry_ref[...] = jnp.zeros(carry_ref.shape, F32)

    q = q_ref[0].astype(BF16)
    own = (lax.broadcasted_iota(jnp.int32, (rows, keys), 1) % FOX_HEADS
           == lax.broadcasted_iota(jnp.int32, (rows, keys), 0) % FOX_HEADS)

    for u in range(pg):
        lg_ref[u:u + 1, :] = lc_ref[pl.ds(pt_ref[b, jg * pg + u], 1), :]
    within, total = _page_forget_scan(lg_ref[...])
    carry = carry_ref[...]
    logits = []
    for u in range(pg):
        d_u = within[u:u + 1] + carry
        carry = carry + total[u:u + 1]
        k = page_refs[u][0, 0, :, 0].reshape(keys, HEAD_DIM).astype(BF16)
        logits.append(jnp.where(own, _dot_nt(q, k) * SCALE - d_u, NEG_INF))
    carry_ref[...] = carry

    def pv(pages):
        def pv_of(p):
            out = None
            for i, u in enumerate(pages):
                v = page_refs[u][0, 0, :, 1].reshape(keys, HEAD_DIM).astype(BF16)
                part = _dot(p[:, i * keys:(i + 1) * keys], _with_ones(v))
                out = part if out is None else out + part
            return out
        return pv_of

    parts = []
    for first in range(0, pg, part_pages):
        pages = range(first, first + part_pages)
        parts.append((jnp.concatenate([logits[u] for u in pages], axis=1), pv(pages)))
    _state_update_parts(parts, m_ref, acc_ref, False)

    @pl.when(jg == pl.num_programs(1) - 1)
    def _():
        within_new, _ = _page_forget_scan(ln_ref[0])
        d_new = (within_new + carry_ref[...])[:, :PAGE_SIZE]
        c = lax.broadcasted_iota(jnp.int32, (rows, PAGE_SIZE), 1)
        r = lax.broadcasted_iota(jnp.int32, (rows, PAGE_SIZE), 0)
        valid = (c < rows) & (c % FOX_HEADS == r % FOX_HEADS) & (c // FOX_HEADS <= r // FOX_HEADS)
        lg = jnp.where(valid, _dot_nt(q, kn_ref[0].astype(BF16)) * SCALE - d_new, NEG_INF)
        _state_update(lg, lambda p: _dot(p, _with_ones(vn_ref[0].astype(BF16))), m_ref, acc_ref, False)
        o_ref[0] = _state_result(acc_ref) * _silu(z_ref[0])


def _fox_sample(cache, page_table, q_rows, logf_cache, k_new, v_new, logf_new, z_rows, pg, n_new):
    n_seq, n_pages = page_table.shape
    n_pool = logf_cache.shape[0]
    rows = n_new * FOX_HEADS
    keys = PAGE_SIZE * FOX_HEADS
    per_seq = lambda *shape: pl.BlockSpec((1,) + shape, lambda b, j, pt: (b,) + (0,) * len(shape))

    def page_spec(u):
        return pl.BlockSpec((1, 1, PAGE_SIZE, 2, FOX_HEADS, HEAD_DIM),
                            lambda b, j, pt: (0, pt[b, j * pg + u], 0, 0, 0, 0))

    grid_spec = pltpu.PrefetchScalarGridSpec(
        num_scalar_prefetch=1,
        grid=(n_seq, n_pages // pg),
        in_specs=[page_spec(u) for u in range(pg)] + [
            per_seq(rows, HEAD_DIM),
            pl.BlockSpec((n_pool, keys), lambda b, j, pt: (0, 0)),
            per_seq(PAGE_SIZE, HEAD_DIM),
            per_seq(PAGE_SIZE, HEAD_DIM),
            per_seq(1, keys),
            per_seq(rows, HEAD_DIM)],
        out_specs=per_seq(rows, HEAD_DIM),
        scratch_shapes=[pltpu.VMEM((rows, LANES), F32), pltpu.VMEM((rows, 2 * HEAD_DIM), F32),
                        pltpu.VMEM((1, keys), F32), pltpu.VMEM((pg, keys), F32)],
    )
    return pl.pallas_call(
        functools.partial(_fox_sample_kernel, pg=pg, part_pages=2, n_new=n_new),
        grid_spec=grid_spec,
        out_shape=jax.ShapeDtypeStruct((n_seq, rows, HEAD_DIM), F32),
        compiler_params=_params(("parallel", "arbitrary")),
        name="fox_sample",
    )(page_table, *([cache] * pg), q_rows, logf_cache, k_new, v_new, logf_new, z_rows)


def _held_pages(page_table, needed, pg):
    n_seq, n_pages = page_table.shape
    slots = page_table.reshape(n_seq * n_pages // pg, pg)
    want = needed.reshape(slots.shape).at[0].set(True)
    step = jnp.arange(slots.shape[0], dtype=jnp.int32)[:, None]
    filled_at = lax.cummax(jnp.where(want, step, 0), axis=0)
    return jnp.take_along_axis(slots, filled_at, axis=0).reshape(n_seq, n_pages)


def _pad_axis(a, axis, size):
    pad = [(0, 0)] * a.ndim
    pad[axis] = (0, size - a.shape[axis])
    return jnp.pad(a, pad)


def _sample_layer(x, cache_nsa_kv, cache_fox_kv, cache_fox_logf, state_nsa_win, page_table,
                  norm_in, wts, cmp_pe, cmp_w_bf, w_out_bf, norm_final):
    n_seq, t_new, d = x.shape
    n_pool = cache_nsa_kv.shape[1]
    n_pages = page_table.shape[1]
    past = n_pages * PAGE_SIZE
    w_buf = state_nsa_win.shape[2]
    assert t_new <= T_PAD and t_new < SEL_BLOCK and past % SEL_BLOCK == 0
    x2d = x.reshape(n_seq * t_new, d)
    proj = _project_all(x2d, norm_in, wts, n_seq * t_new)
    n_gate = 3 * NSA_HEADS
    small = proj["small"].reshape(n_seq, t_new, LANES)
    logf = small[:, :, n_gate:n_gate + FOX_HEADS]

    def head_rows(a):
        a = _pad_axis(a.reshape(n_seq, t_new, NSA_HEADS, -1), 1, T_PAD)
        return jnp.swapaxes(a, 1, 2).reshape(n_seq, NSA_HEADS * T_PAD, a.shape[-1])

    def group_rows(a):
        a = a.reshape(n_seq, t_new, NSA_KV_HEADS, NSA_GROUP, a.shape[-1])
        return jnp.swapaxes(a, 1, 2).reshape(n_seq, NSA_HEADS * t_new, a.shape[-1])

    q_n = proj["q_n"].astype(F32)
    gates = small[:, :, :n_gate].reshape(n_seq, t_new, NSA_HEADS, 3)
    z_n = proj["z_n"].reshape(n_seq, t_new, NSA_HEADS, HEAD_DIM)
    nsa_new = jnp.concatenate([proj["kv4"].reshape(n_seq, t_new, 4 * NSA_KV_HEADS, HEAD_DIM)[:, :, 2 * NSA_KV_HEADS:],
                               proj["kvwin"].reshape(n_seq, t_new, 2 * NSA_KV_HEADS, HEAD_DIM)], axis=2)
    nsa_new = _pad_axis(jnp.swapaxes(nsa_new, 1, 2), 2, PAGE_SIZE)
    win_state = jnp.swapaxes(state_nsa_win[0].reshape(n_seq, w_buf, 2 * NSA_KV_HEADS, HEAD_DIM), 1, 2)
    blk_of_key = jnp.arange(past, dtype=jnp.int32) // SEL_BLOCK
    expand = (jnp.arange(past // SEL_BLOCK, dtype=jnp.int32)[:, None] == blk_of_key[None, :]).astype(BF16)
    nsa_cache = cache_nsa_kv.reshape(cache_nsa_kv.shape[0], n_pool, PAGE_SIZE, 2, HALF_CHANNELS, HEAD_DIM)

    pg = 16
    o_c, sel = _nsa_cmp_sample(nsa_cache, page_table, head_rows(q_n), cmp_pe, cmp_w_bf, pg)
    o_c = jnp.swapaxes(o_c.reshape(n_seq, NSA_HEADS, T_PAD, HEAD_DIM)[:, :, :t_new], 1, 2)
    sel = sel.reshape(n_seq, NSA_KV_HEADS, T_PAD, past // SEL_BLOCK)[:, :, :t_new]
    page_needed = jnp.any(sel.reshape(n_seq, -1, n_pages, PAGE_SIZE // SEL_BLOCK) > 0.5, axis=(1, 3))
    slc_pages = _held_pages(page_table, page_needed, pg)
    sel = jnp.broadcast_to(sel[:, :, :, None, :], (n_seq, NSA_KV_HEADS, t_new, NSA_GROUP, past // SEL_BLOCK))
    mix_n = _nsa_slc_sample(nsa_cache, slc_pages, group_rows(q_n.reshape(n_seq, t_new, NSA_HEADS, HEAD_DIM)),
                            sel.reshape(n_seq, NSA_HEADS * t_new, past // SEL_BLOCK), expand, nsa_new, win_state,
                            group_rows(o_c), group_rows(gates), group_rows(z_n), pg, t_new)
    mix_n = jnp.swapaxes(mix_n.reshape(n_seq, NSA_KV_HEADS, t_new, NSA_GROUP * HEAD_DIM), 1, 2)
    mix_n = mix_n.reshape(n_seq * t_new, NSA_WIDTH).astype(BF16)

    fox_rows = lambda a: a.reshape(n_seq, t_new * FOX_HEADS, HEAD_DIM)
    kv_f = proj["kv_f"].reshape(n_seq, t_new, 2, FOX_HEADS * HEAD_DIM)
    k_new = _pad_axis(fox_rows(kv_f[:, :, 0]), 1, PAGE_SIZE)
    v_new = _pad_axis(fox_rows(kv_f[:, :, 1]), 1, PAGE_SIZE)
    logf_new = _pad_axis(logf.reshape(n_seq, 1, t_new * FOX_HEADS), 2, PAGE_SIZE * FOX_HEADS)
    logf_cache = cache_fox_logf[0].reshape(n_pool, PAGE_SIZE * FOX_HEADS)
    mix_f = _fox_sample(cache_fox_kv, page_table, fox_rows(proj["q_f"].astype(F32)), logf_cache, k_new, v_new,
                        logf_new, fox_rows(proj["z_f"]), 8, t_new)
    mix_f = mix_f.reshape(n_seq * t_new, FOX_WIDTH).astype(BF16)

    y = _merge(x2d, mix_n, mix_f, w_out_bf, norm_final, n_seq * t_new).reshape(n_seq, t_new, d)
    kvwin_new = proj["kvwin"].reshape(n_seq, t_new, 2, NSA_KV_HEADS, HEAD_DIM)
    state = (proj["kv4"].reshape(n_seq, t_new, 4, NSA_KV_HEADS, HEAD_DIM),
             proj["kv_f"].reshape(n_seq, t_new, 2, FOX_HEADS, HEAD_DIM),
             logf,
             jnp.concatenate([state_nsa_win[0], kvwin_new], axis=1)[:, t_new:])
    return y, state


def kernel(x_prompt, x_sample, cache_nsa_kv, cache_fox_kv, cache_fox_logf, state_nsa_win, page_table,
           norm_in, w_in, b_gate, b_forget, cmp_pe, cmp_w, w_out, norm_final):
    assert norm_in.shape[0] == 1, "single-layer trunk"
    wts = _split_weights(w_in[0], b_gate[0], b_forget[0])
    cmp_w_bf = cmp_w[0].reshape(2, CMP_BLOCK * HEAD_DIM, HEAD_DIM).astype(BF16)
    w_out_bf = w_out[0].astype(BF16)
    y_p, st_p = _prompt_layer(x_prompt, norm_in[0], wts, cmp_pe[0], cmp_w_bf, w_out_bf, norm_final)
    y_s, st_s = _sample_layer(x_sample, cache_nsa_kv, cache_fox_kv, cache_fox_logf, state_nsa_win,
                              page_table, norm_in[0], wts, cmp_pe[0], cmp_w_bf, w_out_bf, norm_final)
    outs = [y_p, y_s]
    for s_p, s_s in zip(st_p, st_s):
        outs.extend([s_p[None], s_s[None]])
    return tuple(outs)
```

```python
import functools

import jax
import jax.numpy as jnp
from jax import lax
from jax.experimental import pallas as pl
from jax.experimental.pallas import tpu as pltpu

F32 = jnp.float32
BF16 = jnp.bfloat16
NEG_INF = float("-inf")

HEAD_DIM = 128
NSA_HEADS = 8
FOX_HEADS = 8
NSA_KV_HEADS = 2
NSA_GROUP = NSA_HEADS // NSA_KV_HEADS
NSA_WIDTH = NSA_HEADS * HEAD_DIM
FOX_WIDTH = FOX_HEADS * HEAD_DIM
KV_WIDTH = NSA_KV_HEADS * HEAD_DIM
NSA_CHANNELS = 4 * NSA_KV_HEADS
CMP_BLOCK = 32
SEL_BLOCK = 64
SEL_TOPK = 16
WINDOW = 512
PAGE_SIZE = 128
FORCED_SCORE = 1e4
RMS_EPS = 1e-6
SCALE = HEAD_DIM ** -0.5
LOG2E = 1.4426950408889634
T_PAD = 8
LANES = 128
SUBLANES = 8
PROJ_TN = 512
STAGE_PITCH = 40
VMEM_LIMIT = 56 * 1024 * 1024


def _params(sem):
    return pltpu.CompilerParams(dimension_semantics=sem, vmem_limit_bytes=VMEM_LIMIT)


def _dot(a, b):
    return jnp.dot(a, b, preferred_element_type=F32)


def _dot_nt(a, b):
    return lax.dot_general(a, b, (((1,), (1,)), ((), ())), preferred_element_type=F32)


def _sigmoid(x):
    return 1.0 / (1.0 + jnp.exp(-x))


def _silu(x):
    return x * _sigmoid(x)


def _tile_lanes(x, n):
    return x if n == 1 else jnp.concatenate([x] * n, axis=1)


def _with_ones(v):
    return jnp.concatenate([v, jnp.ones(v.shape, v.dtype)], axis=1)


def _state_init(m_ref, acc_ref):
    m_ref[...] = jnp.full(m_ref.shape, NEG_INF, F32)
    acc_ref[...] = jnp.zeros(acc_ref.shape, F32)


def _probs(lg, m_old, may_be_empty, exp=jnp.exp):
    m_new = jnp.maximum(m_old, jnp.max(lg, axis=1, keepdims=True))
    m_use = jnp.where(m_new == NEG_INF, 0.0, m_new) if may_be_empty else m_new
    alpha = exp(m_old - m_use)
    p = exp(lg - _tile_lanes(m_use, lg.shape[1] // LANES)).astype(BF16)
    return m_new, alpha, p


def _state_update_chunked(logits_of, n_chunks, chunk, v_ext, m_ref, acc_ref, p_ref, alpha_ref, may_be_empty):
    for c in range(n_chunks):
        rows = slice(c * chunk, (c + 1) * chunk)
        m_new, alpha, p = _probs(logits_of(c), m_ref[rows, :], may_be_empty, jnp.exp2)
        alpha_ref[rows, :] = alpha
        p_ref[rows, :] = p
        m_ref[rows, :] = m_new
    acc_ref[...] = _tile_lanes(alpha_ref[...], 2) * acc_ref[...] + _dot(p_ref[...], v_ext)


def _state_update(lg, pv_of, m_ref, acc_ref, may_be_empty):
    m_new, alpha, p = _probs(lg, m_ref[...], may_be_empty)
    acc_ref[...] = _tile_lanes(alpha, 2) * acc_ref[...] + pv_of(p)
    m_ref[...] = m_new


def _state_update_parts(parts, m_ref, acc_ref, may_be_empty):
    local = []
    for lg, pv_of in parts:
        m_c = jnp.broadcast_to(jnp.max(lg, axis=1, keepdims=True), m_ref.shape)
        m_use = jnp.where(m_c == NEG_INF, 0.0, m_c) if may_be_empty else m_c
        p = jnp.exp(lg - _tile_lanes(m_use, lg.shape[1] // LANES)).astype(BF16)
        local.append((m_c, pv_of(p)))
    m_old = m_ref[...]
    m_new = m_old
    for m_c, _ in local:
        m_new = jnp.maximum(m_new, m_c)
    acc = _tile_lanes(jnp.exp(m_old - m_new), 2) * acc_ref[...]
    for m_c, pv in local:
        acc = acc + _tile_lanes(jnp.exp(m_c - m_new), 2) * pv
    acc_ref[...] = acc
    m_ref[...] = m_new


def _state_result(acc_ref):
    acc = acc_ref[...]
    return acc[:, :HEAD_DIM] / jnp.maximum(acc[:, HEAD_DIM:], 1e-30)


def _lane_cumsum(x):
    n = x.shape[-1]
    lane = lax.broadcasted_iota(jnp.int32, x.shape, x.ndim - 1)
    s = 1
    while s < n:
        x = x + jnp.where(lane >= s, pltpu.roll(x, s, axis=x.ndim - 1), 0.0)
        s *= 2
    return x


def _head_slope(head):
    if isinstance(head, int):
        return 2.0 ** -(head + 1)
    return lax.bitcast_convert_type((126 - head) << 23, F32)


def _rms_kernel(x_ref, g_ref, o_ref):
    x = x_ref[...]
    ms = jnp.mean(x * x, axis=-1, keepdims=True)
    o_ref[...] = (x * lax.rsqrt(ms + RMS_EPS) * g_ref[...]).astype(o_ref.dtype)


def _rmsnorm(x2d, g, out_dtype, tm):
    m, d = x2d.shape
    return pl.pallas_call(
        _rms_kernel,
        grid=(m // tm,),
        in_specs=[pl.BlockSpec((tm, d), lambda i: (i, 0)), pl.BlockSpec((1, d), lambda i: (0, 0))],
        out_specs=pl.BlockSpec((tm, d), lambda i: (i, 0)),
        out_shape=jax.ShapeDtypeStruct((m, d), out_dtype),
        compiler_params=_params(("parallel",)),
        name="rmsnorm",
    )(x2d, g.reshape(1, d))


def _proj_kernel(h_ref, w_ref, *o_refs):
    acc = _dot(h_ref[...], w_ref[...])
    for o_ref in o_refs:
        o_ref[...] = acc.astype(o_ref.dtype)


def _project(h, w, col0, n, out_dtypes, tm, name):
    m, d = h.shape
    j0 = col0 // PROJ_TN
    return pl.pallas_call(
        _proj_kernel,
        grid=(m // tm, n // PROJ_TN),
        in_specs=[pl.BlockSpec((tm, d), lambda i, j: (i, 0)), pl.BlockSpec((d, PROJ_TN), lambda i, j: (0, j0 + j))],
        out_specs=[pl.BlockSpec((tm, PROJ_TN), lambda i, j: (i, j)) for _ in out_dtypes],
        out_shape=[jax.ShapeDtypeStruct((m, n), dt) for dt in out_dtypes],
        compiler_params=_params(("parallel", "arbitrary")),
        name=name,
    )(h, w)


def _small_kernel(h_ref, w_ref, b_ref, o_ref, *, n_gate):
    z = _dot(h_ref[...], w_ref[...]) + b_ref[...]
    lane = lax.broadcasted_iota(jnp.int32, z.shape, 1)
    log_sig = jnp.minimum(z, 0.0) - jnp.log1p(jnp.exp(-jnp.abs(z)))
    o_ref[...] = jnp.where(lane < n_gate, _sigmoid(z), log_sig)


def _project_small(h, w, b, tm, n_gate):
    m, d = h.shape
    return pl.pallas_call(
        functools.partial(_small_kernel, n_gate=n_gate),
        grid=(m // tm,),
        in_specs=[pl.BlockSpec((tm, d), lambda i: (i, 0)), pl.BlockSpec((d, LANES), lambda i: (0, 0)),
                  pl.BlockSpec((1, LANES), lambda i: (0, 0))],
        out_specs=pl.BlockSpec((tm, LANES), lambda i: (i, 0)),
        out_shape=jax.ShapeDtypeStruct((m, LANES), F32),
        compiler_params=_params(("parallel",)),
        name="proj_gates_logf",
    )(h, w, b)


PROJ_GROUPS = (("q_n", NSA_WIDTH), ("kv4", 4 * KV_WIDTH), ("kvwin", 2 * KV_WIDTH), ("z_n", NSA_WIDTH),
               ("q_f", FOX_WIDTH), ("kv_f", 2 * FOX_WIDTH), ("z_f", FOX_WIDTH))
PROJ_OUT_DTYPES = {"q_n": (BF16,), "kv4": (F32, BF16), "kvwin": (F32, BF16), "z_n": (F32,),
                   "q_f": (BF16,), "kv_f": (F32, BF16), "z_f": (F32,)}


def _regroup_kernel(w_ref, o_ref, *slab_refs, spans, slab_starts):
    col = 0
    for a, b in spans:
        o_ref[:, col:col + (b - a)] = w_ref[:, a:b].astype(o_ref.dtype)
        col += b - a
    for slab_ref, start in zip(slab_refs, slab_starts):
        slab_ref[...] = w_ref[:, start:start + LANES]


def _regroup_columns(w, spans, slab_starts):
    rows, cols = w.shape
    tr = 256
    width = sum(b - a for a, b in spans)
    row_block = lambda n: pl.BlockSpec((tr, n), lambda i: (i, 0))
    return pl.pallas_call(
        functools.partial(_regroup_kernel, spans=spans, slab_starts=slab_starts),
        grid=(rows // tr,),
        in_specs=[row_block(cols)],
        out_specs=[row_block(width)] + [row_block(LANES) for _ in slab_starts],
        out_shape=[jax.ShapeDtypeStruct((rows, width), BF16)]
        + [jax.ShapeDtypeStruct((rows, LANES), F32) for _ in slab_starts],
        compiler_params=_params(("parallel",)),
        name="regroup_weights",
    )(w)


def _split_weights(w_in, b_gate, b_forget):
    cuts = [NSA_WIDTH, 6 * KV_WIDTH, 3 * NSA_HEADS, NSA_WIDTH, FOX_WIDTH, FOX_WIDTH, FOX_WIDTH, FOX_HEADS, FOX_WIDTH]
    offs = [0]
    for c in cuts:
        offs.append(offs[-1] + c)
    small_spans = ((offs[2], offs[3]), (offs[7], offs[8]))
    slab_starts = tuple(a // LANES * LANES for a, _ in small_spans)
    assert all(b <= s + LANES for (_, b), s in zip(small_spans, slab_starts))
    main, *slabs = _regroup_columns(w_in, ((offs[0], offs[2]), (offs[3], offs[7]), (offs[8], offs[9])), slab_starts)
    n_small = 3 * NSA_HEADS + FOX_HEADS
    w_small = jnp.concatenate([slab[:, a - s:b - s] for slab, (a, b), s in zip(slabs, small_spans, slab_starts)],
                              axis=1)
    w_small = jnp.pad(w_small, ((0, 0), (0, LANES - n_small))).astype(BF16)
    b_small = jnp.pad(jnp.concatenate([b_gate, b_forget]), (0, LANES - n_small)).reshape(1, LANES).astype(F32)
    return {"main": main, "small": w_small, "b_small": b_small}


def _proj_kv4_kernel(h_ref, w_ref, chan_ref, cmp_ref, bf_ref):
    acc = _dot(h_ref[...], w_ref[...])
    for c in range(NSA_CHANNELS):
        chan_ref[:, c, :] = acc[:, c * HEAD_DIM:(c + 1) * HEAD_DIM]
    cmp_ref[...] = acc[:, :2 * KV_WIDTH]
    bf_ref[...] = acc.astype(bf_ref.dtype)


def _project_kv4(h, w, col0, tm):
    m, d = h.shape
    n = 4 * KV_WIDTH
    assert col0 % n == 0
    return pl.pallas_call(
        _proj_kv4_kernel,
        grid=(m // tm,),
        in_specs=[pl.BlockSpec((tm, d), lambda i: (i, 0)), pl.BlockSpec((d, n), lambda i: (0, col0 // n))],
        out_specs=[pl.BlockSpec((tm, NSA_CHANNELS, HEAD_DIM), lambda i: (i, 0, 0)),
                   pl.BlockSpec((tm, 2 * KV_WIDTH), lambda i: (i, 0)),
                   pl.BlockSpec((tm, n), lambda i: (i, 0))],
        out_shape=[jax.ShapeDtypeStruct((m, NSA_CHANNELS, HEAD_DIM), F32),
                   jax.ShapeDtypeStruct((m, 2 * KV_WIDTH), F32),
                   jax.ShapeDtypeStruct((m, n), BF16)],
        compiler_params=_params(("parallel",)),
        name="proj_kv4_state",
    )(h, w)


def _project_all(x2d, norm_in, wts, tm, kv4_state_order=False):
    h = _rmsnorm(x2d, norm_in, BF16, min(tm, 512))
    out = {}
    col0 = 0
    for name, width in PROJ_GROUPS:
        if name == "kv4" and kv4_state_order:
            out["kv4_state"], out["kv4_cmp"], out["kv4_bf"] = _project_kv4(h, wts["main"], col0, tm)
            col0 += width
            continue
        res = _project(h, wts["main"], col0, width, PROJ_OUT_DTYPES[name], tm, "proj_" + name)
        out[name] = res[0]
        if len(res) > 1:
            out[name + "_bf"] = res[1]
        col0 += width
    out["small"] = _project_small(h, wts["small"], wts["b_small"], min(tm, 512), 3 * NSA_HEADS)
    return out


def _merge_kernel(x_ref, mn_ref, mf_ref, wn_ref, wf_ref, g_ref, o_ref):
    y = x_ref[...] + _dot(mn_ref[...], wn_ref[...]) + _dot(mf_ref[...], wf_ref[...])
    ms = jnp.mean(y * y, axis=-1, keepdims=True)
    o_ref[...] = y * lax.rsqrt(ms + RMS_EPS) * g_ref[...]


def _merge(x2d, mix_n, mix_f, w_out_bf, norm_final, tm):
    m, d = x2d.shape
    return pl.pallas_call(
        _merge_kernel,
        grid=(m // tm,),
        in_specs=[pl.BlockSpec((tm, d), lambda i: (i, 0)),
                  pl.BlockSpec((tm, NSA_WIDTH), lambda i: (i, 0)),
                  pl.BlockSpec((tm, FOX_WIDTH), lambda i: (i, 0)),
                  pl.BlockSpec((NSA_WIDTH, d), lambda i: (0, 0)),
                  pl.BlockSpec((FOX_WIDTH, d), lambda i: (NSA_WIDTH // FOX_WIDTH, 0)),
                  pl.BlockSpec((1, d), lambda i: (0, 0))],
        out_specs=pl.BlockSpec((tm, d), lambda i: (i, 0)),
        out_shape=jax.ShapeDtypeStruct((m, d), F32),
        compiler_params=_params(("parallel",)),
        name="merge_out_proj",
    )(x2d, mix_n, mix_f, w_out_bf, w_out_bf, norm_final.reshape(1, d))


def _cumsum_kernel(x_ref, o_ref):
    o_ref[...] = _lane_cumsum(x_ref[...])


def _cumsum_rows(x):
    return pl.pallas_call(
        _cumsum_kernel,
        out_shape=jax.ShapeDtypeStruct(x.shape, F32),
        name="forget_cumsum",
    )(x)


def _compress_rows(x_ref, pe_ref, kind, w, xs_ref, nb, starts, stride):
    half = nb // 2
    for c in range(CMP_BLOCK):
        pe_c = pe_ref[kind, c:c + 1, :]
        for par in range(2):
            xc = x_ref[pl.ds(starts[par] + c, half, stride=stride), :]
            xs_ref[par * half:(par + 1) * half, c * HEAD_DIM:(c + 1) * HEAD_DIM] = (xc + pe_c).astype(BF16)
    return _dot(xs_ref[...], w)


def _compress_kernel(x_ref, pe_ref, w_ref, o_ref, xs_ref, *, nb):
    out = _compress_rows(x_ref, pe_ref, 0, w_ref[0], xs_ref, nb, (0, CMP_BLOCK), 2 * CMP_BLOCK)
    o_ref[0, 0, 0] = out.astype(o_ref.dtype)


def _compress_prompt(kv4, cmp_pe, cmp_w_bf, batch, seq):
    nb = seq // CMP_BLOCK
    return pl.pallas_call(
        functools.partial(_compress_kernel, nb=nb),
        grid=(batch, 2, NSA_KV_HEADS),
        in_specs=[pl.BlockSpec((seq, HEAD_DIM), lambda b, k, h: (b, k * NSA_KV_HEADS + h)),
                  pl.BlockSpec((1, CMP_BLOCK, HEAD_DIM), lambda b, k, h: (k, 0, 0)),
                  pl.BlockSpec((1, CMP_BLOCK * HEAD_DIM, HEAD_DIM), lambda b, k, h: (k, 0, 0))],
        out_specs=pl.BlockSpec((1, 1, 1, nb, HEAD_DIM), lambda b, k, h: (b, k, h, 0, 0)),
        out_shape=jax.ShapeDtypeStruct((batch, 2, NSA_KV_HEADS, nb, HEAD_DIM), BF16),
        scratch_shapes=[pltpu.VMEM((nb, CMP_BLOCK * HEAD_DIM), BF16)],
        compiler_params=_params(("parallel", "parallel", "arbitrary")),
        name="compress_prompt",
    )(kv4, cmp_pe, cmp_w_bf)


def _select_blocks(score_t, cur, n_keep):
    n_blk = score_t.shape[0]
    j = lax.broadcasted_iota(jnp.int32, score_t.shape, 0)
    forced = (j == 0) | (j == cur) | (j == cur - 1)
    s = jnp.where(j <= cur, jnp.where(forced, FORCED_SCORE, score_t), -1.0)
    n_grp = n_blk // SUBLANES
    grp = [s[r * SUBLANES:(r + 1) * SUBLANES] for r in range(n_grp)]
    cnt = [jnp.zeros(grp[0].shape, F32) for _ in range(n_grp)]
    sub = lax.broadcasted_iota(jnp.int32, grp[0].shape, 0)
    for jj in range(n_blk):
        row = s[jj:jj + 1, :]
        r_j, off = divmod(jj, SUBLANES)
        for r in range(n_grp):
            if r < r_j:
                hit = row > grp[r]
            elif r > r_j:
                hit = row >= grp[r]
            else:
                hit = (row > grp[r]) | ((row == grp[r]) & (sub > off))
            cnt[r] = cnt[r] + jnp.where(hit, 1.0, 0.0)
    return jnp.where((jnp.concatenate(cnt, axis=0) < n_keep) & (j <= cur), 1.0, 0.0)


def _nsa_prompt_kernel(q_ref, kc_ref, vc_ref, ks_ref, vs_ref, kw_ref, vw_ref, g_ref, z_ref, e_ref, o_ref,
                       m_ref, acc_ref, alpha_ref, ps_ref, pw_ref, mix_ref, bias_ref, *, tq, tk, tw, ck, seq):
    qi = pl.program_id(1)
    s0 = qi * tq
    n_cmp = seq // CMP_BLOCK
    n_sel = seq // SEL_BLOCK
    row_t = s0 + lax.broadcasted_iota(jnp.int32, (tq, 1), 0)
    kv_heads = range(NSA_KV_HEADS)
    heads = lambda hkv: [hkv * NSA_GROUP + g for g in range(NSA_GROUP)]
    slopes = lambda hkv: [_head_slope(h) for h in heads(hkv)]
    cols = lambda hkv: slice(hkv * HEAD_DIM, (hkv + 1) * HEAD_DIM)

    def q_rows(hkv):
        return jnp.concatenate([q_ref[:, h * HEAD_DIM:(h + 1) * HEAD_DIM] for h in heads(hkv)], axis=0)

    def gate_col(hkv, br):
        return jnp.concatenate([g_ref[:, 3 * h + br:3 * h + br + 1] for h in heads(hkv)], axis=0)

    sels = []
    for hkv in kv_heads:
        sc = _dot_nt(q_rows(hkv), kc_ref[0, 0, hkv]) * SCALE
        lane = lax.broadcasted_iota(jnp.int32, (tq, n_cmp), 1)
        blk = jnp.where(lane < n_cmp // 2, 2 * lane, 2 * lane - (n_cmp - 1))
        dist_c = (row_t - ((blk + 1) * CMP_BLOCK - 1)).astype(F32)
        valid_c = dist_c >= 0
        score = jnp.zeros((tq, n_cmp), F32)
        probs = []
        for g, slope in enumerate(slopes(hkv)):
            lg = jnp.where(valid_c, sc[g * tq:(g + 1) * tq] - slope * dist_c, NEG_INF)
            mx = jnp.max(lg, axis=1, keepdims=True)
            mx = jnp.where(mx == NEG_INF, 0.0, mx)
            p = jnp.exp(lg - mx)
            p = p / jnp.maximum(jnp.sum(p, axis=1, keepdims=True), 1e-30)
            score = score + p
            probs.append(p)
        o_c = _dot(jnp.concatenate(probs, axis=0).astype(BF16), vc_ref[0, 0, hkv])
        mix_ref[hkv] = gate_col(hkv, 0) * o_c

        score_t = score.T
        score_t = score_t[:n_sel] + score_t[n_sel:]
        cur = (s0 + lax.broadcasted_iota(jnp.int32, (n_sel, tq), 1)) // SEL_BLOCK
        sel_t = _select_blocks(score_t, cur, SEL_TOPK)
        sels.append(jnp.concatenate([sel_t, jnp.zeros((LANES - n_sel, tq), F32)], axis=0).T.astype(BF16))

    def bias_body(c, carry):
        c0 = pl.multiple_of(c * ck, ck)
        causal = c0 + lax.broadcasted_iota(jnp.int32, (tq, ck), 1) <= row_t
        for hkv in kv_heads:
            expanded = _dot(sels[hkv], e_ref[:, pl.ds(c0, ck)])
            bias_ref[hkv, :, pl.ds(c0, ck)] = jnp.where((expanded > 0.5) & causal, 0.0, NEG_INF)
        return carry

    lax.fori_loop(0, (s0 + tq + ck - 1) // ck, bias_body, 0)

    def stream(hkv, k, bias, key_rel, v, p_ref, may_be_empty):
        s = _dot_nt(q_rows(hkv), k)
        sl = slopes(hkv)
        logits_of = lambda g: s[g * tq:(g + 1) * tq] * (SCALE * LOG2E) + (bias + (sl[g] * LOG2E) * key_rel)
        _state_update_chunked(logits_of, NSA_GROUP, tq, _with_ones(v), m_ref.at[hkv], acc_ref.at[hkv],
                              p_ref.at[hkv], alpha_ref.at[hkv], may_be_empty)

    _state_init(m_ref, acc_ref)

    def slc_body(i, carry):
        k0 = pl.multiple_of(i * tk, tk)
        key_rel = (k0 - s0 + lax.broadcasted_iota(jnp.int32, (1, tk), 1)).astype(F32)
        for hkv in kv_heads:
            stream(hkv, ks_ref[pl.ds(k0, tk), cols(hkv)], bias_ref[hkv, :, pl.ds(k0, tk)], key_rel,
                   vs_ref[pl.ds(k0, tk), cols(hkv)], ps_ref, False)
        return carry

    lax.fori_loop(0, (s0 + tq + tk - 1) // tk, slc_body, 0)
    for hkv in kv_heads:
        mix_ref[hkv] = mix_ref[hkv] + gate_col(hkv, 1) * _state_result(acc_ref.at[hkv])

    _state_init(m_ref, acc_ref)

    def win_body(w, carry):
        k0 = pl.multiple_of(s0 - WINDOW + w * tw, tw)
        disti = row_t - (k0 + lax.broadcasted_iota(jnp.int32, (tq, tw), 1))
        bias = jnp.where((disti >= 0) & (disti < WINDOW), 0.0, NEG_INF)
        key_rel = (k0 - s0 + lax.broadcasted_iota(jnp.int32, (1, tw), 1)).astype(F32)
        for hkv in kv_heads:
            stream(hkv, kw_ref[pl.ds(k0, tw), cols(hkv)], bias, key_rel,
                   vw_ref[pl.ds(k0, tw), cols(hkv)], pw_ref, True)
        return carry

    n_win_tiles = (WINDOW + tq) // tw
    lax.fori_loop(jnp.maximum(0, (WINDOW - s0) // tw), n_win_tiles, win_body, 0)

    for hkv in kv_heads:
        mix = mix_ref[hkv] + gate_col(hkv, 2) * _state_result(acc_ref.at[hkv])
        for g, h in enumerate(heads(hkv)):
            z = z_ref[:, h * HEAD_DIM:(h + 1) * HEAD_DIM]
            o_ref[:, h * HEAD_DIM:(h + 1) * HEAD_DIM] = (mix[g * tq:(g + 1) * tq] * _silu(z)).astype(o_ref.dtype)


def _nsa_prompt(proj, kcv, expand, batch, seq):
    tq, tk, tw, ck = 256, 256, 256, 512
    assert seq % ck == 0 and WINDOW % tw == 0 and tq % tw == 0 and seq // SEL_BLOCK <= LANES
    nq = seq // tq
    rows = NSA_GROUP * tq
    kernel = functools.partial(_nsa_prompt_kernel, tq=tq, tk=tk, tw=tw, ck=ck, seq=seq)
    cmp_spec = lambda kind: pl.BlockSpec((1, 1, NSA_KV_HEADS, seq // CMP_BLOCK, HEAD_DIM),
                                         lambda b, i: (b, kind, 0, 0, 0))
    return pl.pallas_call(
        kernel,
        grid=(batch, nq),
        in_specs=[pl.BlockSpec((tq, NSA_WIDTH), lambda b, i: (b * nq + i, 0)),
                  cmp_spec(0), cmp_spec(1),
                  pl.BlockSpec((seq, KV_WIDTH), lambda b, i: (b, 2)),
                  pl.BlockSpec((seq, KV_WIDTH), lambda b, i: (b, 3)),
                  pl.BlockSpec((seq, KV_WIDTH), lambda b, i: (b, 0)),
                  pl.BlockSpec((seq, KV_WIDTH), lambda b, i: (b, 1)),
                  pl.BlockSpec((tq, LANES), lambda b, i: (b * nq + i, 0)),
                  pl.BlockSpec((tq, NSA_WIDTH), lambda b, i: (b * nq + i, 0)),
                  pl.BlockSpec((LANES, seq), lambda b, i: (0, 0))],
        out_specs=pl.BlockSpec((tq, NSA_WIDTH), lambda b, i: (b * nq + i, 0)),
        out_shape=jax.ShapeDtypeStruct((batch * seq, NSA_WIDTH), BF16),
        scratch_shapes=[pltpu.VMEM((NSA_KV_HEADS, rows, LANES), F32),
                        pltpu.VMEM((NSA_KV_HEADS, rows, 2 * HEAD_DIM), F32),
                        pltpu.VMEM((NSA_KV_HEADS, rows, LANES), F32),
                        pltpu.VMEM((NSA_KV_HEADS, rows, tk), BF16),
                        pltpu.VMEM((NSA_KV_HEADS, rows, tw), BF16),
                        pltpu.VMEM((NSA_KV_HEADS, rows, HEAD_DIM), F32),
                        pltpu.VMEM((NSA_KV_HEADS, tq, seq), F32)],
        compiler_params=_params(("parallel", "arbitrary")),
        name="nsa_prompt",
    )(proj["q_n"], kcv, kcv, proj["kv4_bf"], proj["kv4_bf"], proj["kvwin_bf"], proj["kvwin_bf"],
      proj["small"], proj["z_n"], expand)


def _fox_prompt_kernel(q_ref, k_ref, v_ref, d_ref, z_ref, o_ref, m_ref, acc_ref, alpha_ref, p_ref, *,
                       t, chunk, n_heads):
    qi = pl.program_id(2)
    _state_init(m_ref, acc_ref)

    def tile(i, causal):
        k0 = pl.multiple_of(i * t, t)
        for hh in range(n_heads):
            cols = slice(hh * HEAD_DIM, (hh + 1) * HEAD_DIM)
            s = _dot_nt(q_ref[:, cols], k_ref[pl.ds(k0, t), cols])
            d_row = d_ref[hh, :, pl.ds(k0, t)] * LOG2E

            def logits_of(c, s=s, d_row=d_row):
                lg = s[c * chunk:(c + 1) * chunk] * (SCALE * LOG2E) - d_row
                if causal:
                    row = c * chunk + lax.broadcasted_iota(jnp.int32, (chunk, t), 0)
                    col = lax.broadcasted_iota(jnp.int32, (chunk, t), 1)
                    lg = jnp.where(col <= row, lg, NEG_INF)
                return lg

            _state_update_chunked(logits_of, t // chunk, chunk, _with_ones(v_ref[pl.ds(k0, t), cols]),
                                  m_ref.at[hh], acc_ref.at[hh], p_ref.at[hh], alpha_ref.at[hh], False)

    def body(i, carry):
        tile(i, False)
        return carry

    lax.fori_loop(0, qi, body, 0)
    tile(qi, True)
    for hh in range(n_heads):
        cols = slice(hh * HEAD_DIM, (hh + 1) * HEAD_DIM)
        o_ref[:, cols] = (_state_result(acc_ref.at[hh]) * _silu(z_ref[:, cols])).astype(o_ref.dtype)


def _fox_prompt(proj, d_rows, batch, seq):
    t, chunk, n_heads = 512, 128, 4
    nq = seq // t
    width = n_heads * HEAD_DIM
    n_hg = FOX_HEADS // n_heads
    return pl.pallas_call(
        functools.partial(_fox_prompt_kernel, t=t, chunk=chunk, n_heads=n_heads),
        grid=(batch, n_hg, nq),
        in_specs=[pl.BlockSpec((t, width), lambda b, h, i: (b * nq + i, h)),
                  pl.BlockSpec((seq, width), lambda b, h, i: (b, h)),
                  pl.BlockSpec((seq, width), lambda b, h, i: (b, n_hg + h)),
                  pl.BlockSpec((n_heads, 1, seq), lambda b, h, i: (b * n_hg + h, 0, 0)),
                  pl.BlockSpec((t, width), lambda b, h, i: (b * nq + i, h))],
        out_specs=pl.BlockSpec((t, width), lambda b, h, i: (b * nq + i, h)),
        out_shape=jax.ShapeDtypeStruct((batch * seq, FOX_WIDTH), BF16),
        scratch_shapes=[pltpu.VMEM((n_heads, t, LANES), F32), pltpu.VMEM((n_heads, t, 2 * HEAD_DIM), F32),
                        pltpu.VMEM((n_heads, t, LANES), F32), pltpu.VMEM((n_heads, t, t), BF16)],
        compiler_params=_params(("parallel", "parallel", "arbitrary")),
        name="fox_prompt",
    )(proj["q_f"], proj["kv_f_bf"], proj["kv_f_bf"], d_rows, proj["z_f"])


def _prompt_layer(x, norm_in, wts, cmp_pe, cmp_w_bf, w_out_bf, norm_final):
    batch, seq, d = x.shape
    x2d = x.reshape(batch * seq, d)
    proj = _project_all(x2d, norm_in, wts, 1024, kv4_state_order=True)
    n_gate = 3 * NSA_HEADS
    logf = proj["small"][:, n_gate:n_gate + FOX_HEADS].reshape(batch, seq, FOX_HEADS)
    d_rows = _cumsum_rows(jnp.swapaxes(logf, 1, 2).reshape(batch * FOX_HEADS, seq))
    kcv = _compress_prompt(proj["kv4_cmp"], cmp_pe, cmp_w_bf, batch, seq)
    blk_of_key = jnp.arange(seq, dtype=jnp.int32) // SEL_BLOCK
    expand = (jnp.arange(LANES, dtype=jnp.int32)[:, None] == blk_of_key[None, :]).astype(BF16)
    mix_n = _nsa_prompt(proj, kcv, expand, batch, seq)
    mix_f = _fox_prompt(proj, d_rows.reshape(batch * FOX_HEADS, 1, seq), batch, seq)
    y = _merge(x2d, mix_n, mix_f, w_out_bf, norm_final, 512).reshape(batch, seq, d)
    w_keep = min(WINDOW, seq)
    state = (proj["kv4_state"].reshape(batch, seq, 4, NSA_KV_HEADS, HEAD_DIM),
             proj["kv_f"].reshape(batch, seq, 2, FOX_HEADS, HEAD_DIM),
             logf,
             proj["kvwin"].reshape(batch, seq, 2 * KV_WIDTH)[:, seq - w_keep:].reshape(
                 batch, w_keep, 2, NSA_KV_HEADS, HEAD_DIM))
    return y, state


HALF_CHANNELS = NSA_CHANNELS // 2


def _nsa_page_specs(pg, half):
    def spec(u):
        return pl.BlockSpec((1, 1, PAGE_SIZE, 1, HALF_CHANNELS, HEAD_DIM),
                            lambda b, j, pt: (0, pt[b, j * pg + u], 0, half, 0, 0))
    return [spec(u) for u in range(pg)]


def _channel(page_ref, ch, start=0, n=PAGE_SIZE):
    rows = page_ref.reshape(PAGE_SIZE * HALF_CHANNELS, HEAD_DIM)
    return rows[pl.ds(start * HALF_CHANNELS + ch, n, stride=HALF_CHANNELS), :]


def _nsa_cmp_kernel(pt_ref, *refs, pg, n_pages):
    page_refs = refs[:pg]
    q_ref, pe_ref, w_ref, oc_ref, sel_ref, stage_ref, xs_ref = refs[pg:]
    jg = pl.program_id(1)
    past = n_pages * PAGE_SIZE
    nb = past // CMP_BLOCK
    half = nb // 2
    blocks_per_page = PAGE_SIZE // CMP_BLOCK
    for u in range(pg):
        for bl in range(blocks_per_page):
            m = (jg * pg + u) * (blocks_per_page // 2) + bl // 2
            dst = pl.multiple_of(((bl % 2) * half + m) * STAGE_PITCH, SUBLANES)
            for ch in range(2 * NSA_KV_HEADS):
                stage_ref[ch, pl.ds(dst, CMP_BLOCK), :] = _channel(page_refs[u], ch, bl * CMP_BLOCK, CMP_BLOCK)

    @pl.when(jg == pl.num_programs(1) - 1)
    def _():
        cmp = [[_compress_rows(stage_ref.at[kind * NSA_KV_HEADS + h], pe_ref, kind, w_ref[kind], xs_ref, nb,
                               (0, half * STAGE_PITCH), STAGE_PITCH).astype(BF16)
                for h in range(NSA_KV_HEADS)] for kind in range(2)]
        rows = NSA_GROUP * T_PAD
        for hkv in range(NSA_KV_HEADS):
            q = q_ref[0, hkv * rows:(hkv + 1) * rows, :].astype(BF16)
            kc, vc = cmp[0][hkv], cmp[1][hkv]
            s = _dot_nt(q, kc) * SCALE
            lane = lax.broadcasted_iota(jnp.int32, (rows, nb), 1)
            blk = jnp.where(lane < half, 2 * lane, 2 * lane - (nb - 1))
            row = lax.broadcasted_iota(jnp.int32, (rows, nb), 0)
            slope = _head_slope(hkv * NSA_GROUP + row // T_PAD)
            dist = (past + row % T_PAD - ((blk + 1) * CMP_BLOCK - 1)).astype(F32)
            lg = jnp.where(dist >= 0, s - slope * dist, NEG_INF)
            mx = jnp.max(lg, axis=1, keepdims=True)
            mx = jnp.where(mx == NEG_INF, 0.0, mx)
            p = jnp.exp(lg - mx)
            p = p / jnp.maximum(jnp.sum(p, axis=1, keepdims=True), 1e-30)
            oc_ref[0, hkv * rows:(hkv + 1) * rows, :] = _dot(p.astype(BF16), vc)
            score = p[0:T_PAD]
            for g in range(1, NSA_GROUP):
                score = score + p[g * T_PAD:(g + 1) * T_PAD]
            score = score[:, :half] + score[:, half:]
            blk_s = lax.broadcasted_iota(jnp.int32, (T_PAD, half), 1).astype(F32)
            forced = (blk_s == 0) | (blk_s == half - 1)
            s_left = jnp.where(forced, NEG_INF, score)
            picked = jnp.zeros((T_PAD, half), F32)
            for _ in range(SEL_TOPK - 3):
                best = jnp.max(s_left, axis=1, keepdims=True)
                first = jnp.min(jnp.where(s_left == best, blk_s, float(half)), axis=1, keepdims=True)
                hit = blk_s == first
                picked = jnp.where(hit, 1.0, picked)
                s_left = jnp.where(hit, NEG_INF, s_left)
            sel_ref[0, hkv * T_PAD:(hkv + 1) * T_PAD, :] = jnp.where(forced, 1.0, picked)


def _nsa_cmp_sample(cache, page_table, q_rows, cmp_pe, cmp_w_bf, pg):
    n_seq, n_pages = page_table.shape
    past = n_pages * PAGE_SIZE
    nb = past // CMP_BLOCK
    assert n_pages % pg == 0 and past // SEL_BLOCK >= SEL_TOPK
    rows = NSA_HEADS * T_PAD
    grid_spec = pltpu.PrefetchScalarGridSpec(
        num_scalar_prefetch=1,
        grid=(n_seq, n_pages // pg),
        in_specs=_nsa_page_specs(pg, 0) + [
            pl.BlockSpec((1, rows, HEAD_DIM), lambda b, j, pt: (b, 0, 0)),
            pl.BlockSpec((2, CMP_BLOCK, HEAD_DIM), lambda b, j, pt: (0, 0, 0)),
            pl.BlockSpec((2, CMP_BLOCK * HEAD_DIM, HEAD_DIM), lambda b, j, pt: (0, 0, 0))],
        out_specs=[pl.BlockSpec((1, rows, HEAD_DIM), lambda b, j, pt: (b, 0, 0)),
                   pl.BlockSpec((1, NSA_KV_HEADS * T_PAD, past // SEL_BLOCK), lambda b, j, pt: (b, 0, 0))],
        scratch_shapes=[pltpu.VMEM((2 * NSA_KV_HEADS, nb * STAGE_PITCH, HEAD_DIM), F32),
                        pltpu.VMEM((nb, CMP_BLOCK * HEAD_DIM), BF16)],
    )
    return pl.pallas_call(
        functools.partial(_nsa_cmp_kernel, pg=pg, n_pages=n_pages),
        grid_spec=grid_spec,
        out_shape=[jax.ShapeDtypeStruct((n_seq, rows, HEAD_DIM), F32),
                   jax.ShapeDtypeStruct((n_seq, NSA_KV_HEADS * T_PAD, past // SEL_BLOCK), F32)],
        compiler_params=_params(("parallel", "arbitrary")),
        name="nsa_sample_cmp",
    )(page_table, *([cache] * pg), q_rows, cmp_pe, cmp_w_bf)


def _nsa_slc_kernel(pt_ref, *refs, pg, part_pages, n_pages, w_buf, n_new):
    page_refs = refs[:pg]
    (q_ref, sel_ref, e_ref, new_ref, win_ref, oc_ref, g_ref, z_ref, o_ref, tab_ref, m_ref, acc_ref) = refs[pg:]
    jg = pl.program_id(1)
    past = n_pages * PAGE_SIZE
    hr = n_new * NSA_GROUP
    rows = NSA_KV_HEADS * hr
    row = lax.broadcasted_iota(jnp.int32, (rows, 1), 0)
    t_q = (row % hr) // NSA_GROUP
    slope = _head_slope((row // hr) * NSA_GROUP + row % NSA_GROUP)
    q = [q_ref[0, h * hr:(h + 1) * hr, :].astype(BF16) for h in range(NSA_KV_HEADS)]

    def scores(k_of):
        return jnp.concatenate([_dot_nt(q[h], k_of(h)) for h in range(NSA_KV_HEADS)], axis=0) * SCALE

    def pv(v_of):
        return lambda p: jnp.concatenate(
            [_dot(p[h * hr:(h + 1) * hr], _with_ones(v_of(h))) for h in range(NSA_KV_HEADS)], axis=0)


    @pl.when(jg == 0)
    def _():
        expanded = _dot(sel_ref[0].astype(BF16), e_ref[...])
        key = lax.broadcasted_iota(jnp.int32, (rows, past), 1)
        tab_ref[...] = jnp.where(expanded > 0.5, 0.0, NEG_INF) + slope * (key - past).astype(F32)
        _state_init(m_ref, acc_ref)

    def page_stream(ch, pages):
        return jnp.concatenate([_channel(page_refs[u], ch).astype(BF16) for u in pages], axis=0)

    parts = []
    for first in range(0, pg, part_pages):
        pages = range(first, first + part_pages)
        c0 = pl.multiple_of((jg * pg + first) * PAGE_SIZE, part_pages * PAGE_SIZE)
        lg = (scores(lambda h: page_stream(h, pages))
              + tab_ref[:, pl.ds(c0, part_pages * PAGE_SIZE)])
        parts.append((lg, pv(lambda h, pages=pages: page_stream(NSA_KV_HEADS + h, pages))))
    _state_update_parts(parts, m_ref, acc_ref, True)

    @pl.when(jg == pl.num_programs(1) - 1)
    def _():
        new = lambda ch: new_ref[0, ch].astype(BF16)
        j_new = lax.broadcasted_iota(jnp.int32, (rows, PAGE_SIZE), 1)
        bias_new = jnp.where((j_new <= t_q) & (j_new < n_new), slope * j_new.astype(F32), NEG_INF)
        _state_update(scores(lambda h: new(h)) + bias_new, pv(lambda h: new(NSA_KV_HEADS + h)),
                      m_ref, acc_ref, False)
        o_s = _state_result(acc_ref)

        _state_init(m_ref, acc_ref)
        i_w = lax.broadcasted_iota(jnp.int32, (rows, w_buf), 1)
        dist_w = t_q + w_buf - i_w
        valid_w = (dist_w >= 0) & (dist_w < WINDOW) & (past - w_buf + i_w >= 0)
        bias_w = jnp.where(valid_w, slope * (i_w - w_buf).astype(F32), NEG_INF)
        win = lambda ch: win_ref[0, ch].astype(BF16)
        _state_update(scores(lambda h: win(h)) + bias_w, pv(lambda h: win(NSA_KV_HEADS + h)),
                      m_ref, acc_ref, True)
        _state_update(scores(lambda h: new(2 * NSA_KV_HEADS + h)) + bias_new,
                      pv(lambda h: new(3 * NSA_KV_HEADS + h)), m_ref, acc_ref, True)
        o_w = _state_result(acc_ref)

        gates = g_ref[0]
        mix = gates[:, 0:1] * oc_ref[0] + gates[:, 1:2] * o_s + gates[:, 2:3] * o_w
        o_ref[0] = mix * _silu(z_ref[0])


def _nsa_slc_sample(cache, page_table, q_rows, sel_rows, expand, new_rows, win_state, o_c_rows, gates_rows, z_rows,
                    pg, n_new):
    n_seq, n_pages = page_table.shape
    past = n_pages * PAGE_SIZE
    w_buf = win_state.shape[2]
    rows = n_new * NSA_HEADS
    per_seq = lambda *shape: pl.BlockSpec((1,) + shape, lambda b, j, pt: (b,) + (0,) * len(shape))
    grid_spec = pltpu.PrefetchScalarGridSpec(
        num_scalar_prefetch=1,
        grid=(n_seq, n_pages // pg),
        in_specs=_nsa_page_specs(pg, 1) + [
            per_seq(rows, HEAD_DIM),
            per_seq(rows, past // SEL_BLOCK),
            pl.BlockSpec((past // SEL_BLOCK, past), lambda b, j, pt: (0, 0)),
            per_seq(4 * NSA_KV_HEADS, PAGE_SIZE, HEAD_DIM),
            per_seq(2 * NSA_KV_HEADS, w_buf, HEAD_DIM),
            per_seq(rows, HEAD_DIM),
            per_seq(rows, 3),
            per_seq(rows, HEAD_DIM)],
        out_specs=per_seq(rows, HEAD_DIM),
        scratch_shapes=[pltpu.VMEM((rows, past), F32), pltpu.VMEM((rows, LANES), F32),
                        pltpu.VMEM((rows, 2 * HEAD_DIM), F32)],
    )
    return pl.pallas_call(
        functools.partial(_nsa_slc_kernel, pg=pg, part_pages=2, n_pages=n_pages, w_buf=w_buf, n_new=n_new),
        grid_spec=grid_spec,
        out_shape=jax.ShapeDtypeStruct((n_seq, rows, HEAD_DIM), F32),
        compiler_params=_params(("parallel", "arbitrary")),
        name="nsa_sample_slc_win",
    )(page_table, *([cache] * pg), q_rows, sel_rows, expand, new_rows, win_state, o_c_rows, gates_rows, z_rows)


def _page_forget_scan(x):
    n = x.shape[1]
    lane = lax.broadcasted_iota(jnp.int32, x.shape, 1)
    s = FOX_HEADS
    while s < n:
        x = x + jnp.where(lane >= s, pltpu.roll(x, s, axis=1), 0.0)
        s *= 2
    total = jnp.where(lane >= n - FOX_HEADS, x, 0.0)
    s = FOX_HEADS
    while s < n:
        total = total + pltpu.roll(total, n - s, axis=1)
        s *= 2
    return x, total


def _fox_sample_kernel(pt_ref, *refs, pg, part_pages, n_new):
    page_refs = refs[:pg]
    (q_ref, lc_ref, kn_ref, vn_ref, ln_ref, z_ref, o_ref, m_ref, acc_ref, carry_ref, lg_ref) = refs[pg:]
    b = pl.program_id(0)
    jg = pl.program_id(1)
    rows = n_new * FOX_HEADS
    keys = PAGE_SIZE * FOX_HEADS

    @pl.when(jg == 0)
    def _():
        _state_init(m_ref, acc_ref)
        carry_ref[...] = jnp.zeros(carry_ref.shape, F32)

    q = q_ref[0].astype(BF16)
    own = (lax.broadcasted_iota(jnp.int32, (rows, keys), 1) % FOX_HEADS
           == lax.broadcasted_iota(jnp.int32, (rows, keys), 0) % FOX_HEADS)

    for u in range(pg):
        lg_ref[u:u + 1, :] = lc_ref[pl.ds(pt_ref[b, jg * pg + u], 1), :]
    within, total = _page_forget_scan(lg_ref[...])
    carry = carry_ref[...]
    logits = []
    for u in range(pg):
        d_u = within[u:u + 1] + carry
        carry = carry + total[u:u + 1]
        k = page_refs[u][0, 0, :, 0].reshape(keys, HEAD_DIM).astype(BF16)
        logits.append(jnp.where(own, _dot_nt(q, k) * SCALE - d_u, NEG_INF))
    carry_ref[...] = carry

    def pv(pages):
        def pv_of(p):
            out = None
            for i, u in enumerate(pages):
                v = page_refs[u][0, 0, :, 1].reshape(keys, HEAD_DIM).astype(BF16)
                part = _dot(p[:, i * keys:(i + 1) * keys], _with_ones(v))
                out = part if out is None else out + part
            return out
        return pv_of

    parts = []
    for first in range(0, pg, part_pages):
        pages = range(first, first + part_pages)
        parts.append((jnp.concatenate([logits[u] for u in pages], axis=1), pv(pages)))
    _state_update_parts(parts, m_ref, acc_ref, False)

    @pl.when(jg == pl.num_programs(1) - 1)
    def _():
        within_new, _ = _page_forget_scan(ln_ref[0])
        d_new = (within_new + carry_ref[...])[:, :PAGE_SIZE]
        c = lax.broadcasted_iota(jnp.int32, (rows, PAGE_SIZE), 1)
        r = lax.broadcasted_iota(jnp.int32, (rows, PAGE_SIZE), 0)
        valid = (c < rows) & (c % FOX_HEADS == r % FOX_HEADS) & (c // FOX_HEADS <= r // FOX_HEADS)
        lg = jnp.where(valid, _dot_nt(q, kn_ref[0].astype(BF16)) * SCALE - d_new, NEG_INF)
        _state_update(lg, lambda p: _dot(p, _with_ones(vn_ref[0].astype(BF16))), m_ref, acc_ref, False)
        o_ref[0] = _state_result(acc_ref) * _silu(z_ref[0])


def _fox_sample(cache, page_table, q_rows, logf_cache, k_new, v_new, logf_new, z_rows, pg, n_new):
    n_seq, n_pages = page_table.shape
    n_pool = logf_cache.shape[0]
    rows = n_new * FOX_HEADS
    keys = PAGE_SIZE * FOX_HEADS
    per_seq = lambda *shape: pl.BlockSpec((1,) + shape, lambda b, j, pt: (b,) + (0,) * len(shape))

    def page_spec(u):
        return pl.BlockSpec((1, 1, PAGE_SIZE, 2, FOX_HEADS, HEAD_DIM),
                            lambda b, j, pt: (0, pt[b, j * pg + u], 0, 0, 0, 0))

    grid_spec = pltpu.PrefetchScalarGridSpec(
        num_scalar_prefetch=1,
        grid=(n_seq, n_pages // pg),
        in_specs=[page_spec(u) for u in range(pg)] + [
            per_seq(rows, HEAD_DIM),
            pl.BlockSpec((n_pool, keys), lambda b, j, pt: (0, 0), pipeline_mode=pl.Buffered(1)),
            per_seq(PAGE_SIZE, HEAD_DIM),
            per_seq(PAGE_SIZE, HEAD_DIM),
            per_seq(1, keys),
            per_seq(rows, HEAD_DIM)],
        out_specs=per_seq(rows, HEAD_DIM),
        scratch_shapes=[pltpu.VMEM((rows, LANES), F32), pltpu.VMEM((rows, 2 * HEAD_DIM), F32),
                        pltpu.VMEM((1, keys), F32), pltpu.VMEM((pg, keys), F32)],
    )
    return pl.pallas_call(
        functools.partial(_fox_sample_kernel, pg=pg, part_pages=2, n_new=n_new),
        grid_spec=grid_spec,
        out_shape=jax.ShapeDtypeStruct((n_seq, rows, HEAD_DIM), F32),
        compiler_params=_params(("parallel", "arbitrary")),
        name="fox_sample",
    )(page_table, *([cache] * pg), q_rows, logf_cache, k_new, v_new, logf_new, z_rows)


def _held_pages(page_table, needed, pg):
    n_seq, n_pages = page_table.shape
    slots = page_table.reshape(n_seq * n_pages // pg, pg)
    want = needed.reshape(slots.shape).at[0].set(True)
    step = jnp.arange(slots.shape[0], dtype=jnp.int32)[:, None]
    filled_at = lax.cummax(jnp.where(want, step, 0), axis=0)
    return jnp.take_along_axis(slots, filled_at, axis=0).reshape(n_seq, n_pages)


def _pad_axis(a, axis, size):
    pad = [(0, 0)] * a.ndim
    pad[axis] = (0, size - a.shape[axis])
    return jnp.pad(a, pad)


def _sample_layer(x, cache_nsa_kv, cache_fox_kv, cache_fox_logf, state_nsa_win, page_table,
                  norm_in, wts, cmp_pe, cmp_w_bf, w_out_bf, norm_final):
    n_seq, t_new, d = x.shape
    n_pool = cache_nsa_kv.shape[1]
    n_pages = page_table.shape[1]
    past = n_pages * PAGE_SIZE
    w_buf = state_nsa_win.shape[2]
    assert t_new <= T_PAD and t_new < SEL_BLOCK and past % SEL_BLOCK == 0
    x2d = x.reshape(n_seq * t_new, d)
    proj = _project_all(x2d, norm_in, wts, n_seq * t_new)
    n_gate = 3 * NSA_HEADS
    small = proj["small"].reshape(n_seq, t_new, LANES)
    logf = small[:, :, n_gate:n_gate + FOX_HEADS]

    def head_rows(a):
        a = _pad_axis(a.reshape(n_seq, t_new, NSA_HEADS, -1), 1, T_PAD)
        return jnp.swapaxes(a, 1, 2).reshape(n_seq, NSA_HEADS * T_PAD, a.shape[-1])

    def group_rows(a):
        a = a.reshape(n_seq, t_new, NSA_KV_HEADS, NSA_GROUP, a.shape[-1])
        return jnp.swapaxes(a, 1, 2).reshape(n_seq, NSA_HEADS * t_new, a.shape[-1])

    q_n = proj["q_n"].astype(F32)
    gates = small[:, :, :n_gate].reshape(n_seq, t_new, NSA_HEADS, 3)
    z_n = proj["z_n"].reshape(n_seq, t_new, NSA_HEADS, HEAD_DIM)
    nsa_new = jnp.concatenate([proj["kv4"].reshape(n_seq, t_new, 4 * NSA_KV_HEADS, HEAD_DIM)[:, :, 2 * NSA_KV_HEADS:],
                               proj["kvwin"].reshape(n_seq, t_new, 2 * NSA_KV_HEADS, HEAD_DIM)], axis=2)
    nsa_new = _pad_axis(jnp.swapaxes(nsa_new, 1, 2), 2, PAGE_SIZE)
    win_state = jnp.swapaxes(state_nsa_win[0].reshape(n_seq, w_buf, 2 * NSA_KV_HEADS, HEAD_DIM), 1, 2)
    blk_of_key = jnp.arange(past, dtype=jnp.int32) // SEL_BLOCK
    expand = (jnp.arange(past // SEL_BLOCK, dtype=jnp.int32)[:, None] == blk_of_key[None, :]).astype(BF16)
    nsa_cache = cache_nsa_kv.reshape(cache_nsa_kv.shape[0], n_pool, PAGE_SIZE, 2, HALF_CHANNELS, HEAD_DIM)

    pg = 16
    o_c, sel = _nsa_cmp_sample(nsa_cache, page_table, head_rows(q_n), cmp_pe, cmp_w_bf, pg)
    o_c = jnp.swapaxes(o_c.reshape(n_seq, NSA_HEADS, T_PAD, HEAD_DIM)[:, :, :t_new], 1, 2)
    sel = sel.reshape(n_seq, NSA_KV_HEADS, T_PAD, past // SEL_BLOCK)[:, :, :t_new]
    page_needed = jnp.any(sel.reshape(n_seq, -1, n_pages, PAGE_SIZE // SEL_BLOCK) > 0.5, axis=(1, 3))
    slc_pages = _held_pages(page_table, page_needed, pg)
    sel = jnp.broadcast_to(sel[:, :, :, None, :], (n_seq, NSA_KV_HEADS, t_new, NSA_GROUP, past // SEL_BLOCK))
    mix_n = _nsa_slc_sample(nsa_cache, slc_pages, group_rows(q_n.reshape(n_seq, t_new, NSA_HEADS, HEAD_DIM)),
                            sel.reshape(n_seq, NSA_HEADS * t_new, past // SEL_BLOCK), expand, nsa_new, win_state,
                            group_rows(o_c), group_rows(gates), group_rows(z_n), pg, t_new)
    mix_n = jnp.swapaxes(mix_n.reshape(n_seq, NSA_KV_HEADS, t_new, NSA_GROUP * HEAD_DIM), 1, 2)
    mix_n = mix_n.reshape(n_seq * t_new, NSA_WIDTH).astype(BF16)

    fox_rows = lambda a: a.reshape(n_seq, t_new * FOX_HEADS, HEAD_DIM)
    kv_f = proj["kv_f"].reshape(n_seq, t_new, 2, FOX_HEADS * HEAD_DIM)
    k_new = _pad_axis(fox_rows(kv_f[:, :, 0]), 1, PAGE_SIZE)
    v_new = _pad_axis(fox_rows(kv_f[:, :, 1]), 1, PAGE_SIZE)
    logf_new = _pad_axis(logf.reshape(n_seq, 1, t_new * FOX_HEADS), 2, PAGE_SIZE * FOX_HEADS)
    logf_cache = cache_fox_logf[0].reshape(n_pool, PAGE_SIZE * FOX_HEADS)
    mix_f = _fox_sample(cache_fox_kv, page_table, fox_rows(proj["q_f"].astype(F32)), logf_cache, k_new, v_new,
                        logf_new, fox_rows(proj["z_f"]), 16, t_new)
    mix_f = mix_f.reshape(n_seq * t_new, FOX_WIDTH).astype(BF16)

    y = _merge(x2d, mix_n, mix_f, w_out_bf, norm_final, n_seq * t_new).reshape(n_seq, t_new, d)
    kvwin_new = proj["kvwin"].reshape(n_seq, t_new, 2, NSA_KV_HEADS, HEAD_DIM)
    state = (proj["kv4"].reshape(n_seq, t_new, 4, NSA_KV_HEADS, HEAD_DIM),
             proj["kv_f"].reshape(n_seq, t_new, 2, FOX_HEADS, HEAD_DIM),
             logf,
             jnp.concatenate([state_nsa_win[0], kvwin_new], axis=1)[:, t_new:])
    return y, state


def kernel(x_prompt, x_sample, cache_nsa_kv, cache_fox_kv, cache_fox_logf, state_nsa_win, page_table,
           norm_in, w_in, b_gate, b_forget, cmp_pe, cmp_w, w_out, norm_final):
    assert norm_in.shape[0] == 1, "single-layer trunk"
    wts = _split_weights(w_in[0], b_gate[0], b_forget[0])
    cmp_w_bf = cmp_w[0].reshape(2, CMP_BLOCK * HEAD_DIM, HEAD_DIM).astype(BF16)
    w_out_bf = w_out[0].astype(BF16)
    y_p, st_p = _prompt_layer(x_prompt, norm_in[0], wts, cmp_pe[0], cmp_w_bf, w_out_bf, norm_final)
    y_s, st_s = _sample_layer(x_sample, cache_nsa_kv, cache_fox_kv, cache_fox_logf, state_nsa_win,
                              page_table, norm_in[0], wts, cmp_pe[0], cmp_w_bf, w_out_bf, norm_final)
    outs = [y_p, y_s]
    for s_p, s_s in zip(st_p, st_s):
        outs.extend([s_p[None], s_s[None]])
    return tuple(outs)
```

```python
import functools

import jax
import jax.numpy as jnp
from jax import lax
from jax.experimental import pallas as pl
from jax.experimental.pallas import tpu as pltpu

F32 = jnp.float32
BF16 = jnp.bfloat16
NEG_INF = float("-inf")

HEAD_DIM = 128
NSA_HEADS = 8
FOX_HEADS = 8
NSA_KV_HEADS = 2
NSA_GROUP = NSA_HEADS // NSA_KV_HEADS
NSA_WIDTH = NSA_HEADS * HEAD_DIM
FOX_WIDTH = FOX_HEADS * HEAD_DIM
KV_WIDTH = NSA_KV_HEADS * HEAD_DIM
NSA_CHANNELS = 4 * NSA_KV_HEADS
CMP_BLOCK = 32
SEL_BLOCK = 64
SEL_TOPK = 16
WINDOW = 512
PAGE_SIZE = 128
FORCED_SCORE = 1e4
RMS_EPS = 1e-6
SCALE = HEAD_DIM ** -0.5
LOG2E = 1.4426950408889634
T_PAD = 8
LANES = 128
SUBLANES = 8
PROJ_TN = 512
STAGE_PITCH = 40
VMEM_LIMIT = 56 * 1024 * 1024


def _params(sem):
    return pltpu.CompilerParams(dimension_semantics=sem, vmem_limit_bytes=VMEM_LIMIT)


def _dot(a, b):
    return jnp.dot(a, b, preferred_element_type=F32)


def _dot_nt(a, b):
    return lax.dot_general(a, b, (((1,), (1,)), ((), ())), preferred_element_type=F32)


def _sigmoid(x):
    return 1.0 / (1.0 + jnp.exp(-x))


def _silu(x):
    return x * _sigmoid(x)


def _tile_lanes(x, n):
    return x if n == 1 else jnp.concatenate([x] * n, axis=1)


def _with_ones(v):
    return jnp.concatenate([v, jnp.ones(v.shape, v.dtype)], axis=1)


def _state_init(m_ref, acc_ref):
    m_ref[...] = jnp.full(m_ref.shape, NEG_INF, F32)
    acc_ref[...] = jnp.zeros(acc_ref.shape, F32)


def _probs(lg, m_old, may_be_empty, exp=jnp.exp):
    m_new = jnp.maximum(m_old, jnp.max(lg, axis=1, keepdims=True))
    m_use = jnp.where(m_new == NEG_INF, 0.0, m_new) if may_be_empty else m_new
    alpha = exp(m_old - m_use)
    p = exp(lg - _tile_lanes(m_use, lg.shape[1] // LANES)).astype(BF16)
    return m_new, alpha, p


def _state_update_chunked(logits_of, n_chunks, chunk, v_ext, m_ref, acc_ref, p_ref, alpha_ref, may_be_empty):
    for c in range(n_chunks):
        rows = slice(c * chunk, (c + 1) * chunk)
        m_new, alpha, p = _probs(logits_of(c), m_ref[rows, :], may_be_empty, jnp.exp2)
        alpha_ref[rows, :] = alpha
        p_ref[rows, :] = p
        m_ref[rows, :] = m_new
    acc_ref[...] = _tile_lanes(alpha_ref[...], 2) * acc_ref[...] + _dot(p_ref[...], v_ext)


def _state_update(lg, pv_of, m_ref, acc_ref, may_be_empty):
    m_new, alpha, p = _probs(lg, m_ref[...], may_be_empty)
    acc_ref[...] = _tile_lanes(alpha, 2) * acc_ref[...] + pv_of(p)
    m_ref[...] = m_new


def _state_update_parts(parts, m_ref, acc_ref, may_be_empty):
    local = []
    for lg, pv_of in parts:
        m_c = jnp.broadcast_to(jnp.max(lg, axis=1, keepdims=True), m_ref.shape)
        m_use = jnp.where(m_c == NEG_INF, 0.0, m_c) if may_be_empty else m_c
        p = jnp.exp(lg - _tile_lanes(m_use, lg.shape[1] // LANES)).astype(BF16)
        local.append((m_c, pv_of(p)))
    m_old = m_ref[...]
    m_new = m_old
    for m_c, _ in local:
        m_new = jnp.maximum(m_new, m_c)
    acc = _tile_lanes(jnp.exp(m_old - m_new), 2) * acc_ref[...]
    for m_c, pv in local:
        acc = acc + _tile_lanes(jnp.exp(m_c - m_new), 2) * pv
    acc_ref[...] = acc
    m_ref[...] = m_new


def _state_result(acc_ref):
    acc = acc_ref[...]
    return acc[:, :HEAD_DIM] / jnp.maximum(acc[:, HEAD_DIM:], 1e-30)


def _lane_cumsum(x):
    n = x.shape[-1]
    lane = lax.broadcasted_iota(jnp.int32, x.shape, x.ndim - 1)
    s = 1
    while s < n:
        x = x + jnp.where(lane >= s, pltpu.roll(x, s, axis=x.ndim - 1), 0.0)
        s *= 2
    return x


def _head_slope(head):
    if isinstance(head, int):
        return 2.0 ** -(head + 1)
    return lax.bitcast_convert_type((126 - head) << 23, F32)


def _rms_kernel(x_ref, g_ref, o_ref):
    x = x_ref[...]
    ms = jnp.mean(x * x, axis=-1, keepdims=True)
    o_ref[...] = (x * lax.rsqrt(ms + RMS_EPS) * g_ref[...]).astype(o_ref.dtype)


def _rmsnorm(x2d, g, out_dtype, tm):
    m, d = x2d.shape
    return pl.pallas_call(
        _rms_kernel,
        grid=(m // tm,),
        in_specs=[pl.BlockSpec((tm, d), lambda i: (i, 0)), pl.BlockSpec((1, d), lambda i: (0, 0))],
        out_specs=pl.BlockSpec((tm, d), lambda i: (i, 0)),
        out_shape=jax.ShapeDtypeStruct((m, d), out_dtype),
        compiler_params=_params(("parallel",)),
        name="rmsnorm",
    )(x2d, g.reshape(1, d))


def _proj_kernel(h_ref, w_ref, *o_refs):
    acc = _dot(h_ref[...], w_ref[...])
    for o_ref in o_refs:
        o_ref[...] = acc.astype(o_ref.dtype)


def _project(h, w, col0, n, out_dtypes, tm, name):
    m, d = h.shape
    j0 = col0 // PROJ_TN
    return pl.pallas_call(
        _proj_kernel,
        grid=(m // tm, n // PROJ_TN),
        in_specs=[pl.BlockSpec((tm, d), lambda i, j: (i, 0)), pl.BlockSpec((d, PROJ_TN), lambda i, j: (0, j0 + j))],
        out_specs=[pl.BlockSpec((tm, PROJ_TN), lambda i, j: (i, j)) for _ in out_dtypes],
        out_shape=[jax.ShapeDtypeStruct((m, n), dt) for dt in out_dtypes],
        compiler_params=_params(("parallel", "arbitrary")),
        name=name,
    )(h, w)


def _small_kernel(h_ref, w_ref, b_ref, o_ref, *, n_gate):
    z = _dot(h_ref[...], w_ref[...]) + b_ref[...]
    lane = lax.broadcasted_iota(jnp.int32, z.shape, 1)
    log_sig = jnp.minimum(z, 0.0) - jnp.log1p(jnp.exp(-jnp.abs(z)))
    o_ref[...] = jnp.where(lane < n_gate, _sigmoid(z), log_sig)


def _project_small(h, w, b, tm, n_gate):
    m, d = h.shape
    return pl.pallas_call(
        functools.partial(_small_kernel, n_gate=n_gate),
        grid=(m // tm,),
        in_specs=[pl.BlockSpec((tm, d), lambda i: (i, 0)), pl.BlockSpec((d, LANES), lambda i: (0, 0)),
                  pl.BlockSpec((1, LANES), lambda i: (0, 0))],
        out_specs=pl.BlockSpec((tm, LANES), lambda i: (i, 0)),
        out_shape=jax.ShapeDtypeStruct((m, LANES), F32),
        compiler_params=_params(("parallel",)),
        name="proj_gates_logf",
    )(h, w, b)


PROJ_GROUPS = (("q_n", NSA_WIDTH), ("kv4", 4 * KV_WIDTH), ("kvwin", 2 * KV_WIDTH), ("z_n", NSA_WIDTH),
               ("q_f", FOX_WIDTH), ("kv_f", 2 * FOX_WIDTH), ("z_f", FOX_WIDTH))
PROJ_OUT_DTYPES = {"q_n": (BF16,), "kv4": (F32, BF16), "kvwin": (F32, BF16), "z_n": (F32,),
                   "q_f": (BF16,), "kv_f": (F32, BF16), "z_f": (F32,)}


def _regroup_kernel(wt_ref, o_ref):
    o_ref[...] = wt_ref[...].T.astype(o_ref.dtype)


def _regroup_columns(w_t, spans):
    cols, rows = w_t.shape
    starts, col = [], 0
    for a, b in spans:
        assert col % PROJ_TN == 0 and (b - a) % PROJ_TN == 0 and a % SUBLANES == 0
        starts.append((col // PROJ_TN, a - col))
        col += b - a

    n_tiles = col // PROJ_TN

    def source_row(j):
        units = j * (PROJ_TN // SUBLANES) + starts[0][1] // SUBLANES
        for (first_tile, s), (_, s_prev) in zip(starts[1:], starts[:-1]):
            units = units + ((j + n_tiles - first_tile) // n_tiles) * ((s - s_prev) // SUBLANES)
        return units * SUBLANES

    return pl.pallas_call(
        _regroup_kernel,
        grid=(col // PROJ_TN,),
        in_specs=[pl.BlockSpec((pl.Element(PROJ_TN), pl.Element(rows)), lambda j: (source_row(j), 0))],
        out_specs=pl.BlockSpec((rows, PROJ_TN), lambda j: (0, j)),
        out_shape=jax.ShapeDtypeStruct((rows, col), BF16),
        compiler_params=_params(("parallel",)),
        name="regroup_weights",
    )(w_t)


def _split_weights(w_in, b_gate, b_forget):
    cuts = [NSA_WIDTH, 6 * KV_WIDTH, 3 * NSA_HEADS, NSA_WIDTH, FOX_WIDTH, FOX_WIDTH, FOX_WIDTH, FOX_HEADS, FOX_WIDTH]
    offs = [0]
    for c in cuts:
        offs.append(offs[-1] + c)
    w_t = jnp.swapaxes(w_in, 0, 1)
    main = _regroup_columns(w_t, ((offs[0], offs[2]), (offs[3], offs[7]), (offs[8], offs[9])))
    n_small = 3 * NSA_HEADS + FOX_HEADS
    w_small = jnp.concatenate([w_t[offs[2]:offs[3]], w_t[offs[7]:offs[8]]], axis=0).T
    w_small = jnp.pad(w_small, ((0, 0), (0, LANES - n_small))).astype(BF16)
    b_small = jnp.pad(jnp.concatenate([b_gate, b_forget]), (0, LANES - n_small)).reshape(1, LANES).astype(F32)
    return {"main": main, "small": w_small, "b_small": b_small}


def _proj_kv4_kernel(h_ref, w_ref, chan_ref, cmp_ref, bf_ref):
    acc = _dot(h_ref[...], w_ref[...])
    for c in range(NSA_CHANNELS):
        chan_ref[:, c, :] = acc[:, c * HEAD_DIM:(c + 1) * HEAD_DIM]
    cmp_ref[...] = acc[:, :2 * KV_WIDTH]
    bf_ref[...] = acc.astype(bf_ref.dtype)


def _project_kv4(h, w, col0, tm):
    m, d = h.shape
    n = 4 * KV_WIDTH
    assert col0 % n == 0
    return pl.pallas_call(
        _proj_kv4_kernel,
        grid=(m // tm,),
        in_specs=[pl.BlockSpec((tm, d), lambda i: (i, 0)), pl.BlockSpec((d, n), lambda i: (0, col0 // n))],
        out_specs=[pl.BlockSpec((tm, NSA_CHANNELS, HEAD_DIM), lambda i: (i, 0, 0)),
                   pl.BlockSpec((tm, 2 * KV_WIDTH), lambda i: (i, 0)),
                   pl.BlockSpec((tm, n), lambda i: (i, 0))],
        out_shape=[jax.ShapeDtypeStruct((m, NSA_CHANNELS, HEAD_DIM), F32),
                   jax.ShapeDtypeStruct((m, 2 * KV_WIDTH), F32),
                   jax.ShapeDtypeStruct((m, n), BF16)],
        compiler_params=_params(("parallel",)),
        name="proj_kv4_state",
    )(h, w)


def _project_all(x2d, norm_in, wts, tm, kv4_state_order=False):
    h = _rmsnorm(x2d, norm_in, BF16, min(tm, 512))
    out = {}
    col0 = 0
    for name, width in PROJ_GROUPS:
        if name == "kv4" and kv4_state_order:
            out["kv4_state"], out["kv4_cmp"], out["kv4_bf"] = _project_kv4(h, wts["main"], col0, tm)
            col0 += width
            continue
        res = _project(h, wts["main"], col0, width, PROJ_OUT_DTYPES[name], tm, "proj_" + name)
        out[name] = res[0]
        if len(res) > 1:
            out[name + "_bf"] = res[1]
        col0 += width
    out["small"] = _project_small(h, wts["small"], wts["b_small"], min(tm, 512), 3 * NSA_HEADS)
    return out


def _merge_kernel(x_ref, mn_ref, mf_ref, wn_ref, wf_ref, g_ref, o_ref):
    y = x_ref[...] + _dot(mn_ref[...], wn_ref[...]) + _dot(mf_ref[...], wf_ref[...])
    ms = jnp.mean(y * y, axis=-1, keepdims=True)
    o_ref[...] = y * lax.rsqrt(ms + RMS_EPS) * g_ref[...]


def _merge(x2d, mix_n, mix_f, w_out_bf, norm_final, tm):
    m, d = x2d.shape
    return pl.pallas_call(
        _merge_kernel,
        grid=(m // tm,),
        in_specs=[pl.BlockSpec((tm, d), lambda i: (i, 0)),
                  pl.BlockSpec((tm, NSA_WIDTH), lambda i: (i, 0)),
                  pl.BlockSpec((tm, FOX_WIDTH), lambda i: (i, 0)),
                  pl.BlockSpec((NSA_WIDTH, d), lambda i: (0, 0)),
                  pl.BlockSpec((FOX_WIDTH, d), lambda i: (NSA_WIDTH // FOX_WIDTH, 0)),
                  pl.BlockSpec((1, d), lambda i: (0, 0))],
        out_specs=pl.BlockSpec((tm, d), lambda i: (i, 0)),
        out_shape=jax.ShapeDtypeStruct((m, d), F32),
        compiler_params=_params(("parallel",)),
        name="merge_out_proj",
    )(x2d, mix_n, mix_f, w_out_bf, w_out_bf, norm_final.reshape(1, d))


def _cumsum_kernel(x_ref, o_ref):
    o_ref[...] = _lane_cumsum(x_ref[...])


def _cumsum_rows(x):
    return pl.pallas_call(
        _cumsum_kernel,
        out_shape=jax.ShapeDtypeStruct(x.shape, F32),
        name="forget_cumsum",
    )(x)


def _compress_rows(x_ref, pe_ref, kind, w, xs_ref, nb, starts, stride):
    half = nb // 2
    for c in range(CMP_BLOCK):
        pe_c = pe_ref[kind, c:c + 1, :]
        for par in range(2):
            xc = x_ref[pl.ds(starts[par] + c, half, stride=stride), :]
            xs_ref[par * half:(par + 1) * half, c * HEAD_DIM:(c + 1) * HEAD_DIM] = (xc + pe_c).astype(BF16)
    return _dot(xs_ref[...], w)


def _compress_kernel(x_ref, pe_ref, w_ref, o_ref, xs_ref, *, nb):
    out = _compress_rows(x_ref, pe_ref, 0, w_ref[0], xs_ref, nb, (0, CMP_BLOCK), 2 * CMP_BLOCK)
    o_ref[0, 0, 0] = out.astype(o_ref.dtype)


def _compress_prompt(kv4, cmp_pe, cmp_w_bf, batch, seq):
    nb = seq // CMP_BLOCK
    return pl.pallas_call(
        functools.partial(_compress_kernel, nb=nb),
        grid=(batch, 2, NSA_KV_HEADS),
        in_specs=[pl.BlockSpec((seq, HEAD_DIM), lambda b, k, h: (b, k * NSA_KV_HEADS + h)),
                  pl.BlockSpec((1, CMP_BLOCK, HEAD_DIM), lambda b, k, h: (k, 0, 0)),
                  pl.BlockSpec((1, CMP_BLOCK * HEAD_DIM, HEAD_DIM), lambda b, k, h: (k, 0, 0))],
        out_specs=pl.BlockSpec((1, 1, 1, nb, HEAD_DIM), lambda b, k, h: (b, k, h, 0, 0)),
        out_shape=jax.ShapeDtypeStruct((batch, 2, NSA_KV_HEADS, nb, HEAD_DIM), BF16),
        scratch_shapes=[pltpu.VMEM((nb, CMP_BLOCK * HEAD_DIM), BF16)],
        compiler_params=_params(("parallel", "parallel", "arbitrary")),
        name="compress_prompt",
    )(kv4, cmp_pe, cmp_w_bf)


def _select_blocks(score_t, cur, n_keep):
    n_blk = score_t.shape[0]
    j = lax.broadcasted_iota(jnp.int32, score_t.shape, 0)
    forced = (j == 0) | (j == cur) | (j == cur - 1)
    s = jnp.where(j <= cur, jnp.where(forced, FORCED_SCORE, score_t), -1.0)
    n_grp = n_blk // SUBLANES
    grp = [s[r * SUBLANES:(r + 1) * SUBLANES] for r in range(n_grp)]
    cnt = [jnp.zeros(grp[0].shape, F32) for _ in range(n_grp)]
    sub = lax.broadcasted_iota(jnp.int32, grp[0].shape, 0)
    for jj in range(n_blk):
        row = s[jj:jj + 1, :]
        r_j, off = divmod(jj, SUBLANES)
        for r in range(n_grp):
            if r < r_j:
                hit = row > grp[r]
            elif r > r_j:
                hit = row >= grp[r]
            else:
                hit = (row > grp[r]) | ((row == grp[r]) & (sub > off))
            cnt[r] = cnt[r] + jnp.where(hit, 1.0, 0.0)
    return jnp.where((jnp.concatenate(cnt, axis=0) < n_keep) & (j <= cur), 1.0, 0.0)


def _nsa_prompt_kernel(q_ref, kc_ref, vc_ref, ks_ref, vs_ref, kw_ref, vw_ref, g_ref, z_ref, e_ref, o_ref,
                       m_ref, acc_ref, alpha_ref, ps_ref, pw_ref, mix_ref, bias_ref, *, tq, tk, tw, ck, seq):
    qi = pl.program_id(1)
    s0 = qi * tq
    n_cmp = seq // CMP_BLOCK
    n_sel = seq // SEL_BLOCK
    row_t = s0 + lax.broadcasted_iota(jnp.int32, (tq, 1), 0)
    kv_heads = range(NSA_KV_HEADS)
    heads = lambda hkv: [hkv * NSA_GROUP + g for g in range(NSA_GROUP)]
    slopes = lambda hkv: [_head_slope(h) for h in heads(hkv)]
    cols = lambda hkv: slice(hkv * HEAD_DIM, (hkv + 1) * HEAD_DIM)

    def q_rows(hkv):
        return jnp.concatenate([q_ref[:, h * HEAD_DIM:(h + 1) * HEAD_DIM] for h in heads(hkv)], axis=0)

    def gate_col(hkv, br):
        return jnp.concatenate([g_ref[:, 3 * h + br:3 * h + br + 1] for h in heads(hkv)], axis=0)

    sels = []
    for hkv in kv_heads:
        sc = _dot_nt(q_rows(hkv), kc_ref[0, 0, hkv]) * SCALE
        lane = lax.broadcasted_iota(jnp.int32, (tq, n_cmp), 1)
        blk = jnp.where(lane < n_cmp // 2, 2 * lane, 2 * lane - (n_cmp - 1))
        dist_c = (row_t - ((blk + 1) * CMP_BLOCK - 1)).astype(F32)
        valid_c = dist_c >= 0
        score = jnp.zeros((tq, n_cmp), F32)
        probs = []
        for g, slope in enumerate(slopes(hkv)):
            lg = jnp.where(valid_c, sc[g * tq:(g + 1) * tq] - slope * dist_c, NEG_INF)
            mx = jnp.max(lg, axis=1, keepdims=True)
            mx = jnp.where(mx == NEG_INF, 0.0, mx)
            p = jnp.exp(lg - mx)
            p = p / jnp.maximum(jnp.sum(p, axis=1, keepdims=True), 1e-30)
            score = score + p
            probs.append(p)
        o_c = _dot(jnp.concatenate(probs, axis=0).astype(BF16), vc_ref[0, 0, hkv])
        mix_ref[hkv] = gate_col(hkv, 0) * o_c

        score_t = score.T
        score_t = score_t[:n_sel] + score_t[n_sel:]
        cur = (s0 + lax.broadcasted_iota(jnp.int32, (n_sel, tq), 1)) // SEL_BLOCK
        sel_t = _select_blocks(score_t, cur, SEL_TOPK)
        sels.append(jnp.concatenate([sel_t, jnp.zeros((LANES - n_sel, tq), F32)], axis=0).T.astype(BF16))

    def bias_body(c, carry):
        c0 = pl.multiple_of(c * ck, ck)
        causal = c0 + lax.broadcasted_iota(jnp.int32, (tq, ck), 1) <= row_t
        for hkv in kv_heads:
            expanded = _dot(sels[hkv], e_ref[:, pl.ds(c0, ck)])
            bias_ref[hkv, :, pl.ds(c0, ck)] = jnp.where((expanded > 0.5) & causal, 0.0, NEG_INF)
        return carry

    lax.fori_loop(0, (s0 + tq + ck - 1) // ck, bias_body, 0)

    def stream(hkv, k, bias, key_rel, v, p_ref, may_be_empty):
        s = _dot_nt(q_rows(hkv), k)
        sl = slopes(hkv)
        logits_of = lambda g: s[g * tq:(g + 1) * tq] * (SCALE * LOG2E) + (bias + (sl[g] * LOG2E) * key_rel)
        _state_update_chunked(logits_of, NSA_GROUP, tq, _with_ones(v), m_ref.at[hkv], acc_ref.at[hkv],
                              p_ref.at[hkv], alpha_ref.at[hkv], may_be_empty)

    _state_init(m_ref, acc_ref)

    def slc_body(i, carry):
        k0 = pl.multiple_of(i * tk, tk)
        key_rel = (k0 - s0 + lax.broadcasted_iota(jnp.int32, (1, tk), 1)).astype(F32)
        for hkv in kv_heads:
            stream(hkv, ks_ref[pl.ds(k0, tk), cols(hkv)], bias_ref[hkv, :, pl.ds(k0, tk)], key_rel,
                   vs_ref[pl.ds(k0, tk), cols(hkv)], ps_ref, False)
        return carry

    lax.fori_loop(0, (s0 + tq + tk - 1) // tk, slc_body, 0)
    for hkv in kv_heads:
        mix_ref[hkv] = mix_ref[hkv] + gate_col(hkv, 1) * _state_result(acc_ref.at[hkv])

    _state_init(m_ref, acc_ref)

    def win_body(w, carry):
        k0 = pl.multiple_of(s0 - WINDOW + w * tw, tw)
        disti = row_t - (k0 + lax.broadcasted_iota(jnp.int32, (tq, tw), 1))
        bias = jnp.where((disti >= 0) & (disti < WINDOW), 0.0, NEG_INF)
        key_rel = (k0 - s0 + lax.broadcasted_iota(jnp.int32, (1, tw), 1)).astype(F32)
        for hkv in kv_heads:
            stream(hkv, kw_ref[pl.ds(k0, tw), cols(hkv)], bias, key_rel,
                   vw_ref[pl.ds(k0, tw), cols(hkv)], pw_ref, True)
        return carry

    n_win_tiles = (WINDOW + tq) // tw
    lax.fori_loop(jnp.maximum(0, (WINDOW - s0) // tw), n_win_tiles, win_body, 0)

    for hkv in kv_heads:
        mix = mix_ref[hkv] + gate_col(hkv, 2) * _state_result(acc_ref.at[hkv])
        for g, h in enumerate(heads(hkv)):
            z = z_ref[:, h * HEAD_DIM:(h + 1) * HEAD_DIM]
            o_ref[:, h * HEAD_DIM:(h + 1) * HEAD_DIM] = (mix[g * tq:(g + 1) * tq] * _silu(z)).astype(o_ref.dtype)


def _nsa_prompt(proj, kcv, expand, batch, seq):
    tq, tk, tw, ck = 256, 256, 256, 512
    assert seq % ck == 0 and WINDOW % tw == 0 and tq % tw == 0 and seq // SEL_BLOCK <= LANES
    nq = seq // tq
    rows = NSA_GROUP * tq
    kernel = functools.partial(_nsa_prompt_kernel, tq=tq, tk=tk, tw=tw, ck=ck, seq=seq)
    cmp_spec = lambda kind: pl.BlockSpec((1, 1, NSA_KV_HEADS, seq // CMP_BLOCK, HEAD_DIM),
                                         lambda b, i: (b, kind, 0, 0, 0))
    return pl.pallas_call(
        kernel,
        grid=(batch, nq),
        in_specs=[pl.BlockSpec((tq, NSA_WIDTH), lambda b, i: (b * nq + i, 0)),
                  cmp_spec(0), cmp_spec(1),
                  pl.BlockSpec((seq, KV_WIDTH), lambda b, i: (b, 2)),
                  pl.BlockSpec((seq, KV_WIDTH), lambda b, i: (b, 3)),
                  pl.BlockSpec((seq, KV_WIDTH), lambda b, i: (b, 0)),
                  pl.BlockSpec((seq, KV_WIDTH), lambda b, i: (b, 1)),
                  pl.BlockSpec((tq, LANES), lambda b, i: (b * nq + i, 0)),
                  pl.BlockSpec((tq, NSA_WIDTH), lambda b, i: (b * nq + i, 0)),
                  pl.BlockSpec((LANES, seq), lambda b, i: (0, 0))],
        out_specs=pl.BlockSpec((tq, NSA_WIDTH), lambda b, i: (b * nq + i, 0)),
        out_shape=jax.ShapeDtypeStruct((batch * seq, NSA_WIDTH), BF16),
        scratch_shapes=[pltpu.VMEM((NSA_KV_HEADS, rows, LANES), F32),
                        pltpu.VMEM((NSA_KV_HEADS, rows, 2 * HEAD_DIM), F32),
                        pltpu.VMEM((NSA_KV_HEADS, rows, LANES), F32),
                        pltpu.VMEM((NSA_KV_HEADS, rows, tk), BF16),
                        pltpu.VMEM((NSA_KV_HEADS, rows, tw), BF16),
                        pltpu.VMEM((NSA_KV_HEADS, rows, HEAD_DIM), F32),
                        pltpu.VMEM((NSA_KV_HEADS, tq, seq), F32)],
        compiler_params=_params(("parallel", "arbitrary")),
        name="nsa_prompt",
    )(proj["q_n"], kcv, kcv, proj["kv4_bf"], proj["kv4_bf"], proj["kvwin_bf"], proj["kvwin_bf"],
      proj["small"], proj["z_n"], expand)


def _fox_prompt_kernel(q_ref, k_ref, v_ref, d_ref, z_ref, o_ref, m_ref, acc_ref, alpha_ref, p_ref, *,
                       t, chunk, n_heads):
    qi = pl.program_id(2)
    _state_init(m_ref, acc_ref)

    def tile(i, causal):
        k0 = pl.multiple_of(i * t, t)
        for hh in range(n_heads):
            cols = slice(hh * HEAD_DIM, (hh + 1) * HEAD_DIM)
            s = _dot_nt(q_ref[:, cols], k_ref[pl.ds(k0, t), cols])
            d_row = d_ref[hh, :, pl.ds(k0, t)] * LOG2E

            def logits_of(c, s=s, d_row=d_row):
                lg = s[c * chunk:(c + 1) * chunk] * (SCALE * LOG2E) - d_row
                if causal:
                    row = c * chunk + lax.broadcasted_iota(jnp.int32, (chunk, t), 0)
                    col = lax.broadcasted_iota(jnp.int32, (chunk, t), 1)
                    lg = jnp.where(col <= row, lg, NEG_INF)
                return lg

            _state_update_chunked(logits_of, t // chunk, chunk, _with_ones(v_ref[pl.ds(k0, t), cols]),
                                  m_ref.at[hh], acc_ref.at[hh], p_ref.at[hh], alpha_ref.at[hh], False)

    def body(i, carry):
        tile(i, False)
        return carry

    lax.fori_loop(0, qi, body, 0)
    tile(qi, True)
    for hh in range(n_heads):
        cols = slice(hh * HEAD_DIM, (hh + 1) * HEAD_DIM)
        o_ref[:, cols] = (_state_result(acc_ref.at[hh]) * _silu(z_ref[:, cols])).astype(o_ref.dtype)


def _fox_prompt(proj, d_rows, batch, seq):
    t, chunk, n_heads = 512, 128, 4
    nq = seq // t
    width = n_heads * HEAD_DIM
    n_hg = FOX_HEADS // n_heads
    return pl.pallas_call(
        functools.partial(_fox_prompt_kernel, t=t, chunk=chunk, n_heads=n_heads),
        grid=(batch, n_hg, nq),
        in_specs=[pl.BlockSpec((t, width), lambda b, h, i: (b * nq + i, h)),
                  pl.BlockSpec((seq, width), lambda b, h, i: (b, h)),
                  pl.BlockSpec((seq, width), lambda b, h, i: (b, n_hg + h)),
                  pl.BlockSpec((n_heads, 1, seq), lambda b, h, i: (b * n_hg + h, 0, 0)),
                  pl.BlockSpec((t, width), lambda b, h, i: (b * nq + i, h))],
        out_specs=pl.BlockSpec((t, width), lambda b, h, i: (b * nq + i, h)),
        out_shape=jax.ShapeDtypeStruct((batch * seq, FOX_WIDTH), BF16),
        scratch_shapes=[pltpu.VMEM((n_heads, t, LANES), F32), pltpu.VMEM((n_heads, t, 2 * HEAD_DIM), F32),
                        pltpu.VMEM((n_heads, t, LANES), F32), pltpu.VMEM((n_heads, t, t), BF16)],
        compiler_params=_params(("parallel", "parallel", "arbitrary")),
        name="fox_prompt",
    )(proj["q_f"], proj["kv_f_bf"], proj["kv_f_bf"], d_rows, proj["z_f"])


def _prompt_layer(x, norm_in, wts, cmp_pe, cmp_w_bf, w_out_bf, norm_final):
    batch, seq, d = x.shape
    x2d = x.reshape(batch * seq, d)
    proj = _project_all(x2d, norm_in, wts, 1024, kv4_state_order=True)
    n_gate = 3 * NSA_HEADS
    logf = proj["small"][:, n_gate:n_gate + FOX_HEADS].reshape(batch, seq, FOX_HEADS)
    d_rows = _cumsum_rows(jnp.swapaxes(logf, 1, 2).reshape(batch * FOX_HEADS, seq))
    kcv = _compress_prompt(proj["kv4_cmp"], cmp_pe, cmp_w_bf, batch, seq)
    blk_of_key = jnp.arange(seq, dtype=jnp.int32) // SEL_BLOCK
    expand = (jnp.arange(LANES, dtype=jnp.int32)[:, None] == blk_of_key[None, :]).astype(BF16)
    mix_n = _nsa_prompt(proj, kcv, expand, batch, seq)
    mix_f = _fox_prompt(proj, d_rows.reshape(batch * FOX_HEADS, 1, seq), batch, seq)
    y = _merge(x2d, mix_n, mix_f, w_out_bf, norm_final, 512).reshape(batch, seq, d)
    w_keep = min(WINDOW, seq)
    state = (proj["kv4_state"].reshape(batch, seq, 4, NSA_KV_HEADS, HEAD_DIM),
             proj["kv_f"].reshape(batch, seq, 2, FOX_HEADS, HEAD_DIM),
             logf,
             proj["kvwin"].reshape(batch, seq, 2 * KV_WIDTH)[:, seq - w_keep:].reshape(
                 batch, w_keep, 2, NSA_KV_HEADS, HEAD_DIM))
    return y, state


HALF_CHANNELS = NSA_CHANNELS // 2


def _nsa_page_specs(pg, half):
    def spec(u):
        return pl.BlockSpec((1, 1, PAGE_SIZE, 1, HALF_CHANNELS, HEAD_DIM),
                            lambda b, j, pt: (0, pt[b, j * pg + u], 0, half, 0, 0))
    return [spec(u) for u in range(pg)]


def _channel(page_ref, ch, start=0, n=PAGE_SIZE):
    rows = page_ref.reshape(PAGE_SIZE * HALF_CHANNELS, HEAD_DIM)
    return rows[pl.ds(start * HALF_CHANNELS + ch, n, stride=HALF_CHANNELS), :]


def _nsa_cmp_kernel(pt_ref, *refs, pg, n_pages):
    page_refs = refs[:pg]
    q_ref, pe_ref, w_ref, oc_ref, sel_ref, stage_ref, xs_ref = refs[pg:]
    jg = pl.program_id(1)
    past = n_pages * PAGE_SIZE
    nb = past // CMP_BLOCK
    half = nb // 2
    blocks_per_page = PAGE_SIZE // CMP_BLOCK
    for u in range(pg):
        for bl in range(blocks_per_page):
            m = (jg * pg + u) * (blocks_per_page // 2) + bl // 2
            dst = pl.multiple_of(((bl % 2) * half + m) * STAGE_PITCH, SUBLANES)
            for ch in range(2 * NSA_KV_HEADS):
                stage_ref[ch, pl.ds(dst, CMP_BLOCK), :] = _channel(page_refs[u], ch, bl * CMP_BLOCK, CMP_BLOCK)

    @pl.when(jg == pl.num_programs(1) - 1)
    def _():
        cmp = [[_compress_rows(stage_ref.at[kind * NSA_KV_HEADS + h], pe_ref, kind, w_ref[kind], xs_ref, nb,
                               (0, half * STAGE_PITCH), STAGE_PITCH).astype(BF16)
                for h in range(NSA_KV_HEADS)] for kind in range(2)]
        rows = NSA_GROUP * T_PAD
        for hkv in range(NSA_KV_HEADS):
            q = q_ref[0, hkv * rows:(hkv + 1) * rows, :].astype(BF16)
            kc, vc = cmp[0][hkv], cmp[1][hkv]
            s = _dot_nt(q, kc) * SCALE
            lane = lax.broadcasted_iota(jnp.int32, (rows, nb), 1)
            blk = jnp.where(lane < half, 2 * lane, 2 * lane - (nb - 1))
            row = lax.broadcasted_iota(jnp.int32, (rows, nb), 0)
            slope = _head_slope(hkv * NSA_GROUP + row // T_PAD)
            dist = (past + row % T_PAD - ((blk + 1) * CMP_BLOCK - 1)).astype(F32)
            lg = jnp.where(dist >= 0, s - slope * dist, NEG_INF)
            mx = jnp.max(lg, axis=1, keepdims=True)
            mx = jnp.where(mx == NEG_INF, 0.0, mx)
            p = jnp.exp(lg - mx)
            p = p / jnp.maximum(jnp.sum(p, axis=1, keepdims=True), 1e-30)
            oc_ref[0, hkv * rows:(hkv + 1) * rows, :] = _dot(p.astype(BF16), vc)
            score = p[0:T_PAD]
            for g in range(1, NSA_GROUP):
                score = score + p[g * T_PAD:(g + 1) * T_PAD]
            score = score[:, :half] + score[:, half:]
            blk_s = lax.broadcasted_iota(jnp.int32, (T_PAD, half), 1).astype(F32)
            forced = (blk_s == 0) | (blk_s == half - 1)
            s_left = jnp.where(forced, NEG_INF, score)
            picked = jnp.zeros((T_PAD, half), F32)
            for _ in range(SEL_TOPK - 3):
                best = jnp.max(s_left, axis=1, keepdims=True)
                first = jnp.min(jnp.where(s_left == best, blk_s, float(half)), axis=1, keepdims=True)
                hit = blk_s == first
                picked = jnp.where(hit, 1.0, picked)
                s_left = jnp.where(hit, NEG_INF, s_left)
            sel_ref[0, hkv * T_PAD:(hkv + 1) * T_PAD, :] = jnp.where(forced, 1.0, picked)


def _nsa_cmp_sample(cache, page_table, q_rows, cmp_pe, cmp_w_bf, pg):
    n_seq, n_pages = page_table.shape
    past = n_pages * PAGE_SIZE
    nb = past // CMP_BLOCK
    assert n_pages % pg == 0 and past // SEL_BLOCK >= SEL_TOPK
    rows = NSA_HEADS * T_PAD
    grid_spec = pltpu.PrefetchScalarGridSpec(
        num_scalar_prefetch=1,
        grid=(n_seq, n_pages // pg),
        in_specs=_nsa_page_specs(pg, 0) + [
            pl.BlockSpec((1, rows, HEAD_DIM), lambda b, j, pt: (b, 0, 0)),
            pl.BlockSpec((2, CMP_BLOCK, HEAD_DIM), lambda b, j, pt: (0, 0, 0)),
            pl.BlockSpec((2, CMP_BLOCK * HEAD_DIM, HEAD_DIM), lambda b, j, pt: (0, 0, 0))],
        out_specs=[pl.BlockSpec((1, rows, HEAD_DIM), lambda b, j, pt: (b, 0, 0)),
                   pl.BlockSpec((1, NSA_KV_HEADS * T_PAD, past // SEL_BLOCK), lambda b, j, pt: (b, 0, 0))],
        scratch_shapes=[pltpu.VMEM((2 * NSA_KV_HEADS, nb * STAGE_PITCH, HEAD_DIM), F32),
                        pltpu.VMEM((nb, CMP_BLOCK * HEAD_DIM), BF16)],
    )
    return pl.pallas_call(
        functools.partial(_nsa_cmp_kernel, pg=pg, n_pages=n_pages),
        grid_spec=grid_spec,
        out_shape=[jax.ShapeDtypeStruct((n_seq, rows, HEAD_DIM), F32),
                   jax.ShapeDtypeStruct((n_seq, NSA_KV_HEADS * T_PAD, past // SEL_BLOCK), F32)],
        compiler_params=_params(("parallel", "arbitrary")),
        name="nsa_sample_cmp",
    )(page_table, *([cache] * pg), q_rows, cmp_pe, cmp_w_bf)


def _nsa_slc_kernel(pt_ref, *refs, pg, part_pages, n_pages, w_buf, n_new):
    page_refs = refs[:pg]
    (q_ref, sel_ref, e_ref, new_ref, win_ref, oc_ref, g_ref, z_ref, o_ref, tab_ref, m_ref, acc_ref) = refs[pg:]
    jg = pl.program_id(1)
    past = n_pages * PAGE_SIZE
    hr = n_new * NSA_GROUP
    rows = NSA_KV_HEADS * hr
    row = lax.broadcasted_iota(jnp.int32, (rows, 1), 0)
    t_q = (row % hr) // NSA_GROUP
    slope = _head_slope((row // hr) * NSA_GROUP + row % NSA_GROUP)
    q = [q_ref[0, h * hr:(h + 1) * hr, :].astype(BF16) for h in range(NSA_KV_HEADS)]

    def scores(k_of):
        return jnp.concatenate([_dot_nt(q[h], k_of(h)) for h in range(NSA_KV_HEADS)], axis=0) * SCALE

    def pv(v_of):
        return lambda p: jnp.concatenate(
            [_dot(p[h * hr:(h + 1) * hr], _with_ones(v_of(h))) for h in range(NSA_KV_HEADS)], axis=0)


    @pl.when(jg == 0)
    def _():
        expanded = _dot(sel_ref[0].astype(BF16), e_ref[...])
        key = lax.broadcasted_iota(jnp.int32, (rows, past), 1)
        tab_ref[...] = jnp.where(expanded > 0.5, 0.0, NEG_INF) + slope * (key - past).astype(F32)
        _state_init(m_ref, acc_ref)

    def page_stream(ch, pages):
        return jnp.concatenate([_channel(page_refs[u], ch).astype(BF16) for u in pages], axis=0)

    parts = []
    for first in range(0, pg, part_pages):
        pages = range(first, first + part_pages)
        c0 = pl.multiple_of((jg * pg + first) * PAGE_SIZE, part_pages * PAGE_SIZE)
        lg = (scores(lambda h: page_stream(h, pages))
              + tab_ref[:, pl.ds(c0, part_pages * PAGE_SIZE)])
        parts.append((lg, pv(lambda h, pages=pages: page_stream(NSA_KV_HEADS + h, pages))))
    _state_update_parts(parts, m_ref, acc_ref, True)

    @pl.when(jg == pl.num_programs(1) - 1)
    def _():
        new = lambda ch: new_ref[0, ch].astype(BF16)
        j_new = lax.broadcasted_iota(jnp.int32, (rows, PAGE_SIZE), 1)
        bias_new = jnp.where((j_new <= t_q) & (j_new < n_new), slope * j_new.astype(F32), NEG_INF)
        _state_update(scores(lambda h: new(h)) + bias_new, pv(lambda h: new(NSA_KV_HEADS + h)),
                      m_ref, acc_ref, False)
        o_s = _state_result(acc_ref)

        _state_init(m_ref, acc_ref)
        i_w = lax.broadcasted_iota(jnp.int32, (rows, w_buf), 1)
        dist_w = t_q + w_buf - i_w
        valid_w = (dist_w >= 0) & (dist_w < WINDOW) & (past - w_buf + i_w >= 0)
        bias_w = jnp.where(valid_w, slope * (i_w - w_buf).astype(F32), NEG_INF)
        win = lambda ch: win_ref[0, ch].astype(BF16)
        _state_update(scores(lambda h: win(h)) + bias_w, pv(lambda h: win(NSA_KV_HEADS + h)),
                      m_ref, acc_ref, True)
        _state_update(scores(lambda h: new(2 * NSA_KV_HEADS + h)) + bias_new,
                      pv(lambda h: new(3 * NSA_KV_HEADS + h)), m_ref, acc_ref, True)
        o_w = _state_result(acc_ref)

        gates = g_ref[0]
        mix = gates[:, 0:1] * oc_ref[0] + gates[:, 1:2] * o_s + gates[:, 2:3] * o_w
        o_ref[0] = mix * _silu(z_ref[0])


def _nsa_slc_sample(cache, page_table, q_rows, sel_rows, expand, new_rows, win_state, o_c_rows, gates_rows, z_rows,
                    pg, n_new):
    n_seq, n_pages = page_table.shape
    past = n_pages * PAGE_SIZE
    w_buf = win_state.shape[2]
    rows = n_new * NSA_HEADS
    per_seq = lambda *shape: pl.BlockSpec((1,) + shape, lambda b, j, pt: (b,) + (0,) * len(shape))
    grid_spec = pltpu.PrefetchScalarGridSpec(
        num_scalar_prefetch=1,
        grid=(n_seq, n_pages // pg),
        in_specs=_nsa_page_specs(pg, 1) + [
            per_seq(rows, HEAD_DIM),
            per_seq(rows, past // SEL_BLOCK),
            pl.BlockSpec((past // SEL_BLOCK, past), lambda b, j, pt: (0, 0)),
            per_seq(4 * NSA_KV_HEADS, PAGE_SIZE, HEAD_DIM),
            per_seq(2 * NSA_KV_HEADS, w_buf, HEAD_DIM),
            per_seq(rows, HEAD_DIM),
            per_seq(rows, 3),
            per_seq(rows, HEAD_DIM)],
        out_specs=per_seq(rows, HEAD_DIM),
        scratch_shapes=[pltpu.VMEM((rows, past), F32), pltpu.VMEM((rows, LANES), F32),
                        pltpu.VMEM((rows, 2 * HEAD_DIM), F32)],
    )
    return pl.pallas_call(
        functools.partial(_nsa_slc_kernel, pg=pg, part_pages=2, n_pages=n_pages, w_buf=w_buf, n_new=n_new),
        grid_spec=grid_spec,
        out_shape=jax.ShapeDtypeStruct((n_seq, rows, HEAD_DIM), F32),
        compiler_params=_params(("parallel", "arbitrary")),
        name="nsa_sample_slc_win",
    )(page_table, *([cache] * pg), q_rows, sel_rows, expand, new_rows, win_state, o_c_rows, gates_rows, z_rows)


def _page_forget_scan(x):
    n = x.shape[1]
    lane = lax.broadcasted_iota(jnp.int32, x.shape, 1)
    s = FOX_HEADS
    while s < n:
        x = x + jnp.where(lane >= s, pltpu.roll(x, s, axis=1), 0.0)
        s *= 2
    total = jnp.where(lane >= n - FOX_HEADS, x, 0.0)
    s = FOX_HEADS
    while s < n:
        total = total + pltpu.roll(total, n - s, axis=1)
        s *= 2
    return x, total


def _fox_sample_kernel(pt_ref, *refs, pg, part_pages, n_new):
    page_refs = refs[:pg]
    (q_ref, lc_ref, kn_ref, vn_ref, ln_ref, z_ref, o_ref, m_ref, acc_ref, carry_ref, lg_ref) = refs[pg:]
    b = pl.program_id(0)
    jg = pl.program_id(1)
    rows = n_new * FOX_HEADS
    keys = PAGE_SIZE * FOX_HEADS

    @pl.when(jg == 0)
    def _():
        _state_init(m_ref, acc_ref)
        carry_ref[...] = jnp.zeros(carry_ref.shape, F32)

    q = q_ref[0].astype(BF16)
    own = (lax.broadcasted_iota(jnp.int32, (rows, keys), 1) % FOX_HEADS
           == lax.broadcasted_iota(jnp.int32, (rows, keys), 0) % FOX_HEADS)

    for u in range(pg):
        lg_ref[u:u + 1, :] = lc_ref[pl.ds(pt_ref[b, jg * pg + u], 1), :]
    within, total = _page_forget_scan(lg_ref[...])
    carry = carry_ref[...]
    logits = []
    for u in range(pg):
        d_u = within[u:u + 1] + carry
        carry = carry + total[u:u + 1]
        k = page_refs[u][0, 0, :, 0].reshape(keys, HEAD_DIM).astype(BF16)
        logits.append(jnp.where(own, _dot_nt(q, k) * SCALE - d_u, NEG_INF))
    carry_ref[...] = carry

    def pv(pages):
        def pv_of(p):
            out = None
            for i, u in enumerate(pages):
                v = page_refs[u][0, 0, :, 1].reshape(keys, HEAD_DIM).astype(BF16)
                part = _dot(p[:, i * keys:(i + 1) * keys], _with_ones(v))
                out = part if out is None else out + part
            return out
        return pv_of

    parts = []
    for first in range(0, pg, part_pages):
        pages = range(first, first + part_pages)
        parts.append((jnp.concatenate([logits[u] for u in pages], axis=1), pv(pages)))
    _state_update_parts(parts, m_ref, acc_ref, False)

    @pl.when(jg == pl.num_programs(1) - 1)
    def _():
        within_new, _ = _page_forget_scan(ln_ref[0])
        d_new = (within_new + carry_ref[...])[:, :PAGE_SIZE]
        c = lax.broadcasted_iota(jnp.int32, (rows, PAGE_SIZE), 1)
        r = lax.broadcasted_iota(jnp.int32, (rows, PAGE_SIZE), 0)
        valid = (c < rows) & (c % FOX_HEADS == r % FOX_HEADS) & (c // FOX_HEADS <= r // FOX_HEADS)
        lg = jnp.where(valid, _dot_nt(q, kn_ref[0].astype(BF16)) * SCALE - d_new, NEG_INF)
        _state_update(lg, lambda p: _dot(p, _with_ones(vn_ref[0].astype(BF16))), m_ref, acc_ref, False)
        o_ref[0] = _state_result(acc_ref) * _silu(z_ref[0])


def _fox_sample(cache, page_table, q_rows, logf_cache, k_new, v_new, logf_new, z_rows, pg, n_new):
    n_seq, n_pages = page_table.shape
    n_pool = logf_cache.shape[0]
    rows = n_new * FOX_HEADS
    keys = PAGE_SIZE * FOX_HEADS
    per_seq = lambda *shape: pl.BlockSpec((1,) + shape, lambda b, j, pt: (b,) + (0,) * len(shape))

    def page_spec(u):
        return pl.BlockSpec((1, 1, PAGE_SIZE, 2, FOX_HEADS, HEAD_DIM),
                            lambda b, j, pt: (0, pt[b, j * pg + u], 0, 0, 0, 0))

    grid_spec = pltpu.PrefetchScalarGridSpec(
        num_scalar_prefetch=1,
        grid=(n_seq, n_pages // pg),
        in_specs=[page_spec(u) for u in range(pg)] + [
            per_seq(rows, HEAD_DIM),
            pl.BlockSpec((n_pool, keys), lambda b, j, pt: (0, 0), pipeline_mode=pl.Buffered(1)),
            per_seq(PAGE_SIZE, HEAD_DIM),
            per_seq(PAGE_SIZE, HEAD_DIM),
            per_seq(1, keys),
            per_seq(rows, HEAD_DIM)],
        out_specs=per_seq(rows, HEAD_DIM),
        scratch_shapes=[pltpu.VMEM((rows, LANES), F32), pltpu.VMEM((rows, 2 * HEAD_DIM), F32),
                        pltpu.VMEM((1, keys), F32), pltpu.VMEM((pg, keys), F32)],
    )
    return pl.pallas_call(
        functools.partial(_fox_sample_kernel, pg=pg, part_pages=2, n_new=n_new),
        grid_spec=grid_spec,
        out_shape=jax.ShapeDtypeStruct((n_seq, rows, HEAD_DIM), F32),
        compiler_params=_params(("parallel", "arbitrary")),
        name="fox_sample",
    )(page_table, *([cache] * pg), q_rows, logf_cache, k_new, v_new, logf_new, z_rows)


def _held_pages(page_table, needed, pg):
    n_seq, n_pages = page_table.shape
    slots = page_table.reshape(n_seq * n_pages // pg, pg)
    want = needed.reshape(slots.shape).at[0].set(True)
    step = jnp.arange(slots.shape[0], dtype=jnp.int32)[:, None]
    filled_at = lax.cummax(jnp.where(want, step, 0), axis=0)
    return jnp.take_along_axis(slots, filled_at, axis=0).reshape(n_seq, n_pages)


def _pad_axis(a, axis, size):
    pad = [(0, 0)] * a.ndim
    pad[axis] = (0, size - a.shape[axis])
    return jnp.pad(a, pad)


def _sample_layer(x, cache_nsa_kv, cache_fox_kv, cache_fox_logf, state_nsa_win, page_table,
                  norm_in, wts, cmp_pe, cmp_w_bf, w_out_bf, norm_final):
    n_seq, t_new, d = x.shape
    n_pool = cache_nsa_kv.shape[1]
    n_pages = page_table.shape[1]
    past = n_pages * PAGE_SIZE
    w_buf = state_nsa_win.shape[2]
    assert t_new <= T_PAD and t_new < SEL_BLOCK and past % SEL_BLOCK == 0
    x2d = x.reshape(n_seq * t_new, d)
    proj = _project_all(x2d, norm_in, wts, n_seq * t_new)
    n_gate = 3 * NSA_HEADS
    small = proj["small"].reshape(n_seq, t_new, LANES)
    logf = small[:, :, n_gate:n_gate + FOX_HEADS]

    def head_rows(a):
        a = _pad_axis(a.reshape(n_seq, t_new, NSA_HEADS, -1), 1, T_PAD)
        return jnp.swapaxes(a, 1, 2).reshape(n_seq, NSA_HEADS * T_PAD, a.shape[-1])

    def group_rows(a):
        a = a.reshape(n_seq, t_new, NSA_KV_HEADS, NSA_GROUP, a.shape[-1])
        return jnp.swapaxes(a, 1, 2).reshape(n_seq, NSA_HEADS * t_new, a.shape[-1])

    q_n = proj["q_n"].astype(F32)
    gates = small[:, :, :n_gate].reshape(n_seq, t_new, NSA_HEADS, 3)
    z_n = proj["z_n"].reshape(n_seq, t_new, NSA_HEADS, HEAD_DIM)
    nsa_new = jnp.concatenate([proj["kv4"].reshape(n_seq, t_new, 4 * NSA_KV_HEADS, HEAD_DIM)[:, :, 2 * NSA_KV_HEADS:],
                               proj["kvwin"].reshape(n_seq, t_new, 2 * NSA_KV_HEADS, HEAD_DIM)], axis=2)
    nsa_new = _pad_axis(jnp.swapaxes(nsa_new, 1, 2), 2, PAGE_SIZE)
    win_state = jnp.swapaxes(state_nsa_win[0].reshape(n_seq, w_buf, 2 * NSA_KV_HEADS, HEAD_DIM), 1, 2)
    blk_of_key = jnp.arange(past, dtype=jnp.int32) // SEL_BLOCK
    expand = (jnp.arange(past // SEL_BLOCK, dtype=jnp.int32)[:, None] == blk_of_key[None, :]).astype(BF16)
    nsa_cache = cache_nsa_kv.reshape(cache_nsa_kv.shape[0], n_pool, PAGE_SIZE, 2, HALF_CHANNELS, HEAD_DIM)

    pg = 16
    o_c, sel = _nsa_cmp_sample(nsa_cache, page_table, head_rows(q_n), cmp_pe, cmp_w_bf, pg)
    o_c = jnp.swapaxes(o_c.reshape(n_seq, NSA_HEADS, T_PAD, HEAD_DIM)[:, :, :t_new], 1, 2)
    sel = sel.reshape(n_seq, NSA_KV_HEADS, T_PAD, past // SEL_BLOCK)[:, :, :t_new]
    page_needed = jnp.any(sel.reshape(n_seq, -1, n_pages, PAGE_SIZE // SEL_BLOCK) > 0.5, axis=(1, 3))
    slc_pages = _held_pages(page_table, page_needed, pg)
    sel = jnp.broadcast_to(sel[:, :, :, None, :], (n_seq, NSA_KV_HEADS, t_new, NSA_GROUP, past // SEL_BLOCK))
    mix_n = _nsa_slc_sample(nsa_cache, slc_pages, group_rows(q_n.reshape(n_seq, t_new, NSA_HEADS, HEAD_DIM)),
                            sel.reshape(n_seq, NSA_HEADS * t_new, past // SEL_BLOCK), expand, nsa_new, win_state,
                            group_rows(o_c), group_rows(gates), group_rows(z_n), pg, t_new)
    mix_n = jnp.swapaxes(mix_n.reshape(n_seq, NSA_KV_HEADS, t_new, NSA_GROUP * HEAD_DIM), 1, 2)
    mix_n = mix_n.reshape(n_seq * t_new, NSA_WIDTH).astype(BF16)

    fox_rows = lambda a: a.reshape(n_seq, t_new * FOX_HEADS, HEAD_DIM)
    kv_f = proj["kv_f"].reshape(n_seq, t_new, 2, FOX_HEADS * HEAD_DIM)
    k_new = _pad_axis(fox_rows(kv_f[:, :, 0]), 1, PAGE_SIZE)
    v_new = _pad_axis(fox_rows(kv_f[:, :, 1]), 1, PAGE_SIZE)
    logf_new = _pad_axis(logf.reshape(n_seq, 1, t_new * FOX_HEADS), 2, PAGE_SIZE * FOX_HEADS)
    logf_cache = cache_fox_logf[0].reshape(n_pool, PAGE_SIZE * FOX_HEADS)
    mix_f = _fox_sample(cache_fox_kv, page_table, fox_rows(proj["q_f"].astype(F32)), logf_cache, k_new, v_new,
                        logf_new, fox_rows(proj["z_f"]), 16, t_new)
    mix_f = mix_f.reshape(n_seq * t_new, FOX_WIDTH).astype(BF16)

    y = _merge(x2d, mix_n, mix_f, w_out_bf, norm_final, n_seq * t_new).reshape(n_seq, t_new, d)
    kvwin_new = proj["kvwin"].reshape(n_seq, t_new, 2, NSA_KV_HEADS, HEAD_DIM)
    state = (proj["kv4"].reshape(n_seq, t_new, 4, NSA_KV_HEADS, HEAD_DIM),
             proj["kv_f"].reshape(n_seq, t_new, 2, FOX_HEADS, HEAD_DIM),
             logf,
             jnp.concatenate([state_nsa_win[0], kvwin_new], axis=1)[:, t_new:])
    return y, state


def kernel(x_prompt, x_sample, cache_nsa_kv, cache_fox_kv, cache_fox_logf, state_nsa_win, page_table,
           norm_in, w_in, b_gate, b_forget, cmp_pe, cmp_w, w_out, norm_final):
    assert norm_in.shape[0] == 1, "single-layer trunk"
    wts = _split_weights(w_in[0], b_gate[0], b_forget[0])
    cmp_w_bf = cmp_w[0].reshape(2, CMP_BLOCK * HEAD_DIM, HEAD_DIM).astype(BF16)
    w_out_bf = w_out[0].astype(BF16)
    y_p, st_p = _prompt_layer(x_prompt, norm_in[0], wts, cmp_pe[0], cmp_w_bf, w_out_bf, norm_final)
    y_s, st_s = _sample_layer(x_sample, cache_nsa_kv, cache_fox_kv, cache_fox_logf, state_nsa_win,
                              page_table, norm_in[0], wts, cmp_pe[0], cmp_w_bf, w_out_bf, norm_final)
    outs = [y_p, y_s]
    for s_p, s_s in zip(st_p, st_s):
        outs.extend([s_p[None], s_s[None]])
    return tuple(outs)
```

```python
import functools

import jax
import jax.numpy as jnp
from jax import lax
from jax.experimental import pallas as pl
from jax.experimental.pallas import tpu as pltpu

F32 = jnp.float32
BF16 = jnp.bfloat16
NEG_INF = float("-inf")

HEAD_DIM = 128
NSA_HEADS = 8
FOX_HEADS = 8
NSA_KV_HEADS = 2
NSA_GROUP = NSA_HEADS // NSA_KV_HEADS
NSA_WIDTH = NSA_HEADS * HEAD_DIM
FOX_WIDTH = FOX_HEADS * HEAD_DIM
KV_WIDTH = NSA_KV_HEADS * HEAD_DIM
NSA_CHANNELS = 4 * NSA_KV_HEADS
CMP_BLOCK = 32
SEL_BLOCK = 64
SEL_TOPK = 16
WINDOW = 512
PAGE_SIZE = 128
FORCED_SCORE = 1e4
RMS_EPS = 1e-6
SCALE = HEAD_DIM ** -0.5
LOG2E = 1.4426950408889634
T_PAD = 8
LANES = 128
SUBLANES = 8
PROJ_TN = 512
STAGE_PITCH = 40
VMEM_LIMIT = 56 * 1024 * 1024


def _params(sem):
    return pltpu.CompilerParams(dimension_semantics=sem, vmem_limit_bytes=VMEM_LIMIT)


def _dot(a, b):
    return jnp.dot(a, b, preferred_element_type=F32)


def _dot_nt(a, b):
    return lax.dot_general(a, b, (((1,), (1,)), ((), ())), preferred_element_type=F32)


def _sigmoid(x):
    return 1.0 / (1.0 + jnp.exp(-x))


def _silu(x):
    return x * _sigmoid(x)


def _tile_lanes(x, n):
    return x if n == 1 else jnp.concatenate([x] * n, axis=1)


def _with_ones(v):
    return jnp.concatenate([v, jnp.ones(v.shape, v.dtype)], axis=1)


def _state_init(m_ref, acc_ref):
    m_ref[...] = jnp.full(m_ref.shape, NEG_INF, F32)
    acc_ref[...] = jnp.zeros(acc_ref.shape, F32)


def _probs(lg, m_old, may_be_empty, exp=jnp.exp):
    m_new = jnp.maximum(m_old, jnp.max(lg, axis=1, keepdims=True))
    m_use = jnp.where(m_new == NEG_INF, 0.0, m_new) if may_be_empty else m_new
    alpha = exp(m_old - m_use)
    p = exp(lg - _tile_lanes(m_use, lg.shape[1] // LANES)).astype(BF16)
    return m_new, alpha, p


def _state_update_chunked(logits_of, n_chunks, chunk, v_ext, m_ref, acc_ref, p_ref, alpha_ref, may_be_empty):
    for c in range(n_chunks):
        rows = slice(c * chunk, (c + 1) * chunk)
        m_new, alpha, p = _probs(logits_of(c), m_ref[rows, :], may_be_empty, jnp.exp2)
        alpha_ref[rows, :] = alpha
        p_ref[rows, :] = p
        m_ref[rows, :] = m_new
    acc_ref[...] = _tile_lanes(alpha_ref[...], 2) * acc_ref[...] + _dot(p_ref[...], v_ext)


def _state_update(lg, pv_of, m_ref, acc_ref, may_be_empty):
    m_new, alpha, p = _probs(lg, m_ref[...], may_be_empty)
    acc_ref[...] = _tile_lanes(alpha, 2) * acc_ref[...] + pv_of(p)
    m_ref[...] = m_new


def _state_update_parts(parts, m_ref, acc_ref, may_be_empty):
    local = []
    for lg, pv_of in parts:
        m_c = jnp.broadcast_to(jnp.max(lg, axis=1, keepdims=True), m_ref.shape)
        m_use = jnp.where(m_c == NEG_INF, 0.0, m_c) if may_be_empty else m_c
        p = jnp.exp(lg - _tile_lanes(m_use, lg.shape[1] // LANES)).astype(BF16)
        local.append((m_c, pv_of(p)))
    m_old = m_ref[...]
    m_new = m_old
    for m_c, _ in local:
        m_new = jnp.maximum(m_new, m_c)
    acc = _tile_lanes(jnp.exp(m_old - m_new), 2) * acc_ref[...]
    for m_c, pv in local:
        acc = acc + _tile_lanes(jnp.exp(m_c - m_new), 2) * pv
    acc_ref[...] = acc
    m_ref[...] = m_new


def _state_result(acc_ref):
    acc = acc_ref[...]
    return acc[:, :HEAD_DIM] / jnp.maximum(acc[:, HEAD_DIM:], 1e-30)


def _lane_cumsum(x):
    n = x.shape[-1]
    lane = lax.broadcasted_iota(jnp.int32, x.shape, x.ndim - 1)
    s = 1
    while s < n:
        x = x + jnp.where(lane >= s, pltpu.roll(x, s, axis=x.ndim - 1), 0.0)
        s *= 2
    return x


def _head_slope(head):
    if isinstance(head, int):
        return 2.0 ** -(head + 1)
    return lax.bitcast_convert_type((126 - head) << 23, F32)


def _rms_kernel(x_ref, g_ref, o_ref):
    x = x_ref[...]
    ms = jnp.mean(x * x, axis=-1, keepdims=True)
    o_ref[...] = (x * lax.rsqrt(ms + RMS_EPS) * g_ref[...]).astype(o_ref.dtype)


def _rmsnorm(x2d, g, out_dtype, tm):
    m, d = x2d.shape
    return pl.pallas_call(
        _rms_kernel,
        grid=(m // tm,),
        in_specs=[pl.BlockSpec((tm, d), lambda i: (i, 0)), pl.BlockSpec((1, d), lambda i: (0, 0))],
        out_specs=pl.BlockSpec((tm, d), lambda i: (i, 0)),
        out_shape=jax.ShapeDtypeStruct((m, d), out_dtype),
        compiler_params=_params(("parallel",)),
        name="rmsnorm",
    )(x2d, g.reshape(1, d))


def _proj_kernel(h_ref, w_ref, *o_refs):
    acc = _dot(h_ref[...], w_ref[...])
    for o_ref in o_refs:
        o_ref[...] = acc.astype(o_ref.dtype)


def _project(h, w, col0, n, out_dtypes, tm, name):
    m, d = h.shape
    j0 = col0 // PROJ_TN
    return pl.pallas_call(
        _proj_kernel,
        grid=(m // tm, n // PROJ_TN),
        in_specs=[pl.BlockSpec((tm, d), lambda i, j: (i, 0)), pl.BlockSpec((d, PROJ_TN), lambda i, j: (0, j0 + j))],
        out_specs=[pl.BlockSpec((tm, PROJ_TN), lambda i, j: (i, j)) for _ in out_dtypes],
        out_shape=[jax.ShapeDtypeStruct((m, n), dt) for dt in out_dtypes],
        compiler_params=_params(("parallel", "arbitrary")),
        name=name,
    )(h, w)


def _small_kernel(h_ref, w_ref, b_ref, o_ref, *, n_gate):
    z = _dot(h_ref[...], w_ref[...].astype(BF16)) + b_ref[...]
    lane = lax.broadcasted_iota(jnp.int32, z.shape, 1)
    log_sig = jnp.minimum(z, 0.0) - jnp.log1p(jnp.exp(-jnp.abs(z)))
    o_ref[...] = jnp.where(lane < n_gate, _sigmoid(z), log_sig)


def _project_small(h, w, b, tm, n_gate):
    m, d = h.shape
    return pl.pallas_call(
        functools.partial(_small_kernel, n_gate=n_gate),
        grid=(m // tm,),
        in_specs=[pl.BlockSpec((tm, d), lambda i: (i, 0)), pl.BlockSpec((d, LANES), lambda i: (0, 0)),
                  pl.BlockSpec((1, LANES), lambda i: (0, 0))],
        out_specs=pl.BlockSpec((tm, LANES), lambda i: (i, 0)),
        out_shape=jax.ShapeDtypeStruct((m, LANES), F32),
        compiler_params=_params(("parallel",)),
        name="proj_gates_logf",
    )(h, w, b)


PROJ_GROUPS = (("q_n", NSA_WIDTH), ("kv4", 4 * KV_WIDTH), ("kvwin", 2 * KV_WIDTH), ("z_n", NSA_WIDTH),
               ("q_f", FOX_WIDTH), ("kv_f", 2 * FOX_WIDTH), ("z_f", FOX_WIDTH))
PROJ_OUT_DTYPES = {"q_n": (BF16,), "kv4": (F32, BF16), "kvwin": (F32, BF16), "z_n": (F32,),
                   "q_f": (BF16,), "kv_f": (F32, BF16), "z_f": (F32,)}


def _regroup_kernel(wt_ref, o_ref):
    o_ref[...] = wt_ref[...].T.astype(o_ref.dtype)


def _regroup_columns(w_t, spans):
    cols, rows = w_t.shape
    starts, col = [], 0
    for a, b in spans:
        assert col % PROJ_TN == 0 and (b - a) % PROJ_TN == 0 and a % SUBLANES == 0
        starts.append((col // PROJ_TN, a - col))
        col += b - a

    n_tiles = col // PROJ_TN

    def source_row(j):
        units = j * (PROJ_TN // SUBLANES) + starts[0][1] // SUBLANES
        for (first_tile, s), (_, s_prev) in zip(starts[1:], starts[:-1]):
            units = units + ((j + n_tiles - first_tile) // n_tiles) * ((s - s_prev) // SUBLANES)
        return units * SUBLANES

    return pl.pallas_call(
        _regroup_kernel,
        grid=(col // PROJ_TN,),
        in_specs=[pl.BlockSpec((pl.Element(PROJ_TN), pl.Element(rows)), lambda j: (source_row(j), 0))],
        out_specs=pl.BlockSpec((rows, PROJ_TN), lambda j: (0, j)),
        out_shape=jax.ShapeDtypeStruct((rows, col), BF16),
        compiler_params=_params(("parallel",)),
        name="regroup_weights",
    )(w_t)


def _split_weights(w_in, b_gate, b_forget):
    cuts = [NSA_WIDTH, 6 * KV_WIDTH, 3 * NSA_HEADS, NSA_WIDTH, FOX_WIDTH, FOX_WIDTH, FOX_WIDTH, FOX_HEADS, FOX_WIDTH]
    offs = [0]
    for c in cuts:
        offs.append(offs[-1] + c)
    w_t = jnp.swapaxes(w_in, 0, 1)
    main = _regroup_columns(w_t, ((offs[0], offs[2]), (offs[3], offs[7]), (offs[8], offs[9])))
    n_small = 3 * NSA_HEADS + FOX_HEADS
    w_small = jnp.concatenate([w_t[offs[2]:offs[3]], w_t[offs[7]:offs[8]]], axis=0).T
    w_small = jnp.pad(w_small, ((0, 0), (0, LANES - n_small)))
    b_small = jnp.pad(jnp.concatenate([b_gate, b_forget]), (0, LANES - n_small)).reshape(1, LANES).astype(F32)
    return {"main": main, "small": w_small, "b_small": b_small}


def _proj_kv4_kernel(h_ref, w_ref, chan_ref, cmp_ref, bf_ref):
    acc = _dot(h_ref[...], w_ref[...])
    for c in range(NSA_CHANNELS):
        chan_ref[:, c, :] = acc[:, c * HEAD_DIM:(c + 1) * HEAD_DIM]
    cmp_ref[...] = acc[:, :2 * KV_WIDTH]
    bf_ref[...] = acc.astype(bf_ref.dtype)


def _project_kv4(h, w, col0, tm):
    m, d = h.shape
    n = 4 * KV_WIDTH
    assert col0 % n == 0
    return pl.pallas_call(
        _proj_kv4_kernel,
        grid=(m // tm,),
        in_specs=[pl.BlockSpec((tm, d), lambda i: (i, 0)), pl.BlockSpec((d, n), lambda i: (0, col0 // n))],
        out_specs=[pl.BlockSpec((tm, NSA_CHANNELS, HEAD_DIM), lambda i: (i, 0, 0)),
                   pl.BlockSpec((tm, 2 * KV_WIDTH), lambda i: (i, 0)),
                   pl.BlockSpec((tm, n), lambda i: (i, 0))],
        out_shape=[jax.ShapeDtypeStruct((m, NSA_CHANNELS, HEAD_DIM), F32),
                   jax.ShapeDtypeStruct((m, 2 * KV_WIDTH), F32),
                   jax.ShapeDtypeStruct((m, n), BF16)],
        compiler_params=_params(("parallel",)),
        name="proj_kv4_state",
    )(h, w)


def _project_all(x2d, norm_in, wts, tm, kv4_state_order=False):
    h = _rmsnorm(x2d, norm_in, BF16, min(tm, 512))
    out = {}
    col0 = 0
    for name, width in PROJ_GROUPS:
        if name == "kv4" and kv4_state_order:
            out["kv4_state"], out["kv4_cmp"], out["kv4_bf"] = _project_kv4(h, wts["main"], col0, tm)
            col0 += width
            continue
        res = _project(h, wts["main"], col0, width, PROJ_OUT_DTYPES[name], tm, "proj_" + name)
        out[name] = res[0]
        if len(res) > 1:
            out[name + "_bf"] = res[1]
        col0 += width
    out["small"] = _project_small(h, wts["small"], wts["b_small"], min(tm, 512), 3 * NSA_HEADS)
    return out


def _merge_kernel(x_ref, mn_ref, mf_ref, wn_ref, wf_ref, g_ref, o_ref):
    y = x_ref[...] + _dot(mn_ref[...], wn_ref[...]) + _dot(mf_ref[...], wf_ref[...])
    ms = jnp.mean(y * y, axis=-1, keepdims=True)
    o_ref[...] = y * lax.rsqrt(ms + RMS_EPS) * g_ref[...]


def _merge(x2d, mix_n, mix_f, w_out_bf, norm_final, tm):
    m, d = x2d.shape
    return pl.pallas_call(
        _merge_kernel,
        grid=(m // tm,),
        in_specs=[pl.BlockSpec((tm, d), lambda i: (i, 0)),
                  pl.BlockSpec((tm, NSA_WIDTH), lambda i: (i, 0)),
                  pl.BlockSpec((tm, FOX_WIDTH), lambda i: (i, 0)),
                  pl.BlockSpec((NSA_WIDTH, d), lambda i: (0, 0)),
                  pl.BlockSpec((FOX_WIDTH, d), lambda i: (NSA_WIDTH // FOX_WIDTH, 0)),
                  pl.BlockSpec((1, d), lambda i: (0, 0))],
        out_specs=pl.BlockSpec((tm, d), lambda i: (i, 0)),
        out_shape=jax.ShapeDtypeStruct((m, d), F32),
        compiler_params=_params(("parallel",)),
        name="merge_out_proj",
    )(x2d, mix_n, mix_f, w_out_bf, w_out_bf, norm_final.reshape(1, d))


def _cumsum_kernel(x_ref, o_ref):
    o_ref[...] = _lane_cumsum(x_ref[...])


def _cumsum_rows(x):
    return pl.pallas_call(
        _cumsum_kernel,
        out_shape=jax.ShapeDtypeStruct(x.shape, F32),
        name="forget_cumsum",
    )(x)


def _compress_rows(x_ref, pe_ref, kind, w, xs_ref, nb, starts, stride):
    half = nb // 2
    for c in range(CMP_BLOCK):
        pe_c = pe_ref[kind, c:c + 1, :]
        for par in range(2):
            xc = x_ref[pl.ds(starts[par] + c, half, stride=stride), :]
            xs_ref[par * half:(par + 1) * half, c * HEAD_DIM:(c + 1) * HEAD_DIM] = (xc + pe_c).astype(BF16)
    return _dot(xs_ref[...], w)


def _compress_kernel(x_ref, pe_ref, w_ref, o_ref, xs_ref, *, nb):
    out = _compress_rows(x_ref, pe_ref, 0, w_ref[0], xs_ref, nb, (0, CMP_BLOCK), 2 * CMP_BLOCK)
    o_ref[0, 0, 0] = out.astype(o_ref.dtype)


def _compress_prompt(kv4, cmp_pe, cmp_w_bf, batch, seq):
    nb = seq // CMP_BLOCK
    return pl.pallas_call(
        functools.partial(_compress_kernel, nb=nb),
        grid=(batch, 2, NSA_KV_HEADS),
        in_specs=[pl.BlockSpec((seq, HEAD_DIM), lambda b, k, h: (b, k * NSA_KV_HEADS + h)),
                  pl.BlockSpec((1, CMP_BLOCK, HEAD_DIM), lambda b, k, h: (k, 0, 0)),
                  pl.BlockSpec((1, CMP_BLOCK * HEAD_DIM, HEAD_DIM), lambda b, k, h: (k, 0, 0))],
        out_specs=pl.BlockSpec((1, 1, 1, nb, HEAD_DIM), lambda b, k, h: (b, k, h, 0, 0)),
        out_shape=jax.ShapeDtypeStruct((batch, 2, NSA_KV_HEADS, nb, HEAD_DIM), BF16),
        scratch_shapes=[pltpu.VMEM((nb, CMP_BLOCK * HEAD_DIM), BF16)],
        compiler_params=_params(("parallel", "parallel", "arbitrary")),
        name="compress_prompt",
    )(kv4, cmp_pe, cmp_w_bf)


def _select_blocks(score_t, cur, n_keep):
    n_blk = score_t.shape[0]
    j = lax.broadcasted_iota(jnp.int32, score_t.shape, 0)
    forced = (j == 0) | (j == cur) | (j == cur - 1)
    s = jnp.where(j <= cur, jnp.where(forced, FORCED_SCORE, score_t), -1.0)
    n_grp = n_blk // SUBLANES
    grp = [s[r * SUBLANES:(r + 1) * SUBLANES] for r in range(n_grp)]
    cnt = [jnp.zeros(grp[0].shape, F32) for _ in range(n_grp)]
    sub = lax.broadcasted_iota(jnp.int32, grp[0].shape, 0)
    for jj in range(n_blk):
        row = s[jj:jj + 1, :]
        r_j, off = divmod(jj, SUBLANES)
        for r in range(n_grp):
            if r < r_j:
                hit = row > grp[r]
            elif r > r_j:
                hit = row >= grp[r]
            else:
                hit = (row > grp[r]) | ((row == grp[r]) & (sub > off))
            cnt[r] = cnt[r] + jnp.where(hit, 1.0, 0.0)
    return jnp.where((jnp.concatenate(cnt, axis=0) < n_keep) & (j <= cur), 1.0, 0.0)


def _nsa_prompt_kernel(q_ref, kc_ref, vc_ref, ks_ref, vs_ref, kw_ref, vw_ref, g_ref, z_ref, e_ref, o_ref,
                       m_ref, acc_ref, alpha_ref, ps_ref, pw_ref, mix_ref, bias_ref, *, tq, tk, tw, ck, seq):
    qi = pl.program_id(1)
    s0 = qi * tq
    n_cmp = seq // CMP_BLOCK
    n_sel = seq // SEL_BLOCK
    row_t = s0 + lax.broadcasted_iota(jnp.int32, (tq, 1), 0)
    kv_heads = range(NSA_KV_HEADS)
    heads = lambda hkv: [hkv * NSA_GROUP + g for g in range(NSA_GROUP)]
    slopes = lambda hkv: [_head_slope(h) for h in heads(hkv)]
    cols = lambda hkv: slice(hkv * HEAD_DIM, (hkv + 1) * HEAD_DIM)

    def q_rows(hkv):
        return jnp.concatenate([q_ref[:, h * HEAD_DIM:(h + 1) * HEAD_DIM] for h in heads(hkv)], axis=0)

    def gate_col(hkv, br):
        return jnp.concatenate([g_ref[:, 3 * h + br:3 * h + br + 1] for h in heads(hkv)], axis=0)

    sels = []
    for hkv in kv_heads:
        sc = _dot_nt(q_rows(hkv), kc_ref[0, 0, hkv]) * SCALE
        lane = lax.broadcasted_iota(jnp.int32, (tq, n_cmp), 1)
        blk = jnp.where(lane < n_cmp // 2, 2 * lane, 2 * lane - (n_cmp - 1))
        dist_c = (row_t - ((blk + 1) * CMP_BLOCK - 1)).astype(F32)
        valid_c = dist_c >= 0
        score = jnp.zeros((tq, n_cmp), F32)
        probs = []
        for g, slope in enumerate(slopes(hkv)):
            lg = jnp.where(valid_c, sc[g * tq:(g + 1) * tq] - slope * dist_c, NEG_INF)
            mx = jnp.max(lg, axis=1, keepdims=True)
            mx = jnp.where(mx == NEG_INF, 0.0, mx)
            p = jnp.exp(lg - mx)
            p = p / jnp.maximum(jnp.sum(p, axis=1, keepdims=True), 1e-30)
            score = score + p
            probs.append(p)
        o_c = _dot(jnp.concatenate(probs, axis=0).astype(BF16), vc_ref[0, 0, hkv])
        mix_ref[hkv] = gate_col(hkv, 0) * o_c

        score_t = score.T
        score_t = score_t[:n_sel] + score_t[n_sel:]
        cur = (s0 + lax.broadcasted_iota(jnp.int32, (n_sel, tq), 1)) // SEL_BLOCK
        sel_t = _select_blocks(score_t, cur, SEL_TOPK)
        sels.append(jnp.concatenate([sel_t, jnp.zeros((LANES - n_sel, tq), F32)], axis=0).T.astype(BF16))

    def bias_body(c, carry):
        c0 = pl.multiple_of(c * ck, ck)
        causal = c0 + lax.broadcasted_iota(jnp.int32, (tq, ck), 1) <= row_t
        for hkv in kv_heads:
            expanded = _dot(sels[hkv], e_ref[:, pl.ds(c0, ck)])
            bias_ref[hkv, :, pl.ds(c0, ck)] = jnp.where((expanded > 0.5) & causal, 0.0, NEG_INF)
        return carry

    lax.fori_loop(0, (s0 + tq + ck - 1) // ck, bias_body, 0)

    def stream(hkv, k, bias, key_rel, v, p_ref, may_be_empty):
        s = _dot_nt(q_rows(hkv), k)
        sl = slopes(hkv)
        logits_of = lambda g: s[g * tq:(g + 1) * tq] * (SCALE * LOG2E) + (bias + (sl[g] * LOG2E) * key_rel)
        _state_update_chunked(logits_of, NSA_GROUP, tq, _with_ones(v), m_ref.at[hkv], acc_ref.at[hkv],
                              p_ref.at[hkv], alpha_ref.at[hkv], may_be_empty)

    _state_init(m_ref, acc_ref)

    def slc_body(i, carry):
        k0 = pl.multiple_of(i * tk, tk)
        key_rel = (k0 - s0 + lax.broadcasted_iota(jnp.int32, (1, tk), 1)).astype(F32)
        for hkv in kv_heads:
            stream(hkv, ks_ref[pl.ds(k0, tk), cols(hkv)], bias_ref[hkv, :, pl.ds(k0, tk)], key_rel,
                   vs_ref[pl.ds(k0, tk), cols(hkv)], ps_ref, False)
        return carry

    lax.fori_loop(0, (s0 + tq + tk - 1) // tk, slc_body, 0)
    for hkv in kv_heads:
        mix_ref[hkv] = mix_ref[hkv] + gate_col(hkv, 1) * _state_result(acc_ref.at[hkv])

    _state_init(m_ref, acc_ref)

    def win_body(w, carry):
        k0 = pl.multiple_of(s0 - WINDOW + w * tw, tw)
        disti = row_t - (k0 + lax.broadcasted_iota(jnp.int32, (tq, tw), 1))
        bias = jnp.where((disti >= 0) & (disti < WINDOW), 0.0, NEG_INF)
        key_rel = (k0 - s0 + lax.broadcasted_iota(jnp.int32, (1, tw), 1)).astype(F32)
        for hkv in kv_heads:
            stream(hkv, kw_ref[pl.ds(k0, tw), cols(hkv)], bias, key_rel,
                   vw_ref[pl.ds(k0, tw), cols(hkv)], pw_ref, True)
        return carry

    n_win_tiles = (WINDOW + tq) // tw
    lax.fori_loop(jnp.maximum(0, (WINDOW - s0) // tw), n_win_tiles, win_body, 0)

    for hkv in kv_heads:
        mix = mix_ref[hkv] + gate_col(hkv, 2) * _state_result(acc_ref.at[hkv])
        for g, h in enumerate(heads(hkv)):
            z = z_ref[:, h * HEAD_DIM:(h + 1) * HEAD_DIM]
            o_ref[:, h * HEAD_DIM:(h + 1) * HEAD_DIM] = (mix[g * tq:(g + 1) * tq] * _silu(z)).astype(o_ref.dtype)


def _nsa_prompt(proj, kcv, expand, batch, seq):
    tq, tk, tw, ck = 256, 256, 256, 512
    assert seq % ck == 0 and WINDOW % tw == 0 and tq % tw == 0 and seq // SEL_BLOCK <= LANES
    nq = seq // tq
    rows = NSA_GROUP * tq
    kernel = functools.partial(_nsa_prompt_kernel, tq=tq, tk=tk, tw=tw, ck=ck, seq=seq)
    cmp_spec = lambda kind: pl.BlockSpec((1, 1, NSA_KV_HEADS, seq // CMP_BLOCK, HEAD_DIM),
                                         lambda b, i: (b, kind, 0, 0, 0))
    return pl.pallas_call(
        kernel,
        grid=(batch, nq),
        in_specs=[pl.BlockSpec((tq, NSA_WIDTH), lambda b, i: (b * nq + i, 0)),
                  cmp_spec(0), cmp_spec(1),
                  pl.BlockSpec((seq, KV_WIDTH), lambda b, i: (b, 2)),
                  pl.BlockSpec((seq, KV_WIDTH), lambda b, i: (b, 3)),
                  pl.BlockSpec((seq, KV_WIDTH), lambda b, i: (b, 0)),
                  pl.BlockSpec((seq, KV_WIDTH), lambda b, i: (b, 1)),
                  pl.BlockSpec((tq, LANES), lambda b, i: (b * nq + i, 0)),
                  pl.BlockSpec((tq, NSA_WIDTH), lambda b, i: (b * nq + i, 0)),
                  pl.BlockSpec((LANES, seq), lambda b, i: (0, 0))],
        out_specs=pl.BlockSpec((tq, NSA_WIDTH), lambda b, i: (b * nq + i, 0)),
        out_shape=jax.ShapeDtypeStruct((batch * seq, NSA_WIDTH), BF16),
        scratch_shapes=[pltpu.VMEM((NSA_KV_HEADS, rows, LANES), F32),
                        pltpu.VMEM((NSA_KV_HEADS, rows, 2 * HEAD_DIM), F32),
                        pltpu.VMEM((NSA_KV_HEADS, rows, LANES), F32),
                        pltpu.VMEM((NSA_KV_HEADS, rows, tk), BF16),
                        pltpu.VMEM((NSA_KV_HEADS, rows, tw), BF16),
                        pltpu.VMEM((NSA_KV_HEADS, rows, HEAD_DIM), F32),
                        pltpu.VMEM((NSA_KV_HEADS, tq, seq), F32)],
        compiler_params=_params(("parallel", "arbitrary")),
        name="nsa_prompt",
    )(proj["q_n"], kcv, kcv, proj["kv4_bf"], proj["kv4_bf"], proj["kvwin_bf"], proj["kvwin_bf"],
      proj["small"], proj["z_n"], expand)


def _fox_prompt_kernel(q_ref, k_ref, v_ref, d_ref, z_ref, o_ref, m_ref, acc_ref, alpha_ref, p_ref, *,
                       t, chunk, n_heads):
    qi = pl.program_id(2)
    _state_init(m_ref, acc_ref)

    def tile(i, causal):
        k0 = pl.multiple_of(i * t, t)
        for hh in range(n_heads):
            cols = slice(hh * HEAD_DIM, (hh + 1) * HEAD_DIM)
            s = _dot_nt(q_ref[:, cols], k_ref[pl.ds(k0, t), cols])
            d_row = d_ref[hh, :, pl.ds(k0, t)] * LOG2E

            def logits_of(c, s=s, d_row=d_row):
                lg = s[c * chunk:(c + 1) * chunk] * (SCALE * LOG2E) - d_row
                if causal:
                    row = c * chunk + lax.broadcasted_iota(jnp.int32, (chunk, t), 0)
                    col = lax.broadcasted_iota(jnp.int32, (chunk, t), 1)
                    lg = jnp.where(col <= row, lg, NEG_INF)
                return lg

            _state_update_chunked(logits_of, t // chunk, chunk, _with_ones(v_ref[pl.ds(k0, t), cols]),
                                  m_ref.at[hh], acc_ref.at[hh], p_ref.at[hh], alpha_ref.at[hh], False)

    def body(i, carry):
        tile(i, False)
        return carry

    lax.fori_loop(0, qi, body, 0)
    tile(qi, True)
    for hh in range(n_heads):
        cols = slice(hh * HEAD_DIM, (hh + 1) * HEAD_DIM)
        o_ref[:, cols] = (_state_result(acc_ref.at[hh]) * _silu(z_ref[:, cols])).astype(o_ref.dtype)


def _fox_prompt(proj, d_rows, batch, seq):
    t, chunk, n_heads = 512, 128, 4
    nq = seq // t
    width = n_heads * HEAD_DIM
    n_hg = FOX_HEADS // n_heads
    return pl.pallas_call(
        functools.partial(_fox_prompt_kernel, t=t, chunk=chunk, n_heads=n_heads),
        grid=(batch, n_hg, nq),
        in_specs=[pl.BlockSpec((t, width), lambda b, h, i: (b * nq + i, h)),
                  pl.BlockSpec((seq, width), lambda b, h, i: (b, h)),
                  pl.BlockSpec((seq, width), lambda b, h, i: (b, n_hg + h)),
                  pl.BlockSpec((n_heads, 1, seq), lambda b, h, i: (b * n_hg + h, 0, 0)),
                  pl.BlockSpec((t, width), lambda b, h, i: (b * nq + i, h))],
        out_specs=pl.BlockSpec((t, width), lambda b, h, i: (b * nq + i, h)),
        out_shape=jax.ShapeDtypeStruct((batch * seq, FOX_WIDTH), BF16),
        scratch_shapes=[pltpu.VMEM((n_heads, t, LANES), F32), pltpu.VMEM((n_heads, t, 2 * HEAD_DIM), F32),
                        pltpu.VMEM((n_heads, t, LANES), F32), pltpu.VMEM((n_heads, t, t), BF16)],
        compiler_params=_params(("parallel", "parallel", "arbitrary")),
        name="fox_prompt",
    )(proj["q_f"], proj["kv_f_bf"], proj["kv_f_bf"], d_rows, proj["z_f"])


def _prompt_layer(x, norm_in, wts, cmp_pe, cmp_w_bf, w_out_bf, norm_final):
    batch, seq, d = x.shape
    x2d = x.reshape(batch * seq, d)
    proj = _project_all(x2d, norm_in, wts, 1024, kv4_state_order=True)
    n_gate = 3 * NSA_HEADS
    logf = proj["small"][:, n_gate:n_gate + FOX_HEADS].reshape(batch, seq, FOX_HEADS)
    d_rows = _cumsum_rows(jnp.swapaxes(logf, 1, 2).reshape(batch * FOX_HEADS, seq))
    kcv = _compress_prompt(proj["kv4_cmp"], cmp_pe, cmp_w_bf, batch, seq)
    blk_of_key = jnp.arange(seq, dtype=jnp.int32) // SEL_BLOCK
    expand = (jnp.arange(LANES, dtype=jnp.int32)[:, None] == blk_of_key[None, :]).astype(BF16)
    mix_n = _nsa_prompt(proj, kcv, expand, batch, seq)
    mix_f = _fox_prompt(proj, d_rows.reshape(batch * FOX_HEADS, 1, seq), batch, seq)
    y = _merge(x2d, mix_n, mix_f, w_out_bf, norm_final, 512).reshape(batch, seq, d)
    w_keep = min(WINDOW, seq)
    state = (proj["kv4_state"].reshape(batch, seq, 4, NSA_KV_HEADS, HEAD_DIM),
             proj["kv_f"].reshape(batch, seq, 2, FOX_HEADS, HEAD_DIM),
             logf,
             proj["kvwin"].reshape(batch, seq, 2 * KV_WIDTH)[:, seq - w_keep:].reshape(
                 batch, w_keep, 2, NSA_KV_HEADS, HEAD_DIM))
    return y, state


HALF_CHANNELS = NSA_CHANNELS // 2


def _nsa_page_specs(pg, half):
    def spec(u):
        return pl.BlockSpec((1, 1, PAGE_SIZE, 1, HALF_CHANNELS, HEAD_DIM),
                            lambda b, j, pt: (0, pt[b, j * pg + u], 0, half, 0, 0))
    return [spec(u) for u in range(pg)]


def _channel(page_ref, ch, start=0, n=PAGE_SIZE):
    rows = page_ref.reshape(PAGE_SIZE * HALF_CHANNELS, HEAD_DIM)
    return rows[pl.ds(start * HALF_CHANNELS + ch, n, stride=HALF_CHANNELS), :]


def _nsa_cmp_kernel(pt_ref, *refs, pg, n_pages):
    page_refs = refs[:pg]
    q_ref, pe_ref, w_ref, oc_ref, sel_ref, stage_ref, xs_ref = refs[pg:]
    jg = pl.program_id(1)
    past = n_pages * PAGE_SIZE
    nb = past // CMP_BLOCK
    half = nb // 2
    blocks_per_page = PAGE_SIZE // CMP_BLOCK
    for u in range(pg):
        for bl in range(blocks_per_page):
            m = (jg * pg + u) * (blocks_per_page // 2) + bl // 2
            dst = pl.multiple_of(((bl % 2) * half + m) * STAGE_PITCH, SUBLANES)
            for ch in range(2 * NSA_KV_HEADS):
                stage_ref[ch, pl.ds(dst, CMP_BLOCK), :] = _channel(page_refs[u], ch, bl * CMP_BLOCK, CMP_BLOCK)

    @pl.when(jg == pl.num_programs(1) - 1)
    def _():
        cmp = [[_compress_rows(stage_ref.at[kind * NSA_KV_HEADS + h], pe_ref, kind, w_ref[kind], xs_ref, nb,
                               (0, half * STAGE_PITCH), STAGE_PITCH).astype(BF16)
                for h in range(NSA_KV_HEADS)] for kind in range(2)]
        rows = NSA_GROUP * T_PAD
        for hkv in range(NSA_KV_HEADS):
            q = q_ref[0, hkv * rows:(hkv + 1) * rows, :].astype(BF16)
            kc, vc = cmp[0][hkv], cmp[1][hkv]
            s = _dot_nt(q, kc) * SCALE
            lane = lax.broadcasted_iota(jnp.int32, (rows, nb), 1)
            blk = jnp.where(lane < half, 2 * lane, 2 * lane - (nb - 1))
            row = lax.broadcasted_iota(jnp.int32, (rows, nb), 0)
            slope = _head_slope(hkv * NSA_GROUP + row // T_PAD)
            dist = (past + row % T_PAD - ((blk + 1) * CMP_BLOCK - 1)).astype(F32)
            lg = jnp.where(dist >= 0, s - slope * dist, NEG_INF)
            mx = jnp.max(lg, axis=1, keepdims=True)
            mx = jnp.where(mx == NEG_INF, 0.0, mx)
            p = jnp.exp(lg - mx)
            p = p / jnp.maximum(jnp.sum(p, axis=1, keepdims=True), 1e-30)
            oc_ref[0, hkv * rows:(hkv + 1) * rows, :] = _dot(p.astype(BF16), vc)
            score = p[0:T_PAD]
            for g in range(1, NSA_GROUP):
                score = score + p[g * T_PAD:(g + 1) * T_PAD]
            score = score[:, :half] + score[:, half:]
            blk_s = lax.broadcasted_iota(jnp.int32, (T_PAD, half), 1).astype(F32)
            forced = (blk_s == 0) | (blk_s == half - 1)
            s_left = jnp.where(forced, NEG_INF, score)
            picked = jnp.zeros((T_PAD, half), F32)
            for _ in range(SEL_TOPK - 3):
                best = jnp.max(s_left, axis=1, keepdims=True)
                first = jnp.min(jnp.where(s_left == best, blk_s, float(half)), axis=1, keepdims=True)
                hit = blk_s == first
                picked = jnp.where(hit, 1.0, picked)
                s_left = jnp.where(hit, NEG_INF, s_left)
            sel_ref[0, hkv * T_PAD:(hkv + 1) * T_PAD, :] = jnp.where(forced, 1.0, picked)


def _nsa_cmp_sample(cache, page_table, q_rows, cmp_pe, cmp_w_bf, pg):
    n_seq, n_pages = page_table.shape
    past = n_pages * PAGE_SIZE
    nb = past // CMP_BLOCK
    assert n_pages % pg == 0 and past // SEL_BLOCK >= SEL_TOPK
    rows = NSA_HEADS * T_PAD
    grid_spec = pltpu.PrefetchScalarGridSpec(
        num_scalar_prefetch=1,
        grid=(n_seq, n_pages // pg),
        in_specs=_nsa_page_specs(pg, 0) + [
            pl.BlockSpec((1, rows, HEAD_DIM), lambda b, j, pt: (b, 0, 0)),
            pl.BlockSpec((2, CMP_BLOCK, HEAD_DIM), lambda b, j, pt: (0, 0, 0)),
            pl.BlockSpec((2, CMP_BLOCK * HEAD_DIM, HEAD_DIM), lambda b, j, pt: (0, 0, 0))],
        out_specs=[pl.BlockSpec((1, rows, HEAD_DIM), lambda b, j, pt: (b, 0, 0)),
                   pl.BlockSpec((1, NSA_KV_HEADS * T_PAD, past // SEL_BLOCK), lambda b, j, pt: (b, 0, 0))],
        scratch_shapes=[pltpu.VMEM((2 * NSA_KV_HEADS, nb * STAGE_PITCH, HEAD_DIM), F32),
                        pltpu.VMEM((nb, CMP_BLOCK * HEAD_DIM), BF16)],
    )
    return pl.pallas_call(
        functools.partial(_nsa_cmp_kernel, pg=pg, n_pages=n_pages),
        grid_spec=grid_spec,
        out_shape=[jax.ShapeDtypeStruct((n_seq, rows, HEAD_DIM), F32),
                   jax.ShapeDtypeStruct((n_seq, NSA_KV_HEADS * T_PAD, past // SEL_BLOCK), F32)],
        compiler_params=_params(("parallel", "arbitrary")),
        name="nsa_sample_cmp",
    )(page_table, *([cache] * pg), q_rows, cmp_pe, cmp_w_bf)


def _nsa_slc_kernel(pt_ref, *refs, pg, part_pages, n_pages, w_buf, n_new):
    page_refs = refs[:pg]
    (q_ref, sel_ref, e_ref, new_ref, win_ref, oc_ref, g_ref, z_ref, o_ref, tab_ref, m_ref, acc_ref) = refs[pg:]
    jg = pl.program_id(1)
    past = n_pages * PAGE_SIZE
    hr = n_new * NSA_GROUP
    rows = NSA_KV_HEADS * hr
    row = lax.broadcasted_iota(jnp.int32, (rows, 1), 0)
    t_q = (row % hr) // NSA_GROUP
    slope = _head_slope((row // hr) * NSA_GROUP + row % NSA_GROUP)
    q = [q_ref[0, h * hr:(h + 1) * hr, :].astype(BF16) for h in range(NSA_KV_HEADS)]

    def scores(k_of):
        return jnp.concatenate([_dot_nt(q[h], k_of(h)) for h in range(NSA_KV_HEADS)], axis=0) * SCALE

    def pv(v_of):
        return lambda p: jnp.concatenate(
            [_dot(p[h * hr:(h + 1) * hr], _with_ones(v_of(h))) for h in range(NSA_KV_HEADS)], axis=0)


    @pl.when(jg == 0)
    def _():
        expanded = _dot(sel_ref[0].astype(BF16), e_ref[...])
        key = lax.broadcasted_iota(jnp.int32, (rows, past), 1)
        tab_ref[...] = jnp.where(expanded > 0.5, 0.0, NEG_INF) + slope * (key - past).astype(F32)
        _state_init(m_ref, acc_ref)

    def page_stream(ch, pages):
        return jnp.concatenate([_channel(page_refs[u], ch).astype(BF16) for u in pages], axis=0)

    parts = []
    for first in range(0, pg, part_pages):
        pages = range(first, first + part_pages)
        c0 = pl.multiple_of((jg * pg + first) * PAGE_SIZE, part_pages * PAGE_SIZE)
        lg = (scores(lambda h: page_stream(h, pages))
              + tab_ref[:, pl.ds(c0, part_pages * PAGE_SIZE)])
        parts.append((lg, pv(lambda h, pages=pages: page_stream(NSA_KV_HEADS + h, pages))))
    _state_update_parts(parts, m_ref, acc_ref, True)

    @pl.when(jg == pl.num_programs(1) - 1)
    def _():
        new = lambda ch: new_ref[0, ch].astype(BF16)
        j_new = lax.broadcasted_iota(jnp.int32, (rows, PAGE_SIZE), 1)
        bias_new = jnp.where((j_new <= t_q) & (j_new < n_new), slope * j_new.astype(F32), NEG_INF)
        _state_update(scores(lambda h: new(h)) + bias_new, pv(lambda h: new(NSA_KV_HEADS + h)),
                      m_ref, acc_ref, False)
        o_s = _state_result(acc_ref)

        _state_init(m_ref, acc_ref)
        i_w = lax.broadcasted_iota(jnp.int32, (rows, w_buf), 1)
        dist_w = t_q + w_buf - i_w
        valid_w = (dist_w >= 0) & (dist_w < WINDOW) & (past - w_buf + i_w >= 0)
        bias_w = jnp.where(valid_w, slope * (i_w - w_buf).astype(F32), NEG_INF)
        win = lambda ch: win_ref[0, ch].astype(BF16)
        _state_update(scores(lambda h: win(h)) + bias_w, pv(lambda h: win(NSA_KV_HEADS + h)),
                      m_ref, acc_ref, True)
        _state_update(scores(lambda h: new(2 * NSA_KV_HEADS + h)) + bias_new,
                      pv(lambda h: new(3 * NSA_KV_HEADS + h)), m_ref, acc_ref, True)
        o_w = _state_result(acc_ref)

        gates = g_ref[0]
        mix = gates[:, 0:1] * oc_ref[0] + gates[:, 1:2] * o_s + gates[:, 2:3] * o_w
        o_ref[0] = mix * _silu(z_ref[0])


def _nsa_slc_sample(cache, page_table, q_rows, sel_rows, expand, new_rows, win_state, o_c_rows, gates_rows, z_rows,
                    pg, n_new):
    n_seq, n_pages = page_table.shape
    past = n_pages * PAGE_SIZE
    w_buf = win_state.shape[2]
    rows = n_new * NSA_HEADS
    per_seq = lambda *shape: pl.BlockSpec((1,) + shape, lambda b, j, pt: (b,) + (0,) * len(shape))
    grid_spec = pltpu.PrefetchScalarGridSpec(
        num_scalar_prefetch=1,
        grid=(n_seq, n_pages // pg),
        in_specs=_nsa_page_specs(pg, 1) + [
            per_seq(rows, HEAD_DIM),
            per_seq(rows, past // SEL_BLOCK),
            pl.BlockSpec((past // SEL_BLOCK, past), lambda b, j, pt: (0, 0)),
            per_seq(4 * NSA_KV_HEADS, PAGE_SIZE, HEAD_DIM),
            per_seq(2 * NSA_KV_HEADS, w_buf, HEAD_DIM),
            per_seq(rows, HEAD_DIM),
            per_seq(rows, 3),
            per_seq(rows, HEAD_DIM)],
        out_specs=per_seq(rows, HEAD_DIM),
        scratch_shapes=[pltpu.VMEM((rows, past), F32), pltpu.VMEM((rows, LANES), F32),
                        pltpu.VMEM((rows, 2 * HEAD_DIM), F32)],
    )
    return pl.pallas_call(
        functools.partial(_nsa_slc_kernel, pg=pg, part_pages=2, n_pages=n_pages, w_buf=w_buf, n_new=n_new),
        grid_spec=grid_spec,
        out_shape=jax.ShapeDtypeStruct((n_seq, rows, HEAD_DIM), F32),
        compiler_params=_params(("parallel", "arbitrary")),
        name="nsa_sample_slc_win",
    )(page_table, *([cache] * pg), q_rows, sel_rows, expand, new_rows, win_state, o_c_rows, gates_rows, z_rows)


def _page_forget_scan(x):
    n = x.shape[1]
    lane = lax.broadcasted_iota(jnp.int32, x.shape, 1)
    s = FOX_HEADS
    while s < n:
        x = x + jnp.where(lane >= s, pltpu.roll(x, s, axis=1), 0.0)
        s *= 2
    total = jnp.where(lane >= n - FOX_HEADS, x, 0.0)
    s = FOX_HEADS
    while s < n:
        total = total + pltpu.roll(total, n - s, axis=1)
        s *= 2
    return x, total


def _fox_sample_kernel(pt_ref, *refs, pg, part_pages, n_new):
    page_refs = refs[:pg]
    (q_ref, lc_ref, kn_ref, vn_ref, ln_ref, z_ref, o_ref, m_ref, acc_ref, carry_ref, lg_ref) = refs[pg:]
    b = pl.program_id(0)
    jg = pl.program_id(1)
    rows = n_new * FOX_HEADS
    keys = PAGE_SIZE * FOX_HEADS

    @pl.when(jg == 0)
    def _():
        _state_init(m_ref, acc_ref)
        carry_ref[...] = jnp.zeros(carry_ref.shape, F32)

    q = q_ref[0].astype(BF16)
    own = (lax.broadcasted_iota(jnp.int32, (rows, keys), 1) % FOX_HEADS
           == lax.broadcasted_iota(jnp.int32, (rows, keys), 0) % FOX_HEADS)

    for u in range(pg):
        lg_ref[u:u + 1, :] = lc_ref[pl.ds(pt_ref[b, jg * pg + u], 1), :]
    within, total = _page_forget_scan(lg_ref[...])
    carry = carry_ref[...]
    logits = []
    for u in range(pg):
        d_u = within[u:u + 1] + carry
        carry = carry + total[u:u + 1]
        k = page_refs[u][0, 0, :, 0].reshape(keys, HEAD_DIM).astype(BF16)
        logits.append(jnp.where(own, _dot_nt(q, k) * SCALE - d_u, NEG_INF))
    carry_ref[...] = carry

    def pv(pages):
        def pv_of(p):
            out = None
            for i, u in enumerate(pages):
                v = page_refs[u][0, 0, :, 1].reshape(keys, HEAD_DIM).astype(BF16)
                part = _dot(p[:, i * keys:(i + 1) * keys], _with_ones(v))
                out = part if out is None else out + part
            return out
        return pv_of

    parts = []
    for first in range(0, pg, part_pages):
        pages = range(first, first + part_pages)
        parts.append((jnp.concatenate([logits[u] for u in pages], axis=1), pv(pages)))
    _state_update_parts(parts, m_ref, acc_ref, False)

    @pl.when(jg == pl.num_programs(1) - 1)
    def _():
        within_new, _ = _page_forget_scan(ln_ref[0])
        d_new = (within_new + carry_ref[...])[:, :PAGE_SIZE]
        c = lax.broadcasted_iota(jnp.int32, (rows, PAGE_SIZE), 1)
        r = lax.broadcasted_iota(jnp.int32, (rows, PAGE_SIZE), 0)
        valid = (c < rows) & (c % FOX_HEADS == r % FOX_HEADS) & (c // FOX_HEADS <= r // FOX_HEADS)
        lg = jnp.where(valid, _dot_nt(q, kn_ref[0].astype(BF16)) * SCALE - d_new, NEG_INF)
        _state_update(lg, lambda p: _dot(p, _with_ones(vn_ref[0].astype(BF16))), m_ref, acc_ref, False)
        o_ref[0] = _state_result(acc_ref) * _silu(z_ref[0])


def _fox_sample(cache, page_table, q_rows, logf_cache, k_new, v_new, logf_new, z_rows, pg, n_new):
    n_seq, n_pages = page_table.shape
    n_pool = logf_cache.shape[0]
    rows = n_new * FOX_HEADS
    keys = PAGE_SIZE * FOX_HEADS
    per_seq = lambda *shape: pl.BlockSpec((1,) + shape, lambda b, j, pt: (b,) + (0,) * len(shape))

    def page_spec(u):
        return pl.BlockSpec((1, 1, PAGE_SIZE, 2, FOX_HEADS, HEAD_DIM),
                            lambda b, j, pt: (0, pt[b, j * pg + u], 0, 0, 0, 0))

    grid_spec = pltpu.PrefetchScalarGridSpec(
        num_scalar_prefetch=1,
        grid=(n_seq, n_pages // pg),
        in_specs=[page_spec(u) for u in range(pg)] + [
            per_seq(rows, HEAD_DIM),
            pl.BlockSpec((n_pool, keys), lambda b, j, pt: (0, 0), pipeline_mode=pl.Buffered(1)),
            per_seq(PAGE_SIZE, HEAD_DIM),
            per_seq(PAGE_SIZE, HEAD_DIM),
            per_seq(1, keys),
            per_seq(rows, HEAD_DIM)],
        out_specs=per_seq(rows, HEAD_DIM),
        scratch_shapes=[pltpu.VMEM((rows, LANES), F32), pltpu.VMEM((rows, 2 * HEAD_DIM), F32),
                        pltpu.VMEM((1, keys), F32), pltpu.VMEM((pg, keys), F32)],
    )
    return pl.pallas_call(
        functools.partial(_fox_sample_kernel, pg=pg, part_pages=2, n_new=n_new),
        grid_spec=grid_spec,
        out_shape=jax.ShapeDtypeStruct((n_seq, rows, HEAD_DIM), F32),
        compiler_params=_params(("parallel", "arbitrary")),
        name="fox_sample",
    )(page_table, *([cache] * pg), q_rows, logf_cache, k_new, v_new, logf_new, z_rows)


def _held_pages(page_table, needed, pg):
    n_seq, n_pages = page_table.shape
    slots = page_table.reshape(n_seq * n_pages // pg, pg)
    want = needed.reshape(slots.shape).at[0].set(True)
    step = jnp.arange(slots.shape[0], dtype=jnp.int32)[:, None]
    filled_at = lax.cummax(jnp.where(want, step, 0), axis=0)
    return jnp.take_along_axis(slots, filled_at, axis=0).reshape(n_seq, n_pages)


def _pad_axis(a, axis, size):
    pad = [(0, 0)] * a.ndim
    pad[axis] = (0, size - a.shape[axis])
    return jnp.pad(a, pad)


def _sample_layer(x, cache_nsa_kv, cache_fox_kv, cache_fox_logf, state_nsa_win, page_table,
                  norm_in, wts, cmp_pe, cmp_w_bf, w_out_bf, norm_final):
    n_seq, t_new, d = x.shape
    n_pool = cache_nsa_kv.shape[1]
    n_pages = page_table.shape[1]
    past = n_pages * PAGE_SIZE
    w_buf = state_nsa_win.shape[2]
    assert t_new <= T_PAD and t_new < SEL_BLOCK and past % SEL_BLOCK == 0
    x2d = x.reshape(n_seq * t_new, d)
    proj = _project_all(x2d, norm_in, wts, n_seq * t_new)
    n_gate = 3 * NSA_HEADS
    small = proj["small"].reshape(n_seq, t_new, LANES)
    logf = small[:, :, n_gate:n_gate + FOX_HEADS]

    def head_rows(a):
        a = _pad_axis(a.reshape(n_seq, t_new, NSA_HEADS, -1), 1, T_PAD)
        return jnp.swapaxes(a, 1, 2).reshape(n_seq, NSA_HEADS * T_PAD, a.shape[-1])

    def group_rows(a):
        a = a.reshape(n_seq, t_new, NSA_KV_HEADS, NSA_GROUP, a.shape[-1])
        return jnp.swapaxes(a, 1, 2).reshape(n_seq, NSA_HEADS * t_new, a.shape[-1])

    q_n = proj["q_n"].astype(F32)
    gates = small[:, :, :n_gate].reshape(n_seq, t_new, NSA_HEADS, 3)
    z_n = proj["z_n"].reshape(n_seq, t_new, NSA_HEADS, HEAD_DIM)
    nsa_new = jnp.concatenate([proj["kv4"].reshape(n_seq, t_new, 4 * NSA_KV_HEADS, HEAD_DIM)[:, :, 2 * NSA_KV_HEADS:],
                               proj["kvwin"].reshape(n_seq, t_new, 2 * NSA_KV_HEADS, HEAD_DIM)], axis=2)
    nsa_new = _pad_axis(jnp.swapaxes(nsa_new, 1, 2), 2, PAGE_SIZE)
    win_state = jnp.swapaxes(state_nsa_win[0].reshape(n_seq, w_buf, 2 * NSA_KV_HEADS, HEAD_DIM), 1, 2)
    blk_of_key = jnp.arange(past, dtype=jnp.int32) // SEL_BLOCK
    expand = (jnp.arange(past // SEL_BLOCK, dtype=jnp.int32)[:, None] == blk_of_key[None, :]).astype(BF16)
    nsa_cache = cache_nsa_kv.reshape(cache_nsa_kv.shape[0], n_pool, PAGE_SIZE, 2, HALF_CHANNELS, HEAD_DIM)

    pg = 16
    o_c, sel = _nsa_cmp_sample(nsa_cache, page_table, head_rows(q_n), cmp_pe, cmp_w_bf, pg)
    o_c = jnp.swapaxes(o_c.reshape(n_seq, NSA_HEADS, T_PAD, HEAD_DIM)[:, :, :t_new], 1, 2)
    sel = sel.reshape(n_seq, NSA_KV_HEADS, T_PAD, past // SEL_BLOCK)[:, :, :t_new]
    page_needed = jnp.any(sel.reshape(n_seq, -1, n_pages, PAGE_SIZE // SEL_BLOCK) > 0.5, axis=(1, 3))
    slc_pages = _held_pages(page_table, page_needed, pg)
    sel = jnp.broadcast_to(sel[:, :, :, None, :], (n_seq, NSA_KV_HEADS, t_new, NSA_GROUP, past // SEL_BLOCK))
    mix_n = _nsa_slc_sample(nsa_cache, slc_pages, group_rows(q_n.reshape(n_seq, t_new, NSA_HEADS, HEAD_DIM)),
                            sel.reshape(n_seq, NSA_HEADS * t_new, past // SEL_BLOCK), expand, nsa_new, win_state,
                            group_rows(o_c), group_rows(gates), group_rows(z_n), pg, t_new)
    mix_n = jnp.swapaxes(mix_n.reshape(n_seq, NSA_KV_HEADS, t_new, NSA_GROUP * HEAD_DIM), 1, 2)
    mix_n = mix_n.reshape(n_seq * t_new, NSA_WIDTH).astype(BF16)

    fox_rows = lambda a: a.reshape(n_seq, t_new * FOX_HEADS, HEAD_DIM)
    kv_f = proj["kv_f"].reshape(n_seq, t_new, 2, FOX_HEADS * HEAD_DIM)
    k_new = _pad_axis(fox_rows(kv_f[:, :, 0]), 1, PAGE_SIZE)
    v_new = _pad_axis(fox_rows(kv_f[:, :, 1]), 1, PAGE_SIZE)
    logf_new = _pad_axis(logf.reshape(n_seq, 1, t_new * FOX_HEADS), 2, PAGE_SIZE * FOX_HEADS)
    logf_cache = cache_fox_logf[0].reshape(n_pool, PAGE_SIZE * FOX_HEADS)
    mix_f = _fox_sample(cache_fox_kv, page_table, fox_rows(proj["q_f"].astype(F32)), logf_cache, k_new, v_new,
                        logf_new, fox_rows(proj["z_f"]), 16, t_new)
    mix_f = mix_f.reshape(n_seq * t_new, FOX_WIDTH).astype(BF16)

    y = _merge(x2d, mix_n, mix_f, w_out_bf, norm_final, n_seq * t_new).reshape(n_seq, t_new, d)
    kvwin_new = proj["kvwin"].reshape(n_seq, t_new, 2, NSA_KV_HEADS, HEAD_DIM)
    state = (proj["kv4"].reshape(n_seq, t_new, 4, NSA_KV_HEADS, HEAD_DIM),
             proj["kv_f"].reshape(n_seq, t_new, 2, FOX_HEADS, HEAD_DIM),
             logf,
             jnp.concatenate([state_nsa_win[0], kvwin_new], axis=1)[:, t_new:])
    return y, state


def kernel(x_prompt, x_sample, cache_nsa_kv, cache_fox_kv, cache_fox_logf, state_nsa_win, page_table,
           norm_in, w_in, b_gate, b_forget, cmp_pe, cmp_w, w_out, norm_final):
    assert norm_in.shape[0] == 1, "single-layer trunk"
    wts = _split_weights(w_in[0], b_gate[0], b_forget[0])
    cmp_w_bf = cmp_w[0].reshape(2, CMP_BLOCK * HEAD_DIM, HEAD_DIM).astype(BF16)
    w_out_bf = w_out[0].astype(BF16)
    y_p, st_p = _prompt_layer(x_prompt, norm_in[0], wts, cmp_pe[0], cmp_w_bf, w_out_bf, norm_final)
    y_s, st_s = _sample_layer(x_sample, cache_nsa_kv, cache_fox_kv, cache_fox_logf, state_nsa_win,
                              page_table, norm_in[0], wts, cmp_pe[0], cmp_w_bf, w_out_bf, norm_final)
    outs = [y_p, y_s]
    for s_p, s_s in zip(st_p, st_s):
        outs.extend([s_p[None], s_s[None]])
    return tuple(outs)
```

```python
import functools

import jax
import jax.numpy as jnp
from jax import lax
from jax.experimental import pallas as pl
from jax.experimental.pallas import tpu as pltpu

F32 = jnp.float32
BF16 = jnp.bfloat16
NEG_INF = float("-inf")

HEAD_DIM = 128
NSA_HEADS = 8
FOX_HEADS = 8
NSA_KV_HEADS = 2
NSA_GROUP = NSA_HEADS // NSA_KV_HEADS
NSA_WIDTH = NSA_HEADS * HEAD_DIM
FOX_WIDTH = FOX_HEADS * HEAD_DIM
KV_WIDTH = NSA_KV_HEADS * HEAD_DIM
NSA_CHANNELS = 4 * NSA_KV_HEADS
CMP_BLOCK = 32
SEL_BLOCK = 64
SEL_TOPK = 16
WINDOW = 512
PAGE_SIZE = 128
FORCED_SCORE = 1e4
RMS_EPS = 1e-6
SCALE = HEAD_DIM ** -0.5
LOG2E = 1.4426950408889634
T_PAD = 8
LANES = 128
SUBLANES = 8
PROJ_TN = 512
STAGE_PITCH = 40
VMEM_LIMIT = 56 * 1024 * 1024


def _params(sem):
    return pltpu.CompilerParams(dimension_semantics=sem, vmem_limit_bytes=VMEM_LIMIT)


def _dot(a, b):
    return jnp.dot(a, b, preferred_element_type=F32)


def _dot_nt(a, b):
    return lax.dot_general(a, b, (((1,), (1,)), ((), ())), preferred_element_type=F32)


def _sigmoid(x):
    return 1.0 / (1.0 + jnp.exp(-x))


def _silu(x):
    return x * _sigmoid(x)


def _tile_lanes(x, n):
    return x if n == 1 else jnp.concatenate([x] * n, axis=1)


def _with_ones(v):
    return jnp.concatenate([v, jnp.ones(v.shape, v.dtype)], axis=1)


def _state_init(m_ref, acc_ref):
    m_ref[...] = jnp.full(m_ref.shape, NEG_INF, F32)
    acc_ref[...] = jnp.zeros(acc_ref.shape, F32)


def _probs(lg, m_old, may_be_empty, exp=jnp.exp):
    m_new = jnp.maximum(m_old, jnp.max(lg, axis=1, keepdims=True))
    m_use = jnp.where(m_new == NEG_INF, 0.0, m_new) if may_be_empty else m_new
    alpha = exp(m_old - m_use)
    p = exp(lg - _tile_lanes(m_use, lg.shape[1] // LANES)).astype(BF16)
    return m_new, alpha, p


def _state_update_chunked(logits_of, n_chunks, chunk, v_ext, m_ref, acc_ref, p_ref, alpha_ref, may_be_empty):
    for c in range(n_chunks):
        rows = slice(c * chunk, (c + 1) * chunk)
        m_new, alpha, p = _probs(logits_of(c), m_ref[rows, :], may_be_empty, jnp.exp2)
        alpha_ref[rows, :] = alpha
        p_ref[rows, :] = p
        m_ref[rows, :] = m_new
    acc_ref[...] = _tile_lanes(alpha_ref[...], 2) * acc_ref[...] + _dot(p_ref[...], v_ext)


def _state_update(lg, pv_of, m_ref, acc_ref, may_be_empty):
    m_new, alpha, p = _probs(lg, m_ref[...], may_be_empty)
    acc_ref[...] = _tile_lanes(alpha, 2) * acc_ref[...] + pv_of(p)
    m_ref[...] = m_new


def _state_update_parts(parts, m_ref, acc_ref, may_be_empty):
    local = []
    for lg, pv_of in parts:
        m_c = jnp.broadcast_to(jnp.max(lg, axis=1, keepdims=True), m_ref.shape)
        m_use = jnp.where(m_c == NEG_INF, 0.0, m_c) if may_be_empty else m_c
        p = jnp.exp(lg - _tile_lanes(m_use, lg.shape[1] // LANES)).astype(BF16)
        local.append((m_c, pv_of(p)))
    m_old = m_ref[...]
    m_new = m_old
    for m_c, _ in local:
        m_new = jnp.maximum(m_new, m_c)
    acc = _tile_lanes(jnp.exp(m_old - m_new), 2) * acc_ref[...]
    for m_c, pv in local:
        acc = acc + _tile_lanes(jnp.exp(m_c - m_new), 2) * pv
    acc_ref[...] = acc
    m_ref[...] = m_new


def _state_result(acc_ref):
    acc = acc_ref[...]
    return acc[:, :HEAD_DIM] / jnp.maximum(acc[:, HEAD_DIM:], 1e-30)


def _lane_cumsum(x):
    n = x.shape[-1]
    lane = lax.broadcasted_iota(jnp.int32, x.shape, x.ndim - 1)
    s = 1
    while s < n:
        x = x + jnp.where(lane >= s, pltpu.roll(x, s, axis=x.ndim - 1), 0.0)
        s *= 2
    return x


def _head_slope(head):
    if isinstance(head, int):
        return 2.0 ** -(head + 1)
    return lax.bitcast_convert_type((126 - head) << 23, F32)


def _rms_kernel(x_ref, g_ref, o_ref):
    x = x_ref[...]
    ms = jnp.mean(x * x, axis=-1, keepdims=True)
    o_ref[...] = (x * lax.rsqrt(ms + RMS_EPS) * g_ref[...]).astype(o_ref.dtype)


def _rmsnorm(x2d, g, out_dtype, tm):
    m, d = x2d.shape
    return pl.pallas_call(
        _rms_kernel,
        grid=(m // tm,),
        in_specs=[pl.BlockSpec((tm, d), lambda i: (i, 0)), pl.BlockSpec((1, d), lambda i: (0, 0))],
        out_specs=pl.BlockSpec((tm, d), lambda i: (i, 0)),
        out_shape=jax.ShapeDtypeStruct((m, d), out_dtype),
        compiler_params=_params(("parallel",)),
        name="rmsnorm",
    )(x2d, g.reshape(1, d))


def _proj_kernel(h_ref, w_ref, *o_refs):
    acc = _dot(h_ref[...], w_ref[...])
    for o_ref in o_refs:
        o_ref[...] = acc.astype(o_ref.dtype)


def _project(h, w, col0, n, out_dtypes, tm, name):
    m, d = h.shape
    j0 = col0 // PROJ_TN
    return pl.pallas_call(
        _proj_kernel,
        grid=(m // tm, n // PROJ_TN),
        in_specs=[pl.BlockSpec((tm, d), lambda i, j: (i, 0)), pl.BlockSpec((d, PROJ_TN), lambda i, j: (0, j0 + j))],
        out_specs=[pl.BlockSpec((tm, PROJ_TN), lambda i, j: (i, j)) for _ in out_dtypes],
        out_shape=[jax.ShapeDtypeStruct((m, n), dt) for dt in out_dtypes],
        compiler_params=_params(("parallel", "arbitrary")),
        name=name,
    )(h, w)


def _small_kernel(h_ref, w_ref, b_ref, o_ref, *, n_gate):
    z = _dot(h_ref[...], w_ref[...].astype(BF16)) + b_ref[...]
    lane = lax.broadcasted_iota(jnp.int32, z.shape, 1)
    log_sig = jnp.minimum(z, 0.0) - jnp.log1p(jnp.exp(-jnp.abs(z)))
    o_ref[...] = jnp.where(lane < n_gate, _sigmoid(z), log_sig)


def _project_small(h, w, b, tm, n_gate):
    m, d = h.shape
    return pl.pallas_call(
        functools.partial(_small_kernel, n_gate=n_gate),
        grid=(m // tm,),
        in_specs=[pl.BlockSpec((tm, d), lambda i: (i, 0)), pl.BlockSpec((d, LANES), lambda i: (0, 0)),
                  pl.BlockSpec((1, LANES), lambda i: (0, 0))],
        out_specs=pl.BlockSpec((tm, LANES), lambda i: (i, 0)),
        out_shape=jax.ShapeDtypeStruct((m, LANES), F32),
        compiler_params=_params(("parallel",)),
        name="proj_gates_logf",
    )(h, w, b)


PROJ_GROUPS = (("q_n", NSA_WIDTH), ("kv4", 4 * KV_WIDTH), ("kvwin", 2 * KV_WIDTH), ("z_n", NSA_WIDTH),
               ("q_f", FOX_WIDTH), ("kv_f", 2 * FOX_WIDTH), ("z_f", FOX_WIDTH))
PROJ_OUT_DTYPES = {"q_n": (BF16,), "kv4": (F32, BF16), "kvwin": (F32, BF16), "z_n": (F32,),
                   "q_f": (BF16,), "kv_f": (F32, BF16), "z_f": (F32,)}


def _regroup_kernel(wt_ref, o_ref):
    o_ref[...] = wt_ref[...].T.astype(o_ref.dtype)


def _regroup_columns(w_t, spans):
    cols, rows = w_t.shape
    starts, col = [], 0
    for a, b in spans:
        assert col % PROJ_TN == 0 and (b - a) % PROJ_TN == 0 and a % SUBLANES == 0
        starts.append((col // PROJ_TN, a - col))
        col += b - a

    n_tiles = col // PROJ_TN

    def source_row(j):
        units = j * (PROJ_TN // SUBLANES) + starts[0][1] // SUBLANES
        for (first_tile, s), (_, s_prev) in zip(starts[1:], starts[:-1]):
            units = units + ((j + n_tiles - first_tile) // n_tiles) * ((s - s_prev) // SUBLANES)
        return units * SUBLANES

    return pl.pallas_call(
        _regroup_kernel,
        grid=(col // PROJ_TN,),
        in_specs=[pl.BlockSpec((pl.Element(PROJ_TN), pl.Element(rows)), lambda j: (source_row(j), 0))],
        out_specs=pl.BlockSpec((rows, PROJ_TN), lambda j: (0, j)),
        out_shape=jax.ShapeDtypeStruct((rows, col), BF16),
        compiler_params=_params(("parallel",)),
        name="regroup_weights",
    )(w_t)


def _split_weights(w_in, b_gate, b_forget):
    cuts = [NSA_WIDTH, 6 * KV_WIDTH, 3 * NSA_HEADS, NSA_WIDTH, FOX_WIDTH, FOX_WIDTH, FOX_WIDTH, FOX_HEADS, FOX_WIDTH]
    offs = [0]
    for c in cuts:
        offs.append(offs[-1] + c)
    w_t = jnp.swapaxes(w_in, 0, 1)
    main = _regroup_columns(w_t, ((offs[0], offs[2]), (offs[3], offs[7]), (offs[8], offs[9])))
    n_small = 3 * NSA_HEADS + FOX_HEADS
    w_small = jnp.concatenate([w_t[offs[2]:offs[3]], w_t[offs[7]:offs[8]]], axis=0).T
    w_small = jnp.pad(w_small, ((0, 0), (0, LANES - n_small)))
    b_small = jnp.pad(jnp.concatenate([b_gate, b_forget]), (0, LANES - n_small)).reshape(1, LANES).astype(F32)
    return {"main": main, "small": w_small, "b_small": b_small}


def _proj_kv4_kernel(h_ref, w_ref, chan_ref, cmp_ref, bf_ref):
    acc = _dot(h_ref[...], w_ref[...])
    for c in range(NSA_CHANNELS):
        chan_ref[:, c, :] = acc[:, c * HEAD_DIM:(c + 1) * HEAD_DIM]
    cmp_ref[...] = acc[:, :2 * KV_WIDTH]
    bf_ref[...] = acc.astype(bf_ref.dtype)


def _project_kv4(h, w, col0, tm):
    m, d = h.shape
    n = 4 * KV_WIDTH
    assert col0 % n == 0
    return pl.pallas_call(
        _proj_kv4_kernel,
        grid=(m // tm,),
        in_specs=[pl.BlockSpec((tm, d), lambda i: (i, 0)), pl.BlockSpec((d, n), lambda i: (0, col0 // n))],
        out_specs=[pl.BlockSpec((tm, NSA_CHANNELS, HEAD_DIM), lambda i: (i, 0, 0)),
                   pl.BlockSpec((tm, 2 * KV_WIDTH), lambda i: (i, 0)),
                   pl.BlockSpec((tm, n), lambda i: (i, 0))],
        out_shape=[jax.ShapeDtypeStruct((m, NSA_CHANNELS, HEAD_DIM), F32),
                   jax.ShapeDtypeStruct((m, 2 * KV_WIDTH), F32),
                   jax.ShapeDtypeStruct((m, n), BF16)],
        compiler_params=_params(("parallel",)),
        name="proj_kv4_state",
    )(h, w)


def _project_all(x2d, norm_in, wts, tm, kv4_state_order=False):
    h = _rmsnorm(x2d, norm_in, BF16, min(tm, 512))
    out = {}
    col0 = 0
    for name, width in PROJ_GROUPS:
        if name == "kv4" and kv4_state_order:
            out["kv4_state"], out["kv4_cmp"], out["kv4_bf"] = _project_kv4(h, wts["main"], col0, tm)
            col0 += width
            continue
        res = _project(h, wts["main"], col0, width, PROJ_OUT_DTYPES[name], tm, "proj_" + name)
        out[name] = res[0]
        if len(res) > 1:
            out[name + "_bf"] = res[1]
        col0 += width
    out["small"] = _project_small(h, wts["small"], wts["b_small"], min(tm, 512), 3 * NSA_HEADS)
    return out


def _merge_kernel(x_ref, mn_ref, mf_ref, wn_ref, wf_ref, g_ref, o_ref):
    y = x_ref[...] + _dot(mn_ref[...], wn_ref[...]) + _dot(mf_ref[...], wf_ref[...])
    ms = jnp.mean(y * y, axis=-1, keepdims=True)
    o_ref[...] = y * lax.rsqrt(ms + RMS_EPS) * g_ref[...]


def _merge(x2d, mix_n, mix_f, w_out_bf, norm_final, tm):
    m, d = x2d.shape
    return pl.pallas_call(
        _merge_kernel,
        grid=(m // tm,),
        in_specs=[pl.BlockSpec((tm, d), lambda i: (i, 0)),
                  pl.BlockSpec((tm, NSA_WIDTH), lambda i: (i, 0)),
                  pl.BlockSpec((tm, FOX_WIDTH), lambda i: (i, 0)),
                  pl.BlockSpec((NSA_WIDTH, d), lambda i: (0, 0)),
                  pl.BlockSpec((FOX_WIDTH, d), lambda i: (NSA_WIDTH // FOX_WIDTH, 0)),
                  pl.BlockSpec((1, d), lambda i: (0, 0))],
        out_specs=pl.BlockSpec((tm, d), lambda i: (i, 0)),
        out_shape=jax.ShapeDtypeStruct((m, d), F32),
        compiler_params=_params(("parallel",)),
        name="merge_out_proj",
    )(x2d, mix_n, mix_f, w_out_bf, w_out_bf, norm_final.reshape(1, d))


def _cumsum_kernel(x_ref, o_ref):
    o_ref[...] = _lane_cumsum(x_ref[...])


def _cumsum_rows(x):
    return pl.pallas_call(
        _cumsum_kernel,
        out_shape=jax.ShapeDtypeStruct(x.shape, F32),
        name="forget_cumsum",
    )(x)


def _compress_rows(x_ref, pe_ref, kind, w, xs_ref, nb, starts, stride):
    half = nb // 2
    for c in range(CMP_BLOCK):
        pe_c = pe_ref[kind, c:c + 1, :]
        for par in range(2):
            xc = x_ref[pl.ds(starts[par] + c, half, stride=stride), :]
            xs_ref[par * half:(par + 1) * half, c * HEAD_DIM:(c + 1) * HEAD_DIM] = (xc + pe_c).astype(BF16)
    return _dot(xs_ref[...], w)


def _compress_kernel(x_ref, pe_ref, w_ref, o_ref, xs_ref, *, nb):
    out = _compress_rows(x_ref, pe_ref, 0, w_ref[0], xs_ref, nb, (0, CMP_BLOCK), 2 * CMP_BLOCK)
    o_ref[0, 0, 0] = out.astype(o_ref.dtype)


def _compress_prompt(kv4, cmp_pe, cmp_w_bf, batch, seq):
    nb = seq // CMP_BLOCK
    return pl.pallas_call(
        functools.partial(_compress_kernel, nb=nb),
        grid=(batch, 2, NSA_KV_HEADS),
        in_specs=[pl.BlockSpec((seq, HEAD_DIM), lambda b, k, h: (b, k * NSA_KV_HEADS + h)),
                  pl.BlockSpec((1, CMP_BLOCK, HEAD_DIM), lambda b, k, h: (k, 0, 0)),
                  pl.BlockSpec((1, CMP_BLOCK * HEAD_DIM, HEAD_DIM), lambda b, k, h: (k, 0, 0))],
        out_specs=pl.BlockSpec((1, 1, 1, nb, HEAD_DIM), lambda b, k, h: (b, k, h, 0, 0)),
        out_shape=jax.ShapeDtypeStruct((batch, 2, NSA_KV_HEADS, nb, HEAD_DIM), BF16),
        scratch_shapes=[pltpu.VMEM((nb, CMP_BLOCK * HEAD_DIM), BF16)],
        compiler_params=_params(("parallel", "parallel", "arbitrary")),
        name="compress_prompt",
    )(kv4, cmp_pe, cmp_w_bf)


def _select_blocks(score_t, cur, n_keep):
    n_blk = score_t.shape[0]
    j = lax.broadcasted_iota(jnp.int32, score_t.shape, 0)
    forced = (j == 0) | (j == cur) | (j == cur - 1)
    s = jnp.where(j <= cur, jnp.where(forced, FORCED_SCORE, score_t), -1.0)
    n_grp = n_blk // SUBLANES
    grp = [s[r * SUBLANES:(r + 1) * SUBLANES] for r in range(n_grp)]
    cnt = [jnp.zeros(grp[0].shape, F32) for _ in range(n_grp)]
    sub = lax.broadcasted_iota(jnp.int32, grp[0].shape, 0)
    for jj in range(n_blk):
        row = s[jj:jj + 1, :]
        r_j, off = divmod(jj, SUBLANES)
        for r in range(n_grp):
            if r < r_j:
                hit = row > grp[r]
            elif r > r_j:
                hit = row >= grp[r]
            else:
                hit = (row > grp[r]) | ((row == grp[r]) & (sub > off))
            cnt[r] = cnt[r] + jnp.where(hit, 1.0, 0.0)
    return jnp.where((jnp.concatenate(cnt, axis=0) < n_keep) & (j <= cur), 1.0, 0.0)


def _nsa_prompt_kernel(q_ref, kc_ref, vc_ref, ks_ref, vs_ref, kw_ref, vw_ref, g_ref, z_ref, e_ref, o_ref,
                       m_ref, acc_ref, alpha_ref, ps_ref, pw_ref, mix_ref, bias_ref, *, tq, tk, tw, ck, seq):
    qi = pl.program_id(1)
    s0 = qi * tq
    n_cmp = seq // CMP_BLOCK
    n_sel = seq // SEL_BLOCK
    row_t = s0 + lax.broadcasted_iota(jnp.int32, (tq, 1), 0)
    kv_heads = range(NSA_KV_HEADS)
    heads = lambda hkv: [hkv * NSA_GROUP + g for g in range(NSA_GROUP)]
    slopes = lambda hkv: [_head_slope(h) for h in heads(hkv)]
    cols = lambda hkv: slice(hkv * HEAD_DIM, (hkv + 1) * HEAD_DIM)

    def q_rows(hkv):
        return jnp.concatenate([q_ref[:, h * HEAD_DIM:(h + 1) * HEAD_DIM] for h in heads(hkv)], axis=0)

    def gate_col(hkv, br):
        return jnp.concatenate([g_ref[:, 3 * h + br:3 * h + br + 1] for h in heads(hkv)], axis=0)

    sels = []
    for hkv in kv_heads:
        sc = _dot_nt(q_rows(hkv), kc_ref[0, 0, hkv]) * SCALE
        lane = lax.broadcasted_iota(jnp.int32, (tq, n_cmp), 1)
        blk = jnp.where(lane < n_cmp // 2, 2 * lane, 2 * lane - (n_cmp - 1))
        dist_c = (row_t - ((blk + 1) * CMP_BLOCK - 1)).astype(F32)
        valid_c = dist_c >= 0
        score = jnp.zeros((tq, n_cmp), F32)
        probs = []
        for g, slope in enumerate(slopes(hkv)):
            lg = jnp.where(valid_c, sc[g * tq:(g + 1) * tq] - slope * dist_c, NEG_INF)
            mx = jnp.max(lg, axis=1, keepdims=True)
            mx = jnp.where(mx == NEG_INF, 0.0, mx)
            p = jnp.exp(lg - mx)
            p = p / jnp.maximum(jnp.sum(p, axis=1, keepdims=True), 1e-30)
            score = score + p
            probs.append(p)
        o_c = _dot(jnp.concatenate(probs, axis=0).astype(BF16), vc_ref[0, 0, hkv])
        mix_ref[hkv] = gate_col(hkv, 0) * o_c

        score_t = score.T
        score_t = score_t[:n_sel] + score_t[n_sel:]
        cur = (s0 + lax.broadcasted_iota(jnp.int32, (n_sel, tq), 1)) // SEL_BLOCK
        sel_t = _select_blocks(score_t, cur, SEL_TOPK)
        sels.append(jnp.concatenate([sel_t, jnp.zeros((LANES - n_sel, tq), F32)], axis=0).T.astype(BF16))

    def bias_body(c, carry):
        c0 = pl.multiple_of(c * ck, ck)
        causal = c0 + lax.broadcasted_iota(jnp.int32, (tq, ck), 1) <= row_t
        for hkv in kv_heads:
            expanded = _dot(sels[hkv], e_ref[:, pl.ds(c0, ck)])
            bias_ref[hkv, :, pl.ds(c0, ck)] = jnp.where((expanded > 0.5) & causal, 0.0, NEG_INF)
        return carry

    lax.fori_loop(0, (s0 + tq + ck - 1) // ck, bias_body, 0)

    def stream(hkv, k, bias, key_rel, v, p_ref, may_be_empty):
        s = _dot_nt(q_rows(hkv), k)
        sl = slopes(hkv)
        logits_of = lambda g: s[g * tq:(g + 1) * tq] * (SCALE * LOG2E) + (bias + (sl[g] * LOG2E) * key_rel)
        _state_update_chunked(logits_of, NSA_GROUP, tq, _with_ones(v), m_ref.at[hkv], acc_ref.at[hkv],
                              p_ref.at[hkv], alpha_ref.at[hkv], may_be_empty)

    _state_init(m_ref, acc_ref)

    def slc_body(i, carry):
        k0 = pl.multiple_of(i * tk, tk)
        key_rel = (k0 - s0 + lax.broadcasted_iota(jnp.int32, (1, tk), 1)).astype(F32)
        for hkv in kv_heads:
            stream(hkv, ks_ref[pl.ds(k0, tk), cols(hkv)], bias_ref[hkv, :, pl.ds(k0, tk)], key_rel,
                   vs_ref[pl.ds(k0, tk), cols(hkv)], ps_ref, False)
        return carry

    lax.fori_loop(0, (s0 + tq + tk - 1) // tk, slc_body, 0)
    for hkv in kv_heads:
        mix_ref[hkv] = mix_ref[hkv] + gate_col(hkv, 1) * _state_result(acc_ref.at[hkv])

    _state_init(m_ref, acc_ref)

    def win_body(w, carry):
        k0 = pl.multiple_of(s0 - WINDOW + w * tw, tw)
        disti = row_t - (k0 + lax.broadcasted_iota(jnp.int32, (tq, tw), 1))
        bias = jnp.where((disti >= 0) & (disti < WINDOW), 0.0, NEG_INF)
        key_rel = (k0 - s0 + lax.broadcasted_iota(jnp.int32, (1, tw), 1)).astype(F32)
        for hkv in kv_heads:
            stream(hkv, kw_ref[pl.ds(k0, tw), cols(hkv)], bias, key_rel,
                   vw_ref[pl.ds(k0, tw), cols(hkv)], pw_ref, True)
        return carry

    n_win_tiles = (WINDOW + tq) // tw
    lax.fori_loop(jnp.maximum(0, (WINDOW - s0) // tw), n_win_tiles, win_body, 0)

    for hkv in kv_heads:
        mix = mix_ref[hkv] + gate_col(hkv, 2) * _state_result(acc_ref.at[hkv])
        for g, h in enumerate(heads(hkv)):
            z = z_ref[:, h * HEAD_DIM:(h + 1) * HEAD_DIM]
            o_ref[:, h * HEAD_DIM:(h + 1) * HEAD_DIM] = (mix[g * tq:(g + 1) * tq] * _silu(z)).astype(o_ref.dtype)


def _nsa_prompt(proj, kcv, expand, batch, seq):
    tq, tk, tw, ck = 256, 256, 256, 512
    assert seq % ck == 0 and WINDOW % tw == 0 and tq % tw == 0 and seq // SEL_BLOCK <= LANES
    nq = seq // tq
    rows = NSA_GROUP * tq
    kernel = functools.partial(_nsa_prompt_kernel, tq=tq, tk=tk, tw=tw, ck=ck, seq=seq)
    cmp_spec = lambda kind: pl.BlockSpec((1, 1, NSA_KV_HEADS, seq // CMP_BLOCK, HEAD_DIM),
                                         lambda b, i: (b, kind, 0, 0, 0))
    return pl.pallas_call(
        kernel,
        grid=(batch, nq),
        in_specs=[pl.BlockSpec((tq, NSA_WIDTH), lambda b, i: (b * nq + i, 0)),
                  cmp_spec(0), cmp_spec(1),
                  pl.BlockSpec((seq, KV_WIDTH), lambda b, i: (b, 2)),
                  pl.BlockSpec((seq, KV_WIDTH), lambda b, i: (b, 3)),
                  pl.BlockSpec((seq, KV_WIDTH), lambda b, i: (b, 0)),
                  pl.BlockSpec((seq, KV_WIDTH), lambda b, i: (b, 1)),
                  pl.BlockSpec((tq, LANES), lambda b, i: (b * nq + i, 0)),
                  pl.BlockSpec((tq, NSA_WIDTH), lambda b, i: (b * nq + i, 0)),
                  pl.BlockSpec((LANES, seq), lambda b, i: (0, 0))],
        out_specs=pl.BlockSpec((tq, NSA_WIDTH), lambda b, i: (b * nq + i, 0)),
        out_shape=jax.ShapeDtypeStruct((batch * seq, NSA_WIDTH), BF16),
        scratch_shapes=[pltpu.VMEM((NSA_KV_HEADS, rows, LANES), F32),
                        pltpu.VMEM((NSA_KV_HEADS, rows, 2 * HEAD_DIM), F32),
                        pltpu.VMEM((NSA_KV_HEADS, rows, LANES), F32),
                        pltpu.VMEM((NSA_KV_HEADS, rows, tk), BF16),
                        pltpu.VMEM((NSA_KV_HEADS, rows, tw), BF16),
                        pltpu.VMEM((NSA_KV_HEADS, rows, HEAD_DIM), F32),
                        pltpu.VMEM((NSA_KV_HEADS, tq, seq), F32)],
        compiler_params=_params(("parallel", "arbitrary")),
        name="nsa_prompt",
    )(proj["q_n"], kcv, kcv, proj["kv4_bf"], proj["kv4_bf"], proj["kvwin_bf"], proj["kvwin_bf"],
      proj["small"], proj["z_n"], expand)


def _fox_prompt_kernel(q_ref, k_ref, v_ref, d_ref, z_ref, o_ref, m_ref, acc_ref, alpha_ref, p_ref, *,
                       t, chunk, n_heads):
    qi = pl.program_id(2)
    _state_init(m_ref, acc_ref)

    def tile(i, causal):
        k0 = pl.multiple_of(i * t, t)
        for hh in range(n_heads):
            cols = slice(hh * HEAD_DIM, (hh + 1) * HEAD_DIM)
            s = _dot_nt(q_ref[:, cols], k_ref[pl.ds(k0, t), cols])
            d_row = d_ref[hh, :, pl.ds(k0, t)] * LOG2E

            def logits_of(c, s=s, d_row=d_row):
                lg = s[c * chunk:(c + 1) * chunk] * (SCALE * LOG2E) - d_row
                if causal:
                    row = c * chunk + lax.broadcasted_iota(jnp.int32, (chunk, t), 0)
                    col = lax.broadcasted_iota(jnp.int32, (chunk, t), 1)
                    lg = jnp.where(col <= row, lg, NEG_INF)
                return lg

            _state_update_chunked(logits_of, t // chunk, chunk, _with_ones(v_ref[pl.ds(k0, t), cols]),
                                  m_ref.at[hh], acc_ref.at[hh], p_ref.at[hh], alpha_ref.at[hh], False)

    def body(i, carry):
        tile(i, False)
        return carry

    lax.fori_loop(0, qi, body, 0)
    tile(qi, True)
    for hh in range(n_heads):
        cols = slice(hh * HEAD_DIM, (hh + 1) * HEAD_DIM)
        o_ref[:, cols] = (_state_result(acc_ref.at[hh]) * _silu(z_ref[:, cols])).astype(o_ref.dtype)


def _fox_prompt(proj, d_rows, batch, seq):
    t, chunk, n_heads = 512, 128, 4
    nq = seq // t
    width = n_heads * HEAD_DIM
    n_hg = FOX_HEADS // n_heads
    return pl.pallas_call(
        functools.partial(_fox_prompt_kernel, t=t, chunk=chunk, n_heads=n_heads),
        grid=(batch, n_hg, nq),
        in_specs=[pl.BlockSpec((t, width), lambda b, h, i: (b * nq + i, h)),
                  pl.BlockSpec((seq, width), lambda b, h, i: (b, h)),
                  pl.BlockSpec((seq, width), lambda b, h, i: (b, n_hg + h)),
                  pl.BlockSpec((n_heads, 1, seq), lambda b, h, i: (b * n_hg + h, 0, 0)),
                  pl.BlockSpec((t, width), lambda b, h, i: (b * nq + i, h))],
        out_specs=pl.BlockSpec((t, width), lambda b, h, i: (b * nq + i, h)),
        out_shape=jax.ShapeDtypeStruct((batch * seq, FOX_WIDTH), BF16),
        scratch_shapes=[pltpu.VMEM((n_heads, t, LANES), F32), pltpu.VMEM((n_heads, t, 2 * HEAD_DIM), F32),
                        pltpu.VMEM((n_heads, t, LANES), F32), pltpu.VMEM((n_heads, t, t), BF16)],
        compiler_params=_params(("parallel", "parallel", "arbitrary")),
        name="fox_prompt",
    )(proj["q_f"], proj["kv_f_bf"], proj["kv_f_bf"], d_rows, proj["z_f"])


def _prompt_layer(x, norm_in, wts, cmp_pe, cmp_w_bf, w_out_bf, norm_final):
    batch, seq, d = x.shape
    x2d = x.reshape(batch * seq, d)
    proj = _project_all(x2d, norm_in, wts, 1024, kv4_state_order=True)
    n_gate = 3 * NSA_HEADS
    logf = proj["small"][:, n_gate:n_gate + FOX_HEADS].reshape(batch, seq, FOX_HEADS)
    d_rows = _cumsum_rows(jnp.swapaxes(logf, 1, 2).reshape(batch * FOX_HEADS, seq))
    kcv = _compress_prompt(proj["kv4_cmp"], cmp_pe, cmp_w_bf, batch, seq)
    blk_of_key = jnp.arange(seq, dtype=jnp.int32) // SEL_BLOCK
    expand = (jnp.arange(LANES, dtype=jnp.int32)[:, None] == blk_of_key[None, :]).astype(BF16)
    mix_n = _nsa_prompt(proj, kcv, expand, batch, seq)
    mix_f = _fox_prompt(proj, d_rows.reshape(batch * FOX_HEADS, 1, seq), batch, seq)
    y = _merge(x2d, mix_n, mix_f, w_out_bf, norm_final, 512).reshape(batch, seq, d)
    w_keep = min(WINDOW, seq)
    state = (proj["kv4_state"].reshape(batch, seq, 4, NSA_KV_HEADS, HEAD_DIM),
             proj["kv_f"].reshape(batch, seq, 2, FOX_HEADS, HEAD_DIM),
             logf,
             proj["kvwin"].reshape(batch, seq, 2 * KV_WIDTH)[:, seq - w_keep:].reshape(
                 batch, w_keep, 2, NSA_KV_HEADS, HEAD_DIM))
    return y, state


HALF_CHANNELS = NSA_CHANNELS // 2


def _nsa_page_specs(pg, half):
    def spec(u):
        return pl.BlockSpec((1, 1, PAGE_SIZE, 1, HALF_CHANNELS, HEAD_DIM),
                            lambda b, j, pt: (0, pt[b, j * pg + u], 0, half, 0, 0))
    return [spec(u) for u in range(pg)]


def _channel(page_ref, ch, start=0, n=PAGE_SIZE):
    rows = page_ref.reshape(PAGE_SIZE * HALF_CHANNELS, HEAD_DIM)
    return rows[pl.ds(start * HALF_CHANNELS + ch, n, stride=HALF_CHANNELS), :]


def _nsa_cmp_kernel(pt_ref, *refs, pg, n_pages):
    page_refs = refs[:pg]
    q_ref, pe_ref, w_ref, oc_ref, sel_ref, stage_ref, xs_ref = refs[pg:]
    jg = pl.program_id(1)
    past = n_pages * PAGE_SIZE
    nb = past // CMP_BLOCK
    half = nb // 2
    blocks_per_page = PAGE_SIZE // CMP_BLOCK
    for u in range(pg):
        for bl in range(blocks_per_page):
            m = (jg * pg + u) * (blocks_per_page // 2) + bl // 2
            dst = pl.multiple_of(((bl % 2) * half + m) * STAGE_PITCH, SUBLANES)
            for ch in range(2 * NSA_KV_HEADS):
                stage_ref[ch, pl.ds(dst, CMP_BLOCK), :] = _channel(page_refs[u], ch, bl * CMP_BLOCK, CMP_BLOCK)

    @pl.when(jg == pl.num_programs(1) - 1)
    def _():
        cmp = [[_compress_rows(stage_ref.at[kind * NSA_KV_HEADS + h], pe_ref, kind, w_ref[kind], xs_ref, nb,
                               (0, half * STAGE_PITCH), STAGE_PITCH).astype(BF16)
                for h in range(NSA_KV_HEADS)] for kind in range(2)]
        rows = NSA_GROUP * T_PAD
        for hkv in range(NSA_KV_HEADS):
            q = q_ref[0, hkv * rows:(hkv + 1) * rows, :].astype(BF16)
            kc, vc = cmp[0][hkv], cmp[1][hkv]
            s = _dot_nt(q, kc) * SCALE
            lane = lax.broadcasted_iota(jnp.int32, (rows, nb), 1)
            blk = jnp.where(lane < half, 2 * lane, 2 * lane - (nb - 1))
            row = lax.broadcasted_iota(jnp.int32, (rows, nb), 0)
            slope = _head_slope(hkv * NSA_GROUP + row // T_PAD)
            dist = (past + row % T_PAD - ((blk + 1) * CMP_BLOCK - 1)).astype(F32)
            lg = jnp.where(dist >= 0, s - slope * dist, NEG_INF)
            mx = jnp.max(lg, axis=1, keepdims=True)
            mx = jnp.where(mx == NEG_INF, 0.0, mx)
            p = jnp.exp(lg - mx)
            p = p / jnp.maximum(jnp.sum(p, axis=1, keepdims=True), 1e-30)
            oc_ref[0, hkv * rows:(hkv + 1) * rows, :] = _dot(p.astype(BF16), vc)
            score = p[0:T_PAD]
            for g in range(1, NSA_GROUP):
                score = score + p[g * T_PAD:(g + 1) * T_PAD]
            score = score[:, :half] + score[:, half:]
            blk_s = lax.broadcasted_iota(jnp.int32, (T_PAD, half), 1).astype(F32)
            forced = (blk_s == 0) | (blk_s == half - 1)
            s_left = jnp.where(forced, NEG_INF, score)
            picked = jnp.zeros((T_PAD, half), F32)
            for _ in range(SEL_TOPK - 3):
                best = jnp.max(s_left, axis=1, keepdims=True)
                first = jnp.min(jnp.where(s_left == best, blk_s, float(half)), axis=1, keepdims=True)
                hit = blk_s == first
                picked = jnp.where(hit, 1.0, picked)
                s_left = jnp.where(hit, NEG_INF, s_left)
            sel_ref[0, hkv * T_PAD:(hkv + 1) * T_PAD, :] = jnp.where(forced, 1.0, picked)


def _nsa_cmp_sample(cache, page_table, q_rows, cmp_pe, cmp_w_bf, pg):
    n_seq, n_pages = page_table.shape
    past = n_pages * PAGE_SIZE
    nb = past // CMP_BLOCK
    assert n_pages % pg == 0 and past // SEL_BLOCK >= SEL_TOPK
    rows = NSA_HEADS * T_PAD
    grid_spec = pltpu.PrefetchScalarGridSpec(
        num_scalar_prefetch=1,
        grid=(n_seq, n_pages // pg),
        in_specs=_nsa_page_specs(pg, 0) + [
            pl.BlockSpec((1, rows, HEAD_DIM), lambda b, j, pt: (b, 0, 0)),
            pl.BlockSpec((2, CMP_BLOCK, HEAD_DIM), lambda b, j, pt: (0, 0, 0)),
            pl.BlockSpec((2, CMP_BLOCK * HEAD_DIM, HEAD_DIM), lambda b, j, pt: (0, 0, 0))],
        out_specs=[pl.BlockSpec((1, rows, HEAD_DIM), lambda b, j, pt: (b, 0, 0)),
                   pl.BlockSpec((1, NSA_KV_HEADS * T_PAD, past // SEL_BLOCK), lambda b, j, pt: (b, 0, 0))],
        scratch_shapes=[pltpu.VMEM((2 * NSA_KV_HEADS, nb * STAGE_PITCH, HEAD_DIM), F32),
                        pltpu.VMEM((nb, CMP_BLOCK * HEAD_DIM), BF16)],
    )
    return pl.pallas_call(
        functools.partial(_nsa_cmp_kernel, pg=pg, n_pages=n_pages),
        grid_spec=grid_spec,
        out_shape=[jax.ShapeDtypeStruct((n_seq, rows, HEAD_DIM), F32),
                   jax.ShapeDtypeStruct((n_seq, NSA_KV_HEADS * T_PAD, past // SEL_BLOCK), F32)],
        compiler_params=_params(("parallel", "arbitrary")),
        name="nsa_sample_cmp",
    )(page_table, *([cache] * pg), q_rows, cmp_pe, cmp_w_bf)


def _nsa_slc_kernel(pt_ref, *refs, pg, part_pages, n_pages, w_buf, n_new):
    page_refs = refs[:pg]
    (q_ref, sel_ref, e_ref, new_ref, win_ref, oc_ref, g_ref, z_ref, o_ref, tab_ref, m_ref, acc_ref) = refs[pg:]
    jg = pl.program_id(1)
    past = n_pages * PAGE_SIZE
    hr = n_new * NSA_GROUP
    rows = NSA_KV_HEADS * hr
    row = lax.broadcasted_iota(jnp.int32, (rows, 1), 0)
    t_q = (row % hr) // NSA_GROUP
    slope = _head_slope((row // hr) * NSA_GROUP + row % NSA_GROUP)
    q = [q_ref[0, h * hr:(h + 1) * hr, :].astype(BF16) for h in range(NSA_KV_HEADS)]

    def scores(k_of):
        return jnp.concatenate([_dot_nt(q[h], k_of(h)) for h in range(NSA_KV_HEADS)], axis=0) * SCALE

    def pv(v_of):
        return lambda p: jnp.concatenate(
            [_dot(p[h * hr:(h + 1) * hr], _with_ones(v_of(h))) for h in range(NSA_KV_HEADS)], axis=0)


    @pl.when(jg == 0)
    def _():
        expanded = _dot(sel_ref[0].astype(BF16), e_ref[...])
        key = lax.broadcasted_iota(jnp.int32, (rows, past), 1)
        tab_ref[...] = jnp.where(expanded > 0.5, 0.0, NEG_INF) + slope * (key - past).astype(F32)
        _state_init(m_ref, acc_ref)

    def page_stream(ch, pages):
        return jnp.concatenate([_channel(page_refs[u], ch).astype(BF16) for u in pages], axis=0)

    parts = []
    for first in range(0, pg, part_pages):
        pages = range(first, first + part_pages)
        c0 = pl.multiple_of((jg * pg + first) * PAGE_SIZE, part_pages * PAGE_SIZE)
        lg = (scores(lambda h: page_stream(h, pages))
              + tab_ref[:, pl.ds(c0, part_pages * PAGE_SIZE)])
        parts.append((lg, pv(lambda h, pages=pages: page_stream(NSA_KV_HEADS + h, pages))))
    _state_update_parts(parts, m_ref, acc_ref, True)

    @pl.when(jg == pl.num_programs(1) - 1)
    def _():
        new = lambda ch: new_ref[0, ch].astype(BF16)
        j_new = lax.broadcasted_iota(jnp.int32, (rows, PAGE_SIZE), 1)
        bias_new = jnp.where((j_new <= t_q) & (j_new < n_new), slope * j_new.astype(F32), NEG_INF)
        _state_update(scores(lambda h: new(h)) + bias_new, pv(lambda h: new(NSA_KV_HEADS + h)),
                      m_ref, acc_ref, False)
        o_s = _state_result(acc_ref)

        _state_init(m_ref, acc_ref)
        i_w = lax.broadcasted_iota(jnp.int32, (rows, w_buf), 1)
        dist_w = t_q + w_buf - i_w
        valid_w = (dist_w >= 0) & (dist_w < WINDOW) & (past - w_buf + i_w >= 0)
        bias_w = jnp.where(valid_w, slope * (i_w - w_buf).astype(F32), NEG_INF)
        win = lambda ch: win_ref[0, ch].astype(BF16)
        _state_update(scores(lambda h: win(h)) + bias_w, pv(lambda h: win(NSA_KV_HEADS + h)),
                      m_ref, acc_ref, True)
        _state_update(scores(lambda h: new(2 * NSA_KV_HEADS + h)) + bias_new,
                      pv(lambda h: new(3 * NSA_KV_HEADS + h)), m_ref, acc_ref, True)
        o_w = _state_result(acc_ref)

        gates = g_ref[0]
        mix = gates[:, 0:1] * oc_ref[0] + gates[:, 1:2] * o_s + gates[:, 2:3] * o_w
        o_ref[0] = mix * _silu(z_ref[0])


def _nsa_slc_sample(cache, page_table, q_rows, sel_rows, expand, new_rows, win_state, o_c_rows, gates_rows, z_rows,
                    pg, n_new):
    n_seq, n_pages = page_table.shape
    past = n_pages * PAGE_SIZE
    w_buf = win_state.shape[2]
    rows = n_new * NSA_HEADS
    per_seq = lambda *shape: pl.BlockSpec((1,) + shape, lambda b, j, pt: (b,) + (0,) * len(shape))
    grid_spec = pltpu.PrefetchScalarGridSpec(
        num_scalar_prefetch=1,
        grid=(n_seq, n_pages // pg),
        in_specs=_nsa_page_specs(pg, 1) + [
            per_seq(rows, HEAD_DIM),
            per_seq(rows, past // SEL_BLOCK),
            pl.BlockSpec((past // SEL_BLOCK, past), lambda b, j, pt: (0, 0)),
            per_seq(4 * NSA_KV_HEADS, PAGE_SIZE, HEAD_DIM),
            per_seq(2 * NSA_KV_HEADS, w_buf, HEAD_DIM),
            per_seq(rows, HEAD_DIM),
            per_seq(rows, 3),
            per_seq(rows, HEAD_DIM)],
        out_specs=per_seq(rows, HEAD_DIM),
        scratch_shapes=[pltpu.VMEM((rows, past), F32), pltpu.VMEM((rows, LANES), F32),
                        pltpu.VMEM((rows, 2 * HEAD_DIM), F32)],
    )
    return pl.pallas_call(
        functools.partial(_nsa_slc_kernel, pg=pg, part_pages=2, n_pages=n_pages, w_buf=w_buf, n_new=n_new),
        grid_spec=grid_spec,
        out_shape=jax.ShapeDtypeStruct((n_seq, rows, HEAD_DIM), F32),
        compiler_params=_params(("parallel", "arbitrary")),
        name="nsa_sample_slc_win",
    )(page_table, *([cache] * pg), q_rows, sel_rows, expand, new_rows, win_state, o_c_rows, gates_rows, z_rows)


def _page_forget_scan(x):
    n = x.shape[1]
    lane = lax.broadcasted_iota(jnp.int32, x.shape, 1)
    s = FOX_HEADS
    while s < n:
        x = x + jnp.where(lane >= s, pltpu.roll(x, s, axis=1), 0.0)
        s *= 2
    total = jnp.where(lane >= n - FOX_HEADS, x, 0.0)
    s = FOX_HEADS
    while s < n:
        total = total + pltpu.roll(total, n - s, axis=1)
        s *= 2
    return x, total


def _fox_sample_kernel(pt_ref, *refs, pg, part_pages, n_new):
    page_refs = refs[:pg]
    (q_ref, lc_ref, kn_ref, vn_ref, ln_ref, z_ref, o_ref, m_ref, acc_ref, carry_ref, lg_ref) = refs[pg:]
    b = pl.program_id(0)
    jg = pl.program_id(1)
    rows = n_new * FOX_HEADS
    keys = PAGE_SIZE * FOX_HEADS

    @pl.when(jg == 0)
    def _():
        _state_init(m_ref, acc_ref)
        carry_ref[...] = jnp.zeros(carry_ref.shape, F32)

    q = q_ref[0].astype(BF16)
    own = (lax.broadcasted_iota(jnp.int32, (rows, keys), 1) % FOX_HEADS
           == lax.broadcasted_iota(jnp.int32, (rows, keys), 0) % FOX_HEADS)

    for u in range(pg):
        lg_ref[u:u + 1, :] = lc_ref[pl.ds(pt_ref[b, jg * pg + u], 1), :]
    within, total = _page_forget_scan(lg_ref[...])
    carry = carry_ref[...]
    logits = []
    for u in range(pg):
        d_u = within[u:u + 1] + carry
        carry = carry + total[u:u + 1]
        k = page_refs[u][0, 0, :, 0].reshape(keys, HEAD_DIM).astype(BF16)
        logits.append(jnp.where(own, _dot_nt(q, k) * SCALE - d_u, NEG_INF))
    carry_ref[...] = carry

    def pv(pages):
        def pv_of(p):
            out = None
            for i, u in enumerate(pages):
                v = page_refs[u][0, 0, :, 1].reshape(keys, HEAD_DIM).astype(BF16)
                part = _dot(p[:, i * keys:(i + 1) * keys], _with_ones(v))
                out = part if out is None else out + part
            return out
        return pv_of

    parts = []
    for first in range(0, pg, part_pages):
        pages = range(first, first + part_pages)
        parts.append((jnp.concatenate([logits[u] for u in pages], axis=1), pv(pages)))
    _state_update_parts(parts, m_ref, acc_ref, False)

    @pl.when(jg == pl.num_programs(1) - 1)
    def _():
        within_new, _ = _page_forget_scan(ln_ref[0])
        d_new = (within_new + carry_ref[...])[:, :PAGE_SIZE]
        c = lax.broadcasted_iota(jnp.int32, (rows, PAGE_SIZE), 1)
        r = lax.broadcasted_iota(jnp.int32, (rows, PAGE_SIZE), 0)
        valid = (c < rows) & (c % FOX_HEADS == r % FOX_HEADS) & (c // FOX_HEADS <= r // FOX_HEADS)
        lg = jnp.where(valid, _dot_nt(q, kn_ref[0].astype(BF16)) * SCALE - d_new, NEG_INF)
        _state_update(lg, lambda p: _dot(p, _with_ones(vn_ref[0].astype(BF16))), m_ref, acc_ref, False)
        o_ref[0] = _state_result(acc_ref) * _silu(z_ref[0])


def _fox_sample(cache, page_table, q_rows, logf_cache, k_new, v_new, logf_new, z_rows, pg, n_new):
    n_seq, n_pages = page_table.shape
    n_pool = logf_cache.shape[0]
    rows = n_new * FOX_HEADS
    keys = PAGE_SIZE * FOX_HEADS
    per_seq = lambda *shape: pl.BlockSpec((1,) + shape, lambda b, j, pt: (b,) + (0,) * len(shape))

    def page_spec(u):
        return pl.BlockSpec((1, 1, PAGE_SIZE, 2, FOX_HEADS, HEAD_DIM),
                            lambda b, j, pt: (0, pt[b, j * pg + u], 0, 0, 0, 0))

    grid_spec = pltpu.PrefetchScalarGridSpec(
        num_scalar_prefetch=1,
        grid=(n_seq, n_pages // pg),
        in_specs=[page_spec(u) for u in range(pg)] + [
            per_seq(rows, HEAD_DIM),
            pl.BlockSpec((n_pool, keys), lambda b, j, pt: (0, 0), pipeline_mode=pl.Buffered(1)),
            per_seq(PAGE_SIZE, HEAD_DIM),
            per_seq(PAGE_SIZE, HEAD_DIM),
            per_seq(1, keys),
            per_seq(rows, HEAD_DIM)],
        out_specs=per_seq(rows, HEAD_DIM),
        scratch_shapes=[pltpu.VMEM((rows, LANES), F32), pltpu.VMEM((rows, 2 * HEAD_DIM), F32),
                        pltpu.VMEM((1, keys), F32), pltpu.VMEM((pg, keys), F32)],
    )
    return pl.pallas_call(
        functools.partial(_fox_sample_kernel, pg=pg, part_pages=2, n_new=n_new),
        grid_spec=grid_spec,
        out_shape=jax.ShapeDtypeStruct((n_seq, rows, HEAD_DIM), F32),
        compiler_params=_params(("parallel", "arbitrary")),
        name="fox_sample",
    )(page_table, *([cache] * pg), q_rows, logf_cache, k_new, v_new, logf_new, z_rows)


def _held_pages(page_table, needed, pg):
    n_seq, n_pages = page_table.shape
    slots = page_table.reshape(n_seq * n_pages // pg, pg)
    want = needed.reshape(slots.shape).at[0].set(True)
    step = jnp.arange(slots.shape[0], dtype=jnp.int32)[:, None]
    filled_at = lax.cummax(jnp.where(want, step, 0), axis=0)
    return jnp.take_along_axis(slots, filled_at, axis=0).reshape(n_seq, n_pages)


def _pad_axis(a, axis, size):
    pad = [(0, 0)] * a.ndim
    pad[axis] = (0, size - a.shape[axis])
    return jnp.pad(a, pad)


def _sample_layer(x, cache_nsa_kv, cache_fox_kv, cache_fox_logf, state_nsa_win, page_table,
                  norm_in, wts, cmp_pe, cmp_w_bf, w_out_bf, norm_final):
    n_seq, t_new, d = x.shape
    n_pool = cache_nsa_kv.shape[1]
    n_pages = page_table.shape[1]
    past = n_pages * PAGE_SIZE
    w_buf = state_nsa_win.shape[2]
    assert t_new <= T_PAD and t_new < SEL_BLOCK and past % SEL_BLOCK == 0
    x2d = x.reshape(n_seq * t_new, d)
    proj = _project_all(x2d, norm_in, wts, n_seq * t_new)
    n_gate = 3 * NSA_HEADS
    small = proj["small"].reshape(n_seq, t_new, LANES)
    logf = small[:, :, n_gate:n_gate + FOX_HEADS]

    def head_rows(a):
        a = _pad_axis(a.reshape(n_seq, t_new, NSA_HEADS, -1), 1, T_PAD)
        return jnp.swapaxes(a, 1, 2).reshape(n_seq, NSA_HEADS * T_PAD, a.shape[-1])

    def group_rows(a):
        a = a.reshape(n_seq, t_new, NSA_KV_HEADS, NSA_GROUP, a.shape[-1])
        return jnp.swapaxes(a, 1, 2).reshape(n_seq, NSA_HEADS * t_new, a.shape[-1])

    q_n = proj["q_n"].astype(F32)
    gates = small[:, :, :n_gate].reshape(n_seq, t_new, NSA_HEADS, 3)
    z_n = proj["z_n"].reshape(n_seq, t_new, NSA_HEADS, HEAD_DIM)
    nsa_new = jnp.concatenate([proj["kv4"].reshape(n_seq, t_new, 4 * NSA_KV_HEADS, HEAD_DIM)[:, :, 2 * NSA_KV_HEADS:],
                               proj["kvwin"].reshape(n_seq, t_new, 2 * NSA_KV_HEADS, HEAD_DIM)], axis=2)
    nsa_new = _pad_axis(jnp.swapaxes(nsa_new, 1, 2), 2, PAGE_SIZE)
    win_state = jnp.swapaxes(state_nsa_win[0].reshape(n_seq, w_buf, 2 * NSA_KV_HEADS, HEAD_DIM), 1, 2)
    blk_of_key = jnp.arange(past, dtype=jnp.int32) // SEL_BLOCK
    expand = (jnp.arange(past // SEL_BLOCK, dtype=jnp.int32)[:, None] == blk_of_key[None, :]).astype(BF16)
    nsa_cache = cache_nsa_kv.reshape(cache_nsa_kv.shape[0], n_pool, PAGE_SIZE, 2, HALF_CHANNELS, HEAD_DIM)

    pg = 16
    o_c, sel = _nsa_cmp_sample(nsa_cache, page_table, head_rows(q_n), cmp_pe, cmp_w_bf, min(2 * pg, n_pages))
    o_c = jnp.swapaxes(o_c.reshape(n_seq, NSA_HEADS, T_PAD, HEAD_DIM)[:, :, :t_new], 1, 2)
    sel = sel.reshape(n_seq, NSA_KV_HEADS, T_PAD, past // SEL_BLOCK)[:, :, :t_new]
    page_needed = jnp.any(sel.reshape(n_seq, -1, n_pages, PAGE_SIZE // SEL_BLOCK) > 0.5, axis=(1, 3))
    slc_pages = _held_pages(page_table, page_needed, pg)
    sel = jnp.broadcast_to(sel[:, :, :, None, :], (n_seq, NSA_KV_HEADS, t_new, NSA_GROUP, past // SEL_BLOCK))
    mix_n = _nsa_slc_sample(nsa_cache, slc_pages, group_rows(q_n.reshape(n_seq, t_new, NSA_HEADS, HEAD_DIM)),
                            sel.reshape(n_seq, NSA_HEADS * t_new, past // SEL_BLOCK), expand, nsa_new, win_state,
                            group_rows(o_c), group_rows(gates), group_rows(z_n), pg, t_new)
    mix_n = jnp.swapaxes(mix_n.reshape(n_seq, NSA_KV_HEADS, t_new, NSA_GROUP * HEAD_DIM), 1, 2)
    mix_n = mix_n.reshape(n_seq * t_new, NSA_WIDTH).astype(BF16)

    fox_rows = lambda a: a.reshape(n_seq, t_new * FOX_HEADS, HEAD_DIM)
    kv_f = proj["kv_f"].reshape(n_seq, t_new, 2, FOX_HEADS * HEAD_DIM)
    k_new = _pad_axis(fox_rows(kv_f[:, :, 0]), 1, PAGE_SIZE)
    v_new = _pad_axis(fox_rows(kv_f[:, :, 1]), 1, PAGE_SIZE)
    logf_new = _pad_axis(logf.reshape(n_seq, 1, t_new * FOX_HEADS), 2, PAGE_SIZE * FOX_HEADS)
    logf_cache = cache_fox_logf[0].reshape(n_pool, PAGE_SIZE * FOX_HEADS)
    mix_f = _fox_sample(cache_fox_kv, page_table, fox_rows(proj["q_f"].astype(F32)), logf_cache, k_new, v_new,
                        logf_new, fox_rows(proj["z_f"]), 16, t_new)
    mix_f = mix_f.reshape(n_seq * t_new, FOX_WIDTH).astype(BF16)

    y = _merge(x2d, mix_n, mix_f, w_out_bf, norm_final, n_seq * t_new).reshape(n_seq, t_new, d)
    kvwin_new = proj["kvwin"].reshape(n_seq, t_new, 2, NSA_KV_HEADS, HEAD_DIM)
    state = (proj["kv4"].reshape(n_seq, t_new, 4, NSA_KV_HEADS, HEAD_DIM),
             proj["kv_f"].reshape(n_seq, t_new, 2, FOX_HEADS, HEAD_DIM),
             logf,
             jnp.concatenate([state_nsa_win[0], kvwin_new], axis=1)[:, t_new:])
    return y, state


def kernel(x_prompt, x_sample, cache_nsa_kv, cache_fox_kv, cache_fox_logf, state_nsa_win, page_table,
           norm_in, w_in, b_gate, b_forget, cmp_pe, cmp_w, w_out, norm_final):
    assert norm_in.shape[0] == 1, "single-layer trunk"
    wts = _split_weights(w_in[0], b_gate[0], b_forget[0])
    cmp_w_bf = cmp_w[0].reshape(2, CMP_BLOCK * HEAD_DIM, HEAD_DIM).astype(BF16)
    w_out_bf = w_out[0].astype(BF16)
    y_p, st_p = _prompt_layer(x_prompt, norm_in[0], wts, cmp_pe[0], cmp_w_bf, w_out_bf, norm_final)
    y_s, st_s = _sample_layer(x_sample, cache_nsa_kv, cache_fox_kv, cache_fox_logf, state_nsa_win,
                              page_table, norm_in[0], wts, cmp_pe[0], cmp_w_bf, w_out_bf, norm_final)
    outs = [y_p, y_s]
    for s_p, s_s in zip(st_p, st_s):
        outs.extend([s_p[None], s_s[None]])
    return tuple(outs)
```

```python
import functools

import jax
import jax.numpy as jnp
from jax import lax
from jax.experimental import pallas as pl
from jax.experimental.pallas import tpu as pltpu

F32 = jnp.float32
BF16 = jnp.bfloat16
NEG_INF = float("-inf")

HEAD_DIM = 128
NSA_HEADS = 8
FOX_HEADS = 8
NSA_KV_HEADS = 2
NSA_GROUP = NSA_HEADS // NSA_KV_HEADS
NSA_WIDTH = NSA_HEADS * HEAD_DIM
FOX_WIDTH = FOX_HEADS * HEAD_DIM
KV_WIDTH = NSA_KV_HEADS * HEAD_DIM
NSA_CHANNELS = 4 * NSA_KV_HEADS
CMP_BLOCK = 32
SEL_BLOCK = 64
SEL_TOPK = 16
WINDOW = 512
PAGE_SIZE = 128
FORCED_SCORE = 1e4
RMS_EPS = 1e-6
SCALE = HEAD_DIM ** -0.5
LOG2E = 1.4426950408889634
T_PAD = 8
LANES = 128
SUBLANES = 8
PROJ_TN = 512
STAGE_PITCH = 40
VMEM_LIMIT = 56 * 1024 * 1024


def _params(sem):
    return pltpu.CompilerParams(dimension_semantics=sem, vmem_limit_bytes=VMEM_LIMIT)


def _dot(a, b):
    return jnp.dot(a, b, preferred_element_type=F32)


def _dot_nt(a, b):
    return lax.dot_general(a, b, (((1,), (1,)), ((), ())), preferred_element_type=F32)


def _sigmoid(x):
    return 1.0 / (1.0 + jnp.exp(-x))


def _silu(x):
    return x * _sigmoid(x)


def _tile_lanes(x, n):
    return x if n == 1 else jnp.concatenate([x] * n, axis=1)


def _with_ones(v):
    return jnp.concatenate([v, jnp.ones(v.shape, v.dtype)], axis=1)


def _state_init(m_ref, acc_ref):
    m_ref[...] = jnp.full(m_ref.shape, NEG_INF, F32)
    acc_ref[...] = jnp.zeros(acc_ref.shape, F32)


def _probs(lg, m_old, may_be_empty, exp=jnp.exp):
    m_new = jnp.maximum(m_old, jnp.max(lg, axis=1, keepdims=True))
    m_use = jnp.where(m_new == NEG_INF, 0.0, m_new) if may_be_empty else m_new
    alpha = exp(m_old - m_use)
    p = exp(lg - _tile_lanes(m_use, lg.shape[1] // LANES)).astype(BF16)
    return m_new, alpha, p


def _state_update_chunked(logits_of, n_chunks, chunk, v_ext, m_ref, acc_ref, p_ref, alpha_ref, may_be_empty):
    for c in range(n_chunks):
        rows = slice(c * chunk, (c + 1) * chunk)
        m_new, alpha, p = _probs(logits_of(c), m_ref[rows, :], may_be_empty, jnp.exp2)
        alpha_ref[rows, :] = alpha
        p_ref[rows, :] = p
        m_ref[rows, :] = m_new
    acc_ref[...] = _tile_lanes(alpha_ref[...], 2) * acc_ref[...] + _dot(p_ref[...], v_ext)


def _state_update(lg, pv_of, m_ref, acc_ref, may_be_empty):
    m_new, alpha, p = _probs(lg, m_ref[...], may_be_empty)
    acc_ref[...] = _tile_lanes(alpha, 2) * acc_ref[...] + pv_of(p)
    m_ref[...] = m_new


def _state_update_parts(parts, m_ref, acc_ref, may_be_empty):
    local = []
    for lg, pv_of in parts:
        m_c = jnp.broadcast_to(jnp.max(lg, axis=1, keepdims=True), m_ref.shape)
        m_use = jnp.where(m_c == NEG_INF, 0.0, m_c) if may_be_empty else m_c
        p = jnp.exp(lg - _tile_lanes(m_use, lg.shape[1] // LANES)).astype(BF16)
        local.append((m_c, pv_of(p)))
    m_old = m_ref[...]
    m_new = m_old
    for m_c, _ in local:
        m_new = jnp.maximum(m_new, m_c)
    acc = _tile_lanes(jnp.exp(m_old - m_new), 2) * acc_ref[...]
    for m_c, pv in local:
        acc = acc + _tile_lanes(jnp.exp(m_c - m_new), 2) * pv
    acc_ref[...] = acc
    m_ref[...] = m_new


def _state_result(acc_ref):
    acc = acc_ref[...]
    return acc[:, :HEAD_DIM] / jnp.maximum(acc[:, HEAD_DIM:], 1e-30)


def _lane_cumsum(x):
    n = x.shape[-1]
    lane = lax.broadcasted_iota(jnp.int32, x.shape, x.ndim - 1)
    s = 1
    while s < n:
        x = x + jnp.where(lane >= s, pltpu.roll(x, s, axis=x.ndim - 1), 0.0)
        s *= 2
    return x


def _head_slope(head):
    if isinstance(head, int):
        return 2.0 ** -(head + 1)
    return lax.bitcast_convert_type((126 - head) << 23, F32)


def _rms_kernel(x_ref, g_ref, o_ref):
    x = x_ref[...]
    ms = jnp.mean(x * x, axis=-1, keepdims=True)
    o_ref[...] = (x * lax.rsqrt(ms + RMS_EPS) * g_ref[...]).astype(o_ref.dtype)


def _rmsnorm(x2d, g, out_dtype, tm):
    m, d = x2d.shape
    return pl.pallas_call(
        _rms_kernel,
        grid=(m // tm,),
        in_specs=[pl.BlockSpec((tm, d), lambda i: (i, 0)), pl.BlockSpec((1, d), lambda i: (0, 0))],
        out_specs=pl.BlockSpec((tm, d), lambda i: (i, 0)),
        out_shape=jax.ShapeDtypeStruct((m, d), out_dtype),
        compiler_params=_params(("parallel",)),
        name="rmsnorm",
    )(x2d, g.reshape(1, d))


def _proj_kernel(h_ref, w_ref, *o_refs):
    acc = _dot(h_ref[...], w_ref[...])
    for o_ref in o_refs:
        o_ref[...] = acc.astype(o_ref.dtype)


def _project(h, w, col0, n, out_dtypes, tm, name):
    m, d = h.shape
    j0 = col0 // PROJ_TN
    return pl.pallas_call(
        _proj_kernel,
        grid=(m // tm, n // PROJ_TN),
        in_specs=[pl.BlockSpec((tm, d), lambda i, j: (i, 0)), pl.BlockSpec((d, PROJ_TN), lambda i, j: (0, j0 + j))],
        out_specs=[pl.BlockSpec((tm, PROJ_TN), lambda i, j: (i, j)) for _ in out_dtypes],
        out_shape=[jax.ShapeDtypeStruct((m, n), dt) for dt in out_dtypes],
        compiler_params=_params(("parallel", "arbitrary")),
        name=name,
    )(h, w)


def _small_kernel(h_ref, w_ref, b_ref, o_ref, *, n_gate):
    z = _dot(h_ref[...], w_ref[...].astype(BF16)) + b_ref[...]
    lane = lax.broadcasted_iota(jnp.int32, z.shape, 1)
    log_sig = jnp.minimum(z, 0.0) - jnp.log1p(jnp.exp(-jnp.abs(z)))
    o_ref[...] = jnp.where(lane < n_gate, _sigmoid(z), log_sig)


def _project_small(h, w, b, tm, n_gate):
    m, d = h.shape
    return pl.pallas_call(
        functools.partial(_small_kernel, n_gate=n_gate),
        grid=(m // tm,),
        in_specs=[pl.BlockSpec((tm, d), lambda i: (i, 0)), pl.BlockSpec((d, LANES), lambda i: (0, 0)),
                  pl.BlockSpec((1, LANES), lambda i: (0, 0))],
        out_specs=pl.BlockSpec((tm, LANES), lambda i: (i, 0)),
        out_shape=jax.ShapeDtypeStruct((m, LANES), F32),
        compiler_params=_params(("parallel",)),
        name="proj_gates_logf",
    )(h, w, b)


PROJ_GROUPS = (("q_n", NSA_WIDTH), ("kv4", 4 * KV_WIDTH), ("kvwin", 2 * KV_WIDTH), ("z_n", NSA_WIDTH),
               ("q_f", FOX_WIDTH), ("kv_f", 2 * FOX_WIDTH), ("z_f", FOX_WIDTH))
PROJ_OUT_DTYPES = {"q_n": (BF16,), "kv4": (F32, BF16), "kvwin": (F32, BF16), "z_n": (F32,),
                   "q_f": (BF16,), "kv_f": (F32, BF16), "z_f": (F32,)}


def _regroup_kernel(wt_ref, o_ref):
    o_ref[...] = wt_ref[...].T.astype(o_ref.dtype)


def _regroup_columns(w_t, spans):
    cols, rows = w_t.shape
    starts, col = [], 0
    for a, b in spans:
        assert col % PROJ_TN == 0 and (b - a) % PROJ_TN == 0 and a % SUBLANES == 0
        starts.append((col // PROJ_TN, a - col))
        col += b - a

    n_tiles = col // PROJ_TN

    def source_row(j):
        units = j * (PROJ_TN // SUBLANES) + starts[0][1] // SUBLANES
        for (first_tile, s), (_, s_prev) in zip(starts[1:], starts[:-1]):
            units = units + ((j + n_tiles - first_tile) // n_tiles) * ((s - s_prev) // SUBLANES)
        return units * SUBLANES

    return pl.pallas_call(
        _regroup_kernel,
        grid=(col // PROJ_TN,),
        in_specs=[pl.BlockSpec((pl.Element(PROJ_TN), pl.Element(rows)), lambda j: (source_row(j), 0))],
        out_specs=pl.BlockSpec((rows, PROJ_TN), lambda j: (0, j)),
        out_shape=jax.ShapeDtypeStruct((rows, col), BF16),
        compiler_params=_params(("parallel",)),
        name="regroup_weights",
    )(w_t)


def _split_weights(w_in, b_gate, b_forget):
    cuts = [NSA_WIDTH, 6 * KV_WIDTH, 3 * NSA_HEADS, NSA_WIDTH, FOX_WIDTH, FOX_WIDTH, FOX_WIDTH, FOX_HEADS, FOX_WIDTH]
    offs = [0]
    for c in cuts:
        offs.append(offs[-1] + c)
    w_t = jnp.swapaxes(w_in, 0, 1)
    main = _regroup_columns(w_t, ((offs[0], offs[2]), (offs[3], offs[7]), (offs[8], offs[9])))
    n_small = 3 * NSA_HEADS + FOX_HEADS
    w_small = jnp.concatenate([w_t[offs[2]:offs[3]], w_t[offs[7]:offs[8]]], axis=0).T
    w_small = jnp.pad(w_small, ((0, 0), (0, LANES - n_small)))
    b_small = jnp.pad(jnp.concatenate([b_gate, b_forget]), (0, LANES - n_small)).reshape(1, LANES).astype(F32)
    return {"main": main, "small": w_small, "b_small": b_small}


def _proj_kv4_kernel(h_ref, w_ref, chan_ref, cmp_ref, bf_ref):
    acc = _dot(h_ref[...], w_ref[...])
    for c in range(NSA_CHANNELS):
        chan_ref[:, c, :] = acc[:, c * HEAD_DIM:(c + 1) * HEAD_DIM]
    cmp_ref[...] = acc[:, :2 * KV_WIDTH]
    bf_ref[...] = acc.astype(bf_ref.dtype)


def _project_kv4(h, w, col0, tm):
    m, d = h.shape
    n = 4 * KV_WIDTH
    assert col0 % n == 0
    return pl.pallas_call(
        _proj_kv4_kernel,
        grid=(m // tm,),
        in_specs=[pl.BlockSpec((tm, d), lambda i: (i, 0)), pl.BlockSpec((d, n), lambda i: (0, col0 // n))],
        out_specs=[pl.BlockSpec((tm, NSA_CHANNELS, HEAD_DIM), lambda i: (i, 0, 0)),
                   pl.BlockSpec((tm, 2 * KV_WIDTH), lambda i: (i, 0)),
                   pl.BlockSpec((tm, n), lambda i: (i, 0))],
        out_shape=[jax.ShapeDtypeStruct((m, NSA_CHANNELS, HEAD_DIM), F32),
                   jax.ShapeDtypeStruct((m, 2 * KV_WIDTH), F32),
                   jax.ShapeDtypeStruct((m, n), BF16)],
        compiler_params=_params(("parallel",)),
        name="proj_kv4_state",
    )(h, w)


def _project_all(x2d, norm_in, wts, tm, kv4_state_order=False, one_call=False):
    h = _rmsnorm(x2d, norm_in, BF16, min(tm, 512))
    out = {}
    col0 = 0
    if one_call:
        (z,) = _project(h, wts["main"], 0, sum(w for _, w in PROJ_GROUPS), (F32,), tm, "proj_all")
        for name, width in PROJ_GROUPS:
            out[name] = z[:, col0:col0 + width]
            col0 += width
        out["small"] = _project_small(h, wts["small"], wts["b_small"], min(tm, 512), 3 * NSA_HEADS)
        return out
    for name, width in PROJ_GROUPS:
        if name == "kv4" and kv4_state_order:
            out["kv4_state"], out["kv4_cmp"], out["kv4_bf"] = _project_kv4(h, wts["main"], col0, tm)
            col0 += width
            continue
        res = _project(h, wts["main"], col0, width, PROJ_OUT_DTYPES[name], tm, "proj_" + name)
        out[name] = res[0]
        if len(res) > 1:
            out[name + "_bf"] = res[1]
        col0 += width
    out["small"] = _project_small(h, wts["small"], wts["b_small"], min(tm, 512), 3 * NSA_HEADS)
    return out


def _merge_kernel(x_ref, mn_ref, mf_ref, wn_ref, wf_ref, g_ref, o_ref):
    y = x_ref[...] + _dot(mn_ref[...], wn_ref[...]) + _dot(mf_ref[...], wf_ref[...])
    ms = jnp.mean(y * y, axis=-1, keepdims=True)
    o_ref[...] = y * lax.rsqrt(ms + RMS_EPS) * g_ref[...]


def _merge(x2d, mix_n, mix_f, w_out_bf, norm_final, tm):
    m, d = x2d.shape
    return pl.pallas_call(
        _merge_kernel,
        grid=(m // tm,),
        in_specs=[pl.BlockSpec((tm, d), lambda i: (i, 0)),
                  pl.BlockSpec((tm, NSA_WIDTH), lambda i: (i, 0)),
                  pl.BlockSpec((tm, FOX_WIDTH), lambda i: (i, 0)),
                  pl.BlockSpec((NSA_WIDTH, d), lambda i: (0, 0)),
                  pl.BlockSpec((FOX_WIDTH, d), lambda i: (NSA_WIDTH // FOX_WIDTH, 0)),
                  pl.BlockSpec((1, d), lambda i: (0, 0))],
        out_specs=pl.BlockSpec((tm, d), lambda i: (i, 0)),
        out_shape=jax.ShapeDtypeStruct((m, d), F32),
        compiler_params=_params(("parallel",)),
        name="merge_out_proj",
    )(x2d, mix_n, mix_f, w_out_bf, w_out_bf, norm_final.reshape(1, d))


def _cumsum_kernel(x_ref, o_ref):
    o_ref[...] = _lane_cumsum(x_ref[...])


def _cumsum_rows(x):
    return pl.pallas_call(
        _cumsum_kernel,
        out_shape=jax.ShapeDtypeStruct(x.shape, F32),
        name="forget_cumsum",
    )(x)


def _compress_rows(x_ref, pe_ref, kind, w, xs_ref, nb, starts, stride):
    half = nb // 2
    for c in range(CMP_BLOCK):
        pe_c = pe_ref[kind, c:c + 1, :]
        for par in range(2):
            xc = x_ref[pl.ds(starts[par] + c, half, stride=stride), :]
            xs_ref[par * half:(par + 1) * half, c * HEAD_DIM:(c + 1) * HEAD_DIM] = (xc + pe_c).astype(BF16)
    return _dot(xs_ref[...], w)


def _compress_kernel(x_ref, pe_ref, w_ref, o_ref, xs_ref, *, nb):
    out = _compress_rows(x_ref, pe_ref, 0, w_ref[0], xs_ref, nb, (0, CMP_BLOCK), 2 * CMP_BLOCK)
    o_ref[0, 0, 0] = out.astype(o_ref.dtype)


def _compress_prompt(kv4, cmp_pe, cmp_w_bf, batch, seq):
    nb = seq // CMP_BLOCK
    return pl.pallas_call(
        functools.partial(_compress_kernel, nb=nb),
        grid=(batch, 2, NSA_KV_HEADS),
        in_specs=[pl.BlockSpec((seq, HEAD_DIM), lambda b, k, h: (b, k * NSA_KV_HEADS + h)),
                  pl.BlockSpec((1, CMP_BLOCK, HEAD_DIM), lambda b, k, h: (k, 0, 0)),
                  pl.BlockSpec((1, CMP_BLOCK * HEAD_DIM, HEAD_DIM), lambda b, k, h: (k, 0, 0))],
        out_specs=pl.BlockSpec((1, 1, 1, nb, HEAD_DIM), lambda b, k, h: (b, k, h, 0, 0)),
        out_shape=jax.ShapeDtypeStruct((batch, 2, NSA_KV_HEADS, nb, HEAD_DIM), BF16),
        scratch_shapes=[pltpu.VMEM((nb, CMP_BLOCK * HEAD_DIM), BF16)],
        compiler_params=_params(("parallel", "parallel", "arbitrary")),
        name="compress_prompt",
    )(kv4, cmp_pe, cmp_w_bf)


def _select_blocks(score_t, cur, n_keep):
    n_blk = score_t.shape[0]
    j = lax.broadcasted_iota(jnp.int32, score_t.shape, 0)
    forced = (j == 0) | (j == cur) | (j == cur - 1)
    s = jnp.where(j <= cur, jnp.where(forced, FORCED_SCORE, score_t), -1.0)
    n_grp = n_blk // SUBLANES
    grp = [s[r * SUBLANES:(r + 1) * SUBLANES] for r in range(n_grp)]
    cnt = [jnp.zeros(grp[0].shape, F32) for _ in range(n_grp)]
    sub = lax.broadcasted_iota(jnp.int32, grp[0].shape, 0)
    for jj in range(n_blk):
        row = s[jj:jj + 1, :]
        r_j, off = divmod(jj, SUBLANES)
        for r in range(n_grp):
            if r < r_j:
                hit = row > grp[r]
            elif r > r_j:
                hit = row >= grp[r]
            else:
                hit = (row > grp[r]) | ((row == grp[r]) & (sub > off))
            cnt[r] = cnt[r] + jnp.where(hit, 1.0, 0.0)
    return jnp.where((jnp.concatenate(cnt, axis=0) < n_keep) & (j <= cur), 1.0, 0.0)


def _nsa_prompt_kernel(q_ref, kc_ref, vc_ref, ks_ref, vs_ref, kw_ref, vw_ref, g_ref, z_ref, e_ref, o_ref,
                       m_ref, acc_ref, alpha_ref, ps_ref, pw_ref, mix_ref, bias_ref, *, tq, tk, tw, ck, seq):
    qi = pl.program_id(1)
    s0 = qi * tq
    n_cmp = seq // CMP_BLOCK
    n_sel = seq // SEL_BLOCK
    row_t = s0 + lax.broadcasted_iota(jnp.int32, (tq, 1), 0)
    kv_heads = range(NSA_KV_HEADS)
    heads = lambda hkv: [hkv * NSA_GROUP + g for g in range(NSA_GROUP)]
    slopes = lambda hkv: [_head_slope(h) for h in heads(hkv)]
    cols = lambda hkv: slice(hkv * HEAD_DIM, (hkv + 1) * HEAD_DIM)

    def q_rows(hkv):
        return jnp.concatenate([q_ref[:, h * HEAD_DIM:(h + 1) * HEAD_DIM] for h in heads(hkv)], axis=0)

    def gate_col(hkv, br):
        return jnp.concatenate([g_ref[:, 3 * h + br:3 * h + br + 1] for h in heads(hkv)], axis=0)

    sels = []
    for hkv in kv_heads:
        sc = _dot_nt(q_rows(hkv), kc_ref[0, 0, hkv]) * SCALE
        lane = lax.broadcasted_iota(jnp.int32, (tq, n_cmp), 1)
        blk = jnp.where(lane < n_cmp // 2, 2 * lane, 2 * lane - (n_cmp - 1))
        dist_c = (row_t - ((blk + 1) * CMP_BLOCK - 1)).astype(F32)
        valid_c = dist_c >= 0
        score = jnp.zeros((tq, n_cmp), F32)
        probs = []
        for g, slope in enumerate(slopes(hkv)):
            lg = jnp.where(valid_c, sc[g * tq:(g + 1) * tq] - slope * dist_c, NEG_INF)
            mx = jnp.max(lg, axis=1, keepdims=True)
            mx = jnp.where(mx == NEG_INF, 0.0, mx)
            p = jnp.exp(lg - mx)
            p = p / jnp.maximum(jnp.sum(p, axis=1, keepdims=True), 1e-30)
            score = score + p
            probs.append(p)
        o_c = _dot(jnp.concatenate(probs, axis=0).astype(BF16), vc_ref[0, 0, hkv])
        mix_ref[hkv] = gate_col(hkv, 0) * o_c

        score_t = score.T
        score_t = score_t[:n_sel] + score_t[n_sel:]
        cur = (s0 + lax.broadcasted_iota(jnp.int32, (n_sel, tq), 1)) // SEL_BLOCK
        sel_t = _select_blocks(score_t, cur, SEL_TOPK)
        sels.append(jnp.concatenate([sel_t, jnp.zeros((LANES - n_sel, tq), F32)], axis=0).T.astype(BF16))

    def bias_body(c, carry):
        c0 = pl.multiple_of(c * ck, ck)
        causal = c0 + lax.broadcasted_iota(jnp.int32, (tq, ck), 1) <= row_t
        for hkv in kv_heads:
            expanded = _dot(sels[hkv], e_ref[:, pl.ds(c0, ck)])
            bias_ref[hkv, :, pl.ds(c0, ck)] = jnp.where((expanded > 0.5) & causal, 0.0, NEG_INF)
        return carry

    lax.fori_loop(0, (s0 + tq + ck - 1) // ck, bias_body, 0)

    def stream(hkv, k, bias, key_rel, v, p_ref, may_be_empty):
        s = _dot_nt(q_rows(hkv), k)
        sl = slopes(hkv)
        logits_of = lambda g: s[g * tq:(g + 1) * tq] * (SCALE * LOG2E) + (bias + (sl[g] * LOG2E) * key_rel)
        _state_update_chunked(logits_of, NSA_GROUP, tq, _with_ones(v), m_ref.at[hkv], acc_ref.at[hkv],
                              p_ref.at[hkv], alpha_ref.at[hkv], may_be_empty)

    _state_init(m_ref, acc_ref)

    def slc_body(i, carry):
        k0 = pl.multiple_of(i * tk, tk)
        key_rel = (k0 - s0 + lax.broadcasted_iota(jnp.int32, (1, tk), 1)).astype(F32)
        for hkv in kv_heads:
            stream(hkv, ks_ref[pl.ds(k0, tk), cols(hkv)], bias_ref[hkv, :, pl.ds(k0, tk)], key_rel,
                   vs_ref[pl.ds(k0, tk), cols(hkv)], ps_ref, False)
        return carry

    lax.fori_loop(0, (s0 + tq + tk - 1) // tk, slc_body, 0)
    for hkv in kv_heads:
        mix_ref[hkv] = mix_ref[hkv] + gate_col(hkv, 1) * _state_result(acc_ref.at[hkv])

    _state_init(m_ref, acc_ref)

    def win_body(w, carry):
        k0 = pl.multiple_of(s0 - WINDOW + w * tw, tw)
        disti = row_t - (k0 + lax.broadcasted_iota(jnp.int32, (tq, tw), 1))
        bias = jnp.where((disti >= 0) & (disti < WINDOW), 0.0, NEG_INF)
        key_rel = (k0 - s0 + lax.broadcasted_iota(jnp.int32, (1, tw), 1)).astype(F32)
        for hkv in kv_heads:
            stream(hkv, kw_ref[pl.ds(k0, tw), cols(hkv)], bias, key_rel,
                   vw_ref[pl.ds(k0, tw), cols(hkv)], pw_ref, True)
        return carry

    n_win_tiles = (WINDOW + tq) // tw
    lax.fori_loop(jnp.maximum(0, (WINDOW - s0) // tw), n_win_tiles, win_body, 0)

    for hkv in kv_heads:
        mix = mix_ref[hkv] + gate_col(hkv, 2) * _state_result(acc_ref.at[hkv])
        for g, h in enumerate(heads(hkv)):
            z = z_ref[:, h * HEAD_DIM:(h + 1) * HEAD_DIM]
            o_ref[:, h * HEAD_DIM:(h + 1) * HEAD_DIM] = (mix[g * tq:(g + 1) * tq] * _silu(z)).astype(o_ref.dtype)


def _nsa_prompt(proj, kcv, expand, batch, seq):
    tq, tk, tw, ck = 256, 256, 256, 512
    assert seq % ck == 0 and WINDOW % tw == 0 and tq % tw == 0 and seq // SEL_BLOCK <= LANES
    nq = seq // tq
    rows = NSA_GROUP * tq
    kernel = functools.partial(_nsa_prompt_kernel, tq=tq, tk=tk, tw=tw, ck=ck, seq=seq)
    cmp_spec = lambda kind: pl.BlockSpec((1, 1, NSA_KV_HEADS, seq // CMP_BLOCK, HEAD_DIM),
                                         lambda b, i: (b, kind, 0, 0, 0))
    return pl.pallas_call(
        kernel,
        grid=(batch, nq),
        in_specs=[pl.BlockSpec((tq, NSA_WIDTH), lambda b, i: (b * nq + i, 0)),
                  cmp_spec(0), cmp_spec(1),
                  pl.BlockSpec((seq, KV_WIDTH), lambda b, i: (b, 2)),
                  pl.BlockSpec((seq, KV_WIDTH), lambda b, i: (b, 3)),
                  pl.BlockSpec((seq, KV_WIDTH), lambda b, i: (b, 0)),
                  pl.BlockSpec((seq, KV_WIDTH), lambda b, i: (b, 1)),
                  pl.BlockSpec((tq, LANES), lambda b, i: (b * nq + i, 0)),
                  pl.BlockSpec((tq, NSA_WIDTH), lambda b, i: (b * nq + i, 0)),
                  pl.BlockSpec((LANES, seq), lambda b, i: (0, 0))],
        out_specs=pl.BlockSpec((tq, NSA_WIDTH), lambda b, i: (b * nq + i, 0)),
        out_shape=jax.ShapeDtypeStruct((batch * seq, NSA_WIDTH), BF16),
        scratch_shapes=[pltpu.VMEM((NSA_KV_HEADS, rows, LANES), F32),
                        pltpu.VMEM((NSA_KV_HEADS, rows, 2 * HEAD_DIM), F32),
                        pltpu.VMEM((NSA_KV_HEADS, rows, LANES), F32),
                        pltpu.VMEM((NSA_KV_HEADS, rows, tk), BF16),
                        pltpu.VMEM((NSA_KV_HEADS, rows, tw), BF16),
                        pltpu.VMEM((NSA_KV_HEADS, rows, HEAD_DIM), F32),
                        pltpu.VMEM((NSA_KV_HEADS, tq, seq), F32)],
        compiler_params=_params(("parallel", "arbitrary")),
        name="nsa_prompt",
    )(proj["q_n"], kcv, kcv, proj["kv4_bf"], proj["kv4_bf"], proj["kvwin_bf"], proj["kvwin_bf"],
      proj["small"], proj["z_n"], expand)


def _fox_prompt_kernel(q_ref, k_ref, v_ref, d_ref, z_ref, o_ref, m_ref, acc_ref, alpha_ref, p_ref, *,
                       t, chunk, n_heads):
    qi = pl.program_id(2)
    _state_init(m_ref, acc_ref)

    def tile(i, causal):
        k0 = pl.multiple_of(i * t, t)
        for hh in range(n_heads):
            cols = slice(hh * HEAD_DIM, (hh + 1) * HEAD_DIM)
            s = _dot_nt(q_ref[:, cols], k_ref[pl.ds(k0, t), cols])
            d_row = d_ref[hh, :, pl.ds(k0, t)] * LOG2E

            def logits_of(c, s=s, d_row=d_row):
                lg = s[c * chunk:(c + 1) * chunk] * (SCALE * LOG2E) - d_row
                if causal:
                    row = c * chunk + lax.broadcasted_iota(jnp.int32, (chunk, t), 0)
                    col = lax.broadcasted_iota(jnp.int32, (chunk, t), 1)
                    lg = jnp.where(col <= row, lg, NEG_INF)
                return lg

            _state_update_chunked(logits_of, t // chunk, chunk, _with_ones(v_ref[pl.ds(k0, t), cols]),
                                  m_ref.at[hh], acc_ref.at[hh], p_ref.at[hh], alpha_ref.at[hh], False)

    def body(i, carry):
        tile(i, False)
        return carry

    lax.fori_loop(0, qi, body, 0)
    tile(qi, True)
    for hh in range(n_heads):
        cols = slice(hh * HEAD_DIM, (hh + 1) * HEAD_DIM)
        o_ref[:, cols] = (_state_result(acc_ref.at[hh]) * _silu(z_ref[:, cols])).astype(o_ref.dtype)


def _fox_prompt(proj, d_rows, batch, seq):
    t, chunk, n_heads = 512, 128, 4
    nq = seq // t
    width = n_heads * HEAD_DIM
    n_hg = FOX_HEADS // n_heads
    return pl.pallas_call(
        functools.partial(_fox_prompt_kernel, t=t, chunk=chunk, n_heads=n_heads),
        grid=(batch, n_hg, nq),
        in_specs=[pl.BlockSpec((t, width), lambda b, h, i: (b * nq + i, h)),
                  pl.BlockSpec((seq, width), lambda b, h, i: (b, h)),
                  pl.BlockSpec((seq, width), lambda b, h, i: (b, n_hg + h)),
                  pl.BlockSpec((n_heads, 1, seq), lambda b, h, i: (b * n_hg + h, 0, 0)),
                  pl.BlockSpec((t, width), lambda b, h, i: (b * nq + i, h))],
        out_specs=pl.BlockSpec((t, width), lambda b, h, i: (b * nq + i, h)),
        out_shape=jax.ShapeDtypeStruct((batch * seq, FOX_WIDTH), BF16),
        scratch_shapes=[pltpu.VMEM((n_heads, t, LANES), F32), pltpu.VMEM((n_heads, t, 2 * HEAD_DIM), F32),
                        pltpu.VMEM((n_heads, t, LANES), F32), pltpu.VMEM((n_heads, t, t), BF16)],
        compiler_params=_params(("parallel", "parallel", "arbitrary")),
        name="fox_prompt",
    )(proj["q_f"], proj["kv_f_bf"], proj["kv_f_bf"], d_rows, proj["z_f"])


def _prompt_layer(x, norm_in, wts, cmp_pe, cmp_w_bf, w_out_bf, norm_final):
    batch, seq, d = x.shape
    x2d = x.reshape(batch * seq, d)
    proj = _project_all(x2d, norm_in, wts, 1024, kv4_state_order=True)
    n_gate = 3 * NSA_HEADS
    logf = proj["small"][:, n_gate:n_gate + FOX_HEADS].reshape(batch, seq, FOX_HEADS)
    d_rows = _cumsum_rows(jnp.swapaxes(logf, 1, 2).reshape(batch * FOX_HEADS, seq))
    kcv = _compress_prompt(proj["kv4_cmp"], cmp_pe, cmp_w_bf, batch, seq)
    blk_of_key = jnp.arange(seq, dtype=jnp.int32) // SEL_BLOCK
    expand = (jnp.arange(LANES, dtype=jnp.int32)[:, None] == blk_of_key[None, :]).astype(BF16)
    mix_n = _nsa_prompt(proj, kcv, expand, batch, seq)
    mix_f = _fox_prompt(proj, d_rows.reshape(batch * FOX_HEADS, 1, seq), batch, seq)
    y = _merge(x2d, mix_n, mix_f, w_out_bf, norm_final, 512).reshape(batch, seq, d)
    w_keep = min(WINDOW, seq)
    state = (proj["kv4_state"].reshape(batch, seq, 4, NSA_KV_HEADS, HEAD_DIM),
             proj["kv_f"].reshape(batch, seq, 2, FOX_HEADS, HEAD_DIM),
             logf,
             proj["kvwin"].reshape(batch, seq, 2 * KV_WIDTH)[:, seq - w_keep:].reshape(
                 batch, w_keep, 2, NSA_KV_HEADS, HEAD_DIM))
    return y, state


HALF_CHANNELS = NSA_CHANNELS // 2


def _nsa_page_specs(pg, half):
    def spec(u):
        return pl.BlockSpec((1, 1, PAGE_SIZE, 1, HALF_CHANNELS, HEAD_DIM),
                            lambda b, j, pt: (0, pt[b, j * pg + u], 0, half, 0, 0))
    return [spec(u) for u in range(pg)]


def _channel(page_ref, ch, start=0, n=PAGE_SIZE):
    rows = page_ref.reshape(PAGE_SIZE * HALF_CHANNELS, HEAD_DIM)
    return rows[pl.ds(start * HALF_CHANNELS + ch, n, stride=HALF_CHANNELS), :]


def _nsa_cmp_kernel(pt_ref, *refs, pg, n_pages):
    page_refs = refs[:pg]
    q_ref, pe_ref, w_ref, oc_ref, sel_ref, stage_ref, xs_ref = refs[pg:]
    jg = pl.program_id(1)
    past = n_pages * PAGE_SIZE
    nb = past // CMP_BLOCK
    half = nb // 2
    blocks_per_page = PAGE_SIZE // CMP_BLOCK
    for u in range(pg):
        for bl in range(blocks_per_page):
            m = (jg * pg + u) * (blocks_per_page // 2) + bl // 2
            dst = pl.multiple_of(((bl % 2) * half + m) * STAGE_PITCH, SUBLANES)
            for ch in range(2 * NSA_KV_HEADS):
                stage_ref[ch, pl.ds(dst, CMP_BLOCK), :] = _channel(page_refs[u], ch, bl * CMP_BLOCK, CMP_BLOCK)

    @pl.when(jg == pl.num_programs(1) - 1)
    def _():
        cmp = [[_compress_rows(stage_ref.at[kind * NSA_KV_HEADS + h], pe_ref, kind, w_ref[kind], xs_ref, nb,
                               (0, half * STAGE_PITCH), STAGE_PITCH).astype(BF16)
                for h in range(NSA_KV_HEADS)] for kind in range(2)]
        rows = NSA_GROUP * T_PAD
        for hkv in range(NSA_KV_HEADS):
            q = q_ref[0, hkv * rows:(hkv + 1) * rows, :].astype(BF16)
            kc, vc = cmp[0][hkv], cmp[1][hkv]
            s = _dot_nt(q, kc) * SCALE
            lane = lax.broadcasted_iota(jnp.int32, (rows, nb), 1)
            blk = jnp.where(lane < half, 2 * lane, 2 * lane - (nb - 1))
            row = lax.broadcasted_iota(jnp.int32, (rows, nb), 0)
            slope = _head_slope(hkv * NSA_GROUP + row // T_PAD)
            dist = (past + row % T_PAD - ((blk + 1) * CMP_BLOCK - 1)).astype(F32)
            lg = jnp.where(dist >= 0, s - slope * dist, NEG_INF)
            mx = jnp.max(lg, axis=1, keepdims=True)
            mx = jnp.where(mx == NEG_INF, 0.0, mx)
            p = jnp.exp(lg - mx)
            p = p / jnp.maximum(jnp.sum(p, axis=1, keepdims=True), 1e-30)
            oc_ref[0, hkv * rows:(hkv + 1) * rows, :] = _dot(p.astype(BF16), vc)
            score = p[0:T_PAD]
            for g in range(1, NSA_GROUP):
                score = score + p[g * T_PAD:(g + 1) * T_PAD]
            score = score[:, :half] + score[:, half:]
            blk_s = lax.broadcasted_iota(jnp.int32, (T_PAD, half), 1).astype(F32)
            forced = (blk_s == 0) | (blk_s == half - 1)
            s_left = jnp.where(forced, NEG_INF, score)
            picked = jnp.zeros((T_PAD, half), F32)
            for _ in range(SEL_TOPK - 3):
                best = jnp.max(s_left, axis=1, keepdims=True)
                first = jnp.min(jnp.where(s_left == best, blk_s, float(half)), axis=1, keepdims=True)
                hit = blk_s == first
                picked = jnp.where(hit, 1.0, picked)
                s_left = jnp.where(hit, NEG_INF, s_left)
            sel_ref[0, hkv * T_PAD:(hkv + 1) * T_PAD, :] = jnp.where(forced, 1.0, picked)


def _nsa_cmp_sample(cache, page_table, q_rows, cmp_pe, cmp_w_bf, pg):
    n_seq, n_pages = page_table.shape
    past = n_pages * PAGE_SIZE
    nb = past // CMP_BLOCK
    assert n_pages % pg == 0 and past // SEL_BLOCK >= SEL_TOPK
    rows = NSA_HEADS * T_PAD
    grid_spec = pltpu.PrefetchScalarGridSpec(
        num_scalar_prefetch=1,
        grid=(n_seq, n_pages // pg),
        in_specs=_nsa_page_specs(pg, 0) + [
            pl.BlockSpec((1, rows, HEAD_DIM), lambda b, j, pt: (b, 0, 0)),
            pl.BlockSpec((2, CMP_BLOCK, HEAD_DIM), lambda b, j, pt: (0, 0, 0)),
            pl.BlockSpec((2, CMP_BLOCK * HEAD_DIM, HEAD_DIM), lambda b, j, pt: (0, 0, 0))],
        out_specs=[pl.BlockSpec((1, rows, HEAD_DIM), lambda b, j, pt: (b, 0, 0)),
                   pl.BlockSpec((1, NSA_KV_HEADS * T_PAD, past // SEL_BLOCK), lambda b, j, pt: (b, 0, 0))],
        scratch_shapes=[pltpu.VMEM((2 * NSA_KV_HEADS, nb * STAGE_PITCH, HEAD_DIM), F32),
                        pltpu.VMEM((nb, CMP_BLOCK * HEAD_DIM), BF16)],
    )
    return pl.pallas_call(
        functools.partial(_nsa_cmp_kernel, pg=pg, n_pages=n_pages),
        grid_spec=grid_spec,
        out_shape=[jax.ShapeDtypeStruct((n_seq, rows, HEAD_DIM), F32),
                   jax.ShapeDtypeStruct((n_seq, NSA_KV_HEADS * T_PAD, past // SEL_BLOCK), F32)],
        compiler_params=_params(("parallel", "arbitrary")),
        name="nsa_sample_cmp",
    )(page_table, *([cache] * pg), q_rows, cmp_pe, cmp_w_bf)


def _nsa_slc_kernel(pt_ref, *refs, pg, part_pages, n_pages, w_buf, n_new):
    page_refs = refs[:pg]
    (q_ref, sel_ref, e_ref, new_ref, win_ref, oc_ref, g_ref, z_ref, o_ref, tab_ref, m_ref, acc_ref) = refs[pg:]
    jg = pl.program_id(1)
    past = n_pages * PAGE_SIZE
    hr = n_new * NSA_GROUP
    rows = NSA_KV_HEADS * hr
    row = lax.broadcasted_iota(jnp.int32, (rows, 1), 0)
    t_q = (row % hr) // NSA_GROUP
    slope = _head_slope((row // hr) * NSA_GROUP + row % NSA_GROUP)
    q = [q_ref[0, h * hr:(h + 1) * hr, :].astype(BF16) for h in range(NSA_KV_HEADS)]

    def scores(k_of):
        return jnp.concatenate([_dot_nt(q[h], k_of(h)) for h in range(NSA_KV_HEADS)], axis=0) * SCALE

    def pv(v_of):
        return lambda p: jnp.concatenate(
            [_dot(p[h * hr:(h + 1) * hr], _with_ones(v_of(h))) for h in range(NSA_KV_HEADS)], axis=0)


    @pl.when(jg == 0)
    def _():
        expanded = _dot(sel_ref[0].astype(BF16), e_ref[...])
        key = lax.broadcasted_iota(jnp.int32, (rows, past), 1)
        tab_ref[...] = jnp.where(expanded > 0.5, 0.0, NEG_INF) + slope * (key - past).astype(F32)
        _state_init(m_ref, acc_ref)

    def page_stream(ch, pages):
        return jnp.concatenate([_channel(page_refs[u], ch).astype(BF16) for u in pages], axis=0)

    parts = []
    for first in range(0, pg, part_pages):
        pages = range(first, first + part_pages)
        c0 = pl.multiple_of((jg * pg + first) * PAGE_SIZE, part_pages * PAGE_SIZE)
        lg = (scores(lambda h: page_stream(h, pages))
              + tab_ref[:, pl.ds(c0, part_pages * PAGE_SIZE)])
        parts.append((lg, pv(lambda h, pages=pages: page_stream(NSA_KV_HEADS + h, pages))))
    _state_update_parts(parts, m_ref, acc_ref, True)

    @pl.when(jg == pl.num_programs(1) - 1)
    def _():
        new = lambda ch: new_ref[0, ch].astype(BF16)
        j_new = lax.broadcasted_iota(jnp.int32, (rows, PAGE_SIZE), 1)
        bias_new = jnp.where((j_new <= t_q) & (j_new < n_new), slope * j_new.astype(F32), NEG_INF)
        _state_update(scores(lambda h: new(h)) + bias_new, pv(lambda h: new(NSA_KV_HEADS + h)),
                      m_ref, acc_ref, False)
        o_s = _state_result(acc_ref)

        _state_init(m_ref, acc_ref)
        i_w = lax.broadcasted_iota(jnp.int32, (rows, w_buf), 1)
        dist_w = t_q + w_buf - i_w
        valid_w = (dist_w >= 0) & (dist_w < WINDOW) & (past - w_buf + i_w >= 0)
        bias_w = jnp.where(valid_w, slope * (i_w - w_buf).astype(F32), NEG_INF)
        win = lambda ch: win_ref[0, ch].astype(BF16)
        _state_update(scores(lambda h: win(h)) + bias_w, pv(lambda h: win(NSA_KV_HEADS + h)),
                      m_ref, acc_ref, True)
        _state_update(scores(lambda h: new(2 * NSA_KV_HEADS + h)) + bias_new,
                      pv(lambda h: new(3 * NSA_KV_HEADS + h)), m_ref, acc_ref, True)
        o_w = _state_result(acc_ref)

        gates = g_ref[0]
        mix = gates[:, 0:1] * oc_ref[0] + gates[:, 1:2] * o_s + gates[:, 2:3] * o_w
        o_ref[0] = mix * _silu(z_ref[0])


def _nsa_slc_sample(cache, page_table, q_rows, sel_rows, expand, new_rows, win_state, o_c_rows, gates_rows, z_rows,
                    pg, n_new):
    n_seq, n_pages = page_table.shape
    past = n_pages * PAGE_SIZE
    w_buf = win_state.shape[2]
    rows = n_new * NSA_HEADS
    per_seq = lambda *shape: pl.BlockSpec((1,) + shape, lambda b, j, pt: (b,) + (0,) * len(shape))
    grid_spec = pltpu.PrefetchScalarGridSpec(
        num_scalar_prefetch=1,
        grid=(n_seq, n_pages // pg),
        in_specs=_nsa_page_specs(pg, 1) + [
            per_seq(rows, HEAD_DIM),
            per_seq(rows, past // SEL_BLOCK),
            pl.BlockSpec((past // SEL_BLOCK, past), lambda b, j, pt: (0, 0)),
            per_seq(4 * NSA_KV_HEADS, PAGE_SIZE, HEAD_DIM),
            per_seq(2 * NSA_KV_HEADS, w_buf, HEAD_DIM),
            per_seq(rows, HEAD_DIM),
            per_seq(rows, 3),
            per_seq(rows, HEAD_DIM)],
        out_specs=per_seq(rows, HEAD_DIM),
        scratch_shapes=[pltpu.VMEM((rows, past), F32), pltpu.VMEM((rows, LANES), F32),
                        pltpu.VMEM((rows, 2 * HEAD_DIM), F32)],
    )
    return pl.pallas_call(
        functools.partial(_nsa_slc_kernel, pg=pg, part_pages=2, n_pages=n_pages, w_buf=w_buf, n_new=n_new),
        grid_spec=grid_spec,
        out_shape=jax.ShapeDtypeStruct((n_seq, rows, HEAD_DIM), F32),
        compiler_params=_params(("parallel", "arbitrary")),
        name="nsa_sample_slc_win",
    )(page_table, *([cache] * pg), q_rows, sel_rows, expand, new_rows, win_state, o_c_rows, gates_rows, z_rows)


def _page_forget_scan(x):
    n = x.shape[1]
    lane = lax.broadcasted_iota(jnp.int32, x.shape, 1)
    s = FOX_HEADS
    while s < n:
        x = x + jnp.where(lane >= s, pltpu.roll(x, s, axis=1), 0.0)
        s *= 2
    total = jnp.where(lane >= n - FOX_HEADS, x, 0.0)
    s = FOX_HEADS
    while s < n:
        total = total + pltpu.roll(total, n - s, axis=1)
        s *= 2
    return x, total


def _fox_sample_kernel(pt_ref, *refs, pg, part_pages, n_new):
    page_refs = refs[:pg]
    (q_ref, lc_ref, kn_ref, vn_ref, ln_ref, z_ref, o_ref, m_ref, acc_ref, carry_ref, lg_ref) = refs[pg:]
    b = pl.program_id(0)
    jg = pl.program_id(1)
    rows = n_new * FOX_HEADS
    keys = PAGE_SIZE * FOX_HEADS

    @pl.when(jg == 0)
    def _():
        _state_init(m_ref, acc_ref)
        carry_ref[...] = jnp.zeros(carry_ref.shape, F32)

    q = q_ref[0].astype(BF16)
    own = (lax.broadcasted_iota(jnp.int32, (rows, keys), 1) % FOX_HEADS
           == lax.broadcasted_iota(jnp.int32, (rows, keys), 0) % FOX_HEADS)

    for u in range(pg):
        lg_ref[u:u + 1, :] = lc_ref[pl.ds(pt_ref[b, jg * pg + u], 1), :]
    within, total = _page_forget_scan(lg_ref[...])
    carry = carry_ref[...]
    logits = []
    for u in range(pg):
        d_u = within[u:u + 1] + carry
        carry = carry + total[u:u + 1]
        k = page_refs[u][0, 0, :, 0].reshape(keys, HEAD_DIM).astype(BF16)
        logits.append(jnp.where(own, _dot_nt(q, k) * SCALE - d_u, NEG_INF))
    carry_ref[...] = carry

    def pv(pages):
        def pv_of(p):
            out = None
            for i, u in enumerate(pages):
                v = page_refs[u][0, 0, :, 1].reshape(keys, HEAD_DIM).astype(BF16)
                part = _dot(p[:, i * keys:(i + 1) * keys], _with_ones(v))
                out = part if out is None else out + part
            return out
        return pv_of

    parts = []
    for first in range(0, pg, part_pages):
        pages = range(first, first + part_pages)
        parts.append((jnp.concatenate([logits[u] for u in pages], axis=1), pv(pages)))
    _state_update_parts(parts, m_ref, acc_ref, False)

    @pl.when(jg == pl.num_programs(1) - 1)
    def _():
        within_new, _ = _page_forget_scan(ln_ref[0])
        d_new = (within_new + carry_ref[...])[:, :PAGE_SIZE]
        c = lax.broadcasted_iota(jnp.int32, (rows, PAGE_SIZE), 1)
        r = lax.broadcasted_iota(jnp.int32, (rows, PAGE_SIZE), 0)
        valid = (c < rows) & (c % FOX_HEADS == r % FOX_HEADS) & (c // FOX_HEADS <= r // FOX_HEADS)
        lg = jnp.where(valid, _dot_nt(q, kn_ref[0].astype(BF16)) * SCALE - d_new, NEG_INF)
        _state_update(lg, lambda p: _dot(p, _with_ones(vn_ref[0].astype(BF16))), m_ref, acc_ref, False)
        o_ref[0] = _state_result(acc_ref) * _silu(z_ref[0])


def _fox_sample(cache, page_table, q_rows, logf_cache, k_new, v_new, logf_new, z_rows, pg, n_new):
    n_seq, n_pages = page_table.shape
    n_pool = logf_cache.shape[0]
    rows = n_new * FOX_HEADS
    keys = PAGE_SIZE * FOX_HEADS
    per_seq = lambda *shape: pl.BlockSpec((1,) + shape, lambda b, j, pt: (b,) + (0,) * len(shape))

    def page_spec(u):
        return pl.BlockSpec((1, 1, PAGE_SIZE, 2, FOX_HEADS, HEAD_DIM),
                            lambda b, j, pt: (0, pt[b, j * pg + u], 0, 0, 0, 0))

    grid_spec = pltpu.PrefetchScalarGridSpec(
        num_scalar_prefetch=1,
        grid=(n_seq, n_pages // pg),
        in_specs=[page_spec(u) for u in range(pg)] + [
            per_seq(rows, HEAD_DIM),
            pl.BlockSpec((n_pool, keys), lambda b, j, pt: (0, 0), pipeline_mode=pl.Buffered(1)),
            per_seq(PAGE_SIZE, HEAD_DIM),
            per_seq(PAGE_SIZE, HEAD_DIM),
            per_seq(1, keys),
            per_seq(rows, HEAD_DIM)],
        out_specs=per_seq(rows, HEAD_DIM),
        scratch_shapes=[pltpu.VMEM((rows, LANES), F32), pltpu.VMEM((rows, 2 * HEAD_DIM), F32),
                        pltpu.VMEM((1, keys), F32), pltpu.VMEM((pg, keys), F32)],
    )
    return pl.pallas_call(
        functools.partial(_fox_sample_kernel, pg=pg, part_pages=2, n_new=n_new),
        grid_spec=grid_spec,
        out_shape=jax.ShapeDtypeStruct((n_seq, rows, HEAD_DIM), F32),
        compiler_params=_params(("parallel", "arbitrary")),
        name="fox_sample",
    )(page_table, *([cache] * pg), q_rows, logf_cache, k_new, v_new, logf_new, z_rows)


def _held_pages(page_table, needed, pg):
    n_seq, n_pages = page_table.shape
    slots = page_table.reshape(n_seq * n_pages // pg, pg)
    want = needed.reshape(slots.shape).at[0].set(True)
    step = jnp.arange(slots.shape[0], dtype=jnp.int32)[:, None]
    filled_at = lax.cummax(jnp.where(want, step, 0), axis=0)
    return jnp.take_along_axis(slots, filled_at, axis=0).reshape(n_seq, n_pages)


def _pad_axis(a, axis, size):
    pad = [(0, 0)] * a.ndim
    pad[axis] = (0, size - a.shape[axis])
    return jnp.pad(a, pad)


def _sample_layer(x, cache_nsa_kv, cache_fox_kv, cache_fox_logf, state_nsa_win, page_table,
                  norm_in, wts, cmp_pe, cmp_w_bf, w_out_bf, norm_final):
    n_seq, t_new, d = x.shape
    n_pool = cache_nsa_kv.shape[1]
    n_pages = page_table.shape[1]
    past = n_pages * PAGE_SIZE
    w_buf = state_nsa_win.shape[2]
    assert t_new <= T_PAD and t_new < SEL_BLOCK and past % SEL_BLOCK == 0
    x2d = x.reshape(n_seq * t_new, d)
    proj = _project_all(x2d, norm_in, wts, n_seq * t_new, one_call=True)
    n_gate = 3 * NSA_HEADS
    small = proj["small"].reshape(n_seq, t_new, LANES)
    logf = small[:, :, n_gate:n_gate + FOX_HEADS]

    def head_rows(a):
        a = _pad_axis(a.reshape(n_seq, t_new, NSA_HEADS, -1), 1, T_PAD)
        return jnp.swapaxes(a, 1, 2).reshape(n_seq, NSA_HEADS * T_PAD, a.shape[-1])

    def group_rows(a):
        a = a.reshape(n_seq, t_new, NSA_KV_HEADS, NSA_GROUP, a.shape[-1])
        return jnp.swapaxes(a, 1, 2).reshape(n_seq, NSA_HEADS * t_new, a.shape[-1])

    q_n = proj["q_n"].astype(F32)
    gates = small[:, :, :n_gate].reshape(n_seq, t_new, NSA_HEADS, 3)
    z_n = proj["z_n"].reshape(n_seq, t_new, NSA_HEADS, HEAD_DIM)
    nsa_new = jnp.concatenate([proj["kv4"].reshape(n_seq, t_new, 4 * NSA_KV_HEADS, HEAD_DIM)[:, :, 2 * NSA_KV_HEADS:],
                               proj["kvwin"].reshape(n_seq, t_new, 2 * NSA_KV_HEADS, HEAD_DIM)], axis=2)
    nsa_new = _pad_axis(jnp.swapaxes(nsa_new, 1, 2), 2, PAGE_SIZE)
    win_state = jnp.swapaxes(state_nsa_win[0].reshape(n_seq, w_buf, 2 * NSA_KV_HEADS, HEAD_DIM), 1, 2)
    blk_of_key = jnp.arange(past, dtype=jnp.int32) // SEL_BLOCK
    expand = (jnp.arange(past // SEL_BLOCK, dtype=jnp.int32)[:, None] == blk_of_key[None, :]).astype(BF16)
    nsa_cache = cache_nsa_kv.reshape(cache_nsa_kv.shape[0], n_pool, PAGE_SIZE, 2, HALF_CHANNELS, HEAD_DIM)

    pg = 16
    o_c, sel = _nsa_cmp_sample(nsa_cache, page_table, head_rows(q_n), cmp_pe, cmp_w_bf, min(2 * pg, n_pages))
    o_c = jnp.swapaxes(o_c.reshape(n_seq, NSA_HEADS, T_PAD, HEAD_DIM)[:, :, :t_new], 1, 2)
    sel = sel.reshape(n_seq, NSA_KV_HEADS, T_PAD, past // SEL_BLOCK)[:, :, :t_new]
    page_needed = jnp.any(sel.reshape(n_seq, -1, n_pages, PAGE_SIZE // SEL_BLOCK) > 0.5, axis=(1, 3))
    slc_pages = _held_pages(page_table, page_needed, pg)
    sel = jnp.broadcast_to(sel[:, :, :, None, :], (n_seq, NSA_KV_HEADS, t_new, NSA_GROUP, past // SEL_BLOCK))
    mix_n = _nsa_slc_sample(nsa_cache, slc_pages, group_rows(q_n.reshape(n_seq, t_new, NSA_HEADS, HEAD_DIM)),
                            sel.reshape(n_seq, NSA_HEADS * t_new, past // SEL_BLOCK), expand, nsa_new, win_state,
                            group_rows(o_c), group_rows(gates), group_rows(z_n), pg, t_new)
    mix_n = jnp.swapaxes(mix_n.reshape(n_seq, NSA_KV_HEADS, t_new, NSA_GROUP * HEAD_DIM), 1, 2)
    mix_n = mix_n.reshape(n_seq * t_new, NSA_WIDTH).astype(BF16)

    fox_rows = lambda a: a.reshape(n_seq, t_new * FOX_HEADS, HEAD_DIM)
    kv_f = proj["kv_f"].reshape(n_seq, t_new, 2, FOX_HEADS * HEAD_DIM)
    k_new = _pad_axis(fox_rows(kv_f[:, :, 0]), 1, PAGE_SIZE)
    v_new = _pad_axis(fox_rows(kv_f[:, :, 1]), 1, PAGE_SIZE)
    logf_new = _pad_axis(logf.reshape(n_seq, 1, t_new * FOX_HEADS), 2, PAGE_SIZE * FOX_HEADS)
    logf_cache = cache_fox_logf[0].reshape(n_pool, PAGE_SIZE * FOX_HEADS)
    mix_f = _fox_sample(cache_fox_kv, page_table, fox_rows(proj["q_f"].astype(F32)), logf_cache, k_new, v_new,
                        logf_new, fox_rows(proj["z_f"]), 16, t_new)
    mix_f = mix_f.reshape(n_seq * t_new, FOX_WIDTH).astype(BF16)

    y = _merge(x2d, mix_n, mix_f, w_out_bf, norm_final, n_seq * t_new).reshape(n_seq, t_new, d)
    kvwin_new = proj["kvwin"].reshape(n_seq, t_new, 2, NSA_KV_HEADS, HEAD_DIM)
    state = (proj["kv4"].reshape(n_seq, t_new, 4, NSA_KV_HEADS, HEAD_DIM),
             proj["kv_f"].reshape(n_seq, t_new, 2, FOX_HEADS, HEAD_DIM),
             logf,
             jnp.concatenate([state_nsa_win[0], kvwin_new], axis=1)[:, t_new:])
    return y, state


def kernel(x_prompt, x_sample, cache_nsa_kv, cache_fox_kv, cache_fox_logf, state_nsa_win, page_table,
           norm_in, w_in, b_gate, b_forget, cmp_pe, cmp_w, w_out, norm_final):
    assert norm_in.shape[0] == 1, "single-layer trunk"
    wts = _split_weights(w_in[0], b_gate[0], b_forget[0])
    cmp_w_bf = cmp_w[0].reshape(2, CMP_BLOCK * HEAD_DIM, HEAD_DIM).astype(BF16)
    w_out_bf = w_out[0].astype(BF16)
    y_p, st_p = _prompt_layer(x_prompt, norm_in[0], wts, cmp_pe[0], cmp_w_bf, w_out_bf, norm_final)
    y_s, st_s = _sample_layer(x_sample, cache_nsa_kv, cache_fox_kv, cache_fox_logf, state_nsa_win,
                              page_table, norm_in[0], wts, cmp_pe[0], cmp_w_bf, w_out_bf, norm_final)
    outs = [y_p, y_s]
    for s_p, s_s in zip(st_p, st_s):
        outs.extend([s_p[None], s_s[None]])
    return tuple(outs)
```
